```python
import math
import jax
import jax.numpy as jnp
from jax import lax
import numpy as np

D_MODEL = 1024
BATCH = 2
SEQ = 8192
DEPTH = 2
DEC_BATCH = 32
DEC_SEQ = 8
PAST_LEN = 8192
PAGE_SIZE = 128

HEAD_DIM = 64
N_HEADS = D_MODEL // HEAD_DIM
NSA_HEADS = N_HEADS // 2
NSA_KV = 2
NSA_GROUP = NSA_HEADS // NSA_KV
DIFF_HEADS = N_HEADS - NSA_HEADS
DIFF_HALF = HEAD_DIM // 2
L_CMP = 32
L_SEL = 64
N_SELECT = 16
WINDOW = 512
Q_BLOCK = 128
D_FF = 2816
CONV_WIDTH = 3
FORCE_BONUS = 1.0e4
EPS = 1e-6

NSA_Q_W = NSA_HEADS * HEAD_DIM
NSA_KV_W = NSA_KV * HEAD_DIM
NSA_GATE_W = NSA_HEADS * 3
DIFF_QK_W = DIFF_HEADS * HEAD_DIM
DIFF_V_W = DIFF_HEADS * HEAD_DIM
_SPLIT_SIZES = [NSA_Q_W] + [NSA_KV_W] * 6 + [NSA_GATE_W, DIFF_QK_W, DIFF_QK_W, DIFF_V_W]
SPLIT_POINTS = [int(v) for v in np.cumsum(_SPLIT_SIZES[:-1])]
N_IN = int(sum(_SPLIT_SIZES))
MIX_W = NSA_Q_W + DIFF_V_W

kernel_name = "hymba_nsa_diffattn_convglu_step"


def rmsnorm(x, g):
    xf = x.astype(jnp.float32)
    y = xf * lax.rsqrt(jnp.mean(xf * xf, axis=-1, keepdims=True) + EPS)
    return (y * g.astype(jnp.float32)).astype(x.dtype)


def alibi_slopes(n):
    return 2.0 ** (-8.0 * jnp.arange(1, n + 1, dtype=jnp.float32) / n)


def masked_softmax(s, mask):
    p = jax.nn.softmax(jnp.where(mask, s, -1e30), axis=-1)
    return jnp.where(mask, p, 0.0)


def project(h, w_in, nsa_qg, nsa_kg, diff_qg, diff_kg):
    B, T, _ = h.shape
    z = h @ w_in
    q, kc, vc, ks, vs, kw, vw, gate, dq, dk, dv = jnp.split(z, SPLIT_POINTS, axis=-1)
    kv = lambda a: a.reshape(B, T, NSA_KV, HEAD_DIM)
    q = rmsnorm(q.reshape(B, T, NSA_KV, NSA_GROUP, HEAD_DIM), nsa_qg)
    ks = rmsnorm(kv(ks), nsa_kg[1])
    kw = rmsnorm(kv(kw), nsa_kg[2])
    gate = jax.nn.sigmoid(gate.astype(jnp.float32)).reshape(B, T, NSA_KV, NSA_GROUP, 3)
    dq = rmsnorm(dq.reshape(B, T, DIFF_HEADS, 2, DIFF_HALF), diff_qg)
    dk = rmsnorm(dk.reshape(B, T, DIFF_HEADS, 2, DIFF_HALF), diff_kg)
    dv = dv.reshape(B, T, DIFF_HEADS, HEAD_DIM)
    return q, kv(kc), kv(vc), ks, kv(vs), kw, kv(vw), gate, dq, dk, dv


def compress_kv(kcr, vcr, cmp_w, kc_gain):
    B, T, G, HD = kcr.shape
    nc = T // L_CMP
    blocks = lambda a: a.reshape(B, nc, L_CMP, G, HD)
    kc = rmsnorm(jnp.einsum('bnlgd,l->bngd', blocks(kcr), cmp_w[0]), kc_gain)
    vc = jnp.einsum('bnlgd,l->bngd', blocks(vcr), cmp_w[1])
    cend = jnp.arange(nc) * L_CMP + (L_CMP - 1)
    return kc, vc, cend


def to_blocks(a):
    B, T, G, HD = a.shape
    ns = T // L_SEL
    return a.reshape(B, ns, L_SEL, G, HD).transpose(0, 3, 1, 2, 4).reshape(B, G, ns, L_SEL * HD)


def nsa_core(q, qpos, kc, vc, cend, ks_blk, vs_blk, kw, vw, kwpos, gate, slopes):
    f32 = jnp.float32
    B, Q, G, R, HD = q.shape
    NS = ks_blk.shape[2]
    scale = HD ** -0.5
    qf = q.astype(f32)
    sl = slopes[None, :, :, None, None]
    dist_c = (qpos[:, None] - cend[None, :]).astype(f32)
    s_c = jnp.einsum('bqgrd,bcgd->bgrqc', qf, kc.astype(f32)) * scale - sl * dist_c
    p_c = masked_softmax(s_c, cend[None, :] <= qpos[:, None])
    o_c = jnp.einsum('bgrqc,bcgd->bqgrd', p_c, vc.astype(f32))
    imp = p_c.sum(axis=2).reshape(B, G, Q, NS, L_SEL // L_CMP).sum(axis=-1)
    blk = jnp.arange(NS)[None, :]
    cur = (qpos // L_SEL)[:, None]
    forced = (blk == 0) | (blk == cur) | (blk == cur - 1)
    score = jnp.where(blk <= cur, imp + FORCE_BONUS * forced.astype(f32), -jnp.inf)
    n_top = min(N_SELECT, NS)
    top_score, idx = lax.top_k(score, n_top)
    bi = jnp.arange(B)[:, None, None]
    gi = jnp.arange(G)[None, :, None]
    flat = idx.reshape(B, G, Q * n_top)
    ks_g = ks_blk[bi, gi, flat].reshape(B, G, Q, n_top, L_SEL, HD).astype(f32)
    vs_g = vs_blk[bi, gi, flat].reshape(B, G, Q, n_top, L_SEL, HD).astype(f32)
    tok = idx[..., None] * L_SEL + jnp.arange(L_SEL)
    ok = (top_score > -jnp.inf)[..., None] & (tok <= qpos[:, None, None])
    dist_s = (qpos[:, None, None] - tok).astype(f32)[:, :, None]
    s_s = jnp.einsum('bqgrd,bgqkld->bgrqkl', qf, ks_g) * scale - slopes[None, :, :, None, None, None] * dist_s
    p_s = masked_softmax(s_s.reshape(B, G, R, Q, n_top * L_SEL), ok.reshape(B, G, 1, Q, n_top * L_SEL))
    o_s = jnp.einsum('bgrqkl,bgqkld->bqgrd', p_s.reshape(B, G, R, Q, n_top, L_SEL), vs_g)
    dist_w = qpos[:, None] - kwpos[None, :]
    s_w = jnp.einsum('bqgrd,bkgd->bgrqk', qf, kw.astype(f32)) * scale - sl * dist_w.astype(f32)
    mask_w = (dist_w >= 0) & (dist_w < WINDOW) & (kwpos[None, :] >= 0)
    p_w = masked_softmax(s_w, mask_w)
    o_w = jnp.einsum('bgrqk,bkgd->bqgrd', p_w, vw.astype(f32))
    return gate[..., 0:1] * o_c + gate[..., 1:2] * o_s + gate[..., 2:3] * o_w


def nsa_prompt(q, kcr, vcr, ks, vs, kw, vw, gate, cmp_w, kc_gain, slopes):
    B, T, G, HD = ks.shape
    kc, vc, cend = compress_kv(kcr, vcr, cmp_w, kc_gain)
    ks_blk, vs_blk = to_blocks(ks), to_blocks(vs)
    pad = ((0, 0), (WINDOW, 0), (0, 0), (0, 0))
    kw_pad, vw_pad = jnp.pad(kw, pad), jnp.pad(vw, pad)

    def one_block(start):
        qpos = start + jnp.arange(Q_BLOCK)
        kwpos = start - WINDOW + jnp.arange(WINDOW + Q_BLOCK)
        return nsa_core(lax.dynamic_slice_in_dim(q, start, Q_BLOCK, 1), qpos, kc, vc, cend, ks_blk, vs_blk,
                        lax.dynamic_slice_in_dim(kw_pad, start, WINDOW + Q_BLOCK, 1),
                        lax.dynamic_slice_in_dim(vw_pad, start, WINDOW + Q_BLOCK, 1), kwpos,
                        lax.dynamic_slice_in_dim(gate, start, Q_BLOCK, 1), slopes)

    out = lax.map(one_block, jnp.arange(T // Q_BLOCK) * Q_BLOCK)
    return jnp.moveaxis(out, 0, 1).reshape(B, T, G, NSA_GROUP, HD)


def diff_lambda_value(lp, lam_init):
    lp = lp.astype(jnp.float32)
    return jnp.exp(jnp.dot(lp[0], lp[1])) - jnp.exp(jnp.dot(lp[2], lp[3])) + lam_init


def diff_core(q, k, v, qpos, kpos, lam, lam_init, slopes, sub_g):
    f32 = jnp.float32
    scale = DIFF_HALF ** -0.5
    dist = qpos[:, None] - kpos[None, :]
    s = jnp.einsum('bqhid,bkhid->bhiqk', q.astype(f32), k.astype(f32)) * scale
    s = s - slopes[None, :, None, None, None] * dist.astype(f32)
    p = masked_softmax(s, dist >= 0)
    a = p[:, :, 0] - lam * p[:, :, 1]
    o = jnp.einsum('bhqk,bkhd->bqhd', a, v.astype(f32))
    return rmsnorm(o, sub_g) * (1.0 - lam_init)


def diff_prompt(q, k, v, lam, lam_init, slopes, sub_g):
    B, T = q.shape[:2]
    kpos = jnp.arange(T)

    def one_block(start):
        return diff_core(lax.dynamic_slice_in_dim(q, start, Q_BLOCK, 1), k, v, start + jnp.arange(Q_BLOCK),
                         kpos, lam, lam_init, slopes, sub_g)

    out = lax.map(one_block, jnp.arange(T // Q_BLOCK) * Q_BLOCK)
    return jnp.moveaxis(out, 0, 1).reshape(B, T, DIFF_HEADS, HEAD_DIM)


def conv_ffn(h, conv_buf, w_gate, w_up, conv_w, conv_b, w_down):
    T = h.shape[1]
    g = h @ w_gate
    u = h @ w_up
    g_ext = jnp.concatenate([conv_buf.astype(g.dtype), g], axis=1)
    gc = conv_b
    for j in range(CONV_WIDTH):
        gc = gc + conv_w[j] * g_ext[:, j:j + T]
    y = (jax.nn.silu(gc) * u) @ w_down
    return y, g_ext[:, -(CONV_WIDTH - 1):]


def merge(o_nsa, o_diff, w_out, dt):
    B, T = o_nsa.shape[:2]
    o = jnp.concatenate([o_nsa.reshape(B, T, NSA_Q_W).astype(dt), o_diff.reshape(B, T, DIFF_V_W).astype(dt)], axis=-1)
    return o @ w_out


def setup_inputs(seed: int = 0) -> dict:
    key = jax.random.key(seed)
    k = jax.random.split(key, 24)
    n_pages = PAST_LEN // PAGE_SIZE
    n_used = DEC_BATCH * n_pages
    n_pool = n_used + n_used // 4
    w_buf = min(WINDOW, PAST_LEN)
    nrm = lambda kk, shape, s=1.0: s * jax.random.normal(kk, shape, jnp.float32)
    gain = lambda kk, shape: 1.0 + 0.05 * jax.random.normal(kk, shape, jnp.float32)
    perm = jax.random.permutation(k[0], n_pool)
    page_table = perm[:n_used].reshape(DEC_BATCH, n_pages).astype(jnp.int32)
    return {
        'x_prompt': nrm(k[1], (BATCH, SEQ, D_MODEL)),
        'x_sample': nrm(k[2], (DEC_BATCH, DEC_SEQ, D_MODEL)),
        'cache_nsa': nrm(k[3], (DEPTH, n_pool, PAGE_SIZE, 4, NSA_KV, HEAD_DIM)),
        'cache_diff': nrm(k[4], (DEPTH, n_pool, PAGE_SIZE, 2, DIFF_HEADS, HEAD_DIM)),
        'state_win': nrm(k[5], (DEPTH, DEC_BATCH, w_buf, 2, NSA_KV, HEAD_DIM)),
        'state_conv': nrm(k[6], (DEPTH, DEC_BATCH, CONV_WIDTH - 1, D_FF)),
        'page_table': page_table,
        'norm1_g': gain(k[7], (DEPTH, D_MODEL)),
        'norm2_g': gain(k[8], (DEPTH, D_MODEL)),
        'w_in': nrm(k[9], (DEPTH, D_MODEL, N_IN), D_MODEL ** -0.5),
        'w_out': nrm(k[10], (DEPTH, MIX_W, D_MODEL), MIX_W ** -0.5),
        'nsa_qnorm_g': gain(k[11], (DEPTH, HEAD_DIM)),
        'nsa_knorm_g': gain(k[12], (DEPTH, 3, HEAD_DIM)),
        'nsa_cmp_w': (1.0 + 0.1 * jax.random.normal(k[13], (DEPTH, 2, L_CMP), jnp.float32)) / L_CMP,
        'diff_qnorm_g': gain(k[14], (DEPTH, 2, DIFF_HALF)),
        'diff_knorm_g': gain(k[15], (DEPTH, 2, DIFF_HALF)),
        'diff_lambda': nrm(k[16], (DEPTH, 4, DIFF_HALF), 0.1),
        'diff_subnorm_g': gain(k[17], (DEPTH, HEAD_DIM)),
        'w_gate': nrm(k[18], (DEPTH, D_MODEL, D_FF), D_MODEL ** -0.5),
        'w_up': nrm(k[19], (DEPTH, D_MODEL, D_FF), D_MODEL ** -0.5),
        'conv_w': nrm(k[20], (DEPTH, CONV_WIDTH, D_FF), CONV_WIDTH ** -0.5),
        'conv_b': nrm(k[21], (DEPTH, D_FF), 0.01),
        'w_down': nrm(k[22], (DEPTH, D_FF, D_MODEL), D_FF ** -0.5),
    }


def reference(x_prompt, x_sample, cache_nsa, cache_diff, state_win, state_conv, page_table,
              norm1_g, norm2_g, w_in, w_out, nsa_qnorm_g, nsa_knorm_g, nsa_cmp_w,
              diff_qnorm_g, diff_knorm_g, diff_lambda, diff_subnorm_g,
              w_gate, w_up, conv_w, conv_b, w_down):
    nsa_slopes = alibi_slopes(NSA_HEADS).reshape(NSA_KV, NSA_GROUP)
    diff_slopes = alibi_slopes(DIFF_HEADS)
    w_buf = min(WINDOW, PAST_LEN)
    w_keep = min(WINDOW, SEQ)
    total = PAST_LEN + DEC_SEQ
    pad_s = (-total) % L_SEL
    qpos_s = PAST_LEN + jnp.arange(DEC_SEQ)
    kpos_s = jnp.arange(total)
    xp, xs = x_prompt, x_sample
    nsa_p, nsa_s, diff_p, diff_s, win_p, win_s, conv_p, conv_s = [], [], [], [], [], [], [], []
    for l in range(DEPTH):
        lam_init = 0.8 - 0.6 * math.exp(-0.3 * l)
        lam = diff_lambda_value(diff_lambda[l], lam_init)
        h = rmsnorm(xp, norm1_g[l])
        q, kcr, vcr, ks, vs, kw, vw, gate, dq, dk, dv = project(h, w_in[l], nsa_qnorm_g[l], nsa_knorm_g[l],
                                                               diff_qnorm_g[l], diff_knorm_g[l])
        o_nsa = nsa_prompt(q, kcr, vcr, ks, vs, kw, vw, gate, nsa_cmp_w[l], nsa_knorm_g[l, 0], nsa_slopes)
        o_diff = diff_prompt(dq, dk, dv, lam, lam_init, diff_slopes, diff_subnorm_g[l])
        xp = xp + merge(o_nsa, o_diff, w_out[l], xp.dtype)
        f, buf = conv_ffn(rmsnorm(xp, norm2_g[l]), jnp.zeros((BATCH, CONV_WIDTH - 1, D_FF), xp.dtype),
                          w_gate[l], w_up[l], conv_w[l], conv_b[l], w_down[l])
        xp = xp + f
        nsa_p.append(jnp.stack([kcr, vcr, ks, vs], axis=2))
        diff_p.append(jnp.stack([dk.reshape(BATCH, SEQ, DIFF_HEADS, HEAD_DIM), dv], axis=2))
        win_p.append(jnp.stack([kw, vw], axis=2)[:, SEQ - w_keep:])
        conv_p.append(buf)
        h = rmsnorm(xs, norm1_g[l])
        q, kcr, vcr, ks, vs, kw, vw, gate, dq, dk, dv = project(h, w_in[l], nsa_qnorm_g[l], nsa_knorm_g[l],
                                                               diff_qnorm_g[l], diff_knorm_g[l])
        new_nsa = jnp.stack([kcr, vcr, ks, vs], axis=2)
        past = cache_nsa[l][page_table].reshape(DEC_BATCH, PAST_LEN, 4, NSA_KV, HEAD_DIM)
        full = jnp.concatenate([past, new_nsa.astype(past.dtype)], axis=1)
        full = jnp.pad(full, ((0, 0), (0, pad_s), (0, 0), (0, 0), (0, 0)))
        kc, vc, cend = compress_kv(full[:, :, 0], full[:, :, 1], nsa_cmp_w[l], nsa_knorm_g[l, 0])
        new_win = jnp.stack([kw, vw], axis=2)
        win_all = jnp.concatenate([state_win[l], new_win.astype(state_win.dtype)], axis=1)
        kwpos = PAST_LEN - w_buf + jnp.arange(w_buf + DEC_SEQ)
        o_nsa = nsa_core(q, qpos_s, kc, vc, cend, to_blocks(full[:, :, 2]), to_blocks(full[:, :, 3]),
                         win_all[:, :, 0], win_all[:, :, 1], kwpos, gate, nsa_slopes)
        dpast = cache_diff[l][page_table].reshape(DEC_BATCH, PAST_LEN, 2, DIFF_HEADS, HEAD_DIM)
        k_all = jnp.concatenate([dpast[:, :, 0].reshape(DEC_BATCH, PAST_LEN, DIFF_HEADS, 2, DIFF_HALF),
                                 dk.astype(dpast.dtype)], axis=1)
        v_all = jnp.concatenate([dpast[:, :, 1], dv.astype(dpast.dtype)], axis=1)
        o_diff = diff_core(dq, k_all, v_all, qpos_s, kpos_s, lam, lam_init, diff_slopes, diff_subnorm_g[l])
        xs = xs + merge(o_nsa, o_diff, w_out[l], xs.dtype)
        f, buf = conv_ffn(rmsnorm(xs, norm2_g[l]), state_conv[l], w_gate[l], w_up[l], conv_w[l], conv_b[l], w_down[l])
        xs = xs + f
        nsa_s.append(new_nsa)
        diff_s.append(jnp.stack([dk.reshape(DEC_BATCH, DEC_SEQ, DIFF_HEADS, HEAD_DIM), dv], axis=2))
        win_s.append(win_all[:, -w_buf:])
        conv_s.append(buf)
    return (xp, xs,
            jnp.stack(nsa_p), jnp.stack(nsa_s),
            jnp.stack(diff_p), jnp.stack(diff_s),
            jnp.stack(win_p), jnp.stack(win_s),
            jnp.stack(conv_p), jnp.stack(conv_s))
```

```python
import functools
import math

import jax
import jax.numpy as jnp
import numpy as np
from jax import lax
from jax.experimental import pallas as pl
from jax.experimental.pallas import tpu as pltpu

F32 = jnp.float32
BF16 = jnp.bfloat16

HEAD_DIM = 64
NSA_KV = 2
NSA_GROUP = 4
NSA_HEADS = NSA_KV * NSA_GROUP
DIFF_HEADS = 8
DIFF_HALF = HEAD_DIM // 2
L_CMP = 32
L_SEL = 64
N_SELECT = 16
WINDOW = 512
FORCE_BONUS = 1.0e4
EPS = 1e-6
NEG = -1e30

LANES = 128
KEY_TILE = 512
Q_LANES = 128
WIN_ROWS = WINDOW + Q_LANES
AUG0 = HEAD_DIM
N_AUG = 6
VMEM_LIMIT = 56 * 1024 * 1024

C_Q, C_NSA, C_WIN, C_DQ, C_DK, C_DV, C_GATE = 0, 512, 1024, 1280, 1792, 2304, 2816
N_COL = 2944


def _nt(a, b):
    return lax.dot_general(a, b, (((1,), (1,)), ((), ())), preferred_element_type=F32)


def _tn(a, b):
    return lax.dot_general(a, b, (((0,), (0,)), ((), ())), preferred_element_type=F32)


def _mm(a, b):
    return jnp.dot(a, b, preferred_element_type=F32)


def _group_meansq(z, smat, inv_n):
    zz = z * z
    hi = zz.astype(BF16)
    lo = (zz - hi.astype(F32)).astype(BF16)
    return (_mm(hi, smat) + _mm(lo, smat)) * inv_n


def _pos_rows(pos, lane):
    is_aug = (lane >= AUG0) & (lane < AUG0 + N_AUG)
    val = jnp.where((lane & 1) == 0, pos >> 6, pos & 63)
    return jnp.where(is_aug, val, 0).astype(F32)


def _split_heads(slab, fill, lane):
    even = jnp.where(lane < HEAD_DIM, slab, fill)
    odd = jnp.where(lane < HEAD_DIM, pltpu.roll(slab, HEAD_DIM, 1), fill)
    return even, odd


def _proj_kernel(x_ref, g1_ref, w_ref, gain_ref, s64_ref, s32_ref, qcn_ref, qcd_ref, cw_ref,
                 nsa_ref, win_ref, dif_ref, qa_ref, ksa_ref, vs_ref, kwa_ref, vw_ref, gt_ref,
                 dqa_ref, dka_ref, dv_ref, cr_ref, *, seq_len, pos_base):
    tr = x_ref.shape[0]
    i = pl.program_id(0)
    x = x_ref[...]
    h = x * lax.rsqrt(jnp.mean(x * x, axis=-1, keepdims=True) + EPS) * g1_ref[...]
    hb = h.astype(BF16)

    lane = lax.broadcasted_iota(jnp.int32, (tr, LANES), 1)
    row = lax.broadcasted_iota(jnp.int32, (tr, LANES), 0)
    pos = pos_base + (i * tr + row) % seq_len
    prow = _pos_rows(pos, lane)
    s64 = s64_ref[...]
    s32 = s32_ref[...]

    def seg(c0, width):
        return _mm(hb, w_ref[:, c0:c0 + width])

    def normed(z, c0, smat, inv_n):
        return z * lax.rsqrt(_group_meansq(z, smat, inv_n) + EPS) * gain_ref[:, c0:c0 + LANES]

    zq = seg(C_Q, 512)
    for s in range(4):
        zn = normed(zq[:, s * LANES:(s + 1) * LANES], C_Q + s * LANES, s64, 1.0 / HEAD_DIM) * (HEAD_DIM ** -0.5)
        for par in range(2):
            hd = 2 * s + par
            src = zn if par == 0 else pltpu.roll(zn, HEAD_DIM, 1)
            qa = jnp.where(lane < HEAD_DIM, src, qcn_ref[hd:hd + 1, :])
            qa_ref[hd // NSA_GROUP, hd % NSA_GROUP] = qa.astype(BF16)

    zc = seg(C_NSA, 512)
    ks = normed(zc[:, 256:384], C_NSA + 256, s64, 1.0 / HEAD_DIM)
    nsa_ref[:, 0:256] = zc[:, 0:256]
    nsa_ref[:, 256:384] = ks
    nsa_ref[:, 384:512] = zc[:, 384:512]
    k0, k1 = _split_heads(ks, prow, lane)
    ksa_ref[0] = k0.astype(BF16)
    ksa_ref[1] = k1.astype(BF16)
    vs_ref[...] = zc[:, 384:512].astype(BF16)
    craw = zc[:, 0:256].reshape(tr // L_CMP, L_CMP, 256) * cw_ref[...][None]
    cr_ref[...] = jnp.sum(craw, axis=1)

    zw = seg(C_WIN, 256)
    kw = normed(zw[:, 0:128], C_WIN, s64, 1.0 / HEAD_DIM)
    win_ref[:, 0:128] = kw
    win_ref[:, 128:256] = zw[:, 128:256]
    k0, k1 = _split_heads(kw, prow, lane)
    kwa_ref[0] = k0.astype(BF16)
    kwa_ref[1] = k1.astype(BF16)
    vw_ref[...] = zw[:, 128:256].astype(BF16)

    zdq = seg(C_DQ, 512)
    for s in range(4):
        zn = normed(zdq[:, s * LANES:(s + 1) * LANES], C_DQ + s * LANES, s32, 1.0 / DIFF_HALF) * (DIFF_HALF ** -0.5)
        for par in range(2):
            hd = 2 * s + par
            src = zn if par == 0 else pltpu.roll(zn, HEAD_DIM, 1)
            fill = qcd_ref[hd:hd + 1, :]
            dqa_ref[hd, 0] = jnp.where(lane < DIFF_HALF, src, fill).astype(BF16)
            dqa_ref[hd, 1] = jnp.where((lane >= DIFF_HALF) & (lane < HEAD_DIM), src, fill).astype(BF16)

    zdk = seg(C_DK, 512)
    for s in range(4):
        kn = normed(zdk[:, s * LANES:(s + 1) * LANES], C_DK + s * LANES, s32, 1.0 / DIFF_HALF)
        dif_ref[:, s * LANES:(s + 1) * LANES] = kn
        k0, k1 = _split_heads(kn, prow, lane)
        dka_ref[2 * s] = k0.astype(BF16)
        dka_ref[2 * s + 1] = k1.astype(BF16)
    zdv = seg(C_DV, 512)
    dif_ref[:, 512:1024] = zdv
    dv_ref[...] = zdv.astype(BF16)

    zg = seg(C_GATE, LANES)
    sg = 1.0 / (1.0 + jnp.exp(-zg))
    gt_ref[...] = sg.T[0:2 * 16].reshape(NSA_KV, 16, tr)


def _proj(x2d, g1, w, gain, s64, s32, qcn, qcd, cw, *, seq_len, pos_base, tr=512):
    n = x2d.shape[0]
    tr = min(tr, n)
    assert n % tr == 0 and tr % L_CMP == 0
    kern = functools.partial(_proj_kernel, seq_len=seq_len, pos_base=pos_base)
    full = lambda shape: pl.BlockSpec(shape, lambda i: (0,) * len(shape))
    out_shape = (
        jax.ShapeDtypeStruct((n, 512), F32),
        jax.ShapeDtypeStruct((n, 256), F32),
        jax.ShapeDtypeStruct((n, 1024), F32),
        jax.ShapeDtypeStruct((NSA_KV, NSA_GROUP, n, LANES), BF16),
        jax.ShapeDtypeStruct((NSA_KV, n, LANES), BF16),
        jax.ShapeDtypeStruct((n, LANES), BF16),
        jax.ShapeDtypeStruct((NSA_KV, n, LANES), BF16),
        jax.ShapeDtypeStruct((n, LANES), BF16),
        jax.ShapeDtypeStruct((NSA_KV, 16, n), F32),
        jax.ShapeDtypeStruct((DIFF_HEADS, 2, n, LANES), BF16),
        jax.ShapeDtypeStruct((DIFF_HEADS, n, LANES), BF16),
        jax.ShapeDtypeStruct((n, 512), BF16),
        jax.ShapeDtypeStruct((n // L_CMP, 256), F32),
    )
    out_specs = (
        pl.BlockSpec((tr, 512), lambda i: (i, 0)),
        pl.BlockSpec((tr, 256), lambda i: (i, 0)),
        pl.BlockSpec((tr, 1024), lambda i: (i, 0)),
        pl.BlockSpec((NSA_KV, NSA_GROUP, tr, LANES), lambda i: (0, 0, i, 0)),
        pl.BlockSpec((NSA_KV, tr, LANES), lambda i: (0, i, 0)),
        pl.BlockSpec((tr, LANES), lambda i: (i, 0)),
        pl.BlockSpec((NSA_KV, tr, LANES), lambda i: (0, i, 0)),
        pl.BlockSpec((tr, LANES), lambda i: (i, 0)),
        pl.BlockSpec((NSA_KV, 16, tr), lambda i: (0, 0, i)),
        pl.BlockSpec((DIFF_HEADS, 2, tr, LANES), lambda i: (0, 0, i, 0)),
        pl.BlockSpec((DIFF_HEADS, tr, LANES), lambda i: (0, i, 0)),
        pl.BlockSpec((tr, 512), lambda i: (i, 0)),
        pl.BlockSpec((tr // L_CMP, 256), lambda i: (i, 0)),
    )
    in_specs = [
        pl.BlockSpec((tr, x2d.shape[1]), lambda i: (i, 0)),
        full(g1.shape), full(w.shape), full(gain.shape), full(s64.shape), full(s32.shape),
        full(qcn.shape), full(qcd.shape), full(cw.shape),
    ]
    return pl.pallas_call(
        kern, grid=(n // tr,), in_specs=in_specs, out_specs=out_specs, out_shape=out_shape,
        compiler_params=pltpu.CompilerParams(dimension_semantics=("arbitrary",), vmem_limit_bytes=VMEM_LIMIT),
        name="proj",
    )(x2d, g1, w, gain, s64, s32, qcn, qcd, cw)


def _cmpfin_kernel(raw_ref, s64_ref, gain_ref, kc_ref, vc_ref, k_scr, v_scr, *, nsp):
    lane = lax.broadcasted_iota(jnp.int32, (nsp, LANES), 1)
    n = lax.broadcasted_iota(jnp.int32, (nsp, LANES), 0)
    k_scr[...] = raw_ref[0, :, 0:128]
    v_scr[...] = raw_ref[0, :, 128:256]
    for j in range(2):
        k = k_scr[pl.ds(j, nsp, stride=2), :]
        kn = k * lax.rsqrt(_group_meansq(k, s64_ref[...], 1.0 / HEAD_DIM) + EPS) * gain_ref[...]
        cend = n * L_SEL + (L_CMP - 1 + L_CMP * j)
        prow = _pos_rows(cend, lane)
        k0, k1 = _split_heads(kn, prow, lane)
        kc_ref[0, 0, j * nsp:(j + 1) * nsp, :] = k0.astype(BF16)
        kc_ref[1, 0, j * nsp:(j + 1) * nsp, :] = k1.astype(BF16)
        vc_ref[0, j * nsp:(j + 1) * nsp, :] = v_scr[pl.ds(j, nsp, stride=2), :].astype(BF16)


def _cmpfin(raw, s64, gain_kc):
    bx, n2, _ = raw.shape
    nsp = n2 // 2
    assert nsp % 16 == 0
    return pl.pallas_call(
        functools.partial(_cmpfin_kernel, nsp=nsp),
        grid=(bx,),
        in_specs=[pl.BlockSpec((1, n2, 256), lambda b: (b, 0, 0)),
                  pl.BlockSpec(s64.shape, lambda b: (0, 0)),
                  pl.BlockSpec(gain_kc.shape, lambda b: (0, 0))],
        out_specs=(pl.BlockSpec((NSA_KV, 1, n2, LANES), lambda b: (0, b, 0, 0)),
                   pl.BlockSpec((1, n2, LANES), lambda b: (b, 0, 0))),
        out_shape=(jax.ShapeDtypeStruct((NSA_KV, bx, n2, LANES), BF16),
                   jax.ShapeDtypeStruct((bx, n2, LANES), BF16)),
        scratch_shapes=[pltpu.VMEM((n2, LANES), F32), pltpu.VMEM((n2, LANES), F32)],
        name="cmpfin",
    )(raw, s64, gain_kc)


def _flash_init(m_ref, l_ref, acc_ref):
    m_ref[...] = jnp.full(m_ref.shape, NEG, F32)
    l_ref[...] = jnp.zeros(l_ref.shape, F32)
    acc_ref[...] = jnp.zeros(acc_ref.shape, F32)


def _flash_update(s, v_tile, m_ref, l_ref, acc_ref):
    m_old = m_ref[...]
    m_new = jnp.maximum(m_old, jnp.max(s, axis=0, keepdims=True))
    p = jnp.exp(s - m_new)
    a = jnp.exp(m_old - m_new)
    l_ref[...] = a * l_ref[...] + jnp.sum(p, axis=0, keepdims=True)
    m_ref[...] = m_new
    acc_ref[...] = a * acc_ref[...] + _tn(v_tile, p.astype(BF16))


def _masked_softmax_cols(s, valid):
    sm = jnp.where(valid, s, NEG)
    m = jnp.max(sm, axis=0, keepdims=True)
    p = jnp.where(valid, jnp.exp(sm - m), 0.0)
    l = jnp.sum(p, axis=0, keepdims=True)
    return p * jnp.where(l > 0.0, 1.0 / l, 0.0)


def _nsa_kernel(q_ref, kc_ref, vc_ref, ks_ref, vs_ref, kw_ref, vw_ref, gt_ref, o_ref,
                sel_ref, acc_ref, m_ref, l_ref, *, q_base, nsp, tw, win_base):
    g = pl.program_id(1)
    i = pl.program_id(2)
    q0 = q_base + i * Q_LANES
    ncol = NSA_GROUP * Q_LANES
    qa = q_ref[0].reshape(ncol, LANES)
    lane_c = lax.broadcasted_iota(jnp.int32, (1, ncol), 1)
    qpos_c = q0 + (lane_c & (Q_LANES - 1))

    def pick_group(o):
        return jnp.where(g == 0, o[0:HEAD_DIM], o[HEAD_DIM:2 * HEAD_DIM])

    s_c = _nt(kc_ref[0, 0], qa)
    n_row = lax.broadcasted_iota(jnp.int32, (nsp, ncol), 0)
    valid_e = (n_row * L_SEL + (L_CMP - 1)) <= qpos_c
    valid_o = (n_row * L_SEL + (2 * L_CMP - 1)) <= qpos_c
    sm_e = jnp.where(valid_e, s_c[0:nsp], NEG)
    sm_o = jnp.where(valid_o, s_c[nsp:2 * nsp], NEG)
    m = jnp.maximum(jnp.max(sm_e, axis=0, keepdims=True), jnp.max(sm_o, axis=0, keepdims=True))
    p_e = jnp.where(valid_e, jnp.exp(sm_e - m), 0.0)
    p_o = jnp.where(valid_o, jnp.exp(sm_o - m), 0.0)
    l = jnp.sum(p_e, axis=0, keepdims=True) + jnp.sum(p_o, axis=0, keepdims=True)
    inv = jnp.where(l > 0.0, 1.0 / l, 0.0)
    p_e = p_e * inv
    p_o = p_o * inv
    pcat = jnp.concatenate([p_e, p_o], axis=0).astype(BF16)
    o_c = pick_group(_tn(vc_ref[0], pcat))

    imp = p_e[:, 0:Q_LANES] + p_o[:, 0:Q_LANES]
    for r in range(1, NSA_GROUP):
        imp = imp + p_e[:, r * Q_LANES:(r + 1) * Q_LANES] + p_o[:, r * Q_LANES:(r + 1) * Q_LANES]
    blk = lax.broadcasted_iota(jnp.int32, (nsp, Q_LANES), 0)
    qpos_q = q0 + lax.broadcasted_iota(jnp.int32, (nsp, Q_LANES), 1)
    cur = qpos_q >> 6
    forced = (blk == 0) | (blk == cur) | (blk == cur - 1)
    score = jnp.where(blk <= cur, imp + jnp.where(forced, FORCE_BONUS, 0.0), -jnp.inf)
    blk_f = blk.astype(F32)
    sel = jnp.zeros((nsp, Q_LANES), F32)
    for _ in range(N_SELECT):
        top = jnp.max(score, axis=0, keepdims=True)
        first = jnp.min(jnp.where(score == top, blk_f, 1e9), axis=0, keepdims=True)
        pick = blk_f == first
        sel = jnp.where(pick, jnp.where(top > -jnp.inf, 1.0, sel), sel)
        score = jnp.where(pick, -jnp.inf, score)
    sel_ref[...] = sel

    _flash_init(m_ref, l_ref, acc_ref)
    tok_row = lax.broadcasted_iota(jnp.int32, (KEY_TILE, Q_LANES), 0)
    qpos_t = q0 + lax.broadcasted_iota(jnp.int32, (KEY_TILE, Q_LANES), 1)
    blocks_per_tile = KEY_TILE // L_SEL

    def sel_step(kt, carry):
        k0 = pl.multiple_of(kt * KEY_TILE, KEY_TILE)
        s = _nt(ks_ref[0, 0, pl.ds(k0, KEY_TILE), :], qa)
        sel8 = sel_ref[pl.ds(pl.multiple_of(kt * blocks_per_tile, blocks_per_tile), blocks_per_tile), :]
        selt = jnp.concatenate(
            [jnp.broadcast_to(sel8[j:j + 1, :], (L_SEL, Q_LANES)) for j in range(blocks_per_tile)], axis=0)
        valid = jnp.where(k0 + tok_row <= qpos_t, selt, 0.0) > 0.0
        s = jnp.concatenate(
            [jnp.where(valid, s[:, r * Q_LANES:(r + 1) * Q_LANES], NEG) for r in range(NSA_GROUP)], axis=1)
        _flash_update(s, vs_ref[0, pl.ds(k0, KEY_TILE), :], m_ref, l_ref, acc_ref)
        return carry

    n_tiles = (q0 + Q_LANES + KEY_TILE - 1) // KEY_TILE
    lax.fori_loop(0, n_tiles, sel_step, 0)
    l_s = l_ref[...]
    o_s = pick_group(acc_ref[...]) * jnp.where(l_s > 0.0, 1.0 / l_s, 0.0)

    w0 = jnp.clip(q0 - win_base - WINDOW, 0, tw - WIN_ROWS)
    w0 = pl.multiple_of(w0, Q_LANES)
    s_w = _nt(kw_ref[0, 0, pl.ds(w0, WIN_ROWS), :], qa)
    kpos = win_base + w0 + lax.broadcasted_iota(jnp.int32, (WIN_ROWS, ncol), 0)
    dist = qpos_c - kpos
    p_w = _masked_softmax_cols(s_w, (dist >= 0) & (dist < WINDOW))
    o_w = pick_group(_tn(vw_ref[0, pl.ds(w0, WIN_ROWS), :], p_w.astype(BF16)))

    gt = gt_ref[0]
    outs = []
    for r in range(NSA_GROUP):
        cs = slice(r * Q_LANES, (r + 1) * Q_LANES)
        outs.append(gt[r:r + 1, :] * o_c[:, cs] + gt[4 + r:5 + r, :] * o_s[:, cs] + gt[8 + r:9 + r, :] * o_w[:, cs])
    o_ref[...] = jnp.concatenate(outs, axis=0).T


def _nsa(qa, kc, vc, ksa, vs, kwa, vw, gt, *, bx, nqb, q_base, win_base):
    nq = qa.shape[2]
    n2 = kc.shape[2]
    tk = ksa.shape[2]
    tw = kwa.shape[2]
    assert nq == bx * nqb * Q_LANES and tk % KEY_TILE == 0 and tw >= WIN_ROWS
    kern = functools.partial(_nsa_kernel, q_base=q_base, nsp=n2 // 2, tw=tw, win_base=win_base)
    ncol = NSA_GROUP * Q_LANES
    return pl.pallas_call(
        kern, grid=(bx, NSA_KV, nqb),
        in_specs=[
            pl.BlockSpec((1, NSA_GROUP, Q_LANES, LANES), lambda b, g, i: (g, 0, b * nqb + i, 0)),
            pl.BlockSpec((1, 1, n2, LANES), lambda b, g, i: (g, b, 0, 0)),
            pl.BlockSpec((1, n2, LANES), lambda b, g, i: (b, 0, 0)),
            pl.BlockSpec((1, 1, tk, LANES), lambda b, g, i: (g, b, 0, 0)),
            pl.BlockSpec((1, tk, LANES), lambda b, g, i: (b, 0, 0)),
            pl.BlockSpec((1, 1, tw, LANES), lambda b, g, i: (g, b, 0, 0)),
            pl.BlockSpec((1, tw, LANES), lambda b, g, i: (b, 0, 0)),
            pl.BlockSpec((1, 16, Q_LANES), lambda b, g, i: (g, 0, b * nqb + i)),
        ],
        out_specs=pl.BlockSpec((Q_LANES, NSA_GROUP * HEAD_DIM), lambda b, g, i: (b * nqb + i, g)),
        out_shape=jax.ShapeDtypeStruct((nq, NSA_HEADS * HEAD_DIM), F32),
        scratch_shapes=[pltpu.VMEM((n2 // 2, Q_LANES), F32), pltpu.VMEM((2 * HEAD_DIM, ncol), F32),
                        pltpu.VMEM((1, ncol), F32), pltpu.VMEM((1, ncol), F32)],
        compiler_params=pltpu.CompilerParams(dimension_semantics=("arbitrary",) * 3, vmem_limit_bytes=VMEM_LIMIT),
        name="nsa",
    )(qa, kc, vc, ksa, vs, kwa, vw, gt)


def _diff_kernel(q_ref, k_ref, v_ref, lp_ref, sg_ref, o_ref, acc_ref, m_ref, l_ref, *, q_base, tq, lam_init):
    i = pl.program_id(2)
    q0 = q_base + i * tq
    ncol = 2 * tq
    lp = lp_ref[...]
    lam = (jnp.exp(jnp.sum(lp[0:1] * lp[1:2], keepdims=True)) - jnp.exp(jnp.sum(lp[2:3] * lp[3:4], keepdims=True))
           + lam_init)
    n_full = q0 // KEY_TILE
    tok_row = lax.broadcasted_iota(jnp.int32, (KEY_TILE, ncol), 0)
    qpos = q0 + lax.broadcasted_iota(jnp.int32, (KEY_TILE, ncol), 1) % tq
    outs = []
    for hh in range(2):
        qa = q_ref[hh].reshape(ncol, LANES)
        _flash_init(m_ref, l_ref, acc_ref)

        def full_step(kt, carry):
            k0 = pl.multiple_of(kt * KEY_TILE, KEY_TILE)
            s = _nt(k_ref[hh, 0, pl.ds(k0, KEY_TILE), :], qa)
            _flash_update(s, v_ref[0, pl.ds(k0, KEY_TILE), :], m_ref, l_ref, acc_ref)
            return carry

        lax.fori_loop(0, n_full, full_step, 0)
        k0 = pl.multiple_of(n_full * KEY_TILE, KEY_TILE)
        s = _nt(k_ref[hh, 0, pl.ds(k0, KEY_TILE), :], qa)
        s = jnp.where(k0 + tok_row <= qpos, s, NEG)
        _flash_update(s, v_ref[0, pl.ds(k0, KEY_TILE), :], m_ref, l_ref, acc_ref)

        o = acc_ref[hh * HEAD_DIM:(hh + 1) * HEAD_DIM, :] / l_ref[...]
        d = o[:, 0:tq] - lam * o[:, tq:ncol]
        d = d * lax.rsqrt(jnp.mean(d * d, axis=0, keepdims=True) + EPS) * sg_ref[:, 0:tq] * (1.0 - lam_init)
        outs.append(d)
    o_ref[...] = jnp.concatenate(outs, axis=0).T


def _diff(dqa, dka, dv, lp, sg, *, bx, nqb, tq, q_base, lam_init):
    nq = dqa.shape[2]
    tk = dka.shape[2]
    assert nq == bx * nqb * tq and tk % KEY_TILE == 0 and KEY_TILE % tq == 0
    kern = functools.partial(_diff_kernel, q_base=q_base, tq=tq, lam_init=lam_init)
    return pl.pallas_call(
        kern, grid=(bx, DIFF_HEADS // 2, nqb),
        in_specs=[
            pl.BlockSpec((2, 2, tq, LANES), lambda b, h, i: (h, 0, b * nqb + i, 0)),
            pl.BlockSpec((2, 1, tk, LANES), lambda b, h, i: (h, b, 0, 0)),
            pl.BlockSpec((1, tk, LANES), lambda b, h, i: (b, 0, h)),
            pl.BlockSpec(lp.shape, lambda b, h, i: (0, 0)),
            pl.BlockSpec(sg.shape, lambda b, h, i: (0, 0)),
        ],
        out_specs=pl.BlockSpec((tq, LANES), lambda b, h, i: (b * nqb + i, h)),
        out_shape=jax.ShapeDtypeStruct((nq, DIFF_HEADS * HEAD_DIM), F32),
        scratch_shapes=[pltpu.VMEM((2 * HEAD_DIM, 2 * tq), F32), pltpu.VMEM((1, 2 * tq), F32),
                        pltpu.VMEM((1, 2 * tq), F32)],
        compiler_params=pltpu.CompilerParams(dimension_semantics=("arbitrary",) * 3, vmem_limit_bytes=VMEM_LIMIT),
        name="diff",
    )(dqa, dka, dv, lp, sg)


F_CHUNKS = 2


def _ffn_core(x, on, od, wo_ref, g2_ref, wg_ref, wu_ref, cw_ref, cb_ref, wd_ref, prev_rows, g_store):
    o = jnp.concatenate([on, od], axis=1).astype(BF16)
    xm = x + _mm(o, wo_ref[...])
    h2 = (xm * lax.rsqrt(jnp.mean(xm * xm, axis=-1, keepdims=True) + EPS) * g2_ref[...]).astype(BF16)
    d_ff = wg_ref.shape[1]
    fc = d_ff // F_CHUNKS
    y = jnp.zeros(x.shape, F32)
    for c in range(F_CHUNKS):
        c0, c1 = c * fc, (c + 1) * fc
        g = _mm(h2, wg_ref[:, c0:c1])
        u = _mm(h2, wu_ref[:, c0:c1])
        gm1, gm2 = prev_rows(g, c0, c1)
        g_store(g, c0, c1)
        gc = cb_ref[:, c0:c1] + cw_ref[0:1, c0:c1] * gm2 + cw_ref[1:2, c0:c1] * gm1 + cw_ref[2:3, c0:c1] * g
        act = gc * (1.0 / (1.0 + jnp.exp(-gc))) * u
        y = y + _mm(act.astype(BF16), wd_ref[c0:c1, :])
    return xm + y


def _ffn_prompt_kernel(x_ref, on_ref, od_ref, wo_ref, g2_ref, wg_ref, wu_ref, cw_ref, cb_ref, wd_ref,
                       y_ref, cv_ref, carry_ref, *, tiles_per_seq):
    i = pl.program_id(0)
    tr = x_ref.shape[0]

    @pl.when(i % tiles_per_seq == 0)
    def _():
        carry_ref[...] = jnp.zeros(carry_ref.shape, F32)

    def prev_rows(g, c0, c1):
        row = lax.broadcasted_iota(jnp.int32, g.shape, 0)
        p1 = carry_ref[7:8, c0:c1]
        p2 = carry_ref[6:7, c0:c1]
        gm1 = jnp.where(row == 0, p1, pltpu.roll(g, 1, 0))
        gm2 = jnp.where(row == 0, p2, jnp.where(row == 1, p1, pltpu.roll(g, 2, 0)))
        return gm1, gm2

    def g_store(g, c0, c1):
        carry_ref[:, c0:c1] = g[tr - 8:tr]
        cv_ref[0, :, c0:c1] = g[tr - 8:tr]

    y_ref[...] = _ffn_core(x_ref[...], on_ref[...], od_ref[...], wo_ref, g2_ref, wg_ref, wu_ref, cw_ref, cb_ref,
                           wd_ref, prev_rows, g_store)


def _ffn_sample_kernel(x_ref, on_ref, od_ref, st1_ref, st2_ref, wo_ref, g2_ref, wg_ref, wu_ref, cw_ref, cb_ref,
                       wd_ref, y_ref, g_ref, *, seq):
    def prev_rows(g, c0, c1):
        rs = lax.broadcasted_iota(jnp.int32, g.shape, 0) % seq
        s1 = st1_ref[:, c0:c1]
        gm1 = jnp.where(rs == 0, s1, pltpu.roll(g, 1, 0))
        gm2 = jnp.where(rs == 0, st2_ref[:, c0:c1], jnp.where(rs == 1, s1, pltpu.roll(g, 2, 0)))
        return gm1, gm2

    def g_store(g, c0, c1):
        g_ref[:, c0:c1] = g

    y_ref[...] = _ffn_core(x_ref[...], on_ref[...], od_ref[...], wo_ref, g2_ref, wg_ref, wu_ref, cw_ref, cb_ref,
                           wd_ref, prev_rows, g_store)


def _const_spec(a):
    return pl.BlockSpec(a.shape, lambda i: (0,) * a.ndim, pipeline_mode=pl.Buffered(1))


def _ffn_prompt(x2d, on, od, wo, g2, wg, wu, cw, cb, wd, *, seq_len, tr=256):
    n, d = x2d.shape
    f = wg.shape[1]
    assert n % tr == 0 and seq_len % tr == 0
    tps = seq_len // tr
    row = lambda w: pl.BlockSpec((tr, w), lambda i: (i, 0))
    return pl.pallas_call(
        functools.partial(_ffn_prompt_kernel, tiles_per_seq=tps), grid=(n // tr,),
        in_specs=[row(d), row(on.shape[1]), row(od.shape[1])] + [_const_spec(a) for a in (wo, g2, wg, wu, cw, cb, wd)],
        out_specs=(row(d), pl.BlockSpec((1, 8, f), lambda i: (i // tps, 0, 0))),
        out_shape=(jax.ShapeDtypeStruct((n, d), F32), jax.ShapeDtypeStruct((n // seq_len, 8, f), F32)),
        scratch_shapes=[pltpu.VMEM((8, f), F32)],
        compiler_params=pltpu.CompilerParams(dimension_semantics=("arbitrary",), vmem_limit_bytes=VMEM_LIMIT),
        name="ffn_prompt",
    )(x2d, on, od, wo, g2, wg, wu, cw, cb, wd)


def _ffn_sample(x2d, on, od, st1, st2, wo, g2, wg, wu, cw, cb, wd, *, seq):
    n, d = x2d.shape
    f = wg.shape[1]
    args = (x2d, on, od, st1, st2, wo, g2, wg, wu, cw, cb, wd)
    return pl.pallas_call(
        functools.partial(_ffn_sample_kernel, seq=seq), grid=(1,),
        in_specs=[_const_spec(a) for a in args],
        out_specs=(pl.BlockSpec((n, d), lambda i: (0, 0)), pl.BlockSpec((n, f), lambda i: (0, 0))),
        out_shape=(jax.ShapeDtypeStruct((n, d), F32), jax.ShapeDtypeStruct((n, f), F32)),
        compiler_params=pltpu.CompilerParams(dimension_semantics=("arbitrary",), vmem_limit_bytes=VMEM_LIMIT),
        name="ffn_sample",
    )(*args)


def _pages_kernel(pt_ref, nsa_ref, dif_ref, tks_ref, tvs_ref, tdk_ref, tdv_ref, tcr_ref, cw_ref,
                  ks_ref, vs_ref, dk_ref, dv_ref, cr_ref, *, n_pages, page):
    p = pl.program_id(1)

    @pl.when(p < n_pages)
    def _():
        lane = lax.broadcasted_iota(jnp.int32, (page, LANES), 1)
        pos = p * page + lax.broadcasted_iota(jnp.int32, (page, LANES), 0)
        prow = _pos_rows(pos, lane)
        pg = nsa_ref[0, 0].T
        craw = pg[:, 0:256].reshape(page // L_CMP, L_CMP, 256) * cw_ref[...][None]
        cr_ref[0, 0] = jnp.sum(craw, axis=1)
        k0, k1 = _split_heads(pg[:, 256:384], prow, lane)
        ks_ref[0, 0] = k0.astype(BF16)
        ks_ref[1, 0] = k1.astype(BF16)
        vs_ref[0] = pg[:, 384:512].astype(BF16)
        dg = dif_ref[0, 0].T
        for s in range(4):
            k0, k1 = _split_heads(dg[:, s * LANES:(s + 1) * LANES], prow, lane)
            dk_ref[2 * s, 0] = k0.astype(BF16)
            dk_ref[2 * s + 1, 0] = k1.astype(BF16)
        dv_ref[0] = dg[:, 512:1024].astype(BF16)

    @pl.when(p == n_pages)
    def _():
        cr_ref[0, 0] = tcr_ref[0]
        ks_ref[:, 0] = tks_ref[...]
        vs_ref[0] = tvs_ref[...]
        dk_ref[:, 0] = tdk_ref[...]
        dv_ref[0] = tdv_ref[...]

    @pl.when(p > n_pages)
    def _():
        cr_ref[...] = jnp.zeros(cr_ref.shape, F32)
        ks_ref[...] = jnp.zeros(ks_ref.shape, BF16)
        vs_ref[...] = jnp.zeros(vs_ref.shape, BF16)
        dk_ref[...] = jnp.zeros(dk_ref.shape, BF16)
        dv_ref[...] = jnp.zeros(dv_ref.shape, BF16)


def _pages(page_table, cache_nsa, cache_diff, tks, tvs, tdk, tdv, tcr, cw, *, layer, n_steps):
    bs, n_pages = page_table.shape
    page = cache_nsa.shape[3]
    assert page == LANES and tks.shape[1] == bs * page
    tk = n_steps * page
    per = page // L_CMP
    pt = page_table.reshape(-1)

    def pidx(b, p, pt_ref):
        return pt_ref[b * n_pages + jnp.minimum(p, n_pages - 1)]

    grid_spec = pltpu.PrefetchScalarGridSpec(
        num_scalar_prefetch=1, grid=(bs, n_steps),
        in_specs=[
            pl.BlockSpec((1, 1, 512, page), lambda b, p, pt_ref: (layer, pidx(b, p, pt_ref), 0, 0)),
            pl.BlockSpec((1, 1, 1024, page), lambda b, p, pt_ref: (layer, pidx(b, p, pt_ref), 0, 0)),
            pl.BlockSpec((NSA_KV, page, LANES), lambda b, p, pt_ref: (0, b, 0)),
            pl.BlockSpec((page, LANES), lambda b, p, pt_ref: (b, 0)),
            pl.BlockSpec((DIFF_HEADS, page, LANES), lambda b, p, pt_ref: (0, b, 0)),
            pl.BlockSpec((page, 512), lambda b, p, pt_ref: (b, 0)),
            pl.BlockSpec((1, per, 256), lambda b, p, pt_ref: (b, 0, 0)),
            pl.BlockSpec(cw.shape, lambda b, p, pt_ref: (0, 0)),
        ],
        out_specs=(
            pl.BlockSpec((NSA_KV, 1, page, LANES), lambda b, p, pt_ref: (0, b, p, 0)),
            pl.BlockSpec((1, page, LANES), lambda b, p, pt_ref: (b, p, 0)),
            pl.BlockSpec((DIFF_HEADS, 1, page, LANES), lambda b, p, pt_ref: (0, b, p, 0)),
            pl.BlockSpec((1, page, 512), lambda b, p, pt_ref: (b, p, 0)),
            pl.BlockSpec((1, 1, per, 256), lambda b, p, pt_ref: (b, p, 0, 0)),
        ),
    )
    return pl.pallas_call(
        functools.partial(_pages_kernel, n_pages=n_pages, page=page), grid_spec=grid_spec,
        out_shape=(
            jax.ShapeDtypeStruct((NSA_KV, bs, tk, LANES), BF16),
            jax.ShapeDtypeStruct((bs, tk, LANES), BF16),
            jax.ShapeDtypeStruct((DIFF_HEADS, bs, tk, LANES), BF16),
            jax.ShapeDtypeStruct((bs, tk, 512), BF16),
            jax.ShapeDtypeStruct((bs, n_steps, per, 256), F32),
        ),
        compiler_params=pltpu.CompilerParams(dimension_semantics=("arbitrary",) * 2, vmem_limit_bytes=VMEM_LIMIT),
        name="pages",
    )(pt, cache_nsa, cache_diff, tks, tvs, tdk, tdv, tcr.reshape(bs, per, 256), cw)


def _winprep_kernel(st_ref, nw_ref, kw_ref, vw_ref, ns_ref, *, win_base, dec):
    st = st_ref[0, 0].T
    nw = nw_ref[...]
    w_buf = st.shape[0]
    cat = jnp.concatenate([st, nw], axis=0)
    rows = cat.shape[0]
    lane = lax.broadcasted_iota(jnp.int32, (rows, LANES), 1)
    pos = win_base + lax.broadcasted_iota(jnp.int32, (rows, LANES), 0)
    k0, k1 = _split_heads(cat[:, 0:128], _pos_rows(pos, lane), lane)
    kw_ref[0, 0] = k0.astype(BF16)
    kw_ref[1, 0] = k1.astype(BF16)
    vw_ref[0] = cat[:, 128:256].astype(BF16)
    ns_ref[0] = cat[dec:dec + w_buf]


def _winprep(state_win4, new_win, *, layer, win_base, dec):
    _, bs, _, w_buf = state_win4.shape
    rows = w_buf + Q_LANES
    return pl.pallas_call(
        functools.partial(_winprep_kernel, win_base=win_base, dec=dec), grid=(bs,),
        in_specs=[pl.BlockSpec((1, 1, 256, w_buf), lambda b: (layer, b, 0, 0)),
                  pl.BlockSpec((Q_LANES, 256), lambda b: (b, 0))],
        out_specs=(pl.BlockSpec((NSA_KV, 1, rows, LANES), lambda b: (0, b, 0, 0)),
                   pl.BlockSpec((1, rows, LANES), lambda b: (b, 0, 0)),
                   pl.BlockSpec((1, w_buf, 256), lambda b: (b, 0, 0))),
        out_shape=(jax.ShapeDtypeStruct((NSA_KV, bs, rows, LANES), BF16),
                   jax.ShapeDtypeStruct((bs, rows, LANES), BF16),
                   jax.ShapeDtypeStruct((bs, w_buf, 256), F32)),
        name="winprep",
    )(state_win4, new_win)


def _alibi_slopes(n):
    return 2.0 ** (-8.0 * jnp.arange(1, n + 1, dtype=F32) / n)


def _slope_rows(slopes, zero_upper_half_from):
    s0 = slopes.astype(BF16).astype(F32)
    s1 = (slopes - s0).astype(BF16).astype(F32)
    s2 = (slopes - s0 - s1).astype(BF16).astype(F32)
    cols = jnp.stack([64.0 * s0, s0, 64.0 * s1, s1, 64.0 * s2, s2], axis=1)
    out = jnp.zeros((slopes.shape[0], LANES), F32)
    return out.at[:, AUG0:AUG0 + N_AUG].set(cols)


def _block_ones(group):
    idx = np.arange(LANES) // group
    return jnp.asarray((idx[:, None] == idx[None, :]).astype(np.float32), dtype=BF16)


_GATE_SRC = np.full((LANES,), -1, np.int64)
for _g in range(NSA_KV):
    for _r in range(NSA_GROUP):
        for _j in range(3):
            _GATE_SRC[_g * 16 + _j * 4 + _r] = 1280 + _g * 12 + _r * 3 + _j


def _permute_w_in(w):
    main = jnp.concatenate([w[:, 0:1280], w[:, 1304:2840]], axis=1)
    gate = jnp.where(jnp.asarray(_GATE_SRC >= 0)[None, :], w[:, np.maximum(_GATE_SRC, 0)], 0.0)
    return jnp.concatenate([main, gate], axis=1).astype(BF16)


def _gain_row(nsa_qg, nsa_kg, diff_qg, diff_kg):
    one = lambda n: jnp.ones((n,), F32)
    parts = [jnp.tile(nsa_qg, NSA_HEADS), one(256), jnp.tile(nsa_kg[1], NSA_KV), one(128),
             jnp.tile(nsa_kg[2], NSA_KV), one(128), jnp.tile(diff_qg.reshape(-1), DIFF_HEADS),
             jnp.tile(diff_kg.reshape(-1), DIFF_HEADS), one(512 + LANES)]
    return jnp.concatenate(parts)[None, :]


def kernel(x_prompt, x_sample, cache_nsa, cache_diff, state_win, state_conv, page_table, norm1_g, norm2_g, w_in, w_out, nsa_qnorm_g, nsa_knorm_g, nsa_cmp_w, diff_qnorm_g, diff_knorm_g, diff_lambda, diff_subnorm_g, w_gate, w_up, conv_w, conv_b, w_down):
    batch, seq, d_model = x_prompt.shape
    dec_batch, dec_seq, _ = x_sample.shape
    depth, n_pool, page = cache_nsa.shape[:3]
    n_pages = page_table.shape[1]
    past_len = n_pages * page
    w_buf = state_win.shape[2]
    d_ff = w_gate.shape[2]
    assert dec_seq <= L_CMP and past_len % KEY_TILE == 0 and seq % KEY_TILE == 0 and w_buf == WINDOW

    s64, s32 = _block_ones(HEAD_DIM), _block_ones(DIFF_HALF)
    qcn = _slope_rows(_alibi_slopes(NSA_HEADS), 0)
    qcd = _slope_rows(_alibi_slopes(DIFF_HEADS), 0)
    fmaj = lambda a: jnp.moveaxis(a, 2, -1).reshape(a.shape[0], a.shape[1], -1, a.shape[2])
    cache_nsa4 = fmaj(cache_nsa)
    cache_diff4 = fmaj(cache_diff)
    state_win4 = fmaj(state_win)

    s_pad = Q_LANES
    n_steps = -(-(past_len + s_pad) // KEY_TILE) * KEY_TILE // page
    n_steps = -(-n_steps * (page // L_CMP) // 32) * 32 // (page // L_CMP)
    xp = x_prompt.reshape(batch * seq, d_model)
    xs = x_sample.reshape(dec_batch * dec_seq, d_model)

    outs = [[] for _ in range(8)]
    for l in range(depth):
        lam_init = 0.8 - 0.6 * math.exp(-0.3 * l)
        w_in_p = _permute_w_in(w_in[l])
        gain = _gain_row(nsa_qnorm_g[l], nsa_knorm_g[l], diff_qnorm_g[l], diff_knorm_g[l])
        gain_kc = jnp.tile(nsa_knorm_g[l, 0], NSA_KV)[None, :]
        cw = jnp.concatenate([jnp.broadcast_to(nsa_cmp_w[l, 0][:, None], (L_CMP, LANES)),
                              jnp.broadcast_to(nsa_cmp_w[l, 1][:, None], (L_CMP, LANES))], axis=1)
        lp = jnp.zeros((8, LANES), F32).at[0:4, 0:DIFF_HALF].set(diff_lambda[l])
        sg = jnp.broadcast_to(diff_subnorm_g[l][:, None], (HEAD_DIM, 2 * LANES))
        g1 = norm1_g[l][None, :]
        g2 = norm2_g[l][None, :]
        wo, wg, wu, wd = (a.astype(BF16) for a in (w_out[l], w_gate[l], w_up[l], w_down[l]))
        cwf, cbf = conv_w[l], conv_b[l][None, :]
        proj = functools.partial(_proj, g1=g1, w=w_in_p, gain=gain, s64=s64, s32=s32, qcn=qcn, qcd=qcd, cw=cw)

        (nsa_rows, win_rows, dif_rows, qa, ksa, vs, kwa, vw, gt, dqa, dka, dv, craw) = proj(
            xp, seq_len=seq, pos_base=0)
        kc, vc = _cmpfin(craw.reshape(batch, seq // L_CMP, 256), s64, gain_kc)
        o_nsa = _nsa(qa, kc, vc, ksa.reshape(NSA_KV, batch, seq, LANES), vs.reshape(batch, seq, LANES),
                     kwa.reshape(NSA_KV, batch, seq, LANES), vw.reshape(batch, seq, LANES), gt,
                     bx=batch, nqb=seq // Q_LANES, q_base=0, win_base=0)
        o_dif = _diff(dqa, dka.reshape(DIFF_HEADS, batch, seq, LANES), dv.reshape(batch, seq, 512), lp, sg,
                      bx=batch, nqb=seq // 256, tq=256, q_base=0, lam_init=lam_init)
        xp, cv = _ffn_prompt(xp, o_nsa, o_dif, wo, g2, wg, wu, cwf, cbf, wd, seq_len=seq)
        outs[0].append(nsa_rows.reshape(batch, seq, 4, NSA_KV, HEAD_DIM))
        outs[2].append(dif_rows.reshape(batch, seq, 2, DIFF_HEADS, HEAD_DIM))
        outs[4].append(win_rows.reshape(batch, seq, 2, NSA_KV, HEAD_DIM)[:, seq - min(WINDOW, seq):])
        outs[6].append(cv[:, 8 - 2:8])

        xs_pad = jnp.pad(xs.reshape(dec_batch, dec_seq, d_model), ((0, 0), (0, s_pad - dec_seq), (0, 0)))
        (nsa_rows, win_rows, dif_rows, qa, ksa, vs, kwa, vw, gt, dqa, dka, dv, craw) = proj(
            xs_pad.reshape(dec_batch * s_pad, d_model), seq_len=s_pad, pos_base=past_len)
        ks_all, vs_all, dk_all, dv_all, cr_all = _pages(
            page_table, cache_nsa4, cache_diff4, ksa, vs, dka, dv, craw, cw, layer=l, n_steps=n_steps)
        kc, vc = _cmpfin(cr_all.reshape(dec_batch, n_steps * (page // L_CMP), 256), s64, gain_kc)
        kw_all, vw_all, new_state = _winprep(state_win4, win_rows, layer=l, win_base=past_len - w_buf, dec=dec_seq)
        o_nsa = _nsa(qa, kc, vc, ks_all, vs_all, kw_all, vw_all, gt,
                     bx=dec_batch, nqb=1, q_base=past_len, win_base=past_len - w_buf)
        o_dif = _diff(dqa, dk_all, dv_all, lp, sg, bx=dec_batch, nqb=1, tq=Q_LANES, q_base=past_len,
                      lam_init=lam_init)
        unpad = lambda a: a.reshape(dec_batch, s_pad, -1)[:, :dec_seq].reshape(dec_batch * dec_seq, -1)
        st1 = jnp.repeat(state_conv[l][:, 1], dec_seq, axis=0)
        st2 = jnp.repeat(state_conv[l][:, 0], dec_seq, axis=0)
        xs, g_s = _ffn_sample(xs, unpad(o_nsa), unpad(o_dif), st1, st2, wo, g2, wg, wu, cwf, cbf, wd, seq=dec_seq)
        outs[1].append(unpad(nsa_rows).reshape(dec_batch, dec_seq, 4, NSA_KV, HEAD_DIM))
        outs[3].append(unpad(dif_rows).reshape(dec_batch, dec_seq, 2, DIFF_HEADS, HEAD_DIM))
        outs[5].append(new_state.reshape(dec_batch, w_buf, 2, NSA_KV, HEAD_DIM))
        outs[7].append(g_s.reshape(dec_batch, dec_seq, d_ff)[:, dec_seq - 2:])

    stacked = [jnp.stack(o) for o in outs]
    return (xp.reshape(batch, seq, d_model), xs.reshape(dec_batch, dec_seq, d_model), *stacked)
```

```python
import functools
import math

import jax
import jax.numpy as jnp
import numpy as np
from jax import lax
from jax.experimental import pallas as pl
from jax.experimental.pallas import tpu as pltpu

F32 = jnp.float32
BF16 = jnp.bfloat16

HEAD_DIM = 64
NSA_KV = 2
NSA_GROUP = 4
NSA_HEADS = NSA_KV * NSA_GROUP
DIFF_HEADS = 8
DIFF_HALF = HEAD_DIM // 2
L_CMP = 32
L_SEL = 64
N_SELECT = 16
WINDOW = 512
FORCE_BONUS = 1.0e4
EPS = 1e-6
NEG = -1e30
LOG2E = math.log2(math.e)

LANES = 128
KEY_TILE = 512
Q_LANES = 128
WIN_ROWS = WINDOW + Q_LANES
AUG0 = HEAD_DIM
N_AUG = 6
VMEM_LIMIT = 56 * 1024 * 1024

C_Q, C_NSA, C_WIN, C_DQ, C_DK, C_DV, C_GATE = 0, 512, 1024, 1280, 1792, 2304, 2816
N_COL = 2944


def _nt(a, b):
    return lax.dot_general(a, b, (((1,), (1,)), ((), ())), preferred_element_type=F32)


def _tn(a, b):
    return lax.dot_general(a, b, (((0,), (0,)), ((), ())), preferred_element_type=F32)


def _mm(a, b):
    return jnp.dot(a, b, preferred_element_type=F32)


def _group_meansq(z, smat, inv_n):
    zz = z * z
    hi = zz.astype(BF16)
    lo = (zz - hi.astype(F32)).astype(BF16)
    return (_mm(hi, smat) + _mm(lo, smat)) * inv_n


def _pos_rows(pos, lane):
    is_aug = (lane >= AUG0) & (lane < AUG0 + N_AUG)
    val = jnp.where((lane & 1) == 0, pos >> 6, pos & 63)
    return jnp.where(is_aug, val, 0).astype(F32)


def _split_heads(slab, fill, lane):
    even = jnp.where(lane < HEAD_DIM, slab, fill)
    odd = jnp.where(lane < HEAD_DIM, pltpu.roll(slab, HEAD_DIM, 1), fill)
    return even, odd


def _proj_kernel(x_ref, g1_ref, w_ref, gain_ref, s64_ref, s32_ref, qcn_ref, qcd_ref, cw_ref,
                 nsa_ref, win_ref, dif_ref, qa_ref, ksa_ref, vs_ref, kwa_ref, vw_ref, gt_ref,
                 dqa_ref, dka_ref, dv_ref, cr_ref, *, seq_len, pos_base):
    tr = x_ref.shape[0]
    i = pl.program_id(0)
    x = x_ref[...]
    h = x * lax.rsqrt(jnp.mean(x * x, axis=-1, keepdims=True) + EPS) * g1_ref[...]
    hb = h.astype(BF16)

    lane = lax.broadcasted_iota(jnp.int32, (tr, LANES), 1)
    row = lax.broadcasted_iota(jnp.int32, (tr, LANES), 0)
    pos = pos_base + (i * tr + row) % seq_len
    prow = _pos_rows(pos, lane)
    s64 = s64_ref[...]
    s32 = s32_ref[...]

    def seg(c0, width):
        return _mm(hb, w_ref[:, c0:c0 + width])

    def normed(z, c0, smat, inv_n):
        return z * lax.rsqrt(_group_meansq(z, smat, inv_n) + EPS) * gain_ref[:, c0:c0 + LANES]

    zq = seg(C_Q, 512)
    for s in range(4):
        zn = normed(zq[:, s * LANES:(s + 1) * LANES], C_Q + s * LANES, s64, 1.0 / HEAD_DIM) * (HEAD_DIM ** -0.5 * LOG2E)
        for par in range(2):
            hd = 2 * s + par
            src = zn if par == 0 else pltpu.roll(zn, HEAD_DIM, 1)
            qa = jnp.where(lane < HEAD_DIM, src, qcn_ref[hd:hd + 1, :])
            qa_ref[hd // NSA_GROUP, hd % NSA_GROUP] = qa.astype(BF16)

    zc = seg(C_NSA, 512)
    ks = normed(zc[:, 256:384], C_NSA + 256, s64, 1.0 / HEAD_DIM)
    nsa_ref[:, 0:256] = zc[:, 0:256]
    nsa_ref[:, 256:384] = ks
    nsa_ref[:, 384:512] = zc[:, 384:512]
    k0, k1 = _split_heads(ks, prow, lane)
    ksa_ref[0] = k0.astype(BF16)
    ksa_ref[1] = k1.astype(BF16)
    vs_ref[...] = zc[:, 384:512].astype(BF16)
    craw = zc[:, 0:256].reshape(tr // L_CMP, L_CMP, 256) * cw_ref[...][None]
    cr_ref[...] = jnp.sum(craw, axis=1)

    zw = seg(C_WIN, 256)
    kw = normed(zw[:, 0:128], C_WIN, s64, 1.0 / HEAD_DIM)
    win_ref[:, 0:128] = kw
    win_ref[:, 128:256] = zw[:, 128:256]
    k0, k1 = _split_heads(kw, prow, lane)
    kwa_ref[0] = k0.astype(BF16)
    kwa_ref[1] = k1.astype(BF16)
    vw_ref[...] = zw[:, 128:256].astype(BF16)

    zdq = seg(C_DQ, 512)
    for s in range(4):
        zn = normed(zdq[:, s * LANES:(s + 1) * LANES], C_DQ + s * LANES, s32, 1.0 / DIFF_HALF) * (DIFF_HALF ** -0.5 * LOG2E)
        for par in range(2):
            hd = 2 * s + par
            src = zn if par == 0 else pltpu.roll(zn, HEAD_DIM, 1)
            fill = qcd_ref[hd:hd + 1, :]
            dqa_ref[hd, 0] = jnp.where(lane < DIFF_HALF, src, fill).astype(BF16)
            dqa_ref[hd, 1] = jnp.where((lane >= DIFF_HALF) & (lane < HEAD_DIM), src, fill).astype(BF16)

    zdk = seg(C_DK, 512)
    for s in range(4):
        kn = normed(zdk[:, s * LANES:(s + 1) * LANES], C_DK + s * LANES, s32, 1.0 / DIFF_HALF)
        dif_ref[:, s * LANES:(s + 1) * LANES] = kn
        k0, k1 = _split_heads(kn, prow, lane)
        dka_ref[2 * s] = k0.astype(BF16)
        dka_ref[2 * s + 1] = k1.astype(BF16)
    zdv = seg(C_DV, 512)
    dif_ref[:, 512:1024] = zdv
    dv_ref[...] = zdv.astype(BF16)

    zg = seg(C_GATE, LANES)
    sg = 1.0 / (1.0 + jnp.exp(-zg))
    gt_ref[...] = sg.T[0:2 * 16].reshape(NSA_KV, 16, tr)


def _proj(x2d, g1, w, gain, s64, s32, qcn, qcd, cw, *, seq_len, pos_base, tr=512):
    n = x2d.shape[0]
    tr = min(tr, n)
    assert n % tr == 0 and tr % L_CMP == 0
    kern = functools.partial(_proj_kernel, seq_len=seq_len, pos_base=pos_base)
    full = lambda shape: pl.BlockSpec(shape, lambda i: (0,) * len(shape))
    out_shape = (
        jax.ShapeDtypeStruct((n, 512), F32),
        jax.ShapeDtypeStruct((n, 256), F32),
        jax.ShapeDtypeStruct((n, 1024), F32),
        jax.ShapeDtypeStruct((NSA_KV, NSA_GROUP, n, LANES), BF16),
        jax.ShapeDtypeStruct((NSA_KV, n, LANES), BF16),
        jax.ShapeDtypeStruct((n, LANES), BF16),
        jax.ShapeDtypeStruct((NSA_KV, n, LANES), BF16),
        jax.ShapeDtypeStruct((n, LANES), BF16),
        jax.ShapeDtypeStruct((NSA_KV, 16, n), F32),
        jax.ShapeDtypeStruct((DIFF_HEADS, 2, n, LANES), BF16),
        jax.ShapeDtypeStruct((DIFF_HEADS, n, LANES), BF16),
        jax.ShapeDtypeStruct((n, 512), BF16),
        jax.ShapeDtypeStruct((n // L_CMP, 256), F32),
    )
    out_specs = (
        pl.BlockSpec((tr, 512), lambda i: (i, 0)),
        pl.BlockSpec((tr, 256), lambda i: (i, 0)),
        pl.BlockSpec((tr, 1024), lambda i: (i, 0)),
        pl.BlockSpec((NSA_KV, NSA_GROUP, tr, LANES), lambda i: (0, 0, i, 0)),
        pl.BlockSpec((NSA_KV, tr, LANES), lambda i: (0, i, 0)),
        pl.BlockSpec((tr, LANES), lambda i: (i, 0)),
        pl.BlockSpec((NSA_KV, tr, LANES), lambda i: (0, i, 0)),
        pl.BlockSpec((tr, LANES), lambda i: (i, 0)),
        pl.BlockSpec((NSA_KV, 16, tr), lambda i: (0, 0, i)),
        pl.BlockSpec((DIFF_HEADS, 2, tr, LANES), lambda i: (0, 0, i, 0)),
        pl.BlockSpec((DIFF_HEADS, tr, LANES), lambda i: (0, i, 0)),
        pl.BlockSpec((tr, 512), lambda i: (i, 0)),
        pl.BlockSpec((tr // L_CMP, 256), lambda i: (i, 0)),
    )
    in_specs = [
        pl.BlockSpec((tr, x2d.shape[1]), lambda i: (i, 0)),
        full(g1.shape), full(w.shape), full(gain.shape), full(s64.shape), full(s32.shape),
        full(qcn.shape), full(qcd.shape), full(cw.shape),
    ]
    return pl.pallas_call(
        kern, grid=(n // tr,), in_specs=in_specs, out_specs=out_specs, out_shape=out_shape,
        compiler_params=pltpu.CompilerParams(dimension_semantics=("arbitrary",), vmem_limit_bytes=VMEM_LIMIT),
        name="proj",
    )(x2d, g1, w, gain, s64, s32, qcn, qcd, cw)


def _cmpfin_kernel(raw_ref, s64_ref, gain_ref, kc_ref, vc_ref, k_scr, v_scr, *, nsp):
    lane = lax.broadcasted_iota(jnp.int32, (nsp, LANES), 1)
    n = lax.broadcasted_iota(jnp.int32, (nsp, LANES), 0)
    k_scr[...] = raw_ref[0, :, 0:128]
    v_scr[...] = raw_ref[0, :, 128:256]
    for j in range(2):
        k = k_scr[pl.ds(j, nsp, stride=2), :]
        kn = k * lax.rsqrt(_group_meansq(k, s64_ref[...], 1.0 / HEAD_DIM) + EPS) * gain_ref[...]
        cend = n * L_SEL + (L_CMP - 1 + L_CMP * j)
        prow = _pos_rows(cend, lane)
        k0, k1 = _split_heads(kn, prow, lane)
        kc_ref[0, 0, j * nsp:(j + 1) * nsp, :] = k0.astype(BF16)
        kc_ref[1, 0, j * nsp:(j + 1) * nsp, :] = k1.astype(BF16)
        vc_ref[0, j * nsp:(j + 1) * nsp, :] = v_scr[pl.ds(j, nsp, stride=2), :].astype(BF16)


def _cmpfin(raw, s64, gain_kc):
    bx, n2, _ = raw.shape
    nsp = n2 // 2
    assert nsp % 16 == 0
    return pl.pallas_call(
        functools.partial(_cmpfin_kernel, nsp=nsp),
        grid=(bx,),
        in_specs=[pl.BlockSpec((1, n2, 256), lambda b: (b, 0, 0)),
                  pl.BlockSpec(s64.shape, lambda b: (0, 0)),
                  pl.BlockSpec(gain_kc.shape, lambda b: (0, 0))],
        out_specs=(pl.BlockSpec((NSA_KV, 1, n2, LANES), lambda b: (0, b, 0, 0)),
                   pl.BlockSpec((1, n2, LANES), lambda b: (b, 0, 0))),
        out_shape=(jax.ShapeDtypeStruct((NSA_KV, bx, n2, LANES), BF16),
                   jax.ShapeDtypeStruct((bx, n2, LANES), BF16)),
        scratch_shapes=[pltpu.VMEM((n2, LANES), F32), pltpu.VMEM((n2, LANES), F32)],
        name="cmpfin",
    )(raw, s64, gain_kc)


def _flash_init(c, m_ref, l_ref, acc_ref):
    m_ref[c] = jnp.full(m_ref.shape[1:], NEG, F32)
    l_ref[c] = jnp.zeros(l_ref.shape[1:], F32)
    acc_ref[c] = jnp.zeros(acc_ref.shape[1:], F32)


def _flash_probs(c, s, m_ref, l_ref):
    m_old = m_ref[c]
    m_new = jnp.maximum(m_old, jnp.max(s, axis=0, keepdims=True))
    p = jnp.exp2(s - m_new)
    a = jnp.exp2(m_old - m_new)
    l_ref[c] = a * l_ref[c] + jnp.sum(p, axis=0, keepdims=True)
    m_ref[c] = m_new
    return p.astype(BF16), a


def _flash_accum(c, p, a, v_tile, acc_ref):
    acc_ref[c] = a * acc_ref[c] + _tn(v_tile, p)


def _flash_update(c, s, v_tile, m_ref, l_ref, acc_ref):
    p, a = _flash_probs(c, s, m_ref, l_ref)
    _flash_accum(c, p, a, v_tile, acc_ref)


def _masked_softmax_cols(s, valid):
    sm = jnp.where(valid, s, NEG)
    m = jnp.max(sm, axis=0, keepdims=True)
    p = jnp.where(valid, jnp.exp2(sm - m), 0.0)
    l = jnp.sum(p, axis=0, keepdims=True)
    return p * jnp.where(l > 0.0, 1.0 / l, 0.0)


def _nsa_kernel(q_ref, kc_ref, vc_ref, ks_ref, vs_ref, kw_ref, vw_ref, gt_ref, o_ref,
                sel_ref, acc_ref, m_ref, l_ref, *, q_base, nsp, tw, win_base):
    i = pl.program_id(1)
    q0 = q_base + i * Q_LANES
    ncol = NSA_GROUP * Q_LANES
    groups = range(NSA_KV)
    qas = [q_ref[g].reshape(ncol, LANES) for g in groups]
    lane_c = lax.broadcasted_iota(jnp.int32, (1, ncol), 1)
    qpos_c = q0 + (lane_c & (Q_LANES - 1))
    rows_of = lambda g: slice(g * HEAD_DIM, (g + 1) * HEAD_DIM)

    n_row = lax.broadcasted_iota(jnp.int32, (nsp, ncol), 0)
    valid_e = (n_row * L_SEL + (L_CMP - 1)) <= qpos_c
    valid_o = (n_row * L_SEL + (2 * L_CMP - 1)) <= qpos_c
    blk = lax.broadcasted_iota(jnp.int32, (nsp, Q_LANES), 0)
    qpos_q = q0 + lax.broadcasted_iota(jnp.int32, (nsp, Q_LANES), 1)
    cur = qpos_q >> 6
    bonus = jnp.where((blk == 0) | (blk == cur) | (blk == cur - 1), FORCE_BONUS, 0.0)
    in_range = blk <= cur
    blk_f = blk.astype(F32)
    o_c = []
    for g in groups:
        s_c = _nt(kc_ref[g, 0], qas[g])
        sm_e = jnp.where(valid_e, s_c[0:nsp], NEG)
        sm_o = jnp.where(valid_o, s_c[nsp:2 * nsp], NEG)
        m = jnp.maximum(jnp.max(sm_e, axis=0, keepdims=True), jnp.max(sm_o, axis=0, keepdims=True))
        p_e = jnp.where(valid_e, jnp.exp2(sm_e - m), 0.0)
        p_o = jnp.where(valid_o, jnp.exp2(sm_o - m), 0.0)
        l = jnp.sum(p_e, axis=0, keepdims=True) + jnp.sum(p_o, axis=0, keepdims=True)
        inv = jnp.where(l > 0.0, 1.0 / l, 0.0)
        p_e = p_e * inv
        p_o = p_o * inv
        pcat = jnp.concatenate([p_e, p_o], axis=0).astype(BF16)
        o_c.append(_tn(vc_ref[0], pcat)[rows_of(g)])

        imp = p_e[:, 0:Q_LANES] + p_o[:, 0:Q_LANES]
        for r in range(1, NSA_GROUP):
            imp = imp + p_e[:, r * Q_LANES:(r + 1) * Q_LANES] + p_o[:, r * Q_LANES:(r + 1) * Q_LANES]
        score = jnp.where(in_range, imp + bonus, -jnp.inf)
        sel = jnp.zeros((nsp, Q_LANES), F32)
        for _ in range(N_SELECT):
            top = jnp.max(score, axis=0, keepdims=True)
            first = jnp.min(jnp.where(score == top, blk_f, 1e9), axis=0, keepdims=True)
            pick = blk_f == first
            sel = jnp.where(pick, jnp.where(top > -jnp.inf, 1.0, sel), sel)
            score = jnp.where(pick, -jnp.inf, score)
        sel_ref[g] = sel

    for g in groups:
        _flash_init(g, m_ref, l_ref, acc_ref)
    tok_row = lax.broadcasted_iota(jnp.int32, (KEY_TILE, Q_LANES), 0)
    qpos_t = q0 + lax.broadcasted_iota(jnp.int32, (KEY_TILE, Q_LANES), 1)
    blocks_per_tile = KEY_TILE // L_SEL

    def sel_step(kt, carry):
        k0 = pl.multiple_of(kt * KEY_TILE, KEY_TILE)
        causal = k0 + tok_row <= qpos_t
        vt = vs_ref[0, pl.ds(k0, KEY_TILE), :]
        for g in groups:
            s = _nt(ks_ref[g, 0, pl.ds(k0, KEY_TILE), :], qas[g])
            sel8 = sel_ref[g, pl.ds(pl.multiple_of(kt * blocks_per_tile, blocks_per_tile), blocks_per_tile), :]
            selt = jnp.concatenate(
                [jnp.broadcast_to(sel8[j:j + 1, :], (L_SEL, Q_LANES)) for j in range(blocks_per_tile)], axis=0)
            valid = jnp.where(causal, selt, 0.0) > 0.0
            s = jnp.concatenate(
                [jnp.where(valid, s[:, r * Q_LANES:(r + 1) * Q_LANES], NEG) for r in range(NSA_GROUP)], axis=1)
            _flash_update(g, s, vt, m_ref, l_ref, acc_ref)
        return carry

    n_tiles = (q0 + Q_LANES + KEY_TILE - 1) // KEY_TILE
    lax.fori_loop(0, n_tiles, sel_step, 0)

    w0 = jnp.clip(q0 - win_base - WINDOW, 0, tw - WIN_ROWS)
    w0 = pl.multiple_of(w0, Q_LANES)
    kpos = win_base + w0 + lax.broadcasted_iota(jnp.int32, (WIN_ROWS, ncol), 0)
    dist = qpos_c - kpos
    in_window = (dist >= 0) & (dist < WINDOW)
    vwt = vw_ref[0, pl.ds(w0, WIN_ROWS), :]

    outs = []
    for g in groups:
        l_s = l_ref[g]
        o_s = acc_ref[g][rows_of(g)] * jnp.where(l_s > 0.0, 1.0 / l_s, 0.0)
        p_w = _masked_softmax_cols(_nt(kw_ref[g, 0, pl.ds(w0, WIN_ROWS), :], qas[g]), in_window)
        o_w = _tn(vwt, p_w.astype(BF16))[rows_of(g)]
        gt = gt_ref[g]
        for r in range(NSA_GROUP):
            cs = slice(r * Q_LANES, (r + 1) * Q_LANES)
            outs.append(gt[r:r + 1, :] * o_c[g][:, cs] + gt[4 + r:5 + r, :] * o_s[:, cs]
                        + gt[8 + r:9 + r, :] * o_w[:, cs])
    o_ref[...] = jnp.concatenate(outs, axis=0).T


def _nsa(qa, kc, vc, ksa, vs, kwa, vw, gt, *, bx, nqb, q_base, win_base):
    nq = qa.shape[2]
    n2 = kc.shape[2]
    tk = ksa.shape[2]
    tw = kwa.shape[2]
    assert nq == bx * nqb * Q_LANES and tk % KEY_TILE == 0 and tw >= WIN_ROWS
    kern = functools.partial(_nsa_kernel, q_base=q_base, nsp=n2 // 2, tw=tw, win_base=win_base)
    ncol = NSA_GROUP * Q_LANES
    return pl.pallas_call(
        kern, grid=(bx, nqb),
        in_specs=[
            pl.BlockSpec((NSA_KV, NSA_GROUP, Q_LANES, LANES), lambda b, i: (0, 0, b * nqb + i, 0)),
            pl.BlockSpec((NSA_KV, 1, n2, LANES), lambda b, i: (0, b, 0, 0)),
            pl.BlockSpec((1, n2, LANES), lambda b, i: (b, 0, 0)),
            pl.BlockSpec((NSA_KV, 1, tk, LANES), lambda b, i: (0, b, 0, 0)),
            pl.BlockSpec((1, tk, LANES), lambda b, i: (b, 0, 0)),
            pl.BlockSpec((NSA_KV, 1, tw, LANES), lambda b, i: (0, b, 0, 0)),
            pl.BlockSpec((1, tw, LANES), lambda b, i: (b, 0, 0)),
            pl.BlockSpec((NSA_KV, 16, Q_LANES), lambda b, i: (0, 0, b * nqb + i)),
        ],
        out_specs=pl.BlockSpec((Q_LANES, NSA_HEADS * HEAD_DIM), lambda b, i: (b * nqb + i, 0)),
        out_shape=jax.ShapeDtypeStruct((nq, NSA_HEADS * HEAD_DIM), F32),
        scratch_shapes=[pltpu.VMEM((NSA_KV, n2 // 2, Q_LANES), F32), pltpu.VMEM((NSA_KV, 2 * HEAD_DIM, ncol), F32),
                        pltpu.VMEM((NSA_KV, 1, ncol), F32), pltpu.VMEM((NSA_KV, 1, ncol), F32)],
        compiler_params=pltpu.CompilerParams(dimension_semantics=("arbitrary",) * 2, vmem_limit_bytes=VMEM_LIMIT),
        name="nsa",
    )(qa, kc, vc, ksa, vs, kwa, vw, gt)


def _diff_kernel(q_ref, k_ref, v_ref, lp_ref, sg_ref, o_ref, acc_ref, m_ref, l_ref, *, q_base, tq, lam_init):
    i = pl.program_id(2)
    q0 = q_base + i * tq
    ncol = 2 * tq
    heads = range(2)
    lp = lp_ref[...]
    lam = (jnp.exp(jnp.sum(lp[0:1] * lp[1:2], keepdims=True)) - jnp.exp(jnp.sum(lp[2:3] * lp[3:4], keepdims=True))
           + lam_init)
    n_full = q0 // KEY_TILE
    qas = [q_ref[hh].reshape(ncol, LANES) for hh in heads]
    for hh in heads:
        _flash_init(hh, m_ref, l_ref, acc_ref)

    def step(k0, causal):
        vt = v_ref[0, pl.ds(k0, KEY_TILE), :]
        for hh in heads:
            s = _nt(k_ref[hh, 0, pl.ds(k0, KEY_TILE), :], qas[hh])
            if causal is not None:
                s = jnp.where(causal, s, NEG)
            _flash_update(hh, s, vt, m_ref, l_ref, acc_ref)

    def full_step(kt, carry):
        step(pl.multiple_of(kt * KEY_TILE, KEY_TILE), None)
        return carry

    lax.fori_loop(0, n_full, full_step, 0)
    k_diag = pl.multiple_of(n_full * KEY_TILE, KEY_TILE)
    tok_row = lax.broadcasted_iota(jnp.int32, (KEY_TILE, ncol), 0)
    qpos = q0 + lax.broadcasted_iota(jnp.int32, (KEY_TILE, ncol), 1) % tq
    step(k_diag, k_diag + tok_row <= qpos)

    outs = []
    for hh in heads:
        o = acc_ref[hh][hh * HEAD_DIM:(hh + 1) * HEAD_DIM, :] / l_ref[hh]
        d = o[:, 0:tq] - lam * o[:, tq:ncol]
        d = d * lax.rsqrt(jnp.mean(d * d, axis=0, keepdims=True) + EPS) * sg_ref[:, 0:tq] * (1.0 - lam_init)
        outs.append(d)
    o_ref[...] = jnp.concatenate(outs, axis=0).T


def _diff(dqa, dka, dv, lp, sg, *, bx, nqb, tq, q_base, lam_init):
    nq = dqa.shape[2]
    tk = dka.shape[2]
    assert nq == bx * nqb * tq and tk % KEY_TILE == 0 and KEY_TILE % tq == 0
    kern = functools.partial(_diff_kernel, q_base=q_base, tq=tq, lam_init=lam_init)
    return pl.pallas_call(
        kern, grid=(bx, DIFF_HEADS // 2, nqb),
        in_specs=[
            pl.BlockSpec((2, 2, tq, LANES), lambda b, h, i: (h, 0, b * nqb + i, 0)),
            pl.BlockSpec((2, 1, tk, LANES), lambda b, h, i: (h, b, 0, 0)),
            pl.BlockSpec((1, tk, LANES), lambda b, h, i: (b, 0, h)),
            pl.BlockSpec(lp.shape, lambda b, h, i: (0, 0)),
            pl.BlockSpec(sg.shape, lambda b, h, i: (0, 0)),
        ],
        out_specs=pl.BlockSpec((tq, LANES), lambda b, h, i: (b * nqb + i, h)),
        out_shape=jax.ShapeDtypeStruct((nq, DIFF_HEADS * HEAD_DIM), F32),
        scratch_shapes=[pltpu.VMEM((2, 2 * HEAD_DIM, 2 * tq), F32), pltpu.VMEM((2, 1, 2 * tq), F32),
                        pltpu.VMEM((2, 1, 2 * tq), F32)],
        compiler_params=pltpu.CompilerParams(dimension_semantics=("arbitrary",) * 3, vmem_limit_bytes=VMEM_LIMIT),
        name="diff",
    )(dqa, dka, dv, lp, sg)


F_CHUNKS = 2


def _ffn_core(x, on, od, wo_ref, g2_ref, wg_ref, wu_ref, cw_ref, cb_ref, wd_ref, prev_rows, g_store):
    o = jnp.concatenate([on, od], axis=1).astype(BF16)
    xm = x + _mm(o, wo_ref[...])
    h2 = (xm * lax.rsqrt(jnp.mean(xm * xm, axis=-1, keepdims=True) + EPS) * g2_ref[...]).astype(BF16)
    d_ff = wg_ref.shape[1]
    fc = d_ff // F_CHUNKS
    y = jnp.zeros(x.shape, F32)
    for c in range(F_CHUNKS):
        c0, c1 = c * fc, (c + 1) * fc
        g = _mm(h2, wg_ref[:, c0:c1])
        u = _mm(h2, wu_ref[:, c0:c1])
        gm1, gm2 = prev_rows(g, c0, c1)
        g_store(g, c0, c1)
        gc = cb_ref[:, c0:c1] + cw_ref[0:1, c0:c1] * gm2 + cw_ref[1:2, c0:c1] * gm1 + cw_ref[2:3, c0:c1] * g
        act = gc * (1.0 / (1.0 + jnp.exp(-gc))) * u
        y = y + _mm(act.astype(BF16), wd_ref[c0:c1, :])
    return xm + y


def _ffn_prompt_kernel(x_ref, on_ref, od_ref, wo_ref, g2_ref, wg_ref, wu_ref, cw_ref, cb_ref, wd_ref,
                       y_ref, cv_ref, carry_ref, *, tiles_per_seq):
    i = pl.program_id(0)
    tr = x_ref.shape[0]

    @pl.when(i % tiles_per_seq == 0)
    def _():
        carry_ref[...] = jnp.zeros(carry_ref.shape, F32)

    def prev_rows(g, c0, c1):
        row = lax.broadcasted_iota(jnp.int32, g.shape, 0)
        p1 = carry_ref[7:8, c0:c1]
        p2 = carry_ref[6:7, c0:c1]
        gm1 = jnp.where(row == 0, p1, pltpu.roll(g, 1, 0))
        gm2 = jnp.where(row == 0, p2, jnp.where(row == 1, p1, pltpu.roll(g, 2, 0)))
        return gm1, gm2

    def g_store(g, c0, c1):
        carry_ref[:, c0:c1] = g[tr - 8:tr]
        cv_ref[0, :, c0:c1] = g[tr - 8:tr]

    y_ref[...] = _ffn_core(x_ref[...], on_ref[...], od_ref[...], wo_ref, g2_ref, wg_ref, wu_ref, cw_ref, cb_ref,
                           wd_ref, prev_rows, g_store)


def _ffn_sample_kernel(x_ref, on_ref, od_ref, st1_ref, st2_ref, wo_ref, g2_ref, wg_ref, wu_ref, cw_ref, cb_ref,
                       wd_ref, y_ref, g_ref, *, seq):
    def prev_rows(g, c0, c1):
        rs = lax.broadcasted_iota(jnp.int32, g.shape, 0) % seq
        s1 = st1_ref[:, c0:c1]
        gm1 = jnp.where(rs == 0, s1, pltpu.roll(g, 1, 0))
        gm2 = jnp.where(rs == 0, st2_ref[:, c0:c1], jnp.where(rs == 1, s1, pltpu.roll(g, 2, 0)))
        return gm1, gm2

    def g_store(g, c0, c1):
        g_ref[:, c0:c1] = g

    y_ref[...] = _ffn_core(x_ref[...], on_ref[...], od_ref[...], wo_ref, g2_ref, wg_ref, wu_ref, cw_ref, cb_ref,
                           wd_ref, prev_rows, g_store)


def _const_spec(a):
    return pl.BlockSpec(a.shape, lambda i: (0,) * a.ndim, pipeline_mode=pl.Buffered(1))


def _ffn_prompt(x2d, on, od, wo, g2, wg, wu, cw, cb, wd, *, seq_len, tr=256):
    n, d = x2d.shape
    f = wg.shape[1]
    assert n % tr == 0 and seq_len % tr == 0
    tps = seq_len // tr
    row = lambda w: pl.BlockSpec((tr, w), lambda i: (i, 0))
    return pl.pallas_call(
        functools.partial(_ffn_prompt_kernel, tiles_per_seq=tps), grid=(n // tr,),
        in_specs=[row(d), row(on.shape[1]), row(od.shape[1])] + [_const_spec(a) for a in (wo, g2, wg, wu, cw, cb, wd)],
        out_specs=(row(d), pl.BlockSpec((1, 8, f), lambda i: (i // tps, 0, 0))),
        out_shape=(jax.ShapeDtypeStruct((n, d), F32), jax.ShapeDtypeStruct((n // seq_len, 8, f), F32)),
        scratch_shapes=[pltpu.VMEM((8, f), F32)],
        compiler_params=pltpu.CompilerParams(dimension_semantics=("arbitrary",), vmem_limit_bytes=VMEM_LIMIT),
        name="ffn_prompt",
    )(x2d, on, od, wo, g2, wg, wu, cw, cb, wd)


def _ffn_sample(x2d, on, od, st1, st2, wo, g2, wg, wu, cw, cb, wd, *, seq):
    n, d = x2d.shape
    f = wg.shape[1]
    args = (x2d, on, od, st1, st2, wo, g2, wg, wu, cw, cb, wd)
    return pl.pallas_call(
        functools.partial(_ffn_sample_kernel, seq=seq), grid=(1,),
        in_specs=[_const_spec(a) for a in args],
        out_specs=(pl.BlockSpec((n, d), lambda i: (0, 0)), pl.BlockSpec((n, f), lambda i: (0, 0))),
        out_shape=(jax.ShapeDtypeStruct((n, d), F32), jax.ShapeDtypeStruct((n, f), F32)),
        compiler_params=pltpu.CompilerParams(dimension_semantics=("arbitrary",), vmem_limit_bytes=VMEM_LIMIT),
        name="ffn_sample",
    )(*args)


PAGES_PER_STEP = 8
T_NSA, T_WIN, T_DIF, T_ROWS = 0, 512, 768, 1792


def _softmax_rows(s):
    m = jnp.max(s, axis=-1, keepdims=True)
    p = jnp.exp2(s - m)
    return p * (1.0 / jnp.sum(p, axis=-1, keepdims=True))


def _masked_softmax_rows(s, valid):
    sm = jnp.where(valid, s, NEG)
    m = jnp.max(sm, axis=-1, keepdims=True)
    p = jnp.where(valid, jnp.exp2(sm - m), 0.0)
    l = jnp.sum(p, axis=-1, keepdims=True)
    return p * jnp.where(l > 0.0, 1.0 / l, 0.0)


def _sattn_kernel(pt_ref, *refs, n_steps, past_len, dec, lam_init):
    pg = PAGES_PER_STEP
    nsa_k, dif_k, nsa_v, dif_v = (refs[i * pg:(i + 1) * pg] for i in range(4))
    (win_ref, nn_ref, nw_ref, nd_ref, qn_ref, qd_ref, gate_ref, sn_ref, sd_ref, wck_ref, wcv_ref, gkc_ref,
     lp_ref, sg_ref, o_ref, ns_ref,
     ssel, sdif, psel, pdif, kc_scr, tail_scr, oc_scr, ow_scr, on_acc, od_acc) = refs[4 * pg:]
    s = pl.program_id(1)
    span = pg * LANES
    groups, heads = range(NSA_KV), range(DIFF_HEADS)
    cat = lambda parts: jnp.concatenate(parts, axis=1)
    hi_lo = lambda x: (x.astype(BF16), (x - x.astype(BF16).astype(F32)).astype(BF16))

    @pl.when(s < n_steps)
    def _():
        @pl.when(s == 0)
        def _():
            kc_scr[...] = jnp.zeros(kc_scr.shape, F32)

        shift = (s % 4) * (span // L_CMP)
        col = pl.multiple_of((s // 4) * LANES, LANES)
        for part, w_ref in ((0, wck_ref), (1, wcv_ref)):
            x_hi, x_lo = hi_lo(cat([r[0, 0, part * 128:(part + 1) * 128, :] for r in nsa_k]))
            blk = _mm(x_hi, w_ref[0]) + _mm(x_lo, w_ref[0]) + _mm(x_hi, w_ref[1])
            kc_scr[part * 128:(part + 1) * 128, pl.ds(col, LANES)] += pltpu.roll(blk, shift, 1)

        k0 = pl.multiple_of(s * span, span)
        kpos = (k0 + lax.broadcasted_iota(jnp.int32, (1, span), 1)).astype(F32)
        for g in groups:
            k8 = cat([r[0, 0, 256 + g * HEAD_DIM:256 + (g + 1) * HEAD_DIM, :] for r in nsa_k]).astype(BF16)
            ssel[g, :, pl.ds(k0, span)] = _mm(qn_ref[0, g], k8) + cat([sn_ref[g]] * pg) * kpos
        for h in heads:
            k8 = cat([r[0, 0, h * HEAD_DIM:(h + 1) * HEAD_DIM, :] for r in dif_k]).astype(BF16)
            sdif[h, :, pl.ds(k0, span)] = _mm(qd_ref[0, h], k8) + cat([sd_ref[h]] * pg) * kpos

    @pl.when(s == n_steps - 1)
    def _():
        new_rows = cat([nn_ref[...], nw_ref[...], nd_ref[...]])
        tail_scr[...] = jnp.concatenate([new_rows, jnp.zeros((LANES - dec, T_ROWS), F32)], axis=0).T
        nrow, drow = NSA_GROUP * dec, 2 * dec
        tail_pos = (past_len + lax.broadcasted_iota(jnp.int32, (1, LANES), 1)).astype(F32)
        causal32 = (lax.broadcasted_iota(jnp.int32, (nrow, LANES), 1)
                    <= lax.broadcasted_iota(jnp.int32, (nrow, LANES), 0) % dec)
        causal16 = (lax.broadcasted_iota(jnp.int32, (drow, LANES), 1)
                    <= lax.broadcasted_iota(jnp.int32, (drow, LANES), 0) % dec)

        for g in groups:
            kt = tail_scr[T_NSA + 256 + g * HEAD_DIM:T_NSA + 256 + (g + 1) * HEAD_DIM, :].astype(BF16)
            sc = _mm(qn_ref[0, g], kt) + sn_ref[g] * tail_pos
            ssel[g, :, past_len:past_len + LANES] = jnp.where(causal32, sc, NEG)
        for h in heads:
            kt = tail_scr[T_DIF + h * HEAD_DIM:T_DIF + (h + 1) * HEAD_DIM, :].astype(BF16)
            sc = _mm(qd_ref[0, h], kt) + sd_ref[h] * tail_pos
            sdif[h, :, past_len:past_len + LANES] = jnp.where(causal16, sc, NEG)

        n_cmp = kc_scr.shape[1]
        cend = lax.broadcasted_iota(jnp.int32, (nrow, n_cmp), 1) * L_CMP + (L_CMP - 1)
        qpos_c = past_len + lax.broadcasted_iota(jnp.int32, (nrow, n_cmp), 0) % dec
        imps = []
        for g in groups:
            kc = kc_scr[g * HEAD_DIM:(g + 1) * HEAD_DIM, :]
            kc = kc * lax.rsqrt(jnp.mean(kc * kc, axis=0, keepdims=True) + EPS) * cat([gkc_ref[...]] * (n_cmp // LANES))
            s_c = _mm(qn_ref[0, g], kc.astype(BF16)) + cat([sn_ref[g]] * (n_cmp // LANES)) * cend.astype(F32)
            p_c = _masked_softmax_rows(s_c, cend <= qpos_c)
            vc = kc_scr[128 + g * HEAD_DIM:128 + (g + 1) * HEAD_DIM, :].astype(BF16)
            oc_scr[g] = _nt(p_c.astype(BF16), vc)
            imps.append(sum(p_c[r * dec:(r + 1) * dec] for r in range(NSA_GROUP)))
        imp2 = jnp.concatenate(imps, axis=0)
        pair = (lax.broadcasted_iota(jnp.int32, (n_cmp, LANES), 0) // 2
                == lax.broadcasted_iota(jnp.int32, (n_cmp, LANES), 1)).astype(BF16)
        i_hi, i_lo = hi_lo(imp2)
        imp = _mm(i_hi, pair) + _mm(i_lo, pair)

        n_blk = past_len // L_SEL
        blk = lax.broadcasted_iota(jnp.int32, imp.shape, 1)
        blk_f = blk.astype(F32)
        bonus = jnp.where((blk == 0) | (blk == n_blk - 1), FORCE_BONUS, 0.0)
        score = jnp.where(blk < n_blk, imp + bonus, -jnp.inf)
        sel = jnp.zeros(imp.shape, F32)
        for _ in range(N_SELECT - 1):
            top = jnp.max(score, axis=1, keepdims=True)
            first = jnp.min(jnp.where(score == top, blk_f, 1e9), axis=1, keepdims=True)
            pick = blk_f == first
            sel = jnp.where(pick, jnp.where(top > -jnp.inf, 1.0, sel), sel)
            score = jnp.where(pick, -jnp.inf, score)
        sel = sel.astype(BF16)

        for c in range(n_steps):
            tok = c * span + lax.broadcasted_iota(jnp.int32, (LANES, span), 1)
            expand = ((tok >> 6) == lax.broadcasted_iota(jnp.int32, (LANES, span), 0)).astype(BF16)
            chosen = _mm(sel, expand)
            for g in groups:
                keep = jnp.concatenate([chosen[g * dec:(g + 1) * dec]] * NSA_GROUP, axis=0) > 0.5
                ssel[g, :, c * span:(c + 1) * span] = jnp.where(keep, ssel[g, :, c * span:(c + 1) * span], NEG)
        for g in groups:
            psel[g] = _softmax_rows(ssel[g]).astype(BF16)
        for h in heads:
            pdif[h] = _softmax_rows(sdif[h]).astype(BF16)

        w_buf = win_ref.shape[3]
        kwpos = past_len - w_buf + lax.broadcasted_iota(jnp.int32, (nrow, w_buf + LANES), 1)
        dist = past_len + lax.broadcasted_iota(jnp.int32, (nrow, w_buf + LANES), 0) % dec - kwpos
        in_window = (dist >= 0) & (dist < WINDOW)
        for g in groups:
            kw = cat([win_ref[0, 0, g * HEAD_DIM:(g + 1) * HEAD_DIM, :],
                      tail_scr[T_WIN + g * HEAD_DIM:T_WIN + (g + 1) * HEAD_DIM, :]]).astype(BF16)
            vw = cat([win_ref[0, 0, 128 + g * HEAD_DIM:128 + (g + 1) * HEAD_DIM, :],
                      tail_scr[T_WIN + 128 + g * HEAD_DIM:T_WIN + 128 + (g + 1) * HEAD_DIM, :]]).astype(BF16)
            s_w = _mm(qn_ref[0, g], kw) + cat([sn_ref[g]] * (w_buf // LANES + 1)) * kwpos.astype(F32)
            ow_scr[g] = _nt(_masked_softmax_rows(s_w, in_window).astype(BF16), vw)

        shifted = pltpu.roll(win_ref[0, 0], w_buf - dec, 1)
        fresh = cat([jnp.zeros((256, w_buf - LANES), F32), pltpu.roll(tail_scr[T_WIN:T_WIN + 256, :], LANES - dec, 1)])
        ns_ref[0] = jnp.where(lax.broadcasted_iota(jnp.int32, (256, w_buf), 1) >= w_buf - dec, fresh, shifted)

        on_acc[...] = jnp.zeros(on_acc.shape, F32)
        od_acc[...] = jnp.zeros(od_acc.shape, F32)

    @pl.when(s >= n_steps)
    def _():
        k0 = pl.multiple_of((s - n_steps) * span, span)
        for g in groups:
            v8 = cat([r[0, 0, g * HEAD_DIM:(g + 1) * HEAD_DIM, :] for r in nsa_v]).astype(BF16)
            on_acc[g] += _nt(psel[g, :, pl.ds(k0, span)], v8)
        for h in heads:
            v8 = cat([r[0, 0, h * HEAD_DIM:(h + 1) * HEAD_DIM, :] for r in dif_v]).astype(BF16)
            od_acc[h] += _nt(pdif[h, :, pl.ds(k0, span)], v8)

    @pl.when(s == 2 * n_steps - 1)
    def _():
        lp = lp_ref[...]
        lam = (jnp.exp(jnp.sum(lp[0:1] * lp[1:2], keepdims=True))
               - jnp.exp(jnp.sum(lp[2:3] * lp[3:4], keepdims=True)) + lam_init)
        pieces = []
        for g in groups:
            vt = tail_scr[T_NSA + 384 + g * HEAD_DIM:T_NSA + 384 + (g + 1) * HEAD_DIM, :].astype(BF16)
            o_s = on_acc[g] + _nt(psel[g, :, past_len:past_len + LANES], vt)
            o = gate_ref[0, g, 0] * oc_scr[g] + gate_ref[0, g, 1] * o_s + gate_ref[0, g, 2] * ow_scr[g]
            pieces += [o[r * dec:(r + 1) * dec] for r in range(NSA_GROUP)]
        for h in heads:
            vt = tail_scr[T_DIF + 512 + h * HEAD_DIM:T_DIF + 512 + (h + 1) * HEAD_DIM, :].astype(BF16)
            o = od_acc[h] + _nt(pdif[h, :, past_len:past_len + LANES], vt)
            d = o[0:dec] - lam * o[dec:2 * dec]
            pieces.append(d * lax.rsqrt(jnp.mean(d * d, axis=-1, keepdims=True) + EPS) * sg_ref[...] * (1.0 - lam_init))
        o_ref[0] = cat(pieces)


def _sattn(page_table, cache_nsa, cache_diff, state_win, new_nsa, new_win, new_dif, qn, qd, gates, sn, sd,
           wck, wcv, gkc, lp, sg, *, layer, lam_init):
    bs, n_pages = page_table.shape
    page = cache_nsa.shape[3]
    dec = new_nsa.shape[0] // bs
    past_len = n_pages * page
    pg = PAGES_PER_STEP
    n_steps = n_pages // pg
    w_buf = state_win.shape[3]
    assert page == LANES and n_pages % pg == 0 and dec == 8 and w_buf % LANES == 0
    n_cmp = -(-n_steps // 4) * LANES
    tks = past_len + LANES
    pt = page_table.reshape(-1)

    def kpage(j):
        return lambda b, s, pt_ref: (layer, pt_ref[b * n_pages + jnp.minimum(s, n_steps - 1) * pg + j], 0, 0)

    def vpage_blk(j, blk):
        return lambda b, s, pt_ref: (layer, pt_ref[b * n_pages + jnp.maximum(s - n_steps, 0) * pg + j], blk, 0)

    const = lambda a: pl.BlockSpec(a.shape, lambda b, s, pt_ref: (0,) * a.ndim)
    per_b = lambda a: pl.BlockSpec((1,) + a.shape[1:], lambda b, s, pt_ref: (b,) + (0,) * (a.ndim - 1))
    in_specs = (
        [pl.BlockSpec((1, 1, 384, page), kpage(j)) for j in range(pg)]
        + [pl.BlockSpec((1, 1, 512, page), kpage(j)) for j in range(pg)]
        + [pl.BlockSpec((1, 1, 128, page), vpage_blk(j, 3)) for j in range(pg)]
        + [pl.BlockSpec((1, 1, 512, page), vpage_blk(j, 1)) for j in range(pg)]
        + [pl.BlockSpec((1, 1, 256, w_buf), lambda b, s, pt_ref: (layer, b, 0, 0)),
           pl.BlockSpec((dec, 512), lambda b, s, pt_ref: (b, 0)),
           pl.BlockSpec((dec, 256), lambda b, s, pt_ref: (b, 0)),
           pl.BlockSpec((dec, 1024), lambda b, s, pt_ref: (b, 0)),
           per_b(qn), per_b(qd), per_b(gates), const(sn), const(sd), const(wck), const(wcv), const(gkc),
           const(lp), const(sg)])
    out_specs = (pl.BlockSpec((1, dec, 1024), lambda b, s, pt_ref: (b, 0, 0)),
                 pl.BlockSpec((1, 256, w_buf), lambda b, s, pt_ref: (b, 0, 0)))
    scratch = [
        pltpu.VMEM((NSA_KV, NSA_GROUP * dec, tks), F32), pltpu.VMEM((DIFF_HEADS, 2 * dec, tks), F32),
        pltpu.VMEM((NSA_KV, NSA_GROUP * dec, tks), BF16), pltpu.VMEM((DIFF_HEADS, 2 * dec, tks), BF16),
        pltpu.VMEM((256, n_cmp), F32), pltpu.VMEM((T_ROWS, LANES), F32),
        pltpu.VMEM((NSA_KV, NSA_GROUP * dec, HEAD_DIM), F32), pltpu.VMEM((NSA_KV, NSA_GROUP * dec, HEAD_DIM), F32),
        pltpu.VMEM((NSA_KV, NSA_GROUP * dec, HEAD_DIM), F32), pltpu.VMEM((DIFF_HEADS, 2 * dec, HEAD_DIM), F32),
    ]
    grid_spec = pltpu.PrefetchScalarGridSpec(num_scalar_prefetch=1, grid=(bs, 2 * n_steps), in_specs=in_specs,
                                             out_specs=out_specs, scratch_shapes=scratch)
    caches = [cache_nsa] * pg + [cache_diff] * pg + [cache_nsa] * pg + [cache_diff] * pg
    return pl.pallas_call(
        functools.partial(_sattn_kernel, n_steps=n_steps, past_len=past_len, dec=dec, lam_init=lam_init),
        grid_spec=grid_spec,
        out_shape=(jax.ShapeDtypeStruct((bs, dec, 1024), F32), jax.ShapeDtypeStruct((bs, 256, w_buf), F32)),
        compiler_params=pltpu.CompilerParams(dimension_semantics=("arbitrary",) * 2, vmem_limit_bytes=VMEM_LIMIT),
        name="sattn",
    )(pt, *caches, state_win, new_nsa, new_win, new_dif, qn, qd, gates, sn, sd, wck, wcv, gkc, lp, sg)


def _alibi_slopes(n):
    return 2.0 ** (-8.0 * jnp.arange(1, n + 1, dtype=F32) / n)


def _slope_rows(slopes):
    slopes = slopes * LOG2E
    s0 = slopes.astype(BF16).astype(F32)
    s1 = (slopes - s0).astype(BF16).astype(F32)
    s2 = (slopes - s0 - s1).astype(BF16).astype(F32)
    cols = jnp.stack([64.0 * s0, s0, 64.0 * s1, s1, 64.0 * s2, s2], axis=1)
    out = jnp.zeros((slopes.shape[0], LANES), F32)
    return out.at[:, AUG0:AUG0 + N_AUG].set(cols)


def _block_ones(group):
    idx = np.arange(LANES) // group
    return jnp.asarray((idx[:, None] == idx[None, :]).astype(np.float32), dtype=BF16)


_GATE_SRC = np.full((LANES,), -1, np.int64)
for _g in range(NSA_KV):
    for _r in range(NSA_GROUP):
        for _j in range(3):
            _GATE_SRC[_g * 16 + _j * 4 + _r] = 1280 + _g * 12 + _r * 3 + _j


def _permute_w_in(w):
    main = jnp.concatenate([w[:, 0:1280], w[:, 1304:2840]], axis=1)
    gate = jnp.where(jnp.asarray(_GATE_SRC >= 0)[None, :], w[:, np.maximum(_GATE_SRC, 0)], 0.0)
    return jnp.concatenate([main, gate], axis=1).astype(BF16)


def _compress_weights(w):
    t = np.arange(PAGES_PER_STEP * LANES)
    place = jnp.asarray(t[:, None] // L_CMP == np.arange(LANES)[None, :])
    full = jnp.where(place, jnp.tile(w, PAGES_PER_STEP * LANES // L_CMP)[:, None], 0.0)
    hi = full.astype(BF16)
    return jnp.stack([hi, (full - hi.astype(F32)).astype(BF16)])


def _gain_row(nsa_qg, nsa_kg, diff_qg, diff_kg):
    one = lambda n: jnp.ones((n,), F32)
    parts = [jnp.tile(nsa_qg, NSA_HEADS), one(256), jnp.tile(nsa_kg[1], NSA_KV), one(128),
             jnp.tile(nsa_kg[2], NSA_KV), one(128), jnp.tile(diff_qg.reshape(-1), DIFF_HEADS),
             jnp.tile(diff_kg.reshape(-1), DIFF_HEADS), one(512 + LANES)]
    return jnp.concatenate(parts)[None, :]


def kernel(x_prompt, x_sample, cache_nsa, cache_diff, state_win, state_conv, page_table, norm1_g, norm2_g, w_in, w_out, nsa_qnorm_g, nsa_knorm_g, nsa_cmp_w, diff_qnorm_g, diff_knorm_g, diff_lambda, diff_subnorm_g, w_gate, w_up, conv_w, conv_b, w_down):
    batch, seq, d_model = x_prompt.shape
    dec_batch, dec_seq, _ = x_sample.shape
    depth, n_pool, page = cache_nsa.shape[:3]
    n_pages = page_table.shape[1]
    past_len = n_pages * page
    w_buf = state_win.shape[2]
    d_ff = w_gate.shape[2]
    assert dec_seq < L_CMP and seq % KEY_TILE == 0 and w_buf == WINDOW

    s64, s32 = _block_ones(HEAD_DIM), _block_ones(DIFF_HALF)
    qcn = _slope_rows(_alibi_slopes(NSA_HEADS))
    qcd = _slope_rows(_alibi_slopes(DIFF_HEADS))
    fmaj = lambda a: jnp.moveaxis(a, 2, -1).reshape(a.shape[0], a.shape[1], -1, a.shape[2])
    cache_nsa4 = fmaj(cache_nsa)
    cache_diff4 = fmaj(cache_diff)
    state_win4 = fmaj(state_win)

    sn = jnp.broadcast_to((_alibi_slopes(NSA_HEADS) * LOG2E).reshape(NSA_KV, NSA_GROUP, 1, 1),
                          (NSA_KV, NSA_GROUP, dec_seq, LANES)).reshape(NSA_KV, NSA_GROUP * dec_seq, LANES)
    sd = jnp.broadcast_to((_alibi_slopes(DIFF_HEADS) * LOG2E).reshape(DIFF_HEADS, 1, 1), (DIFF_HEADS, 2 * dec_seq, LANES))
    xp = x_prompt.reshape(batch * seq, d_model)
    xs = x_sample.reshape(dec_batch * dec_seq, d_model)

    outs = [[] for _ in range(8)]
    for l in range(depth):
        lam_init = 0.8 - 0.6 * math.exp(-0.3 * l)
        w_in_p = _permute_w_in(w_in[l])
        gain = _gain_row(nsa_qnorm_g[l], nsa_knorm_g[l], diff_qnorm_g[l], diff_knorm_g[l])
        gain_kc = jnp.tile(nsa_knorm_g[l, 0], NSA_KV)[None, :]
        cw = jnp.concatenate([jnp.broadcast_to(nsa_cmp_w[l, 0][:, None], (L_CMP, LANES)),
                              jnp.broadcast_to(nsa_cmp_w[l, 1][:, None], (L_CMP, LANES))], axis=1)
        lp = jnp.zeros((8, LANES), F32).at[0:4, 0:DIFF_HALF].set(diff_lambda[l])
        sg = jnp.broadcast_to(diff_subnorm_g[l][:, None], (HEAD_DIM, 2 * LANES))
        sg_row = jnp.broadcast_to(diff_subnorm_g[l][None, :], (dec_seq, HEAD_DIM))
        gkc = jnp.broadcast_to(nsa_knorm_g[l, 0][:, None], (HEAD_DIM, LANES))
        wck, wcv = _compress_weights(nsa_cmp_w[l, 0]), _compress_weights(nsa_cmp_w[l, 1])
        g1 = norm1_g[l][None, :]
        g2 = norm2_g[l][None, :]
        wo, wg, wu, wd = (a.astype(BF16) for a in (w_out[l], w_gate[l], w_up[l], w_down[l]))
        cwf, cbf = conv_w[l], conv_b[l][None, :]
        proj = functools.partial(_proj, g1=g1, w=w_in_p, gain=gain, s64=s64, s32=s32, qcn=qcn, qcd=qcd, cw=cw)

        (nsa_rows, win_rows, dif_rows, qa, ksa, vs, kwa, vw, gt, dqa, dka, dv, craw) = proj(
            xp, seq_len=seq, pos_base=0)
        kc, vc = _cmpfin(craw.reshape(batch, seq // L_CMP, 256), s64, gain_kc)
        o_nsa = _nsa(qa, kc, vc, ksa.reshape(NSA_KV, batch, seq, LANES), vs.reshape(batch, seq, LANES),
                     kwa.reshape(NSA_KV, batch, seq, LANES), vw.reshape(batch, seq, LANES), gt,
                     bx=batch, nqb=seq // Q_LANES, q_base=0, win_base=0)
        o_dif = _diff(dqa, dka.reshape(DIFF_HEADS, batch, seq, LANES), dv.reshape(batch, seq, 512), lp, sg,
                      bx=batch, nqb=seq // 256, tq=256, q_base=0, lam_init=lam_init)
        xp, cv = _ffn_prompt(xp, o_nsa, o_dif, wo, g2, wg, wu, cwf, cbf, wd, seq_len=seq)
        outs[0].append(nsa_rows.reshape(batch, seq, 4, NSA_KV, HEAD_DIM))
        outs[2].append(dif_rows.reshape(batch, seq, 2, DIFF_HEADS, HEAD_DIM))
        outs[4].append(win_rows.reshape(batch, seq, 2, NSA_KV, HEAD_DIM)[:, seq - min(WINDOW, seq):])
        outs[6].append(cv[:, 8 - 2:8])

        (nsa_rows, win_rows, dif_rows, qa, _, _, _, _, gt, dqa, _, _, _) = proj(xs, seq_len=dec_seq, pos_base=past_len)
        per_seq = lambda a, lead: a[..., :HEAD_DIM].reshape(lead + (dec_batch, dec_seq, HEAD_DIM))
        qn = jnp.transpose(per_seq(qa, (NSA_KV, NSA_GROUP)), (2, 0, 1, 3, 4)).reshape(
            dec_batch, NSA_KV, NSA_GROUP * dec_seq, HEAD_DIM)
        qd = jnp.transpose(per_seq(dqa, (DIFF_HEADS, 2)), (2, 0, 1, 3, 4)).reshape(
            dec_batch, DIFF_HEADS, 2 * dec_seq, HEAD_DIM)
        gates = jnp.transpose(gt.reshape(NSA_KV, 4, NSA_GROUP, dec_batch, dec_seq), (3, 0, 1, 2, 4)).reshape(
            dec_batch, NSA_KV, 4, NSA_GROUP * dec_seq, 1)
        gates = jnp.broadcast_to(gates, gates.shape[:-1] + (HEAD_DIM,))
        o_s, new_state = _sattn(page_table, cache_nsa4, cache_diff4, state_win4, nsa_rows, win_rows, dif_rows,
                                qn, qd, gates, sn, sd, wck, wcv, gkc, lp, sg_row, layer=l, lam_init=lam_init)
        o_s = o_s.reshape(dec_batch * dec_seq, 2 * NSA_HEADS * HEAD_DIM)
        st1 = jnp.repeat(state_conv[l][:, 1], dec_seq, axis=0)
        st2 = jnp.repeat(state_conv[l][:, 0], dec_seq, axis=0)
        xs, g_s = _ffn_sample(xs, o_s[:, :NSA_HEADS * HEAD_DIM], o_s[:, NSA_HEADS * HEAD_DIM:], st1, st2,
                              wo, g2, wg, wu, cwf, cbf, wd, seq=dec_seq)
        outs[1].append(nsa_rows.reshape(dec_batch, dec_seq, 4, NSA_KV, HEAD_DIM))
        outs[3].append(dif_rows.reshape(dec_batch, dec_seq, 2, DIFF_HEADS, HEAD_DIM))
        outs[5].append(jnp.moveaxis(new_state.reshape(dec_batch, 2, NSA_KV, HEAD_DIM, w_buf), -1, 1))
        outs[7].append(g_s.reshape(dec_batch, dec_seq, d_ff)[:, dec_seq - 2:])

    stacked = [jnp.stack(o) for o in outs]
    return (xp.reshape(batch, seq, d_model), xs.reshape(dec_batch, dec_seq, d_model), *stacked)
```

```python
import functools
import math

import jax
import jax.numpy as jnp
import numpy as np
from jax import lax
from jax.experimental import pallas as pl
from jax.experimental.pallas import tpu as pltpu

F32 = jnp.float32
BF16 = jnp.bfloat16

HEAD_DIM = 64
NSA_KV = 2
NSA_GROUP = 4
NSA_HEADS = NSA_KV * NSA_GROUP
DIFF_HEADS = 8
DIFF_HALF = HEAD_DIM // 2
L_CMP = 32
L_SEL = 64
N_SELECT = 16
WINDOW = 512
FORCE_BONUS = 1.0e4
EPS = 1e-6
NEG = -1e30
UNDERFLOW_LOG2 = -160.0
NORM_SLACK = 1.02
LOG2E = math.log2(math.e)

LANES = 128
KEY_TILE = 512
Q_LANES = 128
WIN_ROWS = WINDOW + Q_LANES
AUG0 = HEAD_DIM
N_AUG = 6
VMEM_LIMIT = 56 * 1024 * 1024

C_Q, C_NSA, C_WIN, C_DQ, C_DK, C_DV, C_GATE = 0, 512, 1024, 1280, 1792, 2304, 2816
N_COL = 2944


def _nt(a, b):
    return lax.dot_general(a, b, (((1,), (1,)), ((), ())), preferred_element_type=F32)


def _tn(a, b):
    return lax.dot_general(a, b, (((0,), (0,)), ((), ())), preferred_element_type=F32)


def _mm(a, b):
    return jnp.dot(a, b, preferred_element_type=F32)


def _group_meansq(z, smat, inv_n):
    zz = z * z
    hi = zz.astype(BF16)
    lo = (zz - hi.astype(F32)).astype(BF16)
    return (_mm(hi, smat) + _mm(lo, smat)) * inv_n


def _pos_rows(pos, lane):
    is_aug = (lane >= AUG0) & (lane < AUG0 + N_AUG)
    val = jnp.where((lane & 1) == 0, pos >> 6, pos & 63)
    return jnp.where(is_aug, val, 0).astype(F32)


def _split_heads(slab, fill, lane):
    even = jnp.where(lane < HEAD_DIM, slab, fill)
    odd = jnp.where(lane < HEAD_DIM, pltpu.roll(slab, HEAD_DIM, 1), fill)
    return even, odd


def _proj_kernel(x_ref, g1_ref, w_ref, gain_ref, s64_ref, s32_ref, qcn_ref, qcd_ref, cw_ref, *rest,
                 seq_len, pos_base, fmajor):
    nsa_ref, win_ref, dif_ref, qa_ref, ksa_ref, kwa_ref, gt_ref, dqa_ref, dka_ref, cr_ref = rest[-10:]
    tr = x_ref.shape[0]

    def put(ref, c0, slab):
        if fmajor:
            ref[(0,) * (len(ref.shape) - 2) + (slice(c0, c0 + LANES), slice(None))] = slab.T
        else:
            ref[:, c0:c0 + LANES] = slab
    i = pl.program_id(0)
    x = x_ref[...]
    h = x * lax.rsqrt(jnp.mean(x * x, axis=-1, keepdims=True) + EPS) * g1_ref[...]
    hb = h.astype(BF16)

    lane = lax.broadcasted_iota(jnp.int32, (tr, LANES), 1)
    row = lax.broadcasted_iota(jnp.int32, (tr, LANES), 0)
    pos = pos_base + (i * tr + row) % seq_len
    prow = _pos_rows(pos, lane)
    s64 = s64_ref[...]
    s32 = s32_ref[...]

    def seg(c0, width):
        return _mm(hb, w_ref[:, c0:c0 + width])

    def normed(z, c0, smat, inv_n):
        return z * lax.rsqrt(_group_meansq(z, smat, inv_n) + EPS) * gain_ref[:, c0:c0 + LANES]

    zq = seg(C_Q, 512)
    for s in range(4):
        zn = normed(zq[:, s * LANES:(s + 1) * LANES], C_Q + s * LANES, s64, 1.0 / HEAD_DIM) * (HEAD_DIM ** -0.5 * LOG2E)
        for par in range(2):
            hd = 2 * s + par
            src = zn if par == 0 else pltpu.roll(zn, HEAD_DIM, 1)
            qa = jnp.where(lane < HEAD_DIM, src, qcn_ref[hd:hd + 1, :])
            qa_ref[hd // NSA_GROUP, hd % NSA_GROUP] = qa.astype(BF16)

    zc = seg(C_NSA, 512)
    ks = normed(zc[:, 256:384], C_NSA + 256, s64, 1.0 / HEAD_DIM)
    put(nsa_ref, 0, zc[:, 0:128])
    put(nsa_ref, 128, zc[:, 128:256])
    put(nsa_ref, 256, ks)
    put(nsa_ref, 384, zc[:, 384:512])
    k0, k1 = _split_heads(ks, prow, lane)
    ksa_ref[0] = k0.astype(BF16)
    ksa_ref[1] = k1.astype(BF16)
    craw = zc[:, 0:256].reshape(tr // L_CMP, L_CMP, 256) * cw_ref[...][None]
    cr_ref[...] = jnp.sum(craw, axis=1)

    zw = seg(C_WIN, 256)
    kw = normed(zw[:, 0:128], C_WIN, s64, 1.0 / HEAD_DIM)
    put(win_ref, 0, kw)
    put(win_ref, 128, zw[:, 128:256])
    k0, k1 = _split_heads(kw, prow, lane)
    kwa_ref[0] = k0.astype(BF16)
    kwa_ref[1] = k1.astype(BF16)

    zdq = seg(C_DQ, 512)
    for s in range(4):
        zn = normed(zdq[:, s * LANES:(s + 1) * LANES], C_DQ + s * LANES, s32, 1.0 / DIFF_HALF) * (DIFF_HALF ** -0.5 * LOG2E)
        for par in range(2):
            hd = 2 * s + par
            src = zn if par == 0 else pltpu.roll(zn, HEAD_DIM, 1)
            fill = qcd_ref[hd:hd + 1, :]
            dqa_ref[hd, 0] = jnp.where(lane < DIFF_HALF, src, fill).astype(BF16)
            dqa_ref[hd, 1] = jnp.where((lane >= DIFF_HALF) & (lane < HEAD_DIM), src, fill).astype(BF16)

    zdk = seg(C_DK, 512)
    for s in range(4):
        kn = normed(zdk[:, s * LANES:(s + 1) * LANES], C_DK + s * LANES, s32, 1.0 / DIFF_HALF)
        put(dif_ref, s * LANES, kn)
        k0, k1 = _split_heads(kn, prow, lane)
        dka_ref[2 * s] = k0.astype(BF16)
        dka_ref[2 * s + 1] = k1.astype(BF16)
    zdv = seg(C_DV, 512)
    for s in range(4):
        put(dif_ref, 512 + s * LANES, zdv[:, s * LANES:(s + 1) * LANES])

    zg = seg(C_GATE, LANES)
    sg = 1.0 / (1.0 + jnp.exp(-zg))
    gt_ref[...] = sg.T[0:2 * 16].reshape(NSA_KV, 16, tr)


def _proj(x2d, g1, w, gain, s64, s32, qcn, qcd, cw, *, seq_len, pos_base, stacked=None, tr=512):
    n = x2d.shape[0]
    tr = min(tr, n)
    assert n % tr == 0 and tr % L_CMP == 0
    fmajor = stacked is not None
    kern = functools.partial(_proj_kernel, seq_len=seq_len, pos_base=pos_base, fmajor=fmajor)
    full = lambda shape: pl.BlockSpec(shape, lambda i: (0,) * len(shape))
    if fmajor:
        layer, depth, prev = stacked
        assert seq_len % tr == 0
        tps, nb = seq_len // tr, n // seq_len
        cache_shapes = [jax.ShapeDtypeStruct((depth, nb, 512, seq_len), F32),
                        jax.ShapeDtypeStruct((nb, 256, seq_len), F32),
                        jax.ShapeDtypeStruct((depth, nb, 1024, seq_len), F32)]
        cache_specs = [pl.BlockSpec((1, 1, 512, tr), lambda i: (layer, i // tps, 0, i % tps)),
                       pl.BlockSpec((1, 256, tr), lambda i: (i // tps, 0, i % tps)),
                       pl.BlockSpec((1, 1, 1024, tr), lambda i: (layer, i // tps, 0, i % tps))]
    else:
        prev = None
        cache_shapes = [jax.ShapeDtypeStruct((n, w_), F32) for w_ in (512, 256, 1024)]
        cache_specs = [pl.BlockSpec((tr, w_), lambda i: (i, 0)) for w_ in (512, 256, 1024)]
    out_shape = tuple(cache_shapes) + (
        jax.ShapeDtypeStruct((NSA_KV, NSA_GROUP, n, LANES), BF16),
        jax.ShapeDtypeStruct((NSA_KV, n, LANES), BF16),
        jax.ShapeDtypeStruct((NSA_KV, n, LANES), BF16),
        jax.ShapeDtypeStruct((NSA_KV, 16, n), F32),
        jax.ShapeDtypeStruct((DIFF_HEADS, 2, n, LANES), BF16),
        jax.ShapeDtypeStruct((DIFF_HEADS, n, LANES), BF16),
        jax.ShapeDtypeStruct((n // L_CMP, 256), F32),
    )
    out_specs = tuple(cache_specs) + (
        pl.BlockSpec((NSA_KV, NSA_GROUP, tr, LANES), lambda i: (0, 0, i, 0)),
        pl.BlockSpec((NSA_KV, tr, LANES), lambda i: (0, i, 0)),
        pl.BlockSpec((NSA_KV, tr, LANES), lambda i: (0, i, 0)),
        pl.BlockSpec((NSA_KV, 16, tr), lambda i: (0, 0, i)),
        pl.BlockSpec((DIFF_HEADS, 2, tr, LANES), lambda i: (0, 0, i, 0)),
        pl.BlockSpec((DIFF_HEADS, tr, LANES), lambda i: (0, i, 0)),
        pl.BlockSpec((tr // L_CMP, 256), lambda i: (i, 0)),
    )
    in_specs = [
        pl.BlockSpec((tr, x2d.shape[1]), lambda i: (i, 0)),
        full(g1.shape), full(w.shape), full(gain.shape), full(s64.shape), full(s32.shape),
        full(qcn.shape), full(qcd.shape), full(cw.shape),
    ]
    args = [x2d, g1, w, gain, s64, s32, qcn, qcd, cw]
    aliases = {}
    if prev is not None:
        in_specs += [pl.BlockSpec(memory_space=pl.ANY)] * 2
        aliases = {len(args): 0, len(args) + 1: 2}
        args += list(prev)
    return pl.pallas_call(
        kern, grid=(n // tr,), in_specs=in_specs, out_specs=out_specs, out_shape=out_shape,
        input_output_aliases=aliases,
        compiler_params=pltpu.CompilerParams(dimension_semantics=("arbitrary",), vmem_limit_bytes=VMEM_LIMIT),
        name="proj",
    )(*args)


def _cmpfin_kernel(raw_ref, s64_ref, gain_ref, kc_ref, vc_ref, k_scr, v_scr, *, nsp):
    lane = lax.broadcasted_iota(jnp.int32, (nsp, LANES), 1)
    n = lax.broadcasted_iota(jnp.int32, (nsp, LANES), 0)
    k_scr[...] = raw_ref[0, :, 0:128]
    v_scr[...] = raw_ref[0, :, 128:256]
    for j in range(2):
        k = k_scr[pl.ds(j, nsp, stride=2), :]
        kn = k * lax.rsqrt(_group_meansq(k, s64_ref[...], 1.0 / HEAD_DIM) + EPS) * gain_ref[...]
        cend = n * L_SEL + (L_CMP - 1 + L_CMP * j)
        prow = _pos_rows(cend, lane)
        k0, k1 = _split_heads(kn, prow, lane)
        kc_ref[0, 0, j * nsp:(j + 1) * nsp, :] = k0.astype(BF16)
        kc_ref[1, 0, j * nsp:(j + 1) * nsp, :] = k1.astype(BF16)
        vc_ref[0, j * nsp:(j + 1) * nsp, :] = v_scr[pl.ds(j, nsp, stride=2), :].astype(BF16)


def _cmpfin(raw, s64, gain_kc):
    bx, n2, _ = raw.shape
    nsp = n2 // 2
    assert nsp % 16 == 0
    return pl.pallas_call(
        functools.partial(_cmpfin_kernel, nsp=nsp),
        grid=(bx,),
        in_specs=[pl.BlockSpec((1, n2, 256), lambda b: (b, 0, 0)),
                  pl.BlockSpec(s64.shape, lambda b: (0, 0)),
                  pl.BlockSpec(gain_kc.shape, lambda b: (0, 0))],
        out_specs=(pl.BlockSpec((NSA_KV, 1, n2, LANES), lambda b: (0, b, 0, 0)),
                   pl.BlockSpec((1, n2, LANES), lambda b: (b, 0, 0))),
        out_shape=(jax.ShapeDtypeStruct((NSA_KV, bx, n2, LANES), BF16),
                   jax.ShapeDtypeStruct((bx, n2, LANES), BF16)),
        scratch_shapes=[pltpu.VMEM((n2, LANES), F32), pltpu.VMEM((n2, LANES), F32)],
        name="cmpfin",
    )(raw, s64, gain_kc)


def _flash_init(c, m_ref, l_ref, acc_ref):
    m_ref[c] = jnp.full(m_ref.shape[1:], NEG, F32)
    l_ref[c] = jnp.zeros(l_ref.shape[1:], F32)
    acc_ref[c] = jnp.zeros(acc_ref.shape[1:], F32)


def _flash_probs(c, s, m_ref, l_ref):
    m_old = m_ref[c]
    m_new = jnp.maximum(m_old, jnp.max(s, axis=0, keepdims=True))
    p = jnp.exp2(s - m_new)
    a = jnp.exp2(m_old - m_new)
    l_ref[c] = a * l_ref[c] + jnp.sum(p, axis=0, keepdims=True)
    m_ref[c] = m_new
    return p.astype(BF16), a


def _flash_update(c, s, vt_tile, m_ref, l_ref, acc_ref):
    p, a = _flash_probs(c, s, m_ref, l_ref)
    acc_ref[c] = a * acc_ref[c] + _mm(vt_tile, p)


def _max_key_norm(k_ref_2d, n_keys):
    feat = lax.broadcasted_iota(jnp.int32, (1, LANES), 1) < HEAD_DIM

    def chunk(c, best):
        kf = k_ref_2d[pl.ds(pl.multiple_of(c * KEY_TILE, KEY_TILE), KEY_TILE), :].astype(F32)
        sq = jnp.sum(jnp.where(feat, kf * kf, 0.0), axis=1, keepdims=True)
        return jnp.maximum(best, jnp.max(sq, axis=0, keepdims=True))

    return jnp.sqrt(lax.fori_loop(0, n_keys // KEY_TILE, chunk, jnp.zeros((1, 1), F32)))


def _first_needed_tile(qa, m, k_norm, slope_row, n_max):
    feat = lax.broadcasted_iota(jnp.int32, (1, LANES), 1) < HEAD_DIM
    qf = qa.astype(F32)
    qsq = jnp.where(feat, qf * qf, 0.0)
    hi = qsq.astype(BF16)
    lo = (qsq - hi.astype(F32)).astype(BF16)
    ones = jnp.ones((8, LANES), BF16)
    q_norm = jnp.sqrt((_nt(ones, hi) + _nt(ones, lo))[0:1])
    cutoff = (m + UNDERFLOW_LOG2 - NORM_SLACK * q_norm * k_norm) / slope_row
    tile = jnp.floor(jnp.min(cutoff, axis=1, keepdims=True) * (1.0 / KEY_TILE))
    return jnp.clip(tile, 0.0, n_max.astype(F32)).astype(jnp.int32)[0, 0]


def _masked_softmax_cols(s, valid):
    sm = jnp.where(valid, s, NEG)
    m = jnp.max(sm, axis=0, keepdims=True)
    p = jnp.where(valid, jnp.exp2(sm - m), 0.0)
    l = jnp.sum(p, axis=0, keepdims=True)
    return p * jnp.where(l > 0.0, 1.0 / l, 0.0)


def _nsa_kernel(q_ref, kc_ref, vc_ref, ks_ref, vs_ref, kw_ref, vw_ref, gt_ref, o_ref,
                sel_ref, acc_ref, m_ref, l_ref, kn_ref, *, q_base, nsp, tw, win_base):
    i = pl.program_id(1)
    q0 = q_base + i * Q_LANES
    ncol = NSA_GROUP * Q_LANES
    groups = range(NSA_KV)
    qas = [q_ref[g].reshape(ncol, LANES) for g in groups]
    lane_c = lax.broadcasted_iota(jnp.int32, (1, ncol), 1)
    qpos_c = q0 + (lane_c & (Q_LANES - 1))
    rows_of = lambda g: slice(g * HEAD_DIM, (g + 1) * HEAD_DIM)

    n_row = lax.broadcasted_iota(jnp.int32, (nsp, ncol), 0)
    valid_e = (n_row * L_SEL + (L_CMP - 1)) <= qpos_c
    valid_o = (n_row * L_SEL + (2 * L_CMP - 1)) <= qpos_c
    blk = lax.broadcasted_iota(jnp.int32, (nsp, Q_LANES), 0)
    qpos_q = q0 + lax.broadcasted_iota(jnp.int32, (nsp, Q_LANES), 1)
    cur = qpos_q >> 6
    bonus = jnp.where((blk == 0) | (blk == cur) | (blk == cur - 1), FORCE_BONUS, 0.0)
    in_range = blk <= cur
    blk_f = blk.astype(F32)
    o_c = []
    for g in groups:
        s_c = _nt(kc_ref[g, 0], qas[g])
        sm_e = jnp.where(valid_e, s_c[0:nsp], NEG)
        sm_o = jnp.where(valid_o, s_c[nsp:2 * nsp], NEG)
        m = jnp.maximum(jnp.max(sm_e, axis=0, keepdims=True), jnp.max(sm_o, axis=0, keepdims=True))
        p_e = jnp.where(valid_e, jnp.exp2(sm_e - m), 0.0)
        p_o = jnp.where(valid_o, jnp.exp2(sm_o - m), 0.0)
        l = jnp.sum(p_e, axis=0, keepdims=True) + jnp.sum(p_o, axis=0, keepdims=True)
        inv = jnp.where(l > 0.0, 1.0 / l, 0.0)
        p_e = p_e * inv
        p_o = p_o * inv
        pcat = jnp.concatenate([p_e, p_o], axis=0).astype(BF16)
        o_c.append(_tn(vc_ref[0], pcat)[rows_of(g)])

        imp = p_e[:, 0:Q_LANES] + p_o[:, 0:Q_LANES]
        for r in range(1, NSA_GROUP):
            imp = imp + p_e[:, r * Q_LANES:(r + 1) * Q_LANES] + p_o[:, r * Q_LANES:(r + 1) * Q_LANES]
        score = jnp.where(in_range, imp + bonus, -jnp.inf)
        sel = jnp.zeros((nsp, Q_LANES), F32)
        for _ in range(N_SELECT):
            top = jnp.max(score, axis=0, keepdims=True)
            first = jnp.min(jnp.where(score == top, blk_f, 1e9), axis=0, keepdims=True)
            pick = blk_f == first
            sel = jnp.where(pick, jnp.where(top > -jnp.inf, 1.0, sel), sel)
            score = jnp.where(pick, -jnp.inf, score)
        sel_ref[g] = sel

    @pl.when(i == 0)
    def _():
        for g in groups:
            kn_ref[g] = jnp.broadcast_to(_max_key_norm(ks_ref.at[g, 0], ks_ref.shape[2]), kn_ref.shape[1:])

    for g in groups:
        _flash_init(g, m_ref, l_ref, acc_ref)
    tok_row = lax.broadcasted_iota(jnp.int32, (KEY_TILE, Q_LANES), 0)
    qpos_t = q0 + lax.broadcasted_iota(jnp.int32, (KEY_TILE, Q_LANES), 1)
    blocks_per_tile = KEY_TILE // L_SEL

    def sel_tile(kt, which):
        k0 = pl.multiple_of(kt * KEY_TILE, KEY_TILE)
        causal = k0 + tok_row <= qpos_t
        for g in which:
            vt = vs_ref[0, 0, rows_of(g), pl.ds(k0, KEY_TILE)].astype(BF16)
            s = _nt(ks_ref[g, 0, pl.ds(k0, KEY_TILE), :], qas[g])
            sel8 = sel_ref[g, pl.ds(pl.multiple_of(kt * blocks_per_tile, blocks_per_tile), blocks_per_tile), :]
            selt = jnp.concatenate(
                [jnp.broadcast_to(sel8[j:j + 1, :], (L_SEL, Q_LANES)) for j in range(blocks_per_tile)], axis=0)
            valid = jnp.where(causal, selt, 0.0) > 0.0
            s = jnp.concatenate(
                [jnp.where(valid, s[:, r * Q_LANES:(r + 1) * Q_LANES], NEG) for r in range(NSA_GROUP)], axis=1)
            _flash_update(g, s, vt, m_ref, l_ref, acc_ref)

    n_before = (q0 + Q_LANES + KEY_TILE - 1) // KEY_TILE - 1
    sel_tile(n_before, groups)
    first = []
    for g in groups:
        slope = jnp.concatenate([jnp.full((1, Q_LANES), LOG2E * 2.0 ** -(g * NSA_GROUP + r + 1), F32)
                                 for r in range(NSA_GROUP)], axis=1)
        first.append(_first_needed_tile(qas[g], m_ref[g], kn_ref[g][0:1, 0:1], slope, n_before))
    both_from = jnp.maximum(first[0], first[1])

    def tiles(which):
        def body(kt, carry):
            sel_tile(kt, which)
            return carry
        return body

    lax.fori_loop(first[0], both_from, tiles((0,)), 0)
    lax.fori_loop(first[1], both_from, tiles((1,)), 0)
    lax.fori_loop(both_from, n_before, tiles(groups), 0)

    w0 = jnp.clip(q0 - win_base - WINDOW, 0, tw - WIN_ROWS)
    w0 = pl.multiple_of(w0, Q_LANES)
    kpos = win_base + w0 + lax.broadcasted_iota(jnp.int32, (WIN_ROWS, ncol), 0)
    dist = qpos_c - kpos
    in_window = (dist >= 0) & (dist < WINDOW)

    outs = []
    for g in groups:
        l_s = l_ref[g]
        o_s = acc_ref[g] * jnp.where(l_s > 0.0, 1.0 / l_s, 0.0)
        p_w = _masked_softmax_cols(_nt(kw_ref[g, 0, pl.ds(w0, WIN_ROWS), :], qas[g]), in_window)
        o_w = _mm(vw_ref[0, rows_of(g), pl.ds(w0, WIN_ROWS)].astype(BF16), p_w.astype(BF16))
        gt = gt_ref[g]
        for r in range(NSA_GROUP):
            cs = slice(r * Q_LANES, (r + 1) * Q_LANES)
            outs.append(gt[r:r + 1, :] * o_c[g][:, cs] + gt[4 + r:5 + r, :] * o_s[:, cs]
                        + gt[8 + r:9 + r, :] * o_w[:, cs])
    o_ref[...] = jnp.concatenate(outs, axis=0).T


def _nsa(qa, kc, vc, ksa, nsa_t, kwa, win_t, gt, *, layer, bx, nqb, q_base, win_base):
    nq = qa.shape[2]
    n2 = kc.shape[2]
    tk = ksa.shape[2]
    tw = kwa.shape[2]
    assert nq == bx * nqb * Q_LANES and tk % KEY_TILE == 0 and tw >= WIN_ROWS
    assert nsa_t.shape[2:] == (512, tk) and win_t.shape[1:] == (256, tw)
    kern = functools.partial(_nsa_kernel, q_base=q_base, nsp=n2 // 2, tw=tw, win_base=win_base)
    ncol = NSA_GROUP * Q_LANES
    return pl.pallas_call(
        kern, grid=(bx, nqb),
        in_specs=[
            pl.BlockSpec((NSA_KV, NSA_GROUP, Q_LANES, LANES), lambda b, i: (0, 0, b * nqb + i, 0)),
            pl.BlockSpec((NSA_KV, 1, n2, LANES), lambda b, i: (0, b, 0, 0)),
            pl.BlockSpec((1, n2, LANES), lambda b, i: (b, 0, 0)),
            pl.BlockSpec((NSA_KV, 1, tk, LANES), lambda b, i: (0, b, 0, 0)),
            pl.BlockSpec((1, 1, LANES, tk), lambda b, i: (layer, b, 3, 0)),
            pl.BlockSpec((NSA_KV, 1, tw, LANES), lambda b, i: (0, b, 0, 0)),
            pl.BlockSpec((1, LANES, tw), lambda b, i: (b, 1, 0)),
            pl.BlockSpec((NSA_KV, 16, Q_LANES), lambda b, i: (0, 0, b * nqb + i)),
        ],
        out_specs=pl.BlockSpec((Q_LANES, NSA_HEADS * HEAD_DIM), lambda b, i: (b * nqb + i, 0)),
        out_shape=jax.ShapeDtypeStruct((nq, NSA_HEADS * HEAD_DIM), F32),
        scratch_shapes=[pltpu.VMEM((NSA_KV, n2 // 2, Q_LANES), F32), pltpu.VMEM((NSA_KV, HEAD_DIM, ncol), F32),
                        pltpu.VMEM((NSA_KV, 1, ncol), F32), pltpu.VMEM((NSA_KV, 1, ncol), F32),
                        pltpu.VMEM((NSA_KV, 8, LANES), F32)],
        compiler_params=pltpu.CompilerParams(dimension_semantics=("arbitrary",) * 2, vmem_limit_bytes=VMEM_LIMIT),
        name="nsa",
    )(qa, kc, vc, ksa, nsa_t, kwa, win_t, gt)


def _diff_kernel(q_ref, k_ref, v_ref, sl_ref, lp_ref, sg_ref, o_ref, acc_ref, m_ref, l_ref, kn_ref, *,
                 q_base, tq, lam_init):
    pair = pl.program_id(1)
    i = pl.program_id(2)
    q0 = q_base + i * tq
    ncol = 2 * tq
    heads = range(2)
    lp = lp_ref[...]
    lam = (jnp.exp(jnp.sum(lp[0:1] * lp[1:2], keepdims=True)) - jnp.exp(jnp.sum(lp[2:3] * lp[3:4], keepdims=True))
           + lam_init)
    n_full = q0 // KEY_TILE
    qas = [q_ref[hh].reshape(ncol, LANES) for hh in heads]
    for hh in heads:
        _flash_init(hh, m_ref, l_ref, acc_ref)

    @pl.when(i == 0)
    def _():
        for hh in heads:
            kn_ref[hh] = jnp.broadcast_to(_max_key_norm(k_ref.at[hh, 0], k_ref.shape[2]), kn_ref.shape[1:])

    def step(k0, causal, which):
        for hh in which:
            vt = v_ref[0, 0, hh * HEAD_DIM:(hh + 1) * HEAD_DIM, pl.ds(k0, KEY_TILE)].astype(BF16)
            s = _nt(k_ref[hh, 0, pl.ds(k0, KEY_TILE), :], qas[hh])
            if causal is not None:
                s = jnp.where(causal, s, NEG)
            _flash_update(hh, s, vt, m_ref, l_ref, acc_ref)

    k_diag = pl.multiple_of(n_full * KEY_TILE, KEY_TILE)
    tok_row = lax.broadcasted_iota(jnp.int32, (KEY_TILE, ncol), 0)
    qpos = q0 + lax.broadcasted_iota(jnp.int32, (KEY_TILE, ncol), 1) % tq
    step(k_diag, k_diag + tok_row <= qpos, heads)

    first = []
    for hh in heads:
        slope = jnp.concatenate([sl_ref[pl.ds(2 * pair + hh, 1), :]] * (ncol // LANES), axis=1)
        first.append(_first_needed_tile(qas[hh], m_ref[hh], kn_ref[hh][0:1, 0:1], slope, n_full))
    both_from = jnp.maximum(first[0], first[1])

    def tiles(which):
        def body(kt, carry):
            step(pl.multiple_of(kt * KEY_TILE, KEY_TILE), None, which)
            return carry
        return body

    lax.fori_loop(first[0], both_from, tiles((0,)), 0)
    lax.fori_loop(first[1], both_from, tiles((1,)), 0)
    lax.fori_loop(both_from, n_full, tiles(heads), 0)

    outs = []
    for hh in heads:
        o = acc_ref[hh] / l_ref[hh]
        d = o[:, 0:tq] - lam * o[:, tq:ncol]
        d = d * lax.rsqrt(jnp.mean(d * d, axis=0, keepdims=True) + EPS) * sg_ref[:, 0:tq] * (1.0 - lam_init)
        outs.append(d)
    o_ref[...] = jnp.concatenate(outs, axis=0).T


def _diff(dqa, dka, dif_t, slopes, lp, sg, *, layer, bx, nqb, tq, q_base, lam_init):
    nq = dqa.shape[2]
    tk = dka.shape[2]
    assert nq == bx * nqb * tq and tk % KEY_TILE == 0 and KEY_TILE % tq == 0 and dif_t.shape[2:] == (1024, tk)
    kern = functools.partial(_diff_kernel, q_base=q_base, tq=tq, lam_init=lam_init)
    return pl.pallas_call(
        kern, grid=(bx, DIFF_HEADS // 2, nqb),
        in_specs=[
            pl.BlockSpec((2, 2, tq, LANES), lambda b, h, i: (h, 0, b * nqb + i, 0)),
            pl.BlockSpec((2, 1, tk, LANES), lambda b, h, i: (h, b, 0, 0)),
            pl.BlockSpec((1, 1, LANES, tk), lambda b, h, i: (layer, b, 4 + h, 0)),
            pl.BlockSpec(slopes.shape, lambda b, h, i: (0, 0)),
            pl.BlockSpec(lp.shape, lambda b, h, i: (0, 0)),
            pl.BlockSpec(sg.shape, lambda b, h, i: (0, 0)),
        ],
        out_specs=pl.BlockSpec((tq, LANES), lambda b, h, i: (b * nqb + i, h)),
        out_shape=jax.ShapeDtypeStruct((nq, DIFF_HEADS * HEAD_DIM), F32),
        scratch_shapes=[pltpu.VMEM((2, HEAD_DIM, 2 * tq), F32), pltpu.VMEM((2, 1, 2 * tq), F32),
                        pltpu.VMEM((2, 1, 2 * tq), F32), pltpu.VMEM((2, 8, LANES), F32)],
        compiler_params=pltpu.CompilerParams(dimension_semantics=("arbitrary",) * 3, vmem_limit_bytes=VMEM_LIMIT),
        name="diff",
    )(dqa, dka, dif_t, slopes, lp, sg)


F_CHUNKS = 2


def _ffn_core(x, on, od, wo_ref, g2_ref, wg_ref, wu_ref, cw_ref, cb_ref, wd_ref, prev_rows, g_store):
    o = jnp.concatenate([on, od], axis=1).astype(BF16)
    xm = x + _mm(o, wo_ref[...])
    h2 = (xm * lax.rsqrt(jnp.mean(xm * xm, axis=-1, keepdims=True) + EPS) * g2_ref[...]).astype(BF16)
    d_ff = wg_ref.shape[1]
    fc = d_ff // F_CHUNKS
    y = jnp.zeros(x.shape, F32)
    for c in range(F_CHUNKS):
        c0, c1 = c * fc, (c + 1) * fc
        g = _mm(h2, wg_ref[:, c0:c1])
        u = _mm(h2, wu_ref[:, c0:c1])
        gm1, gm2 = prev_rows(g, c0, c1)
        g_store(g, c0, c1)
        gc = cb_ref[:, c0:c1] + cw_ref[0:1, c0:c1] * gm2 + cw_ref[1:2, c0:c1] * gm1 + cw_ref[2:3, c0:c1] * g
        act = gc * (1.0 / (1.0 + jnp.exp(-gc))) * u
        y = y + _mm(act.astype(BF16), wd_ref[c0:c1, :])
    return xm + y


def _ffn_prompt_kernel(x_ref, on_ref, od_ref, wo_ref, g2_ref, wg_ref, wu_ref, cw_ref, cb_ref, wd_ref,
                       y_ref, cv_ref, carry_ref, *, tiles_per_seq):
    i = pl.program_id(0)
    tr = x_ref.shape[0]

    @pl.when(i % tiles_per_seq == 0)
    def _():
        carry_ref[...] = jnp.zeros(carry_ref.shape, F32)

    def prev_rows(g, c0, c1):
        row = lax.broadcasted_iota(jnp.int32, g.shape, 0)
        p1 = carry_ref[7:8, c0:c1]
        p2 = carry_ref[6:7, c0:c1]
        gm1 = jnp.where(row == 0, p1, pltpu.roll(g, 1, 0))
        gm2 = jnp.where(row == 0, p2, jnp.where(row == 1, p1, pltpu.roll(g, 2, 0)))
        return gm1, gm2

    def g_store(g, c0, c1):
        carry_ref[:, c0:c1] = g[tr - 8:tr]
        cv_ref[0, :, c0:c1] = g[tr - 8:tr]

    y_ref[...] = _ffn_core(x_ref[...], on_ref[...], od_ref[...], wo_ref, g2_ref, wg_ref, wu_ref, cw_ref, cb_ref,
                           wd_ref, prev_rows, g_store)


def _ffn_sample_kernel(x_ref, on_ref, od_ref, st1_ref, st2_ref, wo_ref, g2_ref, wg_ref, wu_ref, cw_ref, cb_ref,
                       wd_ref, y_ref, g_ref, *, seq):
    def prev_rows(g, c0, c1):
        rs = lax.broadcasted_iota(jnp.int32, g.shape, 0) % seq
        s1 = st1_ref[:, c0:c1]
        gm1 = jnp.where(rs == 0, s1, pltpu.roll(g, 1, 0))
        gm2 = jnp.where(rs == 0, st2_ref[:, c0:c1], jnp.where(rs == 1, s1, pltpu.roll(g, 2, 0)))
        return gm1, gm2

    def g_store(g, c0, c1):
        g_ref[:, c0:c1] = g

    y_ref[...] = _ffn_core(x_ref[...], on_ref[...], od_ref[...], wo_ref, g2_ref, wg_ref, wu_ref, cw_ref, cb_ref,
                           wd_ref, prev_rows, g_store)


def _const_spec(a):
    return pl.BlockSpec(a.shape, lambda i: (0,) * a.ndim, pipeline_mode=pl.Buffered(1))


def _ffn_prompt(x2d, on, od, wo, g2, wg, wu, cw, cb, wd, *, seq_len, tr=256):
    n, d = x2d.shape
    f = wg.shape[1]
    assert n % tr == 0 and seq_len % tr == 0
    tps = seq_len // tr
    row = lambda w: pl.BlockSpec((tr, w), lambda i: (i, 0))
    return pl.pallas_call(
        functools.partial(_ffn_prompt_kernel, tiles_per_seq=tps), grid=(n // tr,),
        in_specs=[row(d), row(on.shape[1]), row(od.shape[1])] + [_const_spec(a) for a in (wo, g2, wg, wu, cw, cb, wd)],
        out_specs=(row(d), pl.BlockSpec((1, 8, f), lambda i: (i // tps, 0, 0))),
        out_shape=(jax.ShapeDtypeStruct((n, d), F32), jax.ShapeDtypeStruct((n // seq_len, 8, f), F32)),
        scratch_shapes=[pltpu.VMEM((8, f), F32)],
        compiler_params=pltpu.CompilerParams(dimension_semantics=("arbitrary",), vmem_limit_bytes=VMEM_LIMIT),
        name="ffn_prompt",
    )(x2d, on, od, wo, g2, wg, wu, cw, cb, wd)


def _ffn_sample(x2d, on, od, st1, st2, wo, g2, wg, wu, cw, cb, wd, *, seq):
    n, d = x2d.shape
    f = wg.shape[1]
    args = (x2d, on, od, st1, st2, wo, g2, wg, wu, cw, cb, wd)
    return pl.pallas_call(
        functools.partial(_ffn_sample_kernel, seq=seq), grid=(1,),
        in_specs=[_const_spec(a) for a in args],
        out_specs=(pl.BlockSpec((n, d), lambda i: (0, 0)), pl.BlockSpec((n, f), lambda i: (0, 0))),
        out_shape=(jax.ShapeDtypeStruct((n, d), F32), jax.ShapeDtypeStruct((n, f), F32)),
        compiler_params=pltpu.CompilerParams(dimension_semantics=("arbitrary",), vmem_limit_bytes=VMEM_LIMIT),
        name="ffn_sample",
    )(*args)


PAGES_PER_STEP = 8
T_NSA, T_WIN, T_DIF, T_ROWS = 0, 512, 768, 1792


def _softmax_rows(s):
    m = jnp.max(s, axis=-1, keepdims=True)
    p = jnp.exp2(s - m)
    return p * (1.0 / jnp.sum(p, axis=-1, keepdims=True))


def _masked_softmax_rows(s, valid):
    sm = jnp.where(valid, s, NEG)
    m = jnp.max(sm, axis=-1, keepdims=True)
    p = jnp.where(valid, jnp.exp2(sm - m), 0.0)
    l = jnp.sum(p, axis=-1, keepdims=True)
    return p * jnp.where(l > 0.0, 1.0 / l, 0.0)


def _sattn_kernel(pt_ref, *refs, n_steps, past_len, dec, lam_init):
    pg = PAGES_PER_STEP
    nsa_k, dif_k, nsa_v, dif_v = (refs[i * pg:(i + 1) * pg] for i in range(4))
    (win_ref, nn_ref, nw_ref, nd_ref, qn_ref, qd_ref, gate_ref, sn_ref, sd_ref, wck_ref, wcv_ref, gkc_ref,
     lp_ref, sg_ref, o_ref, ns_ref,
     ssel, sdif, psel, pdif, kc_scr, tail_scr, oc_scr, ow_scr, on_acc, od_acc) = refs[4 * pg:]
    s = pl.program_id(1)
    span = pg * LANES
    groups, heads = range(NSA_KV), range(DIFF_HEADS)
    cat = lambda parts: jnp.concatenate(parts, axis=1)
    hi_lo = lambda x: (x.astype(BF16), (x - x.astype(BF16).astype(F32)).astype(BF16))

    @pl.when(s < n_steps)
    def _():
        @pl.when(s == 0)
        def _():
            kc_scr[...] = jnp.zeros(kc_scr.shape, F32)

        shift = (s % 4) * (span // L_CMP)
        col = pl.multiple_of((s // 4) * LANES, LANES)
        for part, w_ref in ((0, wck_ref), (1, wcv_ref)):
            x_hi, x_lo = hi_lo(cat([r[0, 0, part * 128:(part + 1) * 128, :] for r in nsa_k]))
            blk = _mm(x_hi, w_ref[0]) + _mm(x_lo, w_ref[0]) + _mm(x_hi, w_ref[1])
            kc_scr[part * 128:(part + 1) * 128, pl.ds(col, LANES)] += pltpu.roll(blk, shift, 1)

        k0 = pl.multiple_of(s * span, span)
        kpos = (k0 + lax.broadcasted_iota(jnp.int32, (1, span), 1)).astype(F32)
        for g in groups:
            k8 = cat([r[0, 0, 256 + g * HEAD_DIM:256 + (g + 1) * HEAD_DIM, :] for r in nsa_k]).astype(BF16)
            ssel[g, :, pl.ds(k0, span)] = _mm(qn_ref[0, g], k8) + cat([sn_ref[g]] * pg) * kpos
        for h in heads:
            k8 = cat([r[0, 0, h * HEAD_DIM:(h + 1) * HEAD_DIM, :] for r in dif_k]).astype(BF16)
            sdif[h, :, pl.ds(k0, span)] = _mm(qd_ref[0, h], k8) + cat([sd_ref[h]] * pg) * kpos

    @pl.when(s == n_steps)
    def _():
        new_rows = cat([nn_ref[...], nw_ref[...], nd_ref[...]])
        tail_scr[...] = jnp.concatenate([new_rows, jnp.zeros((LANES - dec, T_ROWS), F32)], axis=0).T
        nrow, drow = NSA_GROUP * dec, 2 * dec
        tail_pos = (past_len + lax.broadcasted_iota(jnp.int32, (1, LANES), 1)).astype(F32)
        causal32 = (lax.broadcasted_iota(jnp.int32, (nrow, LANES), 1)
                    <= lax.broadcasted_iota(jnp.int32, (nrow, LANES), 0) % dec)
        causal16 = (lax.broadcasted_iota(jnp.int32, (drow, LANES), 1)
                    <= lax.broadcasted_iota(jnp.int32, (drow, LANES), 0) % dec)

        for g in groups:
            kt = tail_scr[T_NSA + 256 + g * HEAD_DIM:T_NSA + 256 + (g + 1) * HEAD_DIM, :].astype(BF16)
            sc = _mm(qn_ref[0, g], kt) + sn_ref[g] * tail_pos
            ssel[g, :, past_len:past_len + LANES] = jnp.where(causal32, sc, NEG)
        for h in heads:
            kt = tail_scr[T_DIF + h * HEAD_DIM:T_DIF + (h + 1) * HEAD_DIM, :].astype(BF16)
            sc = _mm(qd_ref[0, h], kt) + sd_ref[h] * tail_pos
            sdif[h, :, past_len:past_len + LANES] = jnp.where(causal16, sc, NEG)

        n_cmp = kc_scr.shape[1]
        cend = lax.broadcasted_iota(jnp.int32, (nrow, n_cmp), 1) * L_CMP + (L_CMP - 1)
        qpos_c = past_len + lax.broadcasted_iota(jnp.int32, (nrow, n_cmp), 0) % dec
        imps = []
        for g in groups:
            kc = kc_scr[g * HEAD_DIM:(g + 1) * HEAD_DIM, :]
            kc = kc * lax.rsqrt(jnp.mean(kc * kc, axis=0, keepdims=True) + EPS) * cat([gkc_ref[...]] * (n_cmp // LANES))
            s_c = _mm(qn_ref[0, g], kc.astype(BF16)) + cat([sn_ref[g]] * (n_cmp // LANES)) * cend.astype(F32)
            p_c = _masked_softmax_rows(s_c, cend <= qpos_c)
            vc = kc_scr[128 + g * HEAD_DIM:128 + (g + 1) * HEAD_DIM, :].astype(BF16)
            oc_scr[g] = _nt(p_c.astype(BF16), vc)
            imps.append(sum(p_c[r * dec:(r + 1) * dec] for r in range(NSA_GROUP)))
        imp2 = jnp.concatenate(imps, axis=0)
        pair = (lax.broadcasted_iota(jnp.int32, (n_cmp, LANES), 0) // 2
                == lax.broadcasted_iota(jnp.int32, (n_cmp, LANES), 1)).astype(BF16)
        i_hi, i_lo = hi_lo(imp2)
        imp = _mm(i_hi, pair) + _mm(i_lo, pair)

        n_blk = past_len // L_SEL
        blk = lax.broadcasted_iota(jnp.int32, imp.shape, 1)
        blk_f = blk.astype(F32)
        bonus = jnp.where((blk == 0) | (blk == n_blk - 1), FORCE_BONUS, 0.0)
        score = jnp.where(blk < n_blk, imp + bonus, -jnp.inf)
        sel = jnp.zeros(imp.shape, F32)
        for _ in range(N_SELECT - 1):
            top = jnp.max(score, axis=1, keepdims=True)
            first = jnp.min(jnp.where(score == top, blk_f, 1e9), axis=1, keepdims=True)
            pick = blk_f == first
            sel = jnp.where(pick, jnp.where(top > -jnp.inf, 1.0, sel), sel)
            score = jnp.where(pick, -jnp.inf, score)
        sel = sel.astype(BF16)

        for c in range(n_steps):
            tok = c * span + lax.broadcasted_iota(jnp.int32, (LANES, span), 1)
            expand = ((tok >> 6) == lax.broadcasted_iota(jnp.int32, (LANES, span), 0)).astype(BF16)
            chosen = _mm(sel, expand)
            for g in groups:
                keep = jnp.concatenate([chosen[g * dec:(g + 1) * dec]] * NSA_GROUP, axis=0) > 0.5
                ssel[g, :, c * span:(c + 1) * span] = jnp.where(keep, ssel[g, :, c * span:(c + 1) * span], NEG)
        for g in groups:
            psel[g] = _softmax_rows(ssel[g]).astype(BF16)
        for h in heads:
            pdif[h] = _softmax_rows(sdif[h]).astype(BF16)

        w_buf = win_ref.shape[3]
        kwpos = past_len - w_buf + lax.broadcasted_iota(jnp.int32, (nrow, w_buf + LANES), 1)
        dist = past_len + lax.broadcasted_iota(jnp.int32, (nrow, w_buf + LANES), 0) % dec - kwpos
        in_window = (dist >= 0) & (dist < WINDOW)
        for g in groups:
            kw = cat([win_ref[0, 0, g * HEAD_DIM:(g + 1) * HEAD_DIM, :],
                      tail_scr[T_WIN + g * HEAD_DIM:T_WIN + (g + 1) * HEAD_DIM, :]]).astype(BF16)
            vw = cat([win_ref[0, 0, 128 + g * HEAD_DIM:128 + (g + 1) * HEAD_DIM, :],
                      tail_scr[T_WIN + 128 + g * HEAD_DIM:T_WIN + 128 + (g + 1) * HEAD_DIM, :]]).astype(BF16)
            s_w = _mm(qn_ref[0, g], kw) + cat([sn_ref[g]] * (w_buf // LANES + 1)) * kwpos.astype(F32)
            ow_scr[g] = _nt(_masked_softmax_rows(s_w, in_window).astype(BF16), vw)

        shifted = pltpu.roll(win_ref[0, 0], w_buf - dec, 1)
        fresh = cat([jnp.zeros((256, w_buf - LANES), F32), pltpu.roll(tail_scr[T_WIN:T_WIN + 256, :], LANES - dec, 1)])
        ns_ref[0] = jnp.where(lax.broadcasted_iota(jnp.int32, (256, w_buf), 1) >= w_buf - dec, fresh, shifted)

        on_acc[...] = jnp.zeros(on_acc.shape, F32)
        od_acc[...] = jnp.zeros(od_acc.shape, F32)

    @pl.when(s >= n_steps)
    def _():
        k0 = pl.multiple_of((s - n_steps) * span, span)
        for g in groups:
            v8 = cat([r[0, 0, g * HEAD_DIM:(g + 1) * HEAD_DIM, :] for r in nsa_v]).astype(BF16)
            on_acc[g] += _nt(psel[g, :, pl.ds(k0, span)], v8)
        for h in heads:
            v8 = cat([r[0, 0, h * HEAD_DIM:(h + 1) * HEAD_DIM, :] for r in dif_v]).astype(BF16)
            od_acc[h] += _nt(pdif[h, :, pl.ds(k0, span)], v8)

    @pl.when(s == 2 * n_steps - 1)
    def _():
        lp = lp_ref[...]
        lam = (jnp.exp(jnp.sum(lp[0:1] * lp[1:2], keepdims=True))
               - jnp.exp(jnp.sum(lp[2:3] * lp[3:4], keepdims=True)) + lam_init)
        pieces = []
        for g in groups:
            vt = tail_scr[T_NSA + 384 + g * HEAD_DIM:T_NSA + 384 + (g + 1) * HEAD_DIM, :].astype(BF16)
            o_s = on_acc[g] + _nt(psel[g, :, past_len:past_len + LANES], vt)
            o = gate_ref[0, g, 0] * oc_scr[g] + gate_ref[0, g, 1] * o_s + gate_ref[0, g, 2] * ow_scr[g]
            pieces += [o[r * dec:(r + 1) * dec] for r in range(NSA_GROUP)]
        for h in heads:
            vt = tail_scr[T_DIF + 512 + h * HEAD_DIM:T_DIF + 512 + (h + 1) * HEAD_DIM, :].astype(BF16)
            o = od_acc[h] + _nt(pdif[h, :, past_len:past_len + LANES], vt)
            d = o[0:dec] - lam * o[dec:2 * dec]
            pieces.append(d * lax.rsqrt(jnp.mean(d * d, axis=-1, keepdims=True) + EPS) * sg_ref[...] * (1.0 - lam_init))
        o_ref[0] = cat(pieces)


def _sattn(page_table, cache_nsa, cache_diff, state_win, new_nsa, new_win, new_dif, qn, qd, gates, sn, sd,
           wck, wcv, gkc, lp, sg, *, layer, lam_init):
    bs, n_pages = page_table.shape
    page = cache_nsa.shape[3]
    dec = new_nsa.shape[0] // bs
    past_len = n_pages * page
    pg = PAGES_PER_STEP
    n_steps = n_pages // pg
    w_buf = state_win.shape[3]
    assert page == LANES and n_pages % pg == 0 and dec == 8 and w_buf % LANES == 0
    n_cmp = -(-n_steps // 4) * LANES
    tks = past_len + LANES
    pt = page_table.reshape(-1)

    def kpage(j):
        return lambda b, s, pt_ref: (layer, pt_ref[b * n_pages + jnp.minimum(s, n_steps - 1) * pg + j], 0, 0)

    def vpage_blk(j, blk):
        return lambda b, s, pt_ref: (layer, pt_ref[b * n_pages + jnp.maximum(s - n_steps, 0) * pg + j], blk, 0)

    const = lambda a: pl.BlockSpec(a.shape, lambda b, s, pt_ref: (0,) * a.ndim)
    per_b = lambda a: pl.BlockSpec((1,) + a.shape[1:], lambda b, s, pt_ref: (b,) + (0,) * (a.ndim - 1))
    in_specs = (
        [pl.BlockSpec((1, 1, 384, page), kpage(j)) for j in range(pg)]
        + [pl.BlockSpec((1, 1, 512, page), kpage(j)) for j in range(pg)]
        + [pl.BlockSpec((1, 1, 128, page), vpage_blk(j, 3)) for j in range(pg)]
        + [pl.BlockSpec((1, 1, 512, page), vpage_blk(j, 1)) for j in range(pg)]
        + [pl.BlockSpec((1, 1, 256, w_buf), lambda b, s, pt_ref: (layer, b, 0, 0)),
           pl.BlockSpec((dec, 512), lambda b, s, pt_ref: (b, 0)),
           pl.BlockSpec((dec, 256), lambda b, s, pt_ref: (b, 0)),
           pl.BlockSpec((dec, 1024), lambda b, s, pt_ref: (b, 0)),
           per_b(qn), per_b(qd), per_b(gates), const(sn), const(sd), const(wck), const(wcv), const(gkc),
           const(lp), const(sg)])
    out_specs = (pl.BlockSpec((1, dec, 1024), lambda b, s, pt_ref: (b, 0, 0)),
                 pl.BlockSpec((1, 256, w_buf), lambda b, s, pt_ref: (b, 0, 0)))
    scratch = [
        pltpu.VMEM((NSA_KV, NSA_GROUP * dec, tks), F32), pltpu.VMEM((DIFF_HEADS, 2 * dec, tks), F32),
        pltpu.VMEM((NSA_KV, NSA_GROUP * dec, tks), BF16), pltpu.VMEM((DIFF_HEADS, 2 * dec, tks), BF16),
        pltpu.VMEM((256, n_cmp), F32), pltpu.VMEM((T_ROWS, LANES), F32),
        pltpu.VMEM((NSA_KV, NSA_GROUP * dec, HEAD_DIM), F32), pltpu.VMEM((NSA_KV, NSA_GROUP * dec, HEAD_DIM), F32),
        pltpu.VMEM((NSA_KV, NSA_GROUP * dec, HEAD_DIM), F32), pltpu.VMEM((DIFF_HEADS, 2 * dec, HEAD_DIM), F32),
    ]
    grid_spec = pltpu.PrefetchScalarGridSpec(num_scalar_prefetch=1, grid=(bs, 2 * n_steps), in_specs=in_specs,
                                             out_specs=out_specs, scratch_shapes=scratch)
    caches = [cache_nsa] * pg + [cache_diff] * pg + [cache_nsa] * pg + [cache_diff] * pg
    return pl.pallas_call(
        functools.partial(_sattn_kernel, n_steps=n_steps, past_len=past_len, dec=dec, lam_init=lam_init),
        grid_spec=grid_spec,
        out_shape=(jax.ShapeDtypeStruct((bs, dec, 1024), F32), jax.ShapeDtypeStruct((bs, 256, w_buf), F32)),
        compiler_params=pltpu.CompilerParams(dimension_semantics=("arbitrary",) * 2, vmem_limit_bytes=VMEM_LIMIT),
        name="sattn",
    )(pt, *caches, state_win, new_nsa, new_win, new_dif, qn, qd, gates, sn, sd, wck, wcv, gkc, lp, sg)


def _alibi_slopes(n):
    return 2.0 ** (-8.0 * jnp.arange(1, n + 1, dtype=F32) / n)


def _slope_rows(slopes):
    slopes = slopes * LOG2E
    s0 = slopes.astype(BF16).astype(F32)
    s1 = (slopes - s0).astype(BF16).astype(F32)
    s2 = (slopes - s0 - s1).astype(BF16).astype(F32)
    cols = jnp.stack([64.0 * s0, s0, 64.0 * s1, s1, 64.0 * s2, s2], axis=1)
    out = jnp.zeros((slopes.shape[0], LANES), F32)
    return out.at[:, AUG0:AUG0 + N_AUG].set(cols)


def _block_ones(group):
    idx = np.arange(LANES) // group
    return jnp.asarray((idx[:, None] == idx[None, :]).astype(np.float32), dtype=BF16)


_GATE_SRC = np.full((LANES,), -1, np.int64)
for _g in range(NSA_KV):
    for _r in range(NSA_GROUP):
        for _j in range(3):
            _GATE_SRC[_g * 16 + _j * 4 + _r] = 1280 + _g * 12 + _r * 3 + _j


def _permute_w_in(w):
    main = jnp.concatenate([w[:, 0:1280], w[:, 1304:2840]], axis=1)
    gate = jnp.where(jnp.asarray(_GATE_SRC >= 0)[None, :], w[:, np.maximum(_GATE_SRC, 0)], 0.0)
    return jnp.concatenate([main, gate], axis=1).astype(BF16)


def _compress_weights(w):
    t = np.arange(PAGES_PER_STEP * LANES)
    place = jnp.asarray(t[:, None] // L_CMP == np.arange(LANES)[None, :])
    full = jnp.where(place, jnp.tile(w, PAGES_PER_STEP * LANES // L_CMP)[:, None], 0.0)
    hi = full.astype(BF16)
    return jnp.stack([hi, (full - hi.astype(F32)).astype(BF16)])


def _gain_row(nsa_qg, nsa_kg, diff_qg, diff_kg):
    one = lambda n: jnp.ones((n,), F32)
    parts = [jnp.tile(nsa_qg, NSA_HEADS), one(256), jnp.tile(nsa_kg[1], NSA_KV), one(128),
             jnp.tile(nsa_kg[2], NSA_KV), one(128), jnp.tile(diff_qg.reshape(-1), DIFF_HEADS),
             jnp.tile(diff_kg.reshape(-1), DIFF_HEADS), one(512 + LANES)]
    return jnp.concatenate(parts)[None, :]


def kernel(x_prompt, x_sample, cache_nsa, cache_diff, state_win, state_conv, page_table, norm1_g, norm2_g, w_in, w_out, nsa_qnorm_g, nsa_knorm_g, nsa_cmp_w, diff_qnorm_g, diff_knorm_g, diff_lambda, diff_subnorm_g, w_gate, w_up, conv_w, conv_b, w_down):
    batch, seq, d_model = x_prompt.shape
    dec_batch, dec_seq, _ = x_sample.shape
    depth, n_pool, page = cache_nsa.shape[:3]
    n_pages = page_table.shape[1]
    past_len = n_pages * page
    w_buf = state_win.shape[2]
    d_ff = w_gate.shape[2]
    assert dec_seq < L_CMP and seq % KEY_TILE == 0 and w_buf == WINDOW

    s64, s32 = _block_ones(HEAD_DIM), _block_ones(DIFF_HALF)
    qcn = _slope_rows(_alibi_slopes(NSA_HEADS))
    qcd = _slope_rows(_alibi_slopes(DIFF_HEADS))
    fmaj = lambda a: jnp.moveaxis(a, 2, -1).reshape(a.shape[0], a.shape[1], -1, a.shape[2])
    cache_nsa4 = fmaj(cache_nsa)
    cache_diff4 = fmaj(cache_diff)
    state_win4 = fmaj(state_win)

    sn = jnp.broadcast_to((_alibi_slopes(NSA_HEADS) * LOG2E).reshape(NSA_KV, NSA_GROUP, 1, 1),
                          (NSA_KV, NSA_GROUP, dec_seq, LANES)).reshape(NSA_KV, NSA_GROUP * dec_seq, LANES)
    sd = jnp.broadcast_to((_alibi_slopes(DIFF_HEADS) * LOG2E).reshape(DIFF_HEADS, 1, 1), (DIFF_HEADS, 2 * dec_seq, LANES))
    xp = x_prompt.reshape(batch * seq, d_model)
    xs = x_sample.reshape(dec_batch * dec_seq, d_model)

    outs = [[] for _ in range(8)]
    for l in range(depth):
        lam_init = 0.8 - 0.6 * math.exp(-0.3 * l)
        w_in_p = _permute_w_in(w_in[l])
        gain = _gain_row(nsa_qnorm_g[l], nsa_knorm_g[l], diff_qnorm_g[l], diff_knorm_g[l])
        gain_kc = jnp.tile(nsa_knorm_g[l, 0], NSA_KV)[None, :]
        cw = jnp.concatenate([jnp.broadcast_to(nsa_cmp_w[l, 0][:, None], (L_CMP, LANES)),
                              jnp.broadcast_to(nsa_cmp_w[l, 1][:, None], (L_CMP, LANES))], axis=1)
        lp = jnp.zeros((8, LANES), F32).at[0:4, 0:DIFF_HALF].set(diff_lambda[l])
        sg = jnp.broadcast_to(diff_subnorm_g[l][:, None], (HEAD_DIM, 2 * LANES))
        sg_row = jnp.broadcast_to(diff_subnorm_g[l][None, :], (dec_seq, HEAD_DIM))
        gkc = jnp.broadcast_to(nsa_knorm_g[l, 0][:, None], (HEAD_DIM, LANES))
        wck, wcv = _compress_weights(nsa_cmp_w[l, 0]), _compress_weights(nsa_cmp_w[l, 1])
        g1 = norm1_g[l][None, :]
        g2 = norm2_g[l][None, :]
        wo, wg, wu, wd = (a.astype(BF16) for a in (w_out[l], w_gate[l], w_up[l], w_down[l]))
        cwf, cbf = conv_w[l], conv_b[l][None, :]
        proj = functools.partial(_proj, g1=g1, w=w_in_p, gain=gain, s64=s64, s32=s32, qcn=qcn, qcd=qcd, cw=cw)

        (nsa_t, win_t, dif_t, qa, ksa, kwa, gt, dqa, dka, craw) = proj(
            xp, seq_len=seq, pos_base=0, stacked=(l, depth, None if l == 0 else (nsa_t, dif_t)))
        kc, vc = _cmpfin(craw.reshape(batch, seq // L_CMP, 256), s64, gain_kc)
        o_nsa = _nsa(qa, kc, vc, ksa.reshape(NSA_KV, batch, seq, LANES), nsa_t,
                     kwa.reshape(NSA_KV, batch, seq, LANES), win_t, gt,
                     layer=l, bx=batch, nqb=seq // Q_LANES, q_base=0, win_base=0)
        o_dif = _diff(dqa, dka.reshape(DIFF_HEADS, batch, seq, LANES), dif_t, sd[:, 0], lp, sg,
                      layer=l, bx=batch, nqb=seq // 256, tq=256, q_base=0, lam_init=lam_init)
        xp, cv = _ffn_prompt(xp, o_nsa, o_dif, wo, g2, wg, wu, cwf, cbf, wd, seq_len=seq)
        w_keep = min(WINDOW, seq)
        outs[4].append(jnp.moveaxis(win_t[:, :, seq - w_keep:].reshape(batch, 2, NSA_KV, HEAD_DIM, w_keep), -1, 1))
        outs[6].append(cv[:, 8 - 2:8])

        (nsa_rows, win_rows, dif_rows, qa, _, _, gt, dqa, _, _) = proj(xs, seq_len=dec_seq, pos_base=past_len)
        per_seq = lambda a, lead: a[..., :HEAD_DIM].reshape(lead + (dec_batch, dec_seq, HEAD_DIM))
        qn = jnp.transpose(per_seq(qa, (NSA_KV, NSA_GROUP)), (2, 0, 1, 3, 4)).reshape(
            dec_batch, NSA_KV, NSA_GROUP * dec_seq, HEAD_DIM)
        qd = jnp.transpose(per_seq(dqa, (DIFF_HEADS, 2)), (2, 0, 1, 3, 4)).reshape(
            dec_batch, DIFF_HEADS, 2 * dec_seq, HEAD_DIM)
        gates = jnp.transpose(gt.reshape(NSA_KV, 4, NSA_GROUP, dec_batch, dec_seq), (3, 0, 1, 2, 4)).reshape(
            dec_batch, NSA_KV, 4, NSA_GROUP * dec_seq, 1)
        gates = jnp.broadcast_to(gates, gates.shape[:-1] + (HEAD_DIM,))
        o_s, new_state = _sattn(page_table, cache_nsa4, cache_diff4, state_win4, nsa_rows, win_rows, dif_rows,
                                qn, qd, gates, sn, sd, wck, wcv, gkc, lp, sg_row, layer=l, lam_init=lam_init)
        o_s = o_s.reshape(dec_batch * dec_seq, 2 * NSA_HEADS * HEAD_DIM)
        st1 = jnp.repeat(state_conv[l][:, 1], dec_seq, axis=0)
        st2 = jnp.repeat(state_conv[l][:, 0], dec_seq, axis=0)
        xs, g_s = _ffn_sample(xs, o_s[:, :NSA_HEADS * HEAD_DIM], o_s[:, NSA_HEADS * HEAD_DIM:], st1, st2,
                              wo, g2, wg, wu, cwf, cbf, wd, seq=dec_seq)
        outs[1].append(nsa_rows.reshape(dec_batch, dec_seq, 4, NSA_KV, HEAD_DIM))
        outs[3].append(dif_rows.reshape(dec_batch, dec_seq, 2, DIFF_HEADS, HEAD_DIM))
        outs[5].append(jnp.moveaxis(new_state.reshape(dec_batch, 2, NSA_KV, HEAD_DIM, w_buf), -1, 1))
        outs[7].append(g_s.reshape(dec_batch, dec_seq, d_ff)[:, dec_seq - 2:])

    outs[0] = jnp.moveaxis(nsa_t.reshape(depth, batch, 4, NSA_KV, HEAD_DIM, seq), -1, 2)
    outs[2] = jnp.moveaxis(dif_t.reshape(depth, batch, 2, DIFF_HEADS, HEAD_DIM, seq), -1, 2)
    stacked = [o if not isinstance(o, list) else jnp.stack(o) for o in outs]
    return (xp.reshape(batch, seq, d_model), xs.reshape(dec_batch, dec_seq, d_model), *stacked)
```

```python
import functools
import math

import jax
import jax.numpy as jnp
import numpy as np
from jax import lax
from jax.experimental import pallas as pl
from jax.experimental.pallas import tpu as pltpu

F32 = jnp.float32
BF16 = jnp.bfloat16

HEAD_DIM = 64
NSA_KV = 2
NSA_GROUP = 4
NSA_HEADS = NSA_KV * NSA_GROUP
DIFF_HEADS = 8
DIFF_HALF = HEAD_DIM // 2
L_CMP = 32
L_SEL = 64
N_SELECT = 16
WINDOW = 512
FORCE_BONUS = 1.0e4
EPS = 1e-6
NEG = -1e30
UNDERFLOW_LOG2 = -160.0
NORM_SLACK = 1.02
LOG2E = math.log2(math.e)

LANES = 128
KEY_TILE = 512
Q_LANES = 128
WIN_ROWS = WINDOW + Q_LANES
AUG0 = HEAD_DIM
N_AUG = 6
VMEM_LIMIT = 56 * 1024 * 1024

C_Q, C_NSA, C_WIN, C_DQ, C_DK, C_DV, C_GATE = 0, 512, 1024, 1280, 1792, 2304, 2816
N_COL = 2944


def _nt(a, b):
    return lax.dot_general(a, b, (((1,), (1,)), ((), ())), preferred_element_type=F32)


def _tn(a, b):
    return lax.dot_general(a, b, (((0,), (0,)), ((), ())), preferred_element_type=F32)


def _mm(a, b):
    return jnp.dot(a, b, preferred_element_type=F32)


def _group_meansq(z, smat, inv_n):
    zz = z * z
    hi = zz.astype(BF16)
    lo = (zz - hi.astype(F32)).astype(BF16)
    return (_mm(hi, smat) + _mm(lo, smat)) * inv_n


def _pos_rows(pos, lane):
    is_aug = (lane >= AUG0) & (lane < AUG0 + N_AUG)
    val = jnp.where((lane & 1) == 0, pos >> 6, pos & 63)
    return jnp.where(is_aug, val, 0).astype(F32)


def _split_heads(slab, fill, lane):
    even = jnp.where(lane < HEAD_DIM, slab, fill)
    odd = jnp.where(lane < HEAD_DIM, pltpu.roll(slab, HEAD_DIM, 1), fill)
    return even, odd


def _proj_kernel(x_ref, g1_ref, w_ref, gain_ref, s64_ref, s32_ref, qcn_ref, qcd_ref, cw_ref, *rest,
                 seq_len, pos_base, fmajor):
    nsa_ref, win_ref, dif_ref, qa_ref, ksa_ref, kwa_ref, gt_ref, dqa_ref, dka_ref, cr_ref = rest[-10:]
    tr = x_ref.shape[0]

    def put(ref, c0, slab):
        if fmajor:
            ref[(0,) * (len(ref.shape) - 2) + (slice(c0, c0 + LANES), slice(None))] = slab.T
        else:
            ref[:, c0:c0 + LANES] = slab
    i = pl.program_id(0)
    x = x_ref[...]
    h = x * lax.rsqrt(jnp.mean(x * x, axis=-1, keepdims=True) + EPS) * g1_ref[...]
    hb = h.astype(BF16)

    lane = lax.broadcasted_iota(jnp.int32, (tr, LANES), 1)
    row = lax.broadcasted_iota(jnp.int32, (tr, LANES), 0)
    pos = pos_base + (i * tr + row) % seq_len
    prow = _pos_rows(pos, lane)
    s64 = s64_ref[...]
    s32 = s32_ref[...]

    def seg(c0, width):
        return _mm(hb, w_ref[:, c0:c0 + width])

    def normed(z, c0, smat, inv_n):
        return z * lax.rsqrt(_group_meansq(z, smat, inv_n) + EPS) * gain_ref[:, c0:c0 + LANES]

    zq = seg(C_Q, 512)
    for s in range(4):
        zn = normed(zq[:, s * LANES:(s + 1) * LANES], C_Q + s * LANES, s64, 1.0 / HEAD_DIM) * (HEAD_DIM ** -0.5 * LOG2E)
        for par in range(2):
            hd = 2 * s + par
            src = zn if par == 0 else pltpu.roll(zn, HEAD_DIM, 1)
            qa = jnp.where(lane < HEAD_DIM, src, qcn_ref[hd:hd + 1, :])
            qa_ref[hd // NSA_GROUP, hd % NSA_GROUP] = qa.astype(BF16)

    zc = seg(C_NSA, 512)
    ks = normed(zc[:, 256:384], C_NSA + 256, s64, 1.0 / HEAD_DIM)
    put(nsa_ref, 0, zc[:, 0:128])
    put(nsa_ref, 128, zc[:, 128:256])
    put(nsa_ref, 256, ks)
    put(nsa_ref, 384, zc[:, 384:512])
    k0, k1 = _split_heads(ks, prow, lane)
    ksa_ref[0] = k0.astype(BF16)
    ksa_ref[1] = k1.astype(BF16)
    craw = zc[:, 0:256].reshape(tr // L_CMP, L_CMP, 256) * cw_ref[...][None]
    cr_ref[...] = jnp.sum(craw, axis=1)

    zw = seg(C_WIN, 256)
    kw = normed(zw[:, 0:128], C_WIN, s64, 1.0 / HEAD_DIM)
    put(win_ref, 0, kw)
    put(win_ref, 128, zw[:, 128:256])
    k0, k1 = _split_heads(kw, prow, lane)
    kwa_ref[0] = k0.astype(BF16)
    kwa_ref[1] = k1.astype(BF16)

    zdq = seg(C_DQ, 512)
    for s in range(4):
        zn = normed(zdq[:, s * LANES:(s + 1) * LANES], C_DQ + s * LANES, s32, 1.0 / DIFF_HALF) * (DIFF_HALF ** -0.5 * LOG2E)
        for par in range(2):
            hd = 2 * s + par
            src = zn if par == 0 else pltpu.roll(zn, HEAD_DIM, 1)
            fill = qcd_ref[hd:hd + 1, :]
            dqa_ref[hd, 0] = jnp.where(lane < DIFF_HALF, src, fill).astype(BF16)
            dqa_ref[hd, 1] = jnp.where((lane >= DIFF_HALF) & (lane < HEAD_DIM), src, fill).astype(BF16)

    zdk = seg(C_DK, 512)
    for s in range(4):
        kn = normed(zdk[:, s * LANES:(s + 1) * LANES], C_DK + s * LANES, s32, 1.0 / DIFF_HALF)
        put(dif_ref, s * LANES, kn)
        k0, k1 = _split_heads(kn, prow, lane)
        dka_ref[2 * s] = k0.astype(BF16)
        dka_ref[2 * s + 1] = k1.astype(BF16)
    zdv = seg(C_DV, 512)
    for s in range(4):
        put(dif_ref, 512 + s * LANES, zdv[:, s * LANES:(s + 1) * LANES])

    zg = seg(C_GATE, LANES)
    sg = 1.0 / (1.0 + jnp.exp(-zg))
    gt_ref[...] = sg.T[0:2 * 16].reshape(NSA_KV, 16, tr)


def _proj(x2d, g1, w, gain, s64, s32, qcn, qcd, cw, *, seq_len, pos_base, stacked=None, tr=512):
    n = x2d.shape[0]
    tr = min(tr, n)
    assert n % tr == 0 and tr % L_CMP == 0
    fmajor = stacked is not None
    kern = functools.partial(_proj_kernel, seq_len=seq_len, pos_base=pos_base, fmajor=fmajor)
    full = lambda shape: pl.BlockSpec(shape, lambda i: (0,) * len(shape))
    if fmajor:
        layer, depth, prev = stacked
        assert seq_len % tr == 0
        tps, nb = seq_len // tr, n // seq_len
        cache_shapes = [jax.ShapeDtypeStruct((depth, nb, 512, seq_len), F32),
                        jax.ShapeDtypeStruct((nb, 256, seq_len), F32),
                        jax.ShapeDtypeStruct((depth, nb, 1024, seq_len), F32)]
        cache_specs = [pl.BlockSpec((1, 1, 512, tr), lambda i: (layer, i // tps, 0, i % tps)),
                       pl.BlockSpec((1, 256, tr), lambda i: (i // tps, 0, i % tps)),
                       pl.BlockSpec((1, 1, 1024, tr), lambda i: (layer, i // tps, 0, i % tps))]
    else:
        prev = None
        cache_shapes = [jax.ShapeDtypeStruct((n, w_), F32) for w_ in (512, 256, 1024)]
        cache_specs = [pl.BlockSpec((tr, w_), lambda i: (i, 0)) for w_ in (512, 256, 1024)]
    out_shape = tuple(cache_shapes) + (
        jax.ShapeDtypeStruct((NSA_KV, NSA_GROUP, n, LANES), BF16),
        jax.ShapeDtypeStruct((NSA_KV, n, LANES), BF16),
        jax.ShapeDtypeStruct((NSA_KV, n, LANES), BF16),
        jax.ShapeDtypeStruct((NSA_KV, 16, n), F32),
        jax.ShapeDtypeStruct((DIFF_HEADS, 2, n, LANES), BF16),
        jax.ShapeDtypeStruct((DIFF_HEADS, n, LANES), BF16),
        jax.ShapeDtypeStruct((n // L_CMP, 256), F32),
    )
    out_specs = tuple(cache_specs) + (
        pl.BlockSpec((NSA_KV, NSA_GROUP, tr, LANES), lambda i: (0, 0, i, 0)),
        pl.BlockSpec((NSA_KV, tr, LANES), lambda i: (0, i, 0)),
        pl.BlockSpec((NSA_KV, tr, LANES), lambda i: (0, i, 0)),
        pl.BlockSpec((NSA_KV, 16, tr), lambda i: (0, 0, i)),
        pl.BlockSpec((DIFF_HEADS, 2, tr, LANES), lambda i: (0, 0, i, 0)),
        pl.BlockSpec((DIFF_HEADS, tr, LANES), lambda i: (0, i, 0)),
        pl.BlockSpec((tr // L_CMP, 256), lambda i: (i, 0)),
    )
    in_specs = [
        pl.BlockSpec((tr, x2d.shape[1]), lambda i: (i, 0)),
        full(g1.shape), full(w.shape), full(gain.shape), full(s64.shape), full(s32.shape),
        full(qcn.shape), full(qcd.shape), full(cw.shape),
    ]
    args = [x2d, g1, w, gain, s64, s32, qcn, qcd, cw]
    aliases = {}
    if prev is not None:
        in_specs += [pl.BlockSpec(memory_space=pl.ANY)] * 2
        aliases = {len(args): 0, len(args) + 1: 2}
        args += list(prev)
    return pl.pallas_call(
        kern, grid=(n // tr,), in_specs=in_specs, out_specs=out_specs, out_shape=out_shape,
        input_output_aliases=aliases,
        compiler_params=pltpu.CompilerParams(dimension_semantics=("arbitrary",), vmem_limit_bytes=VMEM_LIMIT),
        name="proj",
    )(*args)


def _cmpfin_kernel(raw_ref, s64_ref, gain_ref, kc_ref, vc_ref, k_scr, v_scr, *, nsp):
    lane = lax.broadcasted_iota(jnp.int32, (nsp, LANES), 1)
    n = lax.broadcasted_iota(jnp.int32, (nsp, LANES), 0)
    k_scr[...] = raw_ref[0, :, 0:128]
    v_scr[...] = raw_ref[0, :, 128:256]
    for j in range(2):
        k = k_scr[pl.ds(j, nsp, stride=2), :]
        kn = k * lax.rsqrt(_group_meansq(k, s64_ref[...], 1.0 / HEAD_DIM) + EPS) * gain_ref[...]
        cend = n * L_SEL + (L_CMP - 1 + L_CMP * j)
        prow = _pos_rows(cend, lane)
        k0, k1 = _split_heads(kn, prow, lane)
        kc_ref[0, 0, j * nsp:(j + 1) * nsp, :] = k0.astype(BF16)
        kc_ref[1, 0, j * nsp:(j + 1) * nsp, :] = k1.astype(BF16)
        vc_ref[0, j * nsp:(j + 1) * nsp, :] = v_scr[pl.ds(j, nsp, stride=2), :].astype(BF16)


def _cmpfin(raw, s64, gain_kc):
    bx, n2, _ = raw.shape
    nsp = n2 // 2
    assert nsp % 16 == 0
    return pl.pallas_call(
        functools.partial(_cmpfin_kernel, nsp=nsp),
        grid=(bx,),
        in_specs=[pl.BlockSpec((1, n2, 256), lambda b: (b, 0, 0)),
                  pl.BlockSpec(s64.shape, lambda b: (0, 0)),
                  pl.BlockSpec(gain_kc.shape, lambda b: (0, 0))],
        out_specs=(pl.BlockSpec((NSA_KV, 1, n2, LANES), lambda b: (0, b, 0, 0)),
                   pl.BlockSpec((1, n2, LANES), lambda b: (b, 0, 0))),
        out_shape=(jax.ShapeDtypeStruct((NSA_KV, bx, n2, LANES), BF16),
                   jax.ShapeDtypeStruct((bx, n2, LANES), BF16)),
        scratch_shapes=[pltpu.VMEM((n2, LANES), F32), pltpu.VMEM((n2, LANES), F32)],
        name="cmpfin",
    )(raw, s64, gain_kc)


def _flash_init(c, m_ref, l_ref, acc_ref):
    m_ref[c] = jnp.full(m_ref.shape[1:], NEG, F32)
    l_ref[c] = jnp.zeros(l_ref.shape[1:], F32)
    acc_ref[c] = jnp.zeros(acc_ref.shape[1:], F32)


def _flash_probs(c, s, m_ref, l_ref):
    m_old = m_ref[c]
    m_new = jnp.maximum(m_old, jnp.max(s, axis=0, keepdims=True))
    p = jnp.exp2(s - m_new)
    a = jnp.exp2(m_old - m_new)
    l_ref[c] = a * l_ref[c] + jnp.sum(p, axis=0, keepdims=True)
    m_ref[c] = m_new
    return p.astype(BF16), a


def _flash_update(c, s, vt_tile, m_ref, l_ref, acc_ref):
    p, a = _flash_probs(c, s, m_ref, l_ref)
    acc_ref[c] = a * acc_ref[c] + _mm(vt_tile, p)


def _max_key_norm(k_ref_2d, n_keys):
    feat = lax.broadcasted_iota(jnp.int32, (1, LANES), 1) < HEAD_DIM

    def chunk(c, best):
        kf = k_ref_2d[pl.ds(pl.multiple_of(c * KEY_TILE, KEY_TILE), KEY_TILE), :].astype(F32)
        sq = jnp.sum(jnp.where(feat, kf * kf, 0.0), axis=1, keepdims=True)
        return jnp.maximum(best, jnp.max(sq, axis=0, keepdims=True))

    return jnp.sqrt(lax.fori_loop(0, n_keys // KEY_TILE, chunk, jnp.zeros((1, 1), F32)))


def _first_needed_tile(qa, m, k_norm, slope_row, n_max):
    feat = lax.broadcasted_iota(jnp.int32, (1, LANES), 1) < HEAD_DIM
    qf = qa.astype(F32)
    qsq = jnp.where(feat, qf * qf, 0.0)
    hi = qsq.astype(BF16)
    lo = (qsq - hi.astype(F32)).astype(BF16)
    ones = jnp.ones((8, LANES), BF16)
    q_norm = jnp.sqrt((_nt(ones, hi) + _nt(ones, lo))[0:1])
    cutoff = (m + UNDERFLOW_LOG2 - NORM_SLACK * q_norm * k_norm) / slope_row
    tile = jnp.floor(jnp.min(cutoff, axis=1, keepdims=True) * (1.0 / KEY_TILE))
    return jnp.clip(tile, 0.0, n_max.astype(F32)).astype(jnp.int32)[0, 0]


def _sweep_earlier_tiles(first, n_tiles, step):
    both = (0, 1)

    @pl.when(jnp.logical_and(n_tiles % 2 == 1, jnp.minimum(first[0], first[1]) < n_tiles))
    def _():
        step(pl.multiple_of((n_tiles - 1) * KEY_TILE, KEY_TILE), KEY_TILE, both)

    j_end = n_tiles // 2
    j0 = [jnp.minimum(f // 2, j_end) for f in first]
    j_both = jnp.maximum(j0[0], j0[1])

    def run(chains):
        def body(j, carry):
            step(pl.multiple_of(j * 2 * KEY_TILE, 2 * KEY_TILE), 2 * KEY_TILE, chains)
            return carry
        return body

    lax.fori_loop(j0[0], j_both, run((0,)), 0)
    lax.fori_loop(j0[1], j_both, run((1,)), 0)
    lax.fori_loop(j_both, j_end, run(both), 0)


def _masked_softmax_cols(s, valid):
    sm = jnp.where(valid, s, NEG)
    m = jnp.max(sm, axis=0, keepdims=True)
    p = jnp.where(valid, jnp.exp2(sm - m), 0.0)
    l = jnp.sum(p, axis=0, keepdims=True)
    return p * jnp.where(l > 0.0, 1.0 / l, 0.0)


def _nsa_kernel(q_ref, kc_ref, vc_ref, ks_ref, vs_ref, kw_ref, vw_ref, gt_ref, o_ref,
                sel_ref, acc_ref, m_ref, l_ref, kn_ref, *, q_base, nsp, tw, win_base):
    i = pl.program_id(1)
    q0 = q_base + i * Q_LANES
    ncol = NSA_GROUP * Q_LANES
    groups = range(NSA_KV)
    qas = [q_ref[g].reshape(ncol, LANES) for g in groups]
    lane_c = lax.broadcasted_iota(jnp.int32, (1, ncol), 1)
    qpos_c = q0 + (lane_c & (Q_LANES - 1))
    rows_of = lambda g: slice(g * HEAD_DIM, (g + 1) * HEAD_DIM)

    n_row = lax.broadcasted_iota(jnp.int32, (nsp, ncol), 0)
    valid_e = (n_row * L_SEL + (L_CMP - 1)) <= qpos_c
    valid_o = (n_row * L_SEL + (2 * L_CMP - 1)) <= qpos_c
    blk = lax.broadcasted_iota(jnp.int32, (nsp, Q_LANES), 0)
    qpos_q = q0 + lax.broadcasted_iota(jnp.int32, (nsp, Q_LANES), 1)
    cur = qpos_q >> 6
    bonus = jnp.where((blk == 0) | (blk == cur) | (blk == cur - 1), FORCE_BONUS, 0.0)
    in_range = blk <= cur
    blk_f = blk.astype(F32)
    o_c = []
    for g in groups:
        s_c = _nt(kc_ref[g, 0], qas[g])
        sm_e = jnp.where(valid_e, s_c[0:nsp], NEG)
        sm_o = jnp.where(valid_o, s_c[nsp:2 * nsp], NEG)
        m = jnp.maximum(jnp.max(sm_e, axis=0, keepdims=True), jnp.max(sm_o, axis=0, keepdims=True))
        p_e = jnp.where(valid_e, jnp.exp2(sm_e - m), 0.0)
        p_o = jnp.where(valid_o, jnp.exp2(sm_o - m), 0.0)
        l = jnp.sum(p_e, axis=0, keepdims=True) + jnp.sum(p_o, axis=0, keepdims=True)
        inv = jnp.where(l > 0.0, 1.0 / l, 0.0)
        p_e = p_e * inv
        p_o = p_o * inv
        pcat = jnp.concatenate([p_e, p_o], axis=0).astype(BF16)
        o_c.append(_tn(vc_ref[0], pcat)[rows_of(g)])

        imp = p_e[:, 0:Q_LANES] + p_o[:, 0:Q_LANES]
        for r in range(1, NSA_GROUP):
            imp = imp + p_e[:, r * Q_LANES:(r + 1) * Q_LANES] + p_o[:, r * Q_LANES:(r + 1) * Q_LANES]
        score = jnp.where(in_range, imp + bonus, -jnp.inf)
        sel = jnp.zeros((nsp, Q_LANES), F32)
        for _ in range(N_SELECT):
            top = jnp.max(score, axis=0, keepdims=True)
            first = jnp.min(jnp.where(score == top, blk_f, 1e9), axis=0, keepdims=True)
            pick = blk_f == first
            sel = jnp.where(pick, jnp.where(top > -jnp.inf, 1.0, sel), sel)
            score = jnp.where(pick, -jnp.inf, score)
        sel_ref[g] = sel

    @pl.when(i == 0)
    def _():
        for g in groups:
            kn_ref[g] = jnp.broadcast_to(_max_key_norm(ks_ref.at[g, 0], ks_ref.shape[2]), kn_ref.shape[1:])

    for g in groups:
        _flash_init(g, m_ref, l_ref, acc_ref)
    def sel_tile(k0, size, which):
        n_blk = size // L_SEL
        causal = (k0 + lax.broadcasted_iota(jnp.int32, (size, Q_LANES), 0)
                  <= q0 + lax.broadcasted_iota(jnp.int32, (size, Q_LANES), 1))
        for g in which:
            vt = vs_ref[0, 0, rows_of(g), pl.ds(k0, size)].astype(BF16)
            s = _nt(ks_ref[g, 0, pl.ds(k0, size), :], qas[g])
            sel_rows = sel_ref[g, pl.ds(pl.multiple_of(k0 // L_SEL, 8), n_blk), :]
            selt = jnp.concatenate(
                [jnp.broadcast_to(sel_rows[j:j + 1, :], (L_SEL, Q_LANES)) for j in range(n_blk)], axis=0)
            valid = jnp.where(causal, selt, 0.0) > 0.0
            s = jnp.concatenate(
                [jnp.where(valid, s[:, r * Q_LANES:(r + 1) * Q_LANES], NEG) for r in range(NSA_GROUP)], axis=1)
            _flash_update(g, s, vt, m_ref, l_ref, acc_ref)

    n_before = (q0 + Q_LANES + KEY_TILE - 1) // KEY_TILE - 1
    sel_tile(pl.multiple_of(n_before * KEY_TILE, KEY_TILE), KEY_TILE, groups)
    first = []
    for g in groups:
        slope = jnp.concatenate([jnp.full((1, Q_LANES), LOG2E * 2.0 ** -(g * NSA_GROUP + r + 1), F32)
                                 for r in range(NSA_GROUP)], axis=1)
        first.append(_first_needed_tile(qas[g], m_ref[g], kn_ref[g][0:1, 0:1], slope, n_before))
    _sweep_earlier_tiles(first, n_before, sel_tile)

    w0 = jnp.clip(q0 - win_base - WINDOW, 0, tw - WIN_ROWS)
    w0 = pl.multiple_of(w0, Q_LANES)
    kpos = win_base + w0 + lax.broadcasted_iota(jnp.int32, (WIN_ROWS, ncol), 0)
    dist = qpos_c - kpos
    in_window = (dist >= 0) & (dist < WINDOW)

    outs = []
    for g in groups:
        l_s = l_ref[g]
        o_s = acc_ref[g] * jnp.where(l_s > 0.0, 1.0 / l_s, 0.0)
        p_w = _masked_softmax_cols(_nt(kw_ref[g, 0, pl.ds(w0, WIN_ROWS), :], qas[g]), in_window)
        o_w = _mm(vw_ref[0, rows_of(g), pl.ds(w0, WIN_ROWS)].astype(BF16), p_w.astype(BF16))
        gt = gt_ref[g]
        for r in range(NSA_GROUP):
            cs = slice(r * Q_LANES, (r + 1) * Q_LANES)
            outs.append(gt[r:r + 1, :] * o_c[g][:, cs] + gt[4 + r:5 + r, :] * o_s[:, cs]
                        + gt[8 + r:9 + r, :] * o_w[:, cs])
    o_ref[...] = jnp.concatenate(outs, axis=0).T


def _nsa(qa, kc, vc, ksa, nsa_t, kwa, win_t, gt, *, layer, bx, nqb, q_base, win_base):
    nq = qa.shape[2]
    n2 = kc.shape[2]
    tk = ksa.shape[2]
    tw = kwa.shape[2]
    assert nq == bx * nqb * Q_LANES and tk % KEY_TILE == 0 and tw >= WIN_ROWS
    assert nsa_t.shape[2:] == (512, tk) and win_t.shape[1:] == (256, tw)
    kern = functools.partial(_nsa_kernel, q_base=q_base, nsp=n2 // 2, tw=tw, win_base=win_base)
    ncol = NSA_GROUP * Q_LANES
    return pl.pallas_call(
        kern, grid=(bx, nqb),
        in_specs=[
            pl.BlockSpec((NSA_KV, NSA_GROUP, Q_LANES, LANES), lambda b, i: (0, 0, b * nqb + i, 0)),
            pl.BlockSpec((NSA_KV, 1, n2, LANES), lambda b, i: (0, b, 0, 0)),
            pl.BlockSpec((1, n2, LANES), lambda b, i: (b, 0, 0)),
            pl.BlockSpec((NSA_KV, 1, tk, LANES), lambda b, i: (0, b, 0, 0)),
            pl.BlockSpec((1, 1, LANES, tk), lambda b, i: (layer, b, 3, 0)),
            pl.BlockSpec((NSA_KV, 1, tw, LANES), lambda b, i: (0, b, 0, 0)),
            pl.BlockSpec((1, LANES, tw), lambda b, i: (b, 1, 0)),
            pl.BlockSpec((NSA_KV, 16, Q_LANES), lambda b, i: (0, 0, b * nqb + i)),
        ],
        out_specs=pl.BlockSpec((Q_LANES, NSA_HEADS * HEAD_DIM), lambda b, i: (b * nqb + i, 0)),
        out_shape=jax.ShapeDtypeStruct((nq, NSA_HEADS * HEAD_DIM), F32),
        scratch_shapes=[pltpu.VMEM((NSA_KV, n2 // 2, Q_LANES), F32), pltpu.VMEM((NSA_KV, HEAD_DIM, ncol), F32),
                        pltpu.VMEM((NSA_KV, 1, ncol), F32), pltpu.VMEM((NSA_KV, 1, ncol), F32),
                        pltpu.VMEM((NSA_KV, 8, LANES), F32)],
        compiler_params=pltpu.CompilerParams(dimension_semantics=("arbitrary",) * 2, vmem_limit_bytes=VMEM_LIMIT),
        name="nsa",
    )(qa, kc, vc, ksa, nsa_t, kwa, win_t, gt)


def _diff_kernel(q_ref, k_ref, v_ref, sl_ref, lp_ref, sg_ref, o_ref, acc_ref, m_ref, l_ref, kn_ref, *,
                 q_base, tq, lam_init):
    pair = pl.program_id(1)
    i = pl.program_id(2)
    q0 = q_base + i * tq
    ncol = 2 * tq
    heads = range(2)
    lp = lp_ref[...]
    lam = (jnp.exp(jnp.sum(lp[0:1] * lp[1:2], keepdims=True)) - jnp.exp(jnp.sum(lp[2:3] * lp[3:4], keepdims=True))
           + lam_init)
    n_full = q0 // KEY_TILE
    qas = [q_ref[hh].reshape(ncol, LANES) for hh in heads]
    for hh in heads:
        _flash_init(hh, m_ref, l_ref, acc_ref)

    @pl.when(i == 0)
    def _():
        for hh in heads:
            kn_ref[hh] = jnp.broadcast_to(_max_key_norm(k_ref.at[hh, 0], k_ref.shape[2]), kn_ref.shape[1:])

    def step(k0, size, which, causal=None):
        for hh in which:
            vt = v_ref[0, 0, hh * HEAD_DIM:(hh + 1) * HEAD_DIM, pl.ds(k0, size)].astype(BF16)
            s = _nt(k_ref[hh, 0, pl.ds(k0, size), :], qas[hh])
            if causal is not None:
                s = jnp.where(causal, s, NEG)
            _flash_update(hh, s, vt, m_ref, l_ref, acc_ref)

    k_diag = pl.multiple_of(n_full * KEY_TILE, KEY_TILE)
    tok_row = lax.broadcasted_iota(jnp.int32, (KEY_TILE, ncol), 0)
    qpos = q0 + lax.broadcasted_iota(jnp.int32, (KEY_TILE, ncol), 1) % tq
    step(k_diag, KEY_TILE, heads, k_diag + tok_row <= qpos)

    first = []
    for hh in heads:
        slope = jnp.concatenate([sl_ref[pl.ds(2 * pair + hh, 1), :]] * (ncol // LANES), axis=1)
        first.append(_first_needed_tile(qas[hh], m_ref[hh], kn_ref[hh][0:1, 0:1], slope, n_full))
    _sweep_earlier_tiles(first, n_full, step)

    outs = []
    for hh in heads:
        o = acc_ref[hh] / l_ref[hh]
        d = o[:, 0:tq] - lam * o[:, tq:ncol]
        d = d * lax.rsqrt(jnp.mean(d * d, axis=0, keepdims=True) + EPS) * sg_ref[:, 0:tq] * (1.0 - lam_init)
        outs.append(d)
    o_ref[...] = jnp.concatenate(outs, axis=0).T


def _diff(dqa, dka, dif_t, slopes, lp, sg, *, layer, bx, nqb, tq, q_base, lam_init):
    nq = dqa.shape[2]
    tk = dka.shape[2]
    assert nq == bx * nqb * tq and tk % KEY_TILE == 0 and KEY_TILE % tq == 0 and dif_t.shape[2:] == (1024, tk)
    kern = functools.partial(_diff_kernel, q_base=q_base, tq=tq, lam_init=lam_init)
    return pl.pallas_call(
        kern, grid=(bx, DIFF_HEADS // 2, nqb),
        in_specs=[
            pl.BlockSpec((2, 2, tq, LANES), lambda b, h, i: (h, 0, b * nqb + i, 0)),
            pl.BlockSpec((2, 1, tk, LANES), lambda b, h, i: (h, b, 0, 0)),
            pl.BlockSpec((1, 1, LANES, tk), lambda b, h, i: (layer, b, 4 + h, 0)),
            pl.BlockSpec(slopes.shape, lambda b, h, i: (0, 0)),
            pl.BlockSpec(lp.shape, lambda b, h, i: (0, 0)),
            pl.BlockSpec(sg.shape, lambda b, h, i: (0, 0)),
        ],
        out_specs=pl.BlockSpec((tq, LANES), lambda b, h, i: (b * nqb + i, h)),
        out_shape=jax.ShapeDtypeStruct((nq, DIFF_HEADS * HEAD_DIM), F32),
        scratch_shapes=[pltpu.VMEM((2, HEAD_DIM, 2 * tq), F32), pltpu.VMEM((2, 1, 2 * tq), F32),
                        pltpu.VMEM((2, 1, 2 * tq), F32), pltpu.VMEM((2, 8, LANES), F32)],
        compiler_params=pltpu.CompilerParams(dimension_semantics=("arbitrary",) * 3, vmem_limit_bytes=VMEM_LIMIT),
        name="diff",
    )(dqa, dka, dif_t, slopes, lp, sg)


F_CHUNKS = 2


def _ffn_core(x, on, od, wo_ref, g2_ref, wg_ref, wu_ref, cw_ref, cb_ref, wd_ref, prev_rows, g_store):
    o = jnp.concatenate([on, od], axis=1).astype(BF16)
    xm = x + _mm(o, wo_ref[...])
    h2 = (xm * lax.rsqrt(jnp.mean(xm * xm, axis=-1, keepdims=True) + EPS) * g2_ref[...]).astype(BF16)
    d_ff = wg_ref.shape[1]
    fc = d_ff // F_CHUNKS
    y = jnp.zeros(x.shape, F32)
    for c in range(F_CHUNKS):
        c0, c1 = c * fc, (c + 1) * fc
        g = _mm(h2, wg_ref[:, c0:c1])
        u = _mm(h2, wu_ref[:, c0:c1])
        gm1, gm2 = prev_rows(g, c0, c1)
        g_store(g, c0, c1)
        gc = cb_ref[:, c0:c1] + cw_ref[0:1, c0:c1] * gm2 + cw_ref[1:2, c0:c1] * gm1 + cw_ref[2:3, c0:c1] * g
        act = gc * (1.0 / (1.0 + jnp.exp(-gc))) * u
        y = y + _mm(act.astype(BF16), wd_ref[c0:c1, :])
    return xm + y


def _ffn_prompt_kernel(x_ref, on_ref, od_ref, wo_ref, g2_ref, wg_ref, wu_ref, cw_ref, cb_ref, wd_ref,
                       y_ref, cv_ref, carry_ref, *, tiles_per_seq):
    i = pl.program_id(0)
    tr = x_ref.shape[0]

    @pl.when(i % tiles_per_seq == 0)
    def _():
        carry_ref[...] = jnp.zeros(carry_ref.shape, F32)

    def prev_rows(g, c0, c1):
        row = lax.broadcasted_iota(jnp.int32, g.shape, 0)
        p1 = carry_ref[7:8, c0:c1]
        p2 = carry_ref[6:7, c0:c1]
        gm1 = jnp.where(row == 0, p1, pltpu.roll(g, 1, 0))
        gm2 = jnp.where(row == 0, p2, jnp.where(row == 1, p1, pltpu.roll(g, 2, 0)))
        return gm1, gm2

    def g_store(g, c0, c1):
        carry_ref[:, c0:c1] = g[tr - 8:tr]
        cv_ref[0, :, c0:c1] = g[tr - 8:tr]

    y_ref[...] = _ffn_core(x_ref[...], on_ref[...], od_ref[...], wo_ref, g2_ref, wg_ref, wu_ref, cw_ref, cb_ref,
                           wd_ref, prev_rows, g_store)


def _ffn_sample_kernel(x_ref, on_ref, od_ref, st1_ref, st2_ref, wo_ref, g2_ref, wg_ref, wu_ref, cw_ref, cb_ref,
                       wd_ref, y_ref, g_ref, *, seq):
    def prev_rows(g, c0, c1):
        rs = lax.broadcasted_iota(jnp.int32, g.shape, 0) % seq
        s1 = st1_ref[:, c0:c1]
        gm1 = jnp.where(rs == 0, s1, pltpu.roll(g, 1, 0))
        gm2 = jnp.where(rs == 0, st2_ref[:, c0:c1], jnp.where(rs == 1, s1, pltpu.roll(g, 2, 0)))
        return gm1, gm2

    def g_store(g, c0, c1):
        g_ref[:, c0:c1] = g

    y_ref[...] = _ffn_core(x_ref[...], on_ref[...], od_ref[...], wo_ref, g2_ref, wg_ref, wu_ref, cw_ref, cb_ref,
                           wd_ref, prev_rows, g_store)


def _const_spec(a):
    return pl.BlockSpec(a.shape, lambda i: (0,) * a.ndim, pipeline_mode=pl.Buffered(1))


def _ffn_prompt(x2d, on, od, wo, g2, wg, wu, cw, cb, wd, *, seq_len, tr=256):
    n, d = x2d.shape
    f = wg.shape[1]
    assert n % tr == 0 and seq_len % tr == 0
    tps = seq_len // tr
    row = lambda w: pl.BlockSpec((tr, w), lambda i: (i, 0))
    return pl.pallas_call(
        functools.partial(_ffn_prompt_kernel, tiles_per_seq=tps), grid=(n // tr,),
        in_specs=[row(d), row(on.shape[1]), row(od.shape[1])] + [_const_spec(a) for a in (wo, g2, wg, wu, cw, cb, wd)],
        out_specs=(row(d), pl.BlockSpec((1, 8, f), lambda i: (i // tps, 0, 0))),
        out_shape=(jax.ShapeDtypeStruct((n, d), F32), jax.ShapeDtypeStruct((n // seq_len, 8, f), F32)),
        scratch_shapes=[pltpu.VMEM((8, f), F32)],
        compiler_params=pltpu.CompilerParams(dimension_semantics=("arbitrary",), vmem_limit_bytes=VMEM_LIMIT),
        name="ffn_prompt",
    )(x2d, on, od, wo, g2, wg, wu, cw, cb, wd)


def _ffn_sample(x2d, on, od, st1, st2, wo, g2, wg, wu, cw, cb, wd, *, seq):
    n, d = x2d.shape
    f = wg.shape[1]
    args = (x2d, on, od, st1, st2, wo, g2, wg, wu, cw, cb, wd)
    return pl.pallas_call(
        functools.partial(_ffn_sample_kernel, seq=seq), grid=(1,),
        in_specs=[_const_spec(a) for a in args],
        out_specs=(pl.BlockSpec((n, d), lambda i: (0, 0)), pl.BlockSpec((n, f), lambda i: (0, 0))),
        out_shape=(jax.ShapeDtypeStruct((n, d), F32), jax.ShapeDtypeStruct((n, f), F32)),
        compiler_params=pltpu.CompilerParams(dimension_semantics=("arbitrary",), vmem_limit_bytes=VMEM_LIMIT),
        name="ffn_sample",
    )(*args)


PAGES_PER_STEP = 16
STEPS_PER_CMP_TILE = LANES // (PAGES_PER_STEP * LANES // L_CMP)
T_NSA, T_WIN, T_DIF, T_ROWS = 0, 512, 768, 1792


def _softmax_rows(s):
    m = jnp.max(s, axis=-1, keepdims=True)
    p = jnp.exp2(s - m)
    return p * (1.0 / jnp.sum(p, axis=-1, keepdims=True))


def _masked_softmax_rows(s, valid):
    sm = jnp.where(valid, s, NEG)
    m = jnp.max(sm, axis=-1, keepdims=True)
    p = jnp.where(valid, jnp.exp2(sm - m), 0.0)
    l = jnp.sum(p, axis=-1, keepdims=True)
    return p * jnp.where(l > 0.0, 1.0 / l, 0.0)


def _sattn_kernel(pt_ref, *refs, n_steps, past_len, dec, lam_init):
    pg = PAGES_PER_STEP
    nsa_k, dif_k, nsa_v, dif_v = (refs[i * pg:(i + 1) * pg] for i in range(4))
    (win_ref, nn_ref, nw_ref, nd_ref, qn_ref, qd_ref, gate_ref, sn_ref, sd_ref, wck_ref, wcv_ref, gkc_ref,
     lp_ref, sg_ref, o_ref, ns_ref,
     ssel, sdif, psel, pdif, kc_scr, tail_scr, oc_scr, ow_scr, on_acc, od_acc) = refs[4 * pg:]
    s = pl.program_id(1)
    span = pg * LANES
    groups, heads = range(NSA_KV), range(DIFF_HEADS)
    cat = lambda parts: jnp.concatenate(parts, axis=1)
    hi_lo = lambda x: (x.astype(BF16), (x - x.astype(BF16).astype(F32)).astype(BF16))

    @pl.when(s < n_steps)
    def _():
        @pl.when(s == 0)
        def _():
            kc_scr[...] = jnp.zeros(kc_scr.shape, F32)

        shift = (s % STEPS_PER_CMP_TILE) * (span // L_CMP)
        col = pl.multiple_of((s // STEPS_PER_CMP_TILE) * LANES, LANES)
        for part, w_ref in ((0, wck_ref), (1, wcv_ref)):
            x_hi, x_lo = hi_lo(cat([r[0, 0, part * 128:(part + 1) * 128, :] for r in nsa_k]))
            blk = _mm(x_hi, w_ref[0]) + _mm(x_lo, w_ref[0]) + _mm(x_hi, w_ref[1])
            kc_scr[part * 128:(part + 1) * 128, pl.ds(col, LANES)] += pltpu.roll(blk, shift, 1)

        k0 = pl.multiple_of(s * span, span)
        kpos = (k0 + lax.broadcasted_iota(jnp.int32, (1, span), 1)).astype(F32)
        for g in groups:
            k8 = cat([r[0, 0, 256 + g * HEAD_DIM:256 + (g + 1) * HEAD_DIM, :] for r in nsa_k]).astype(BF16)
            ssel[g, :, pl.ds(k0, span)] = _mm(qn_ref[0, g], k8) + cat([sn_ref[g]] * pg) * kpos
        for h in heads:
            k8 = cat([r[0, 0, h * HEAD_DIM:(h + 1) * HEAD_DIM, :] for r in dif_k]).astype(BF16)
            sdif[h, :, pl.ds(k0, span)] = _mm(qd_ref[0, h], k8) + cat([sd_ref[h]] * pg) * kpos

    @pl.when(s == n_steps)
    def _():
        new_rows = cat([nn_ref[...], nw_ref[...], nd_ref[...]])
        tail_scr[...] = jnp.concatenate([new_rows, jnp.zeros((LANES - dec, T_ROWS), F32)], axis=0).T
        nrow, drow = NSA_GROUP * dec, 2 * dec
        tail_pos = (past_len + lax.broadcasted_iota(jnp.int32, (1, LANES), 1)).astype(F32)
        causal32 = (lax.broadcasted_iota(jnp.int32, (nrow, LANES), 1)
                    <= lax.broadcasted_iota(jnp.int32, (nrow, LANES), 0) % dec)
        causal16 = (lax.broadcasted_iota(jnp.int32, (drow, LANES), 1)
                    <= lax.broadcasted_iota(jnp.int32, (drow, LANES), 0) % dec)

        for g in groups:
            kt = tail_scr[T_NSA + 256 + g * HEAD_DIM:T_NSA + 256 + (g + 1) * HEAD_DIM, :].astype(BF16)
            sc = _mm(qn_ref[0, g], kt) + sn_ref[g] * tail_pos
            ssel[g, :, past_len:past_len + LANES] = jnp.where(causal32, sc, NEG)
        for h in heads:
            kt = tail_scr[T_DIF + h * HEAD_DIM:T_DIF + (h + 1) * HEAD_DIM, :].astype(BF16)
            sc = _mm(qd_ref[0, h], kt) + sd_ref[h] * tail_pos
            sdif[h, :, past_len:past_len + LANES] = jnp.where(causal16, sc, NEG)

        n_cmp = kc_scr.shape[1]
        cend = lax.broadcasted_iota(jnp.int32, (nrow, n_cmp), 1) * L_CMP + (L_CMP - 1)
        qpos_c = past_len + lax.broadcasted_iota(jnp.int32, (nrow, n_cmp), 0) % dec
        imps = []
        for g in groups:
            kc = kc_scr[g * HEAD_DIM:(g + 1) * HEAD_DIM, :]
            kc = kc * lax.rsqrt(jnp.mean(kc * kc, axis=0, keepdims=True) + EPS) * cat([gkc_ref[...]] * (n_cmp // LANES))
            s_c = _mm(qn_ref[0, g], kc.astype(BF16)) + cat([sn_ref[g]] * (n_cmp // LANES)) * cend.astype(F32)
            p_c = _masked_softmax_rows(s_c, cend <= qpos_c)
            vc = kc_scr[128 + g * HEAD_DIM:128 + (g + 1) * HEAD_DIM, :].astype(BF16)
            oc_scr[g] = _nt(p_c.astype(BF16), vc)
            imps.append(sum(p_c[r * dec:(r + 1) * dec] for r in range(NSA_GROUP)))
        imp2 = jnp.concatenate(imps, axis=0)
        pair = (lax.broadcasted_iota(jnp.int32, (n_cmp, LANES), 0) // 2
                == lax.broadcasted_iota(jnp.int32, (n_cmp, LANES), 1)).astype(BF16)
        i_hi, i_lo = hi_lo(imp2)
        imp = _mm(i_hi, pair) + _mm(i_lo, pair)

        n_blk = past_len // L_SEL
        blk = lax.broadcasted_iota(jnp.int32, imp.shape, 1)
        blk_f = blk.astype(F32)
        bonus = jnp.where((blk == 0) | (blk == n_blk - 1), FORCE_BONUS, 0.0)
        score = jnp.where(blk < n_blk, imp + bonus, -jnp.inf)
        sel = jnp.zeros(imp.shape, F32)
        for _ in range(N_SELECT - 1):
            top = jnp.max(score, axis=1, keepdims=True)
            first = jnp.min(jnp.where(score == top, blk_f, 1e9), axis=1, keepdims=True)
            pick = blk_f == first
            sel = jnp.where(pick, jnp.where(top > -jnp.inf, 1.0, sel), sel)
            score = jnp.where(pick, -jnp.inf, score)
        sel = sel.astype(BF16)

        for c in range(n_steps):
            tok = c * span + lax.broadcasted_iota(jnp.int32, (LANES, span), 1)
            expand = ((tok >> 6) == lax.broadcasted_iota(jnp.int32, (LANES, span), 0)).astype(BF16)
            chosen = _mm(sel, expand)
            for g in groups:
                keep = jnp.concatenate([chosen[g * dec:(g + 1) * dec]] * NSA_GROUP, axis=0) > 0.5
                ssel[g, :, c * span:(c + 1) * span] = jnp.where(keep, ssel[g, :, c * span:(c + 1) * span], NEG)
        for g in groups:
            psel[g] = _softmax_rows(ssel[g]).astype(BF16)
        for h in heads:
            pdif[h] = _softmax_rows(sdif[h]).astype(BF16)

        w_buf = win_ref.shape[3]
        kwpos = past_len - w_buf + lax.broadcasted_iota(jnp.int32, (nrow, w_buf + LANES), 1)
        dist = past_len + lax.broadcasted_iota(jnp.int32, (nrow, w_buf + LANES), 0) % dec - kwpos
        in_window = (dist >= 0) & (dist < WINDOW)
        for g in groups:
            kw = cat([win_ref[0, 0, g * HEAD_DIM:(g + 1) * HEAD_DIM, :],
                      tail_scr[T_WIN + g * HEAD_DIM:T_WIN + (g + 1) * HEAD_DIM, :]]).astype(BF16)
            vw = cat([win_ref[0, 0, 128 + g * HEAD_DIM:128 + (g + 1) * HEAD_DIM, :],
                      tail_scr[T_WIN + 128 + g * HEAD_DIM:T_WIN + 128 + (g + 1) * HEAD_DIM, :]]).astype(BF16)
            s_w = _mm(qn_ref[0, g], kw) + cat([sn_ref[g]] * (w_buf // LANES + 1)) * kwpos.astype(F32)
            ow_scr[g] = _nt(_masked_softmax_rows(s_w, in_window).astype(BF16), vw)

        shifted = pltpu.roll(win_ref[0, 0], w_buf - dec, 1)
        fresh = cat([jnp.zeros((256, w_buf - LANES), F32), pltpu.roll(tail_scr[T_WIN:T_WIN + 256, :], LANES - dec, 1)])
        ns_ref[0] = jnp.where(lax.broadcasted_iota(jnp.int32, (256, w_buf), 1) >= w_buf - dec, fresh, shifted)

        on_acc[...] = jnp.zeros(on_acc.shape, F32)
        od_acc[...] = jnp.zeros(od_acc.shape, F32)

    @pl.when(s >= n_steps)
    def _():
        k0 = pl.multiple_of((s - n_steps) * span, span)
        for g in groups:
            v8 = cat([r[0, 0, g * HEAD_DIM:(g + 1) * HEAD_DIM, :] for r in nsa_v]).astype(BF16)
            on_acc[g] += _nt(psel[g, :, pl.ds(k0, span)], v8)
        for h in heads:
            v8 = cat([r[0, 0, h * HEAD_DIM:(h + 1) * HEAD_DIM, :] for r in dif_v]).astype(BF16)
            od_acc[h] += _nt(pdif[h, :, pl.ds(k0, span)], v8)

    @pl.when(s == 2 * n_steps - 1)
    def _():
        lp = lp_ref[...]
        lam = (jnp.exp(jnp.sum(lp[0:1] * lp[1:2], keepdims=True))
               - jnp.exp(jnp.sum(lp[2:3] * lp[3:4], keepdims=True)) + lam_init)
        pieces = []
        for g in groups:
            vt = tail_scr[T_NSA + 384 + g * HEAD_DIM:T_NSA + 384 + (g + 1) * HEAD_DIM, :].astype(BF16)
            o_s = on_acc[g] + _nt(psel[g, :, past_len:past_len + LANES], vt)
            o = gate_ref[0, g, 0] * oc_scr[g] + gate_ref[0, g, 1] * o_s + gate_ref[0, g, 2] * ow_scr[g]
            pieces += [o[r * dec:(r + 1) * dec] for r in range(NSA_GROUP)]
        for h in heads:
            vt = tail_scr[T_DIF + 512 + h * HEAD_DIM:T_DIF + 512 + (h + 1) * HEAD_DIM, :].astype(BF16)
            o = od_acc[h] + _nt(pdif[h, :, past_len:past_len + LANES], vt)
            d = o[0:dec] - lam * o[dec:2 * dec]
            pieces.append(d * lax.rsqrt(jnp.mean(d * d, axis=-1, keepdims=True) + EPS) * sg_ref[...] * (1.0 - lam_init))
        o_ref[0] = cat(pieces)


def _sattn(page_table, cache_nsa, cache_diff, state_win, new_nsa, new_win, new_dif, qn, qd, gates, sn, sd,
           wck, wcv, gkc, lp, sg, *, layer, lam_init):
    bs, n_pages = page_table.shape
    page = cache_nsa.shape[3]
    dec = new_nsa.shape[0] // bs
    past_len = n_pages * page
    pg = PAGES_PER_STEP
    n_steps = n_pages // pg
    w_buf = state_win.shape[3]
    assert page == LANES and n_pages % pg == 0 and dec == 8 and w_buf % LANES == 0
    n_cmp = -(-n_steps // STEPS_PER_CMP_TILE) * LANES
    tks = past_len + LANES
    pt = page_table.reshape(-1)

    def kpage(j):
        return lambda b, s, pt_ref: (layer, pt_ref[b * n_pages + jnp.minimum(s, n_steps - 1) * pg + j], 0, 0)

    def vpage_blk(j, blk):
        return lambda b, s, pt_ref: (layer, pt_ref[b * n_pages + jnp.maximum(s - n_steps, 0) * pg + j], blk, 0)

    const = lambda a: pl.BlockSpec(a.shape, lambda b, s, pt_ref: (0,) * a.ndim)
    per_b = lambda a: pl.BlockSpec((1,) + a.shape[1:], lambda b, s, pt_ref: (b,) + (0,) * (a.ndim - 1))
    in_specs = (
        [pl.BlockSpec((1, 1, 384, page), kpage(j)) for j in range(pg)]
        + [pl.BlockSpec((1, 1, 512, page), kpage(j)) for j in range(pg)]
        + [pl.BlockSpec((1, 1, 128, page), vpage_blk(j, 3)) for j in range(pg)]
        + [pl.BlockSpec((1, 1, 512, page), vpage_blk(j, 1)) for j in range(pg)]
        + [pl.BlockSpec((1, 1, 256, w_buf), lambda b, s, pt_ref: (layer, b, 0, 0)),
           pl.BlockSpec((dec, 512), lambda b, s, pt_ref: (b, 0)),
           pl.BlockSpec((dec, 256), lambda b, s, pt_ref: (b, 0)),
           pl.BlockSpec((dec, 1024), lambda b, s, pt_ref: (b, 0)),
           per_b(qn), per_b(qd), per_b(gates), const(sn), const(sd), const(wck), const(wcv), const(gkc),
           const(lp), const(sg)])
    out_specs = (pl.BlockSpec((1, dec, 1024), lambda b, s, pt_ref: (b, 0, 0)),
                 pl.BlockSpec((1, 256, w_buf), lambda b, s, pt_ref: (b, 0, 0)))
    scratch = [
        pltpu.VMEM((NSA_KV, NSA_GROUP * dec, tks), F32), pltpu.VMEM((DIFF_HEADS, 2 * dec, tks), F32),
        pltpu.VMEM((NSA_KV, NSA_GROUP * dec, tks), BF16), pltpu.VMEM((DIFF_HEADS, 2 * dec, tks), BF16),
        pltpu.VMEM((256, n_cmp), F32), pltpu.VMEM((T_ROWS, LANES), F32),
        pltpu.VMEM((NSA_KV, NSA_GROUP * dec, HEAD_DIM), F32), pltpu.VMEM((NSA_KV, NSA_GROUP * dec, HEAD_DIM), F32),
        pltpu.VMEM((NSA_KV, NSA_GROUP * dec, HEAD_DIM), F32), pltpu.VMEM((DIFF_HEADS, 2 * dec, HEAD_DIM), F32),
    ]
    grid_spec = pltpu.PrefetchScalarGridSpec(num_scalar_prefetch=1, grid=(bs, 2 * n_steps), in_specs=in_specs,
                                             out_specs=out_specs, scratch_shapes=scratch)
    caches = [cache_nsa] * pg + [cache_diff] * pg + [cache_nsa] * pg + [cache_diff] * pg
    return pl.pallas_call(
        functools.partial(_sattn_kernel, n_steps=n_steps, past_len=past_len, dec=dec, lam_init=lam_init),
        grid_spec=grid_spec,
        out_shape=(jax.ShapeDtypeStruct((bs, dec, 1024), F32), jax.ShapeDtypeStruct((bs, 256, w_buf), F32)),
        compiler_params=pltpu.CompilerParams(dimension_semantics=("arbitrary",) * 2, vmem_limit_bytes=VMEM_LIMIT),
        name="sattn",
    )(pt, *caches, state_win, new_nsa, new_win, new_dif, qn, qd, gates, sn, sd, wck, wcv, gkc, lp, sg)


def _alibi_slopes(n):
    return 2.0 ** (-8.0 * jnp.arange(1, n + 1, dtype=F32) / n)


def _slope_rows(slopes):
    slopes = slopes * LOG2E
    s0 = slopes.astype(BF16).astype(F32)
    s1 = (slopes - s0).astype(BF16).astype(F32)
    s2 = (slopes - s0 - s1).astype(BF16).astype(F32)
    cols = jnp.stack([64.0 * s0, s0, 64.0 * s1, s1, 64.0 * s2, s2], axis=1)
    out = jnp.zeros((slopes.shape[0], LANES), F32)
    return out.at[:, AUG0:AUG0 + N_AUG].set(cols)


def _block_ones(group):
    idx = np.arange(LANES) // group
    return jnp.asarray((idx[:, None] == idx[None, :]).astype(np.float32), dtype=BF16)


_GATE_SRC = np.full((LANES,), -1, np.int64)
for _g in range(NSA_KV):
    for _r in range(NSA_GROUP):
        for _j in range(3):
            _GATE_SRC[_g * 16 + _j * 4 + _r] = 1280 + _g * 12 + _r * 3 + _j


def _permute_w_in(w):
    main = jnp.concatenate([w[:, 0:1280], w[:, 1304:2840]], axis=1)
    gate = jnp.where(jnp.asarray(_GATE_SRC >= 0)[None, :], w[:, np.maximum(_GATE_SRC, 0)], 0.0)
    return jnp.concatenate([main, gate], axis=1).astype(BF16)


def _compress_weights(w):
    t = np.arange(PAGES_PER_STEP * LANES)
    place = jnp.asarray(t[:, None] // L_CMP == np.arange(LANES)[None, :])
    full = jnp.where(place, jnp.tile(w, PAGES_PER_STEP * LANES // L_CMP)[:, None], 0.0)
    hi = full.astype(BF16)
    return jnp.stack([hi, (full - hi.astype(F32)).astype(BF16)])


def _gain_row(nsa_qg, nsa_kg, diff_qg, diff_kg):
    one = lambda n: jnp.ones((n,), F32)
    parts = [jnp.tile(nsa_qg, NSA_HEADS), one(256), jnp.tile(nsa_kg[1], NSA_KV), one(128),
             jnp.tile(nsa_kg[2], NSA_KV), one(128), jnp.tile(diff_qg.reshape(-1), DIFF_HEADS),
             jnp.tile(diff_kg.reshape(-1), DIFF_HEADS), one(512 + LANES)]
    return jnp.concatenate(parts)[None, :]


def kernel(x_prompt, x_sample, cache_nsa, cache_diff, state_win, state_conv, page_table, norm1_g, norm2_g, w_in, w_out, nsa_qnorm_g, nsa_knorm_g, nsa_cmp_w, diff_qnorm_g, diff_knorm_g, diff_lambda, diff_subnorm_g, w_gate, w_up, conv_w, conv_b, w_down):
    batch, seq, d_model = x_prompt.shape
    dec_batch, dec_seq, _ = x_sample.shape
    depth, n_pool, page = cache_nsa.shape[:3]
    n_pages = page_table.shape[1]
    past_len = n_pages * page
    w_buf = state_win.shape[2]
    d_ff = w_gate.shape[2]
    assert dec_seq < L_CMP and seq % KEY_TILE == 0 and w_buf == WINDOW

    s64, s32 = _block_ones(HEAD_DIM), _block_ones(DIFF_HALF)
    qcn = _slope_rows(_alibi_slopes(NSA_HEADS))
    qcd = _slope_rows(_alibi_slopes(DIFF_HEADS))
    fmaj = lambda a: jnp.moveaxis(a, 2, -1).reshape(a.shape[0], a.shape[1], -1, a.shape[2])
    cache_nsa4 = fmaj(cache_nsa)
    cache_diff4 = fmaj(cache_diff)
    state_win4 = fmaj(state_win)

    sn = jnp.broadcast_to((_alibi_slopes(NSA_HEADS) * LOG2E).reshape(NSA_KV, NSA_GROUP, 1, 1),
                          (NSA_KV, NSA_GROUP, dec_seq, LANES)).reshape(NSA_KV, NSA_GROUP * dec_seq, LANES)
    sd = jnp.broadcast_to((_alibi_slopes(DIFF_HEADS) * LOG2E).reshape(DIFF_HEADS, 1, 1), (DIFF_HEADS, 2 * dec_seq, LANES))
    xp = x_prompt.reshape(batch * seq, d_model)
    xs = x_sample.reshape(dec_batch * dec_seq, d_model)

    outs = [[] for _ in range(8)]
    for l in range(depth):
        lam_init = 0.8 - 0.6 * math.exp(-0.3 * l)
        w_in_p = _permute_w_in(w_in[l])
        gain = _gain_row(nsa_qnorm_g[l], nsa_knorm_g[l], diff_qnorm_g[l], diff_knorm_g[l])
        gain_kc = jnp.tile(nsa_knorm_g[l, 0], NSA_KV)[None, :]
        cw = jnp.concatenate([jnp.broadcast_to(nsa_cmp_w[l, 0][:, None], (L_CMP, LANES)),
                              jnp.broadcast_to(nsa_cmp_w[l, 1][:, None], (L_CMP, LANES))], axis=1)
        lp = jnp.zeros((8, LANES), F32).at[0:4, 0:DIFF_HALF].set(diff_lambda[l])
        sg = jnp.broadcast_to(diff_subnorm_g[l][:, None], (HEAD_DIM, 2 * LANES))
        sg_row = jnp.broadcast_to(diff_subnorm_g[l][None, :], (dec_seq, HEAD_DIM))
        gkc = jnp.broadcast_to(nsa_knorm_g[l, 0][:, None], (HEAD_DIM, LANES))
        wck, wcv = _compress_weights(nsa_cmp_w[l, 0]), _compress_weights(nsa_cmp_w[l, 1])
        g1 = norm1_g[l][None, :]
        g2 = norm2_g[l][None, :]
        wo, wg, wu, wd = (a.astype(BF16) for a in (w_out[l], w_gate[l], w_up[l], w_down[l]))
        cwf, cbf = conv_w[l], conv_b[l][None, :]
        proj = functools.partial(_proj, g1=g1, w=w_in_p, gain=gain, s64=s64, s32=s32, qcn=qcn, qcd=qcd, cw=cw)

        (nsa_t, win_t, dif_t, qa, ksa, kwa, gt, dqa, dka, craw) = proj(
            xp, seq_len=seq, pos_base=0, stacked=(l, depth, None if l == 0 else (nsa_t, dif_t)))
        kc, vc = _cmpfin(craw.reshape(batch, seq // L_CMP, 256), s64, gain_kc)
        o_nsa = _nsa(qa, kc, vc, ksa.reshape(NSA_KV, batch, seq, LANES), nsa_t,
                     kwa.reshape(NSA_KV, batch, seq, LANES), win_t, gt,
                     layer=l, bx=batch, nqb=seq // Q_LANES, q_base=0, win_base=0)
        o_dif = _diff(dqa, dka.reshape(DIFF_HEADS, batch, seq, LANES), dif_t, sd[:, 0], lp, sg,
                      layer=l, bx=batch, nqb=seq // 256, tq=256, q_base=0, lam_init=lam_init)
        xp, cv = _ffn_prompt(xp, o_nsa, o_dif, wo, g2, wg, wu, cwf, cbf, wd, seq_len=seq)
        w_keep = min(WINDOW, seq)
        outs[4].append(jnp.moveaxis(win_t[:, :, seq - w_keep:].reshape(batch, 2, NSA_KV, HEAD_DIM, w_keep), -1, 1))
        outs[6].append(cv[:, 8 - 2:8])

        (nsa_rows, win_rows, dif_rows, qa, _, _, gt, dqa, _, _) = proj(xs, seq_len=dec_seq, pos_base=past_len)
        per_seq = lambda a, lead: a[..., :HEAD_DIM].reshape(lead + (dec_batch, dec_seq, HEAD_DIM))
        qn = jnp.transpose(per_seq(qa, (NSA_KV, NSA_GROUP)), (2, 0, 1, 3, 4)).reshape(
            dec_batch, NSA_KV, NSA_GROUP * dec_seq, HEAD_DIM)
        qd = jnp.transpose(per_seq(dqa, (DIFF_HEADS, 2)), (2, 0, 1, 3, 4)).reshape(
            dec_batch, DIFF_HEADS, 2 * dec_seq, HEAD_DIM)
        gates = jnp.transpose(gt.reshape(NSA_KV, 4, NSA_GROUP, dec_batch, dec_seq), (3, 0, 1, 2, 4)).reshape(
            dec_batch, NSA_KV, 4, NSA_GROUP * dec_seq, 1)
        gates = jnp.broadcast_to(gates, gates.shape[:-1] + (HEAD_DIM,))
        o_s, new_state = _sattn(page_table, cache_nsa4, cache_diff4, state_win4, nsa_rows, win_rows, dif_rows,
                                qn, qd, gates, sn, sd, wck, wcv, gkc, lp, sg_row, layer=l, lam_init=lam_init)
        o_s = o_s.reshape(dec_batch * dec_seq, 2 * NSA_HEADS * HEAD_DIM)
        st1 = jnp.repeat(state_conv[l][:, 1], dec_seq, axis=0)
        st2 = jnp.repeat(state_conv[l][:, 0], dec_seq, axis=0)
        xs, g_s = _ffn_sample(xs, o_s[:, :NSA_HEADS * HEAD_DIM], o_s[:, NSA_HEADS * HEAD_DIM:], st1, st2,
                              wo, g2, wg, wu, cwf, cbf, wd, seq=dec_seq)
        outs[1].append(nsa_rows.reshape(dec_batch, dec_seq, 4, NSA_KV, HEAD_DIM))
        outs[3].append(dif_rows.reshape(dec_batch, dec_seq, 2, DIFF_HEADS, HEAD_DIM))
        outs[5].append(jnp.moveaxis(new_state.reshape(dec_batch, 2, NSA_KV, HEAD_DIM, w_buf), -1, 1))
        outs[7].append(g_s.reshape(dec_batch, dec_seq, d_ff)[:, dec_seq - 2:])

    outs[0] = jnp.moveaxis(nsa_t.reshape(depth, batch, 4, NSA_KV, HEAD_DIM, seq), -1, 2)
    outs[2] = jnp.moveaxis(dif_t.reshape(depth, batch, 2, DIFF_HEADS, HEAD_DIM, seq), -1, 2)
    stacked = [o if not isinstance(o, list) else jnp.stack(o) for o in outs]
    return (xp.reshape(batch, seq, d_model), xs.reshape(dec_batch, dec_seq, d_model), *stacked)
```

```python
import functools
import math

import jax
import jax.numpy as jnp
import numpy as np
from jax import lax
from jax.experimental import pallas as pl
from jax.experimental.pallas import tpu as pltpu

F32 = jnp.float32
BF16 = jnp.bfloat16

HEAD_DIM = 64
NSA_KV = 2
NSA_GROUP = 4
NSA_HEADS = NSA_KV * NSA_GROUP
DIFF_HEADS = 8
DIFF_HALF = HEAD_DIM // 2
L_CMP = 32
L_SEL = 64
N_SELECT = 16
WINDOW = 512
FORCE_BONUS = 1.0e4
EPS = 1e-6
NEG = -1e30
UNDERFLOW_LOG2 = -160.0
NORM_SLACK = 1.02
LOG2E = math.log2(math.e)

LANES = 128
KEY_TILE = 512
Q_LANES = 128
WIN_ROWS = WINDOW + Q_LANES
AUG0 = HEAD_DIM
N_AUG = 6
VMEM_LIMIT = 56 * 1024 * 1024

C_Q, C_NSA, C_WIN, C_DQ, C_DK, C_DV, C_GATE = 0, 512, 1024, 1280, 1792, 2304, 2816
N_COL = 2944


def _nt(a, b):
    return lax.dot_general(a, b, (((1,), (1,)), ((), ())), preferred_element_type=F32)


def _tn(a, b):
    return lax.dot_general(a, b, (((0,), (0,)), ((), ())), preferred_element_type=F32)


def _mm(a, b):
    return jnp.dot(a, b, preferred_element_type=F32)


def _group_meansq(z, smat, inv_n):
    zz = z * z
    hi = zz.astype(BF16)
    lo = (zz - hi.astype(F32)).astype(BF16)
    return (_mm(hi, smat) + _mm(lo, smat)) * inv_n


def _pos_rows(pos, lane):
    is_aug = (lane >= AUG0) & (lane < AUG0 + N_AUG)
    val = jnp.where((lane & 1) == 0, pos >> 6, pos & 63)
    return jnp.where(is_aug, val, 0).astype(F32)


def _split_heads(slab, fill, lane):
    even = jnp.where(lane < HEAD_DIM, slab, fill)
    odd = jnp.where(lane < HEAD_DIM, pltpu.roll(slab, HEAD_DIM, 1), fill)
    return even, odd


def _proj_kernel(x_ref, g1_ref, w_ref, gain_ref, s64_ref, s32_ref, qcn_ref, qcd_ref, cw_ref, *rest,
                 seq_len, pos_base, fmajor):
    nsa_ref, win_ref, dif_ref, qa_ref, ksa_ref, kwa_ref, gt_ref, dqa_ref, dka_ref, cr_ref = rest[-10:]
    tr = x_ref.shape[0]

    def put(ref, c0, slab):
        if fmajor:
            ref[(0,) * (len(ref.shape) - 2) + (slice(c0, c0 + LANES), slice(None))] = slab.T
        else:
            ref[:, c0:c0 + LANES] = slab
    i = pl.program_id(0)
    x = x_ref[...]
    h = x * lax.rsqrt(jnp.mean(x * x, axis=-1, keepdims=True) + EPS) * g1_ref[...]
    hb = h.astype(BF16)

    lane = lax.broadcasted_iota(jnp.int32, (tr, LANES), 1)
    row = lax.broadcasted_iota(jnp.int32, (tr, LANES), 0)
    pos = pos_base + (i * tr + row) % seq_len
    prow = _pos_rows(pos, lane)
    s64 = s64_ref[...]
    s32 = s32_ref[...]

    def seg(c0, width):
        return _mm(hb, w_ref[:, c0:c0 + width])

    def normed(z, c0, smat, inv_n):
        return z * lax.rsqrt(_group_meansq(z, smat, inv_n) + EPS) * gain_ref[:, c0:c0 + LANES]

    zq = seg(C_Q, 512)
    for s in range(4):
        zn = normed(zq[:, s * LANES:(s + 1) * LANES], C_Q + s * LANES, s64, 1.0 / HEAD_DIM) * (HEAD_DIM ** -0.5 * LOG2E)
        for par in range(2):
            hd = 2 * s + par
            src = zn if par == 0 else pltpu.roll(zn, HEAD_DIM, 1)
            qa = jnp.where(lane < HEAD_DIM, src, qcn_ref[hd:hd + 1, :])
            qa_ref[hd // NSA_GROUP, hd % NSA_GROUP] = qa.astype(BF16)

    zc = seg(C_NSA, 512)
    ks = normed(zc[:, 256:384], C_NSA + 256, s64, 1.0 / HEAD_DIM)
    put(nsa_ref, 0, zc[:, 0:128])
    put(nsa_ref, 128, zc[:, 128:256])
    put(nsa_ref, 256, ks)
    put(nsa_ref, 384, zc[:, 384:512])
    k0, k1 = _split_heads(ks, prow, lane)
    ksa_ref[0] = k0.astype(BF16)
    ksa_ref[1] = k1.astype(BF16)
    craw = zc[:, 0:256].reshape(tr // L_CMP, L_CMP, 256) * cw_ref[...][None]
    cr_ref[...] = jnp.sum(craw, axis=1)

    zw = seg(C_WIN, 256)
    kw = normed(zw[:, 0:128], C_WIN, s64, 1.0 / HEAD_DIM)
    put(win_ref, 0, kw)
    put(win_ref, 128, zw[:, 128:256])
    k0, k1 = _split_heads(kw, prow, lane)
    kwa_ref[0] = k0.astype(BF16)
    kwa_ref[1] = k1.astype(BF16)

    zdq = seg(C_DQ, 512)
    for s in range(4):
        zn = normed(zdq[:, s * LANES:(s + 1) * LANES], C_DQ + s * LANES, s32, 1.0 / DIFF_HALF) * (DIFF_HALF ** -0.5 * LOG2E)
        for par in range(2):
            hd = 2 * s + par
            src = zn if par == 0 else pltpu.roll(zn, HEAD_DIM, 1)
            fill = qcd_ref[hd:hd + 1, :]
            dqa_ref[hd, 0] = jnp.where(lane < DIFF_HALF, src, fill).astype(BF16)
            dqa_ref[hd, 1] = jnp.where((lane >= DIFF_HALF) & (lane < HEAD_DIM), src, fill).astype(BF16)

    zdk = seg(C_DK, 512)
    for s in range(4):
        kn = normed(zdk[:, s * LANES:(s + 1) * LANES], C_DK + s * LANES, s32, 1.0 / DIFF_HALF)
        put(dif_ref, s * LANES, kn)
        k0, k1 = _split_heads(kn, prow, lane)
        dka_ref[2 * s] = k0.astype(BF16)
        dka_ref[2 * s + 1] = k1.astype(BF16)
    zdv = seg(C_DV, 512)
    for s in range(4):
        put(dif_ref, 512 + s * LANES, zdv[:, s * LANES:(s + 1) * LANES])

    zg = seg(C_GATE, LANES)
    sg = 1.0 / (1.0 + jnp.exp(-zg))
    gt_ref[...] = sg.T[0:2 * 16].reshape(NSA_KV, 16, tr)


def _proj(x2d, g1, w, gain, s64, s32, qcn, qcd, cw, *, seq_len, pos_base, stacked=None, tr=512):
    n = x2d.shape[0]
    tr = min(tr, n)
    assert n % tr == 0 and tr % L_CMP == 0
    fmajor = stacked is not None
    kern = functools.partial(_proj_kernel, seq_len=seq_len, pos_base=pos_base, fmajor=fmajor)
    full = lambda shape: pl.BlockSpec(shape, lambda i: (0,) * len(shape))
    if fmajor:
        layer, depth, prev = stacked
        assert seq_len % tr == 0
        tps, nb = seq_len // tr, n // seq_len
        cache_shapes = [jax.ShapeDtypeStruct((depth, nb, 512, seq_len), F32),
                        jax.ShapeDtypeStruct((nb, 256, seq_len), F32),
                        jax.ShapeDtypeStruct((depth, nb, 1024, seq_len), F32)]
        cache_specs = [pl.BlockSpec((1, 1, 512, tr), lambda i: (layer, i // tps, 0, i % tps)),
                       pl.BlockSpec((1, 256, tr), lambda i: (i // tps, 0, i % tps)),
                       pl.BlockSpec((1, 1, 1024, tr), lambda i: (layer, i // tps, 0, i % tps))]
    else:
        prev = None
        cache_shapes = [jax.ShapeDtypeStruct((n, w_), F32) for w_ in (512, 256, 1024)]
        cache_specs = [pl.BlockSpec((tr, w_), lambda i: (i, 0)) for w_ in (512, 256, 1024)]
    out_shape = tuple(cache_shapes) + (
        jax.ShapeDtypeStruct((NSA_KV, NSA_GROUP, n, LANES), BF16),
        jax.ShapeDtypeStruct((NSA_KV, n, LANES), BF16),
        jax.ShapeDtypeStruct((NSA_KV, n, LANES), BF16),
        jax.ShapeDtypeStruct((NSA_KV, 16, n), F32),
        jax.ShapeDtypeStruct((DIFF_HEADS, 2, n, LANES), BF16),
        jax.ShapeDtypeStruct((DIFF_HEADS, n, LANES), BF16),
        jax.ShapeDtypeStruct((n // L_CMP, 256), F32),
    )
    out_specs = tuple(cache_specs) + (
        pl.BlockSpec((NSA_KV, NSA_GROUP, tr, LANES), lambda i: (0, 0, i, 0)),
        pl.BlockSpec((NSA_KV, tr, LANES), lambda i: (0, i, 0)),
        pl.BlockSpec((NSA_KV, tr, LANES), lambda i: (0, i, 0)),
        pl.BlockSpec((NSA_KV, 16, tr), lambda i: (0, 0, i)),
        pl.BlockSpec((DIFF_HEADS, 2, tr, LANES), lambda i: (0, 0, i, 0)),
        pl.BlockSpec((DIFF_HEADS, tr, LANES), lambda i: (0, i, 0)),
        pl.BlockSpec((tr // L_CMP, 256), lambda i: (i, 0)),
    )
    in_specs = [
        pl.BlockSpec((tr, x2d.shape[1]), lambda i: (i, 0)),
        full(g1.shape), full(w.shape), full(gain.shape), full(s64.shape), full(s32.shape),
        full(qcn.shape), full(qcd.shape), full(cw.shape),
    ]
    args = [x2d, g1, w, gain, s64, s32, qcn, qcd, cw]
    aliases = {}
    if prev is not None:
        in_specs += [pl.BlockSpec(memory_space=pl.ANY)] * 2
        aliases = {len(args): 0, len(args) + 1: 2}
        args += list(prev)
    return pl.pallas_call(
        kern, grid=(n // tr,), in_specs=in_specs, out_specs=out_specs, out_shape=out_shape,
        input_output_aliases=aliases,
        compiler_params=pltpu.CompilerParams(dimension_semantics=("arbitrary",), vmem_limit_bytes=VMEM_LIMIT),
        name="proj",
    )(*args)


def _cmpfin_kernel(raw_ref, s64_ref, gain_ref, kc_ref, vc_ref, k_scr, v_scr, *, nsp):
    lane = lax.broadcasted_iota(jnp.int32, (nsp, LANES), 1)
    n = lax.broadcasted_iota(jnp.int32, (nsp, LANES), 0)
    k_scr[...] = raw_ref[0, :, 0:128]
    v_scr[...] = raw_ref[0, :, 128:256]
    for j in range(2):
        k = k_scr[pl.ds(j, nsp, stride=2), :]
        kn = k * lax.rsqrt(_group_meansq(k, s64_ref[...], 1.0 / HEAD_DIM) + EPS) * gain_ref[...]
        cend = n * L_SEL + (L_CMP - 1 + L_CMP * j)
        prow = _pos_rows(cend, lane)
        k0, k1 = _split_heads(kn, prow, lane)
        kc_ref[0, 0, j * nsp:(j + 1) * nsp, :] = k0.astype(BF16)
        kc_ref[1, 0, j * nsp:(j + 1) * nsp, :] = k1.astype(BF16)
        vc_ref[0, j * nsp:(j + 1) * nsp, :] = v_scr[pl.ds(j, nsp, stride=2), :].astype(BF16)


def _cmpfin(raw, s64, gain_kc):
    bx, n2, _ = raw.shape
    nsp = n2 // 2
    assert nsp % 16 == 0
    return pl.pallas_call(
        functools.partial(_cmpfin_kernel, nsp=nsp),
        grid=(bx,),
        in_specs=[pl.BlockSpec((1, n2, 256), lambda b: (b, 0, 0)),
                  pl.BlockSpec(s64.shape, lambda b: (0, 0)),
                  pl.BlockSpec(gain_kc.shape, lambda b: (0, 0))],
        out_specs=(pl.BlockSpec((NSA_KV, 1, n2, LANES), lambda b: (0, b, 0, 0)),
                   pl.BlockSpec((1, n2, LANES), lambda b: (b, 0, 0))),
        out_shape=(jax.ShapeDtypeStruct((NSA_KV, bx, n2, LANES), BF16),
                   jax.ShapeDtypeStruct((bx, n2, LANES), BF16)),
        scratch_shapes=[pltpu.VMEM((n2, LANES), F32), pltpu.VMEM((n2, LANES), F32)],
        name="cmpfin",
    )(raw, s64, gain_kc)


def _flash_init(c, m_ref, l_ref, acc_ref):
    m_ref[c] = jnp.full(m_ref.shape[1:], NEG, F32)
    l_ref[c] = jnp.zeros(l_ref.shape[1:], F32)
    acc_ref[c] = jnp.zeros(acc_ref.shape[1:], F32)


def _flash_probs(c, s, m_ref, l_ref):
    m_old = m_ref[c]
    m_new = jnp.maximum(m_old, jnp.max(s, axis=0, keepdims=True))
    p = jnp.exp2(s - m_new)
    a = jnp.exp2(m_old - m_new)
    l_ref[c] = a * l_ref[c] + jnp.sum(p, axis=0, keepdims=True)
    m_ref[c] = m_new
    return p.astype(BF16), a


def _flash_update(c, s, vt_tile, m_ref, l_ref, acc_ref):
    p, a = _flash_probs(c, s, m_ref, l_ref)
    acc_ref[c] = a * acc_ref[c] + _mm(vt_tile, p)


def _max_key_norm(k_ref_2d, n_keys):
    feat = lax.broadcasted_iota(jnp.int32, (1, LANES), 1) < HEAD_DIM

    def chunk(c, best):
        kf = k_ref_2d[pl.ds(pl.multiple_of(c * KEY_TILE, KEY_TILE), KEY_TILE), :].astype(F32)
        sq = jnp.sum(jnp.where(feat, kf * kf, 0.0), axis=1, keepdims=True)
        return jnp.maximum(best, jnp.max(sq, axis=0, keepdims=True))

    return jnp.sqrt(lax.fori_loop(0, n_keys // KEY_TILE, chunk, jnp.zeros((1, 1), F32)))


def _first_needed_tile(qa, m, k_norm, slope_row, n_max):
    feat = lax.broadcasted_iota(jnp.int32, (1, LANES), 1) < HEAD_DIM
    qf = qa.astype(F32)
    qsq = jnp.where(feat, qf * qf, 0.0)
    hi = qsq.astype(BF16)
    lo = (qsq - hi.astype(F32)).astype(BF16)
    ones = jnp.ones((8, LANES), BF16)
    q_norm = jnp.sqrt((_nt(ones, hi) + _nt(ones, lo))[0:1])
    cutoff = (m + UNDERFLOW_LOG2 - NORM_SLACK * q_norm * k_norm) / slope_row
    tile = jnp.floor(jnp.min(cutoff, axis=1, keepdims=True) * (1.0 / KEY_TILE))
    return jnp.clip(tile, 0.0, n_max.astype(F32)).astype(jnp.int32)[0, 0]


def _sweep_earlier_tiles(first, n_tiles, step):
    both = (0, 1)

    @pl.when(jnp.logical_and(n_tiles % 2 == 1, jnp.minimum(first[0], first[1]) < n_tiles))
    def _():
        step(pl.multiple_of((n_tiles - 1) * KEY_TILE, KEY_TILE), KEY_TILE, both)

    j_end = n_tiles // 2
    j0 = [jnp.minimum(f // 2, j_end) for f in first]
    j_both = jnp.maximum(j0[0], j0[1])

    def run(chains):
        def body(j, carry):
            step(pl.multiple_of(j * 2 * KEY_TILE, 2 * KEY_TILE), 2 * KEY_TILE, chains)
            return carry
        return body

    lax.fori_loop(j0[0], j_both, run((0,)), 0)
    lax.fori_loop(j0[1], j_both, run((1,)), 0)
    lax.fori_loop(j_both, j_end, run(both), 0)


def _masked_softmax_cols(s, valid):
    sm = jnp.where(valid, s, NEG)
    m = jnp.max(sm, axis=0, keepdims=True)
    p = jnp.where(valid, jnp.exp2(sm - m), 0.0)
    l = jnp.sum(p, axis=0, keepdims=True)
    return p * jnp.where(l > 0.0, 1.0 / l, 0.0)


def _nsa_kernel(q_ref, kc_ref, vc_ref, ks_ref, vs_ref, kw_ref, vw_ref, gt_ref, o_ref,
                sel_ref, acc_ref, m_ref, l_ref, kn_ref, *, q_base, nsp, tw, win_base):
    i = pl.program_id(1)
    q0 = q_base + i * Q_LANES
    ncol = NSA_GROUP * Q_LANES
    groups = range(NSA_KV)
    qas = [q_ref[g].reshape(ncol, LANES) for g in groups]
    lane_c = lax.broadcasted_iota(jnp.int32, (1, ncol), 1)
    qpos_c = q0 + (lane_c & (Q_LANES - 1))
    rows_of = lambda g: slice(g * HEAD_DIM, (g + 1) * HEAD_DIM)

    n_row = lax.broadcasted_iota(jnp.int32, (nsp, ncol), 0)
    valid_e = (n_row * L_SEL + (L_CMP - 1)) <= qpos_c
    valid_o = (n_row * L_SEL + (2 * L_CMP - 1)) <= qpos_c
    blk = lax.broadcasted_iota(jnp.int32, (nsp, Q_LANES), 0)
    qpos_q = q0 + lax.broadcasted_iota(jnp.int32, (nsp, Q_LANES), 1)
    cur = qpos_q >> 6
    bonus = jnp.where((blk == 0) | (blk == cur) | (blk == cur - 1), FORCE_BONUS, 0.0)
    in_range = blk <= cur
    blk_f = blk.astype(F32)
    o_c = []
    for g in groups:
        s_c = _nt(kc_ref[g, 0], qas[g])
        sm_e = jnp.where(valid_e, s_c[0:nsp], NEG)
        sm_o = jnp.where(valid_o, s_c[nsp:2 * nsp], NEG)
        m = jnp.maximum(jnp.max(sm_e, axis=0, keepdims=True), jnp.max(sm_o, axis=0, keepdims=True))
        p_e = jnp.where(valid_e, jnp.exp2(sm_e - m), 0.0)
        p_o = jnp.where(valid_o, jnp.exp2(sm_o - m), 0.0)
        l = jnp.sum(p_e, axis=0, keepdims=True) + jnp.sum(p_o, axis=0, keepdims=True)
        inv = jnp.where(l > 0.0, 1.0 / l, 0.0)
        p_e = p_e * inv
        p_o = p_o * inv
        pcat = jnp.concatenate([p_e, p_o], axis=0).astype(BF16)
        o_c.append(_tn(vc_ref[0], pcat)[rows_of(g)])

        imp = p_e[:, 0:Q_LANES] + p_o[:, 0:Q_LANES]
        for r in range(1, NSA_GROUP):
            imp = imp + p_e[:, r * Q_LANES:(r + 1) * Q_LANES] + p_o[:, r * Q_LANES:(r + 1) * Q_LANES]
        score = jnp.where(in_range, imp + bonus, -jnp.inf)
        sel = jnp.zeros((nsp, Q_LANES), F32)
        for _ in range(N_SELECT):
            top = jnp.max(score, axis=0, keepdims=True)
            first = jnp.min(jnp.where(score == top, blk_f, 1e9), axis=0, keepdims=True)
            pick = blk_f == first
            sel = jnp.where(pick, jnp.where(top > -jnp.inf, 1.0, sel), sel)
            score = jnp.where(pick, -jnp.inf, score)
        sel_ref[g] = sel

    @pl.when(i == 0)
    def _():
        for g in groups:
            kn_ref[g] = jnp.broadcast_to(_max_key_norm(ks_ref.at[g, 0], ks_ref.shape[2]), kn_ref.shape[1:])

    for g in groups:
        _flash_init(g, m_ref, l_ref, acc_ref)
    def sel_tile(k0, size, which):
        n_blk = size // L_SEL
        causal = (k0 + lax.broadcasted_iota(jnp.int32, (size, Q_LANES), 0)
                  <= q0 + lax.broadcasted_iota(jnp.int32, (size, Q_LANES), 1))
        for g in which:
            vt = vs_ref[0, 0, rows_of(g), pl.ds(k0, size)].astype(BF16)
            s = _nt(ks_ref[g, 0, pl.ds(k0, size), :], qas[g])
            sel_rows = sel_ref[g, pl.ds(pl.multiple_of(k0 // L_SEL, 8), n_blk), :]
            selt = jnp.concatenate(
                [jnp.broadcast_to(sel_rows[j:j + 1, :], (L_SEL, Q_LANES)) for j in range(n_blk)], axis=0)
            valid = jnp.where(causal, selt, 0.0) > 0.0
            s = jnp.concatenate(
                [jnp.where(valid, s[:, r * Q_LANES:(r + 1) * Q_LANES], NEG) for r in range(NSA_GROUP)], axis=1)
            _flash_update(g, s, vt, m_ref, l_ref, acc_ref)

    n_before = (q0 + Q_LANES + KEY_TILE - 1) // KEY_TILE - 1
    sel_tile(pl.multiple_of(n_before * KEY_TILE, KEY_TILE), KEY_TILE, groups)
    first = []
    for g in groups:
        slope = jnp.concatenate([jnp.full((1, Q_LANES), LOG2E * 2.0 ** -(g * NSA_GROUP + r + 1), F32)
                                 for r in range(NSA_GROUP)], axis=1)
        first.append(_first_needed_tile(qas[g], m_ref[g], kn_ref[g][0:1, 0:1], slope, n_before))
    _sweep_earlier_tiles(first, n_before, sel_tile)

    w0 = jnp.clip(q0 - win_base - WINDOW, 0, tw - WIN_ROWS)
    w0 = pl.multiple_of(w0, Q_LANES)
    kpos = win_base + w0 + lax.broadcasted_iota(jnp.int32, (WIN_ROWS, ncol), 0)
    dist = qpos_c - kpos
    in_window = (dist >= 0) & (dist < WINDOW)

    outs = []
    for g in groups:
        l_s = l_ref[g]
        o_s = acc_ref[g] * jnp.where(l_s > 0.0, 1.0 / l_s, 0.0)
        p_w = _masked_softmax_cols(_nt(kw_ref[g, 0, pl.ds(w0, WIN_ROWS), :], qas[g]), in_window)
        o_w = _mm(vw_ref[0, rows_of(g), pl.ds(w0, WIN_ROWS)].astype(BF16), p_w.astype(BF16))
        gt = gt_ref[g]
        for r in range(NSA_GROUP):
            cs = slice(r * Q_LANES, (r + 1) * Q_LANES)
            outs.append(gt[r:r + 1, :] * o_c[g][:, cs] + gt[4 + r:5 + r, :] * o_s[:, cs]
                        + gt[8 + r:9 + r, :] * o_w[:, cs])
    o_ref[...] = jnp.concatenate(outs, axis=0).T


def _nsa(qa, kc, vc, ksa, nsa_t, kwa, win_t, gt, *, layer, bx, nqb, q_base, win_base):
    nq = qa.shape[2]
    n2 = kc.shape[2]
    tk = ksa.shape[2]
    tw = kwa.shape[2]
    assert nq == bx * nqb * Q_LANES and tk % KEY_TILE == 0 and tw >= WIN_ROWS
    assert nsa_t.shape[2:] == (512, tk) and win_t.shape[1:] == (256, tw)
    kern = functools.partial(_nsa_kernel, q_base=q_base, nsp=n2 // 2, tw=tw, win_base=win_base)
    ncol = NSA_GROUP * Q_LANES
    return pl.pallas_call(
        kern, grid=(bx, nqb),
        in_specs=[
            pl.BlockSpec((NSA_KV, NSA_GROUP, Q_LANES, LANES), lambda b, i: (0, 0, b * nqb + i, 0)),
            pl.BlockSpec((NSA_KV, 1, n2, LANES), lambda b, i: (0, b, 0, 0)),
            pl.BlockSpec((1, n2, LANES), lambda b, i: (b, 0, 0)),
            pl.BlockSpec((NSA_KV, 1, tk, LANES), lambda b, i: (0, b, 0, 0)),
            pl.BlockSpec((1, 1, LANES, tk), lambda b, i: (layer, b, 3, 0)),
            pl.BlockSpec((NSA_KV, 1, tw, LANES), lambda b, i: (0, b, 0, 0)),
            pl.BlockSpec((1, LANES, tw), lambda b, i: (b, 1, 0)),
            pl.BlockSpec((NSA_KV, 16, Q_LANES), lambda b, i: (0, 0, b * nqb + i)),
        ],
        out_specs=pl.BlockSpec((Q_LANES, NSA_HEADS * HEAD_DIM), lambda b, i: (b * nqb + i, 0)),
        out_shape=jax.ShapeDtypeStruct((nq, NSA_HEADS * HEAD_DIM), F32),
        scratch_shapes=[pltpu.VMEM((NSA_KV, n2 // 2, Q_LANES), F32), pltpu.VMEM((NSA_KV, HEAD_DIM, ncol), F32),
                        pltpu.VMEM((NSA_KV, 1, ncol), F32), pltpu.VMEM((NSA_KV, 1, ncol), F32),
                        pltpu.VMEM((NSA_KV, 8, LANES), F32)],
        compiler_params=pltpu.CompilerParams(dimension_semantics=("arbitrary",) * 2, vmem_limit_bytes=VMEM_LIMIT),
        name="nsa",
    )(qa, kc, vc, ksa, nsa_t, kwa, win_t, gt)


def _diff_kernel(q_ref, k_ref, v_ref, sl_ref, lp_ref, sg_ref, o_ref, acc_ref, m_ref, l_ref, kn_ref, *,
                 q_base, tq, lam_init):
    pair = pl.program_id(1)
    i = pl.program_id(2)
    q0 = q_base + i * tq
    ncol = 2 * tq
    heads = range(2)
    lp = lp_ref[...]
    lam = (jnp.exp(jnp.sum(lp[0:1] * lp[1:2], keepdims=True)) - jnp.exp(jnp.sum(lp[2:3] * lp[3:4], keepdims=True))
           + lam_init)
    n_full = q0 // KEY_TILE
    qas = [q_ref[hh].reshape(ncol, LANES) for hh in heads]
    for hh in heads:
        _flash_init(hh, m_ref, l_ref, acc_ref)

    @pl.when(i == 0)
    def _():
        for hh in heads:
            kn_ref[hh] = jnp.broadcast_to(_max_key_norm(k_ref.at[hh, 0], k_ref.shape[2]), kn_ref.shape[1:])

    def step(k0, size, which, causal=None):
        for hh in which:
            vt = v_ref[0, 0, hh * HEAD_DIM:(hh + 1) * HEAD_DIM, pl.ds(k0, size)].astype(BF16)
            s = _nt(k_ref[hh, 0, pl.ds(k0, size), :], qas[hh])
            if causal is not None:
                s = jnp.where(causal, s, NEG)
            _flash_update(hh, s, vt, m_ref, l_ref, acc_ref)

    k_diag = pl.multiple_of(n_full * KEY_TILE, KEY_TILE)

    def diagonal(size):
        tok = k_diag + lax.broadcasted_iota(jnp.int32, (size, ncol), 0)
        step(k_diag, size, heads, tok <= q0 + lax.broadcasted_iota(jnp.int32, (size, ncol), 1) % tq)

    short = q0 - n_full * KEY_TILE + tq <= KEY_TILE // 2

    @pl.when(short)
    def _():
        diagonal(KEY_TILE // 2)

    @pl.when(jnp.logical_not(short))
    def _():
        diagonal(KEY_TILE)

    first = []
    for hh in heads:
        slope = jnp.concatenate([sl_ref[pl.ds(2 * pair + hh, 1), :]] * (ncol // LANES), axis=1)
        first.append(_first_needed_tile(qas[hh], m_ref[hh], kn_ref[hh][0:1, 0:1], slope, n_full))
    _sweep_earlier_tiles(first, n_full, step)

    outs = []
    for hh in heads:
        o = acc_ref[hh] / l_ref[hh]
        d = o[:, 0:tq] - lam * o[:, tq:ncol]
        d = d * lax.rsqrt(jnp.mean(d * d, axis=0, keepdims=True) + EPS) * sg_ref[:, 0:tq] * (1.0 - lam_init)
        outs.append(d)
    o_ref[...] = jnp.concatenate(outs, axis=0).T


def _diff(dqa, dka, dif_t, slopes, lp, sg, *, layer, bx, nqb, tq, q_base, lam_init):
    nq = dqa.shape[2]
    tk = dka.shape[2]
    assert nq == bx * nqb * tq and tk % KEY_TILE == 0 and KEY_TILE % tq == 0 and dif_t.shape[2:] == (1024, tk)
    kern = functools.partial(_diff_kernel, q_base=q_base, tq=tq, lam_init=lam_init)
    return pl.pallas_call(
        kern, grid=(bx, DIFF_HEADS // 2, nqb),
        in_specs=[
            pl.BlockSpec((2, 2, tq, LANES), lambda b, h, i: (h, 0, b * nqb + i, 0)),
            pl.BlockSpec((2, 1, tk, LANES), lambda b, h, i: (h, b, 0, 0)),
            pl.BlockSpec((1, 1, LANES, tk), lambda b, h, i: (layer, b, 4 + h, 0)),
            pl.BlockSpec(slopes.shape, lambda b, h, i: (0, 0)),
            pl.BlockSpec(lp.shape, lambda b, h, i: (0, 0)),
            pl.BlockSpec(sg.shape, lambda b, h, i: (0, 0)),
        ],
        out_specs=pl.BlockSpec((tq, LANES), lambda b, h, i: (b * nqb + i, h)),
        out_shape=jax.ShapeDtypeStruct((nq, DIFF_HEADS * HEAD_DIM), F32),
        scratch_shapes=[pltpu.VMEM((2, HEAD_DIM, 2 * tq), F32), pltpu.VMEM((2, 1, 2 * tq), F32),
                        pltpu.VMEM((2, 1, 2 * tq), F32), pltpu.VMEM((2, 8, LANES), F32)],
        compiler_params=pltpu.CompilerParams(dimension_semantics=("arbitrary",) * 3, vmem_limit_bytes=VMEM_LIMIT),
        name="diff",
    )(dqa, dka, dif_t, slopes, lp, sg)


F_CHUNKS = 2


def _ffn_core(x, on, od, wo_ref, g2_ref, wg_ref, wu_ref, cw_ref, cb_ref, wd_ref, prev_rows, g_store):
    o = jnp.concatenate([on, od], axis=1).astype(BF16)
    xm = x + _mm(o, wo_ref[...])
    h2 = (xm * lax.rsqrt(jnp.mean(xm * xm, axis=-1, keepdims=True) + EPS) * g2_ref[...]).astype(BF16)
    d_ff = wg_ref.shape[1]
    fc = d_ff // F_CHUNKS
    y = jnp.zeros(x.shape, F32)
    for c in range(F_CHUNKS):
        c0, c1 = c * fc, (c + 1) * fc
        g = _mm(h2, wg_ref[:, c0:c1])
        u = _mm(h2, wu_ref[:, c0:c1])
        gm1, gm2 = prev_rows(g, c0, c1)
        g_store(g, c0, c1)
        gc = cb_ref[:, c0:c1] + cw_ref[0:1, c0:c1] * gm2 + cw_ref[1:2, c0:c1] * gm1 + cw_ref[2:3, c0:c1] * g
        act = gc * (1.0 / (1.0 + jnp.exp(-gc))) * u
        y = y + _mm(act.astype(BF16), wd_ref[c0:c1, :])
    return xm + y


def _ffn_prompt_kernel(x_ref, on_ref, od_ref, wo_ref, g2_ref, wg_ref, wu_ref, cw_ref, cb_ref, wd_ref,
                       y_ref, cv_ref, carry_ref, *, tiles_per_seq):
    i = pl.program_id(0)
    tr = x_ref.shape[0]

    @pl.when(i % tiles_per_seq == 0)
    def _():
        carry_ref[...] = jnp.zeros(carry_ref.shape, F32)

    def prev_rows(g, c0, c1):
        row = lax.broadcasted_iota(jnp.int32, g.shape, 0)
        p1 = carry_ref[7:8, c0:c1]
        p2 = carry_ref[6:7, c0:c1]
        gm1 = jnp.where(row == 0, p1, pltpu.roll(g, 1, 0))
        gm2 = jnp.where(row == 0, p2, jnp.where(row == 1, p1, pltpu.roll(g, 2, 0)))
        return gm1, gm2

    def g_store(g, c0, c1):
        carry_ref[:, c0:c1] = g[tr - 8:tr]
        cv_ref[0, :, c0:c1] = g[tr - 8:tr]

    y_ref[...] = _ffn_core(x_ref[...], on_ref[...], od_ref[...], wo_ref, g2_ref, wg_ref, wu_ref, cw_ref, cb_ref,
                           wd_ref, prev_rows, g_store)


def _ffn_sample_kernel(x_ref, on_ref, od_ref, st1_ref, st2_ref, wo_ref, g2_ref, wg_ref, wu_ref, cw_ref, cb_ref,
                       wd_ref, y_ref, g_ref, *, seq):
    def prev_rows(g, c0, c1):
        rs = lax.broadcasted_iota(jnp.int32, g.shape, 0) % seq
        s1 = st1_ref[:, c0:c1]
        gm1 = jnp.where(rs == 0, s1, pltpu.roll(g, 1, 0))
        gm2 = jnp.where(rs == 0, st2_ref[:, c0:c1], jnp.where(rs == 1, s1, pltpu.roll(g, 2, 0)))
        return gm1, gm2

    def g_store(g, c0, c1):
        g_ref[:, c0:c1] = g

    y_ref[...] = _ffn_core(x_ref[...], on_ref[...], od_ref[...], wo_ref, g2_ref, wg_ref, wu_ref, cw_ref, cb_ref,
                           wd_ref, prev_rows, g_store)


def _const_spec(a):
    return pl.BlockSpec(a.shape, lambda i: (0,) * a.ndim, pipeline_mode=pl.Buffered(1))


def _ffn_prompt(x2d, on, od, wo, g2, wg, wu, cw, cb, wd, *, seq_len, tr=256):
    n, d = x2d.shape
    f = wg.shape[1]
    assert n % tr == 0 and seq_len % tr == 0
    tps = seq_len // tr
    row = lambda w: pl.BlockSpec((tr, w), lambda i: (i, 0))
    return pl.pallas_call(
        functools.partial(_ffn_prompt_kernel, tiles_per_seq=tps), grid=(n // tr,),
        in_specs=[row(d), row(on.shape[1]), row(od.shape[1])] + [_const_spec(a) for a in (wo, g2, wg, wu, cw, cb, wd)],
        out_specs=(row(d), pl.BlockSpec((1, 8, f), lambda i: (i // tps, 0, 0))),
        out_shape=(jax.ShapeDtypeStruct((n, d), F32), jax.ShapeDtypeStruct((n // seq_len, 8, f), F32)),
        scratch_shapes=[pltpu.VMEM((8, f), F32)],
        compiler_params=pltpu.CompilerParams(dimension_semantics=("arbitrary",), vmem_limit_bytes=VMEM_LIMIT),
        name="ffn_prompt",
    )(x2d, on, od, wo, g2, wg, wu, cw, cb, wd)


def _ffn_sample(x2d, on, od, st1, st2, wo, g2, wg, wu, cw, cb, wd, *, seq):
    n, d = x2d.shape
    f = wg.shape[1]
    args = (x2d, on, od, st1, st2, wo, g2, wg, wu, cw, cb, wd)
    return pl.pallas_call(
        functools.partial(_ffn_sample_kernel, seq=seq), grid=(1,),
        in_specs=[_const_spec(a) for a in args],
        out_specs=(pl.BlockSpec((n, d), lambda i: (0, 0)), pl.BlockSpec((n, f), lambda i: (0, 0))),
        out_shape=(jax.ShapeDtypeStruct((n, d), F32), jax.ShapeDtypeStruct((n, f), F32)),
        compiler_params=pltpu.CompilerParams(dimension_semantics=("arbitrary",), vmem_limit_bytes=VMEM_LIMIT),
        name="ffn_sample",
    )(*args)


PAGES_PER_STEP = 8
STEPS_PER_CMP_TILE = LANES // (PAGES_PER_STEP * LANES // L_CMP)
T_NSA, T_WIN, T_DIF, T_ROWS = 0, 512, 768, 1792


def _softmax_rows(s):
    m = jnp.max(s, axis=-1, keepdims=True)
    p = jnp.exp2(s - m)
    return p * (1.0 / jnp.sum(p, axis=-1, keepdims=True))


def _masked_softmax_rows(s, valid):
    sm = jnp.where(valid, s, NEG)
    m = jnp.max(sm, axis=-1, keepdims=True)
    p = jnp.where(valid, jnp.exp2(sm - m), 0.0)
    l = jnp.sum(p, axis=-1, keepdims=True)
    return p * jnp.where(l > 0.0, 1.0 / l, 0.0)


def _sattn_kernel(pt_ref, *refs, n_steps, past_len, dec, lam_init):
    pg = PAGES_PER_STEP
    nsa_pg, dif_pg = refs[0:pg], refs[pg:2 * pg]
    (win_ref, nn_ref, nw_ref, nd_ref, qn_ref, qd_ref, gate_ref, sn_ref, sd_ref, wck_ref, wcv_ref, gkc_ref,
     lp_ref, sg_ref, o_ref, ns_ref,
     ssel, sdif, psel, pdif, vsel, vdif, kc_scr, tail_scr, oc_scr, ow_scr) = refs[2 * pg:]
    s = pl.program_id(1)
    span = pg * LANES
    groups, heads = range(NSA_KV), range(DIFF_HEADS)
    cat = lambda parts: jnp.concatenate(parts, axis=1)
    hi_lo = lambda x: (x.astype(BF16), (x - x.astype(BF16).astype(F32)).astype(BF16))

    @pl.when(s < n_steps)
    def _():
        @pl.when(s == 0)
        def _():
            kc_scr[...] = jnp.zeros(kc_scr.shape, F32)

        shift = (s % STEPS_PER_CMP_TILE) * (span // L_CMP)
        col = pl.multiple_of((s // STEPS_PER_CMP_TILE) * LANES, LANES)
        for part, w_ref in ((0, wck_ref), (1, wcv_ref)):
            x_hi, x_lo = hi_lo(cat([r[0, 0, part * 128:(part + 1) * 128, :] for r in nsa_pg]))
            blk = _mm(x_hi, w_ref[0]) + _mm(x_lo, w_ref[0]) + _mm(x_hi, w_ref[1])
            kc_scr[part * 128:(part + 1) * 128, pl.ds(col, LANES)] += pltpu.roll(blk, shift, 1)

        k0 = pl.multiple_of(s * span, span)
        kpos = (k0 + lax.broadcasted_iota(jnp.int32, (1, span), 1)).astype(F32)
        for g in groups:
            k8 = cat([r[0, 0, 256 + g * HEAD_DIM:256 + (g + 1) * HEAD_DIM, :] for r in nsa_pg]).astype(BF16)
            ssel[g, :, pl.ds(k0, span)] = _mm(qn_ref[0, g], k8) + cat([sn_ref[g]] * pg) * kpos
        for h in heads:
            k8 = cat([r[0, 0, h * HEAD_DIM:(h + 1) * HEAD_DIM, :] for r in dif_pg]).astype(BF16)
            sdif[h, :, pl.ds(k0, span)] = _mm(qd_ref[0, h], k8) + cat([sd_ref[h]] * pg) * kpos
        vsel[:, pl.ds(k0, span)] = cat([r[0, 0, 384:512, :] for r in nsa_pg]).astype(BF16)
        vdif[:, pl.ds(k0, span)] = cat([r[0, 0, 512:1024, :] for r in dif_pg]).astype(BF16)

    @pl.when(s == n_steps)
    def _():
        new_rows = cat([nn_ref[...], nw_ref[...], nd_ref[...]])
        tail_scr[...] = jnp.concatenate([new_rows, jnp.zeros((LANES - dec, T_ROWS), F32)], axis=0).T
        nrow, drow = NSA_GROUP * dec, 2 * dec
        tail_pos = (past_len + lax.broadcasted_iota(jnp.int32, (1, LANES), 1)).astype(F32)
        causal32 = (lax.broadcasted_iota(jnp.int32, (nrow, LANES), 1)
                    <= lax.broadcasted_iota(jnp.int32, (nrow, LANES), 0) % dec)
        causal16 = (lax.broadcasted_iota(jnp.int32, (drow, LANES), 1)
                    <= lax.broadcasted_iota(jnp.int32, (drow, LANES), 0) % dec)

        for g in groups:
            kt = tail_scr[T_NSA + 256 + g * HEAD_DIM:T_NSA + 256 + (g + 1) * HEAD_DIM, :].astype(BF16)
            sc = _mm(qn_ref[0, g], kt) + sn_ref[g] * tail_pos
            ssel[g, :, past_len:past_len + LANES] = jnp.where(causal32, sc, NEG)
        for h in heads:
            kt = tail_scr[T_DIF + h * HEAD_DIM:T_DIF + (h + 1) * HEAD_DIM, :].astype(BF16)
            sc = _mm(qd_ref[0, h], kt) + sd_ref[h] * tail_pos
            sdif[h, :, past_len:past_len + LANES] = jnp.where(causal16, sc, NEG)

        n_cmp = kc_scr.shape[1]
        cend = lax.broadcasted_iota(jnp.int32, (nrow, n_cmp), 1) * L_CMP + (L_CMP - 1)
        qpos_c = past_len + lax.broadcasted_iota(jnp.int32, (nrow, n_cmp), 0) % dec
        imps = []
        for g in groups:
            kc = kc_scr[g * HEAD_DIM:(g + 1) * HEAD_DIM, :]
            kc = kc * lax.rsqrt(jnp.mean(kc * kc, axis=0, keepdims=True) + EPS) * cat([gkc_ref[...]] * (n_cmp // LANES))
            s_c = _mm(qn_ref[0, g], kc.astype(BF16)) + cat([sn_ref[g]] * (n_cmp // LANES)) * cend.astype(F32)
            p_c = _masked_softmax_rows(s_c, cend <= qpos_c)
            vc = kc_scr[128 + g * HEAD_DIM:128 + (g + 1) * HEAD_DIM, :].astype(BF16)
            oc_scr[g] = _nt(p_c.astype(BF16), vc)
            imps.append(sum(p_c[r * dec:(r + 1) * dec] for r in range(NSA_GROUP)))
        imp2 = jnp.concatenate(imps, axis=0)
        pair = (lax.broadcasted_iota(jnp.int32, (n_cmp, LANES), 0) // 2
                == lax.broadcasted_iota(jnp.int32, (n_cmp, LANES), 1)).astype(BF16)
        i_hi, i_lo = hi_lo(imp2)
        imp = _mm(i_hi, pair) + _mm(i_lo, pair)

        n_blk = past_len // L_SEL
        blk = lax.broadcasted_iota(jnp.int32, imp.shape, 1)
        blk_f = blk.astype(F32)
        bonus = jnp.where((blk == 0) | (blk == n_blk - 1), FORCE_BONUS, 0.0)
        score = jnp.where(blk < n_blk, imp + bonus, -jnp.inf)
        sel = jnp.zeros(imp.shape, F32)
        for _ in range(N_SELECT - 1):
            top = jnp.max(score, axis=1, keepdims=True)
            first = jnp.min(jnp.where(score == top, blk_f, 1e9), axis=1, keepdims=True)
            pick = blk_f == first
            sel = jnp.where(pick, jnp.where(top > -jnp.inf, 1.0, sel), sel)
            score = jnp.where(pick, -jnp.inf, score)
        sel = sel.astype(BF16)

        for c in range(n_steps):
            tok = c * span + lax.broadcasted_iota(jnp.int32, (LANES, span), 1)
            expand = ((tok >> 6) == lax.broadcasted_iota(jnp.int32, (LANES, span), 0)).astype(BF16)
            chosen = _mm(sel, expand)
            for g in groups:
                keep = jnp.concatenate([chosen[g * dec:(g + 1) * dec]] * NSA_GROUP, axis=0) > 0.5
                ssel[g, :, c * span:(c + 1) * span] = jnp.where(keep, ssel[g, :, c * span:(c + 1) * span], NEG)
        for g in groups:
            psel[g] = _softmax_rows(ssel[g]).astype(BF16)
        for h in heads:
            pdif[h] = _softmax_rows(sdif[h]).astype(BF16)

        w_buf = win_ref.shape[3]
        kwpos = past_len - w_buf + lax.broadcasted_iota(jnp.int32, (nrow, w_buf + LANES), 1)
        dist = past_len + lax.broadcasted_iota(jnp.int32, (nrow, w_buf + LANES), 0) % dec - kwpos
        in_window = (dist >= 0) & (dist < WINDOW)
        for g in groups:
            kw = cat([win_ref[0, 0, g * HEAD_DIM:(g + 1) * HEAD_DIM, :],
                      tail_scr[T_WIN + g * HEAD_DIM:T_WIN + (g + 1) * HEAD_DIM, :]]).astype(BF16)
            vw = cat([win_ref[0, 0, 128 + g * HEAD_DIM:128 + (g + 1) * HEAD_DIM, :],
                      tail_scr[T_WIN + 128 + g * HEAD_DIM:T_WIN + 128 + (g + 1) * HEAD_DIM, :]]).astype(BF16)
            s_w = _mm(qn_ref[0, g], kw) + cat([sn_ref[g]] * (w_buf // LANES + 1)) * kwpos.astype(F32)
            ow_scr[g] = _nt(_masked_softmax_rows(s_w, in_window).astype(BF16), vw)

        shifted = pltpu.roll(win_ref[0, 0], w_buf - dec, 1)
        fresh = cat([jnp.zeros((256, w_buf - LANES), F32), pltpu.roll(tail_scr[T_WIN:T_WIN + 256, :], LANES - dec, 1)])
        ns_ref[0] = jnp.where(lax.broadcasted_iota(jnp.int32, (256, w_buf), 1) >= w_buf - dec, fresh, shifted)

        lp = lp_ref[...]
        lam = (jnp.exp(jnp.sum(lp[0:1] * lp[1:2], keepdims=True))
               - jnp.exp(jnp.sum(lp[2:3] * lp[3:4], keepdims=True)) + lam_init)
        pieces = []
        for g in groups:
            vt = tail_scr[T_NSA + 384 + g * HEAD_DIM:T_NSA + 384 + (g + 1) * HEAD_DIM, :].astype(BF16)
            o_s = (_nt(psel[g, :, 0:past_len], vsel[g * HEAD_DIM:(g + 1) * HEAD_DIM, :])
                   + _nt(psel[g, :, past_len:past_len + LANES], vt))
            o = gate_ref[0, g, 0] * oc_scr[g] + gate_ref[0, g, 1] * o_s + gate_ref[0, g, 2] * ow_scr[g]
            pieces += [o[r * dec:(r + 1) * dec] for r in range(NSA_GROUP)]
        for h in heads:
            vt = tail_scr[T_DIF + 512 + h * HEAD_DIM:T_DIF + 512 + (h + 1) * HEAD_DIM, :].astype(BF16)
            o = (_nt(pdif[h, :, 0:past_len], vdif[h * HEAD_DIM:(h + 1) * HEAD_DIM, :])
                 + _nt(pdif[h, :, past_len:past_len + LANES], vt))
            d = o[0:dec] - lam * o[dec:2 * dec]
            pieces.append(d * lax.rsqrt(jnp.mean(d * d, axis=-1, keepdims=True) + EPS) * sg_ref[...] * (1.0 - lam_init))
        o_ref[0] = cat(pieces)


def _sattn(page_table, cache_nsa, cache_diff, state_win, new_nsa, new_win, new_dif, qn, qd, gates, sn, sd,
           wck, wcv, gkc, lp, sg, *, layer, lam_init):
    bs, n_pages = page_table.shape
    page = cache_nsa.shape[3]
    dec = new_nsa.shape[0] // bs
    past_len = n_pages * page
    pg = PAGES_PER_STEP
    n_steps = n_pages // pg
    w_buf = state_win.shape[3]
    assert page == LANES and n_pages % pg == 0 and dec == 8 and w_buf % LANES == 0
    n_cmp = -(-n_steps // STEPS_PER_CMP_TILE) * LANES
    tks = past_len + LANES
    pt = page_table.reshape(-1)

    def kpage(j):
        return lambda b, s, pt_ref: (layer, pt_ref[b * n_pages + jnp.minimum(s, n_steps - 1) * pg + j], 0, 0)

    const = lambda a: pl.BlockSpec(a.shape, lambda b, s, pt_ref: (0,) * a.ndim)
    per_b = lambda a: pl.BlockSpec((1,) + a.shape[1:], lambda b, s, pt_ref: (b,) + (0,) * (a.ndim - 1))
    in_specs = (
        [pl.BlockSpec((1, 1, 512, page), kpage(j)) for j in range(pg)]
        + [pl.BlockSpec((1, 1, 1024, page), kpage(j)) for j in range(pg)]
        + [pl.BlockSpec((1, 1, 256, w_buf), lambda b, s, pt_ref: (layer, b, 0, 0)),
           pl.BlockSpec((dec, 512), lambda b, s, pt_ref: (b, 0)),
           pl.BlockSpec((dec, 256), lambda b, s, pt_ref: (b, 0)),
           pl.BlockSpec((dec, 1024), lambda b, s, pt_ref: (b, 0)),
           per_b(qn), per_b(qd), per_b(gates), const(sn), const(sd), const(wck), const(wcv), const(gkc),
           const(lp), const(sg)])
    out_specs = (pl.BlockSpec((1, dec, 1024), lambda b, s, pt_ref: (b, 0, 0)),
                 pl.BlockSpec((1, 256, w_buf), lambda b, s, pt_ref: (b, 0, 0)))
    scratch = [
        pltpu.VMEM((NSA_KV, NSA_GROUP * dec, tks), F32), pltpu.VMEM((DIFF_HEADS, 2 * dec, tks), F32),
        pltpu.VMEM((NSA_KV, NSA_GROUP * dec, tks), BF16), pltpu.VMEM((DIFF_HEADS, 2 * dec, tks), BF16),
        pltpu.VMEM((NSA_KV * HEAD_DIM, past_len), BF16), pltpu.VMEM((DIFF_HEADS * HEAD_DIM, past_len), BF16),
        pltpu.VMEM((256, n_cmp), F32), pltpu.VMEM((T_ROWS, LANES), F32),
        pltpu.VMEM((NSA_KV, NSA_GROUP * dec, HEAD_DIM), F32), pltpu.VMEM((NSA_KV, NSA_GROUP * dec, HEAD_DIM), F32),
    ]
    grid_spec = pltpu.PrefetchScalarGridSpec(num_scalar_prefetch=1, grid=(bs, n_steps + 1), in_specs=in_specs,
                                             out_specs=out_specs, scratch_shapes=scratch)
    caches = [cache_nsa] * pg + [cache_diff] * pg
    return pl.pallas_call(
        functools.partial(_sattn_kernel, n_steps=n_steps, past_len=past_len, dec=dec, lam_init=lam_init),
        grid_spec=grid_spec,
        out_shape=(jax.ShapeDtypeStruct((bs, dec, 1024), F32), jax.ShapeDtypeStruct((bs, 256, w_buf), F32)),
        compiler_params=pltpu.CompilerParams(dimension_semantics=("arbitrary",) * 2, vmem_limit_bytes=VMEM_LIMIT),
        name="sattn",
    )(pt, *caches, state_win, new_nsa, new_win, new_dif, qn, qd, gates, sn, sd, wck, wcv, gkc, lp, sg)


def _alibi_slopes(n):
    return 2.0 ** (-8.0 * jnp.arange(1, n + 1, dtype=F32) / n)


def _slope_rows(slopes):
    slopes = slopes * LOG2E
    s0 = slopes.astype(BF16).astype(F32)
    s1 = (slopes - s0).astype(BF16).astype(F32)
    s2 = (slopes - s0 - s1).astype(BF16).astype(F32)
    cols = jnp.stack([64.0 * s0, s0, 64.0 * s1, s1, 64.0 * s2, s2], axis=1)
    out = jnp.zeros((slopes.shape[0], LANES), F32)
    return out.at[:, AUG0:AUG0 + N_AUG].set(cols)


def _block_ones(group):
    idx = np.arange(LANES) // group
    return jnp.asarray((idx[:, None] == idx[None, :]).astype(np.float32), dtype=BF16)


_GATE_SRC = np.full((LANES,), -1, np.int64)
for _g in range(NSA_KV):
    for _r in range(NSA_GROUP):
        for _j in range(3):
            _GATE_SRC[_g * 16 + _j * 4 + _r] = 1280 + _g * 12 + _r * 3 + _j


def _permute_w_in(w):
    main = jnp.concatenate([w[:, 0:1280], w[:, 1304:2840]], axis=1)
    gate = jnp.where(jnp.asarray(_GATE_SRC >= 0)[None, :], w[:, np.maximum(_GATE_SRC, 0)], 0.0)
    return jnp.concatenate([main, gate], axis=1).astype(BF16)


def _compress_weights(w):
    t = np.arange(PAGES_PER_STEP * LANES)
    place = jnp.asarray(t[:, None] // L_CMP == np.arange(LANES)[None, :])
    full = jnp.where(place, jnp.tile(w, PAGES_PER_STEP * LANES // L_CMP)[:, None], 0.0)
    hi = full.astype(BF16)
    return jnp.stack([hi, (full - hi.astype(F32)).astype(BF16)])


def _gain_row(nsa_qg, nsa_kg, diff_qg, diff_kg):
    one = lambda n: jnp.ones((n,), F32)
    parts = [jnp.tile(nsa_qg, NSA_HEADS), one(256), jnp.tile(nsa_kg[1], NSA_KV), one(128),
             jnp.tile(nsa_kg[2], NSA_KV), one(128), jnp.tile(diff_qg.reshape(-1), DIFF_HEADS),
             jnp.tile(diff_kg.reshape(-1), DIFF_HEADS), one(512 + LANES)]
    return jnp.concatenate(parts)[None, :]


def kernel(x_prompt, x_sample, cache_nsa, cache_diff, state_win, state_conv, page_table, norm1_g, norm2_g, w_in, w_out, nsa_qnorm_g, nsa_knorm_g, nsa_cmp_w, diff_qnorm_g, diff_knorm_g, diff_lambda, diff_subnorm_g, w_gate, w_up, conv_w, conv_b, w_down):
    batch, seq, d_model = x_prompt.shape
    dec_batch, dec_seq, _ = x_sample.shape
    depth, n_pool, page = cache_nsa.shape[:3]
    n_pages = page_table.shape[1]
    past_len = n_pages * page
    w_buf = state_win.shape[2]
    d_ff = w_gate.shape[2]
    assert dec_seq < L_CMP and seq % KEY_TILE == 0 and w_buf == WINDOW

    s64, s32 = _block_ones(HEAD_DIM), _block_ones(DIFF_HALF)
    qcn = _slope_rows(_alibi_slopes(NSA_HEADS))
    qcd = _slope_rows(_alibi_slopes(DIFF_HEADS))
    fmaj = lambda a: jnp.moveaxis(a, 2, -1).reshape(a.shape[0], a.shape[1], -1, a.shape[2])
    cache_nsa4 = fmaj(cache_nsa)
    cache_diff4 = fmaj(cache_diff)
    state_win4 = fmaj(state_win)

    sn = jnp.broadcast_to((_alibi_slopes(NSA_HEADS) * LOG2E).reshape(NSA_KV, NSA_GROUP, 1, 1),
                          (NSA_KV, NSA_GROUP, dec_seq, LANES)).reshape(NSA_KV, NSA_GROUP * dec_seq, LANES)
    sd = jnp.broadcast_to((_alibi_slopes(DIFF_HEADS) * LOG2E).reshape(DIFF_HEADS, 1, 1), (DIFF_HEADS, 2 * dec_seq, LANES))
    xp = x_prompt.reshape(batch * seq, d_model)
    xs = x_sample.reshape(dec_batch * dec_seq, d_model)

    nsa_t = jnp.zeros((depth, batch, 512, seq), F32)
    dif_t = jnp.zeros((depth, batch, 1024, seq), F32)
    outs = [[] for _ in range(8)]
    for l in range(depth):
        lam_init = 0.8 - 0.6 * math.exp(-0.3 * l)
        w_in_p = _permute_w_in(w_in[l])
        gain = _gain_row(nsa_qnorm_g[l], nsa_knorm_g[l], diff_qnorm_g[l], diff_knorm_g[l])
        gain_kc = jnp.tile(nsa_knorm_g[l, 0], NSA_KV)[None, :]
        cw = jnp.concatenate([jnp.broadcast_to(nsa_cmp_w[l, 0][:, None], (L_CMP, LANES)),
                              jnp.broadcast_to(nsa_cmp_w[l, 1][:, None], (L_CMP, LANES))], axis=1)
        lp = jnp.zeros((8, LANES), F32).at[0:4, 0:DIFF_HALF].set(diff_lambda[l])
        sg = jnp.broadcast_to(diff_subnorm_g[l][:, None], (HEAD_DIM, 2 * LANES))
        sg_row = jnp.broadcast_to(diff_subnorm_g[l][None, :], (dec_seq, HEAD_DIM))
        gkc = jnp.broadcast_to(nsa_knorm_g[l, 0][:, None], (HEAD_DIM, LANES))
        wck, wcv = _compress_weights(nsa_cmp_w[l, 0]), _compress_weights(nsa_cmp_w[l, 1])
        g1 = norm1_g[l][None, :]
        g2 = norm2_g[l][None, :]
        wo, wg, wu, wd = (a.astype(BF16) for a in (w_out[l], w_gate[l], w_up[l], w_down[l]))
        cwf, cbf = conv_w[l], conv_b[l][None, :]
        proj = functools.partial(_proj, g1=g1, w=w_in_p, gain=gain, s64=s64, s32=s32, qcn=qcn, qcd=qcd, cw=cw)

        (nsa_t, win_t, dif_t, qa, ksa, kwa, gt, dqa, dka, craw) = proj(
            xp, seq_len=seq, pos_base=0, stacked=(l, depth, (nsa_t, dif_t)))
        kc, vc = _cmpfin(craw.reshape(batch, seq // L_CMP, 256), s64, gain_kc)
        o_nsa = _nsa(qa, kc, vc, ksa.reshape(NSA_KV, batch, seq, LANES), nsa_t,
                     kwa.reshape(NSA_KV, batch, seq, LANES), win_t, gt,
                     layer=l, bx=batch, nqb=seq // Q_LANES, q_base=0, win_base=0)
        o_dif = _diff(dqa, dka.reshape(DIFF_HEADS, batch, seq, LANES), dif_t, sd[:, 0], lp, sg,
                      layer=l, bx=batch, nqb=seq // 256, tq=256, q_base=0, lam_init=lam_init)
        xp, cv = _ffn_prompt(xp, o_nsa, o_dif, wo, g2, wg, wu, cwf, cbf, wd, seq_len=seq)
        w_keep = min(WINDOW, seq)
        outs[4].append(jnp.moveaxis(win_t[:, :, seq - w_keep:].reshape(batch, 2, NSA_KV, HEAD_DIM, w_keep), -1, 1))
        outs[6].append(cv[:, 8 - 2:8])

        (nsa_rows, win_rows, dif_rows, qa, _, _, gt, dqa, _, _) = proj(xs, seq_len=dec_seq, pos_base=past_len)
        per_seq = lambda a, lead: a[..., :HEAD_DIM].reshape(lead + (dec_batch, dec_seq, HEAD_DIM))
        qn = jnp.transpose(per_seq(qa, (NSA_KV, NSA_GROUP)), (2, 0, 1, 3, 4)).reshape(
            dec_batch, NSA_KV, NSA_GROUP * dec_seq, HEAD_DIM)
        qd = jnp.transpose(per_seq(dqa, (DIFF_HEADS, 2)), (2, 0, 1, 3, 4)).reshape(
            dec_batch, DIFF_HEADS, 2 * dec_seq, HEAD_DIM)
        gates = jnp.transpose(gt.reshape(NSA_KV, 4, NSA_GROUP, dec_batch, dec_seq), (3, 0, 1, 2, 4)).reshape(
            dec_batch, NSA_KV, 4, NSA_GROUP * dec_seq, 1)
        gates = jnp.broadcast_to(gates, gates.shape[:-1] + (HEAD_DIM,))
        o_s, new_state = _sattn(page_table, cache_nsa4, cache_diff4, state_win4, nsa_rows, win_rows, dif_rows,
                                qn, qd, gates, sn, sd, wck, wcv, gkc, lp, sg_row, layer=l, lam_init=lam_init)
        o_s = o_s.reshape(dec_batch * dec_seq, 2 * NSA_HEADS * HEAD_DIM)
        st1 = jnp.repeat(state_conv[l][:, 1], dec_seq, axis=0)
        st2 = jnp.repeat(state_conv[l][:, 0], dec_seq, axis=0)
        xs, g_s = _ffn_sample(xs, o_s[:, :NSA_HEADS * HEAD_DIM], o_s[:, NSA_HEADS * HEAD_DIM:], st1, st2,
                              wo, g2, wg, wu, cwf, cbf, wd, seq=dec_seq)
        outs[1].append(nsa_rows.reshape(dec_batch, dec_seq, 4, NSA_KV, HEAD_DIM))
        outs[3].append(dif_rows.reshape(dec_batch, dec_seq, 2, DIFF_HEADS, HEAD_DIM))
        outs[5].append(jnp.moveaxis(new_state.reshape(dec_batch, 2, NSA_KV, HEAD_DIM, w_buf), -1, 1))
        outs[7].append(g_s.reshape(dec_batch, dec_seq, d_ff)[:, dec_seq - 2:])

    outs[0] = jnp.moveaxis(nsa_t.reshape(depth, batch, 4, NSA_KV, HEAD_DIM, seq), -1, 2)
    outs[2] = jnp.moveaxis(dif_t.reshape(depth, batch, 2, DIFF_HEADS, HEAD_DIM, seq), -1, 2)
    stacked = [o if not isinstance(o, list) else jnp.stack(o) for o in outs]
    return (xp.reshape(batch, seq, d_model), xs.reshape(dec_batch, dec_seq, d_model), *stacked)
```

```python
import functools
import math

import jax
import jax.numpy as jnp
import numpy as np
from jax import lax
from jax.experimental import pallas as pl
from jax.experimental.pallas import tpu as pltpu

F32 = jnp.float32
BF16 = jnp.bfloat16

HEAD_DIM = 64
NSA_KV = 2
NSA_GROUP = 4
NSA_HEADS = NSA_KV * NSA_GROUP
DIFF_HEADS = 8
DIFF_HALF = HEAD_DIM // 2
L_CMP = 32
L_SEL = 64
N_SELECT = 16
WINDOW = 512
FORCE_BONUS = 1.0e4
EPS = 1e-6
NEG = -1e30
UNDERFLOW_LOG2 = -160.0
NORM_SLACK = 1.02
LOG2E = math.log2(math.e)

LANES = 128
KEY_TILE = 512
Q_LANES = 128
DIFF_TQ = 512
WIN_ROWS = WINDOW + Q_LANES
AUG0 = HEAD_DIM
N_AUG = 6
VMEM_LIMIT = 56 * 1024 * 1024

C_Q, C_NSA, C_WIN, C_DQ, C_DK, C_DV, C_GATE = 0, 512, 1024, 1280, 1792, 2304, 2816
N_COL = 2944


def _nt(a, b):
    return lax.dot_general(a, b, (((1,), (1,)), ((), ())), preferred_element_type=F32)


def _tn(a, b):
    return lax.dot_general(a, b, (((0,), (0,)), ((), ())), preferred_element_type=F32)


def _mm(a, b):
    return jnp.dot(a, b, preferred_element_type=F32)


def _group_meansq(z, smat, inv_n):
    zz = z * z
    hi = zz.astype(BF16)
    lo = (zz - hi.astype(F32)).astype(BF16)
    return (_mm(hi, smat) + _mm(lo, smat)) * inv_n


def _pos_rows(pos, lane):
    is_aug = (lane >= AUG0) & (lane < AUG0 + N_AUG)
    val = jnp.where((lane & 1) == 0, pos >> 6, pos & 63)
    return jnp.where(is_aug, val, 0).astype(F32)


def _split_heads(slab, fill, lane):
    even = jnp.where(lane < HEAD_DIM, slab, fill)
    odd = jnp.where(lane < HEAD_DIM, pltpu.roll(slab, HEAD_DIM, 1), fill)
    return even, odd


def _proj_kernel(x_ref, g1_ref, w_ref, gain_ref, s64_ref, s32_ref, qcn_ref, qcd_ref, cw_ref, *rest,
                 seq_len, pos_base, fmajor):
    nsa_ref, win_ref, dif_ref, qa_ref, ksa_ref, kwa_ref, gt_ref, dqa_ref, dka_ref, cr_ref = rest[-10:]
    tr = x_ref.shape[0]

    def put(ref, c0, slab):
        if fmajor:
            ref[(0,) * (len(ref.shape) - 2) + (slice(c0, c0 + LANES), slice(None))] = slab.T
        else:
            ref[:, c0:c0 + LANES] = slab
    i = pl.program_id(0)
    x = x_ref[...]
    h = x * lax.rsqrt(jnp.mean(x * x, axis=-1, keepdims=True) + EPS) * g1_ref[...]
    hb = h.astype(BF16)

    lane = lax.broadcasted_iota(jnp.int32, (tr, LANES), 1)
    row = lax.broadcasted_iota(jnp.int32, (tr, LANES), 0)
    pos = pos_base + (i * tr + row) % seq_len
    prow = _pos_rows(pos, lane)
    s64 = s64_ref[...]
    s32 = s32_ref[...]

    def seg(c0, width):
        return _mm(hb, w_ref[:, c0:c0 + width])

    def normed(z, c0, smat, inv_n):
        return z * lax.rsqrt(_group_meansq(z, smat, inv_n) + EPS) * gain_ref[:, c0:c0 + LANES]

    zq = seg(C_Q, 512)
    for s in range(4):
        zn = normed(zq[:, s * LANES:(s + 1) * LANES], C_Q + s * LANES, s64, 1.0 / HEAD_DIM) * (HEAD_DIM ** -0.5 * LOG2E)
        for par in range(2):
            hd = 2 * s + par
            src = zn if par == 0 else pltpu.roll(zn, HEAD_DIM, 1)
            qa = jnp.where(lane < HEAD_DIM, src, qcn_ref[hd:hd + 1, :])
            qa_ref[hd // NSA_GROUP, hd % NSA_GROUP] = qa.astype(BF16)

    zc = seg(C_NSA, 512)
    ks = normed(zc[:, 256:384], C_NSA + 256, s64, 1.0 / HEAD_DIM)
    put(nsa_ref, 0, zc[:, 0:128])
    put(nsa_ref, 128, zc[:, 128:256])
    put(nsa_ref, 256, ks)
    put(nsa_ref, 384, zc[:, 384:512])
    k0, k1 = _split_heads(ks, prow, lane)
    ksa_ref[0] = k0.astype(BF16)
    ksa_ref[1] = k1.astype(BF16)
    craw = zc[:, 0:256].reshape(tr // L_CMP, L_CMP, 256) * cw_ref[...][None]
    cr_ref[...] = jnp.sum(craw, axis=1)

    zw = seg(C_WIN, 256)
    kw = normed(zw[:, 0:128], C_WIN, s64, 1.0 / HEAD_DIM)
    put(win_ref, 0, kw)
    put(win_ref, 128, zw[:, 128:256])
    k0, k1 = _split_heads(kw, prow, lane)
    kwa_ref[0] = k0.astype(BF16)
    kwa_ref[1] = k1.astype(BF16)

    zdq = seg(C_DQ, 512)
    for s in range(4):
        zn = normed(zdq[:, s * LANES:(s + 1) * LANES], C_DQ + s * LANES, s32, 1.0 / DIFF_HALF) * (DIFF_HALF ** -0.5 * LOG2E)
        for par in range(2):
            hd = 2 * s + par
            src = zn if par == 0 else pltpu.roll(zn, HEAD_DIM, 1)
            fill = qcd_ref[hd:hd + 1, :]
            dqa_ref[hd, 0] = jnp.where(lane < DIFF_HALF, src, fill).astype(BF16)
            dqa_ref[hd, 1] = jnp.where((lane >= DIFF_HALF) & (lane < HEAD_DIM), src, fill).astype(BF16)

    zdk = seg(C_DK, 512)
    for s in range(4):
        kn = normed(zdk[:, s * LANES:(s + 1) * LANES], C_DK + s * LANES, s32, 1.0 / DIFF_HALF)
        put(dif_ref, s * LANES, kn)
        k0, k1 = _split_heads(kn, prow, lane)
        dka_ref[2 * s] = k0.astype(BF16)
        dka_ref[2 * s + 1] = k1.astype(BF16)
    zdv = seg(C_DV, 512)
    for s in range(4):
        put(dif_ref, 512 + s * LANES, zdv[:, s * LANES:(s + 1) * LANES])

    zg = seg(C_GATE, LANES)
    sg = 1.0 / (1.0 + jnp.exp(-zg))
    gt_ref[...] = sg.T[0:2 * 16].reshape(NSA_KV, 16, tr)


def _proj(x2d, g1, w, gain, s64, s32, qcn, qcd, cw, *, seq_len, pos_base, stacked=None, tr=512):
    n = x2d.shape[0]
    tr = min(tr, n)
    assert n % tr == 0 and tr % L_CMP == 0
    fmajor = stacked is not None
    kern = functools.partial(_proj_kernel, seq_len=seq_len, pos_base=pos_base, fmajor=fmajor)
    full = lambda shape: pl.BlockSpec(shape, lambda i: (0,) * len(shape))
    if fmajor:
        layer, depth, prev = stacked
        assert seq_len % tr == 0
        tps, nb = seq_len // tr, n // seq_len
        cache_shapes = [jax.ShapeDtypeStruct((depth, nb, 512, seq_len), F32),
                        jax.ShapeDtypeStruct((nb, 256, seq_len), F32),
                        jax.ShapeDtypeStruct((depth, nb, 1024, seq_len), F32)]
        cache_specs = [pl.BlockSpec((1, 1, 512, tr), lambda i: (layer, i // tps, 0, i % tps)),
                       pl.BlockSpec((1, 256, tr), lambda i: (i // tps, 0, i % tps)),
                       pl.BlockSpec((1, 1, 1024, tr), lambda i: (layer, i // tps, 0, i % tps))]
    else:
        prev = None
        cache_shapes = [jax.ShapeDtypeStruct((n, w_), F32) for w_ in (512, 256, 1024)]
        cache_specs = [pl.BlockSpec((tr, w_), lambda i: (i, 0)) for w_ in (512, 256, 1024)]
    out_shape = tuple(cache_shapes) + (
        jax.ShapeDtypeStruct((NSA_KV, NSA_GROUP, n, LANES), BF16),
        jax.ShapeDtypeStruct((NSA_KV, n, LANES), BF16),
        jax.ShapeDtypeStruct((NSA_KV, n, LANES), BF16),
        jax.ShapeDtypeStruct((NSA_KV, 16, n), F32),
        jax.ShapeDtypeStruct((DIFF_HEADS, 2, n, LANES), BF16),
        jax.ShapeDtypeStruct((DIFF_HEADS, n, LANES), BF16),
        jax.ShapeDtypeStruct((n // L_CMP, 256), F32),
    )
    out_specs = tuple(cache_specs) + (
        pl.BlockSpec((NSA_KV, NSA_GROUP, tr, LANES), lambda i: (0, 0, i, 0)),
        pl.BlockSpec((NSA_KV, tr, LANES), lambda i: (0, i, 0)),
        pl.BlockSpec((NSA_KV, tr, LANES), lambda i: (0, i, 0)),
        pl.BlockSpec((NSA_KV, 16, tr), lambda i: (0, 0, i)),
        pl.BlockSpec((DIFF_HEADS, 2, tr, LANES), lambda i: (0, 0, i, 0)),
        pl.BlockSpec((DIFF_HEADS, tr, LANES), lambda i: (0, i, 0)),
        pl.BlockSpec((tr // L_CMP, 256), lambda i: (i, 0)),
    )
    in_specs = [
        pl.BlockSpec((tr, x2d.shape[1]), lambda i: (i, 0)),
        full(g1.shape), full(w.shape), full(gain.shape), full(s64.shape), full(s32.shape),
        full(qcn.shape), full(qcd.shape), full(cw.shape),
    ]
    args = [x2d, g1, w, gain, s64, s32, qcn, qcd, cw]
    aliases = {}
    if prev is not None:
        in_specs += [pl.BlockSpec(memory_space=pl.ANY)] * 2
        aliases = {len(args): 0, len(args) + 1: 2}
        args += list(prev)
    return pl.pallas_call(
        kern, grid=(n // tr,), in_specs=in_specs, out_specs=out_specs, out_shape=out_shape,
        input_output_aliases=aliases,
        compiler_params=pltpu.CompilerParams(dimension_semantics=("arbitrary",), vmem_limit_bytes=VMEM_LIMIT),
        name="proj",
    )(*args)


def _cmpfin_kernel(raw_ref, s64_ref, gain_ref, kc_ref, vc_ref, k_scr, v_scr, *, nsp):
    lane = lax.broadcasted_iota(jnp.int32, (nsp, LANES), 1)
    n = lax.broadcasted_iota(jnp.int32, (nsp, LANES), 0)
    k_scr[...] = raw_ref[0, :, 0:128]
    v_scr[...] = raw_ref[0, :, 128:256]
    for j in range(2):
        k = k_scr[pl.ds(j, nsp, stride=2), :]
        kn = k * lax.rsqrt(_group_meansq(k, s64_ref[...], 1.0 / HEAD_DIM) + EPS) * gain_ref[...]
        cend = n * L_SEL + (L_CMP - 1 + L_CMP * j)
        prow = _pos_rows(cend, lane)
        k0, k1 = _split_heads(kn, prow, lane)
        kc_ref[0, 0, j * nsp:(j + 1) * nsp, :] = k0.astype(BF16)
        kc_ref[1, 0, j * nsp:(j + 1) * nsp, :] = k1.astype(BF16)
        vc_ref[0, j * nsp:(j + 1) * nsp, :] = v_scr[pl.ds(j, nsp, stride=2), :].astype(BF16)


def _cmpfin(raw, s64, gain_kc):
    bx, n2, _ = raw.shape
    nsp = n2 // 2
    assert nsp % 16 == 0
    return pl.pallas_call(
        functools.partial(_cmpfin_kernel, nsp=nsp),
        grid=(bx,),
        in_specs=[pl.BlockSpec((1, n2, 256), lambda b: (b, 0, 0)),
                  pl.BlockSpec(s64.shape, lambda b: (0, 0)),
                  pl.BlockSpec(gain_kc.shape, lambda b: (0, 0))],
        out_specs=(pl.BlockSpec((NSA_KV, 1, n2, LANES), lambda b: (0, b, 0, 0)),
                   pl.BlockSpec((1, n2, LANES), lambda b: (b, 0, 0))),
        out_shape=(jax.ShapeDtypeStruct((NSA_KV, bx, n2, LANES), BF16),
                   jax.ShapeDtypeStruct((bx, n2, LANES), BF16)),
        scratch_shapes=[pltpu.VMEM((n2, LANES), F32), pltpu.VMEM((n2, LANES), F32)],
        name="cmpfin",
    )(raw, s64, gain_kc)


def _flash_init(c, m_ref, l_ref, acc_ref):
    m_ref[c] = jnp.full(m_ref.shape[1:], NEG, F32)
    l_ref[c] = jnp.zeros(l_ref.shape[1:], F32)
    acc_ref[c] = jnp.zeros(acc_ref.shape[1:], F32)


def _flash_probs(c, s, m_ref, l_ref):
    m_old = m_ref[c]
    m_new = jnp.maximum(m_old, jnp.max(s, axis=0, keepdims=True))
    p = jnp.exp2(s - m_new)
    a = jnp.exp2(m_old - m_new)
    l_ref[c] = a * l_ref[c] + jnp.sum(p, axis=0, keepdims=True)
    m_ref[c] = m_new
    return p.astype(BF16), a


def _flash_update(c, s, vt_tile, m_ref, l_ref, acc_ref):
    p, a = _flash_probs(c, s, m_ref, l_ref)
    acc_ref[c] = a * acc_ref[c] + _mm(vt_tile, p)


def _max_key_norm(k_ref_2d, n_keys):
    feat = lax.broadcasted_iota(jnp.int32, (1, LANES), 1) < HEAD_DIM

    def chunk(c, best):
        kf = k_ref_2d[pl.ds(pl.multiple_of(c * KEY_TILE, KEY_TILE), KEY_TILE), :].astype(F32)
        sq = jnp.sum(jnp.where(feat, kf * kf, 0.0), axis=1, keepdims=True)
        return jnp.maximum(best, jnp.max(sq, axis=0, keepdims=True))

    return jnp.sqrt(lax.fori_loop(0, n_keys // KEY_TILE, chunk, jnp.zeros((1, 1), F32)))


def _first_needed_tile(qa, m, k_norm, slope_row, n_max):
    feat = lax.broadcasted_iota(jnp.int32, (1, LANES), 1) < HEAD_DIM
    qf = qa.astype(F32)
    qsq = jnp.where(feat, qf * qf, 0.0)
    hi = qsq.astype(BF16)
    lo = (qsq - hi.astype(F32)).astype(BF16)
    ones = jnp.ones((8, LANES), BF16)
    q_norm = jnp.sqrt((_nt(ones, hi) + _nt(ones, lo))[0:1])
    cutoff = (m + UNDERFLOW_LOG2 - NORM_SLACK * q_norm * k_norm) / slope_row
    tile = jnp.floor(jnp.min(cutoff, axis=1, keepdims=True) * (1.0 / KEY_TILE))
    return jnp.clip(tile, 0.0, n_max.astype(F32)).astype(jnp.int32)[0, 0]


def _sweep_earlier_tiles(first, n_tiles, step):
    both = (0, 1)

    @pl.when(jnp.logical_and(n_tiles % 2 == 1, jnp.minimum(first[0], first[1]) < n_tiles))
    def _():
        step(pl.multiple_of((n_tiles - 1) * KEY_TILE, KEY_TILE), KEY_TILE, both)

    j_end = n_tiles // 2
    j0 = [jnp.minimum(f // 2, j_end) for f in first]
    j_both = jnp.maximum(j0[0], j0[1])

    def run(chains):
        def body(j, carry):
            step(pl.multiple_of(j * 2 * KEY_TILE, 2 * KEY_TILE), 2 * KEY_TILE, chains)
            return carry
        return body

    lax.fori_loop(j0[0], j_both, run((0,)), 0)
    lax.fori_loop(j0[1], j_both, run((1,)), 0)
    lax.fori_loop(j_both, j_end, run(both), 0)


def _masked_softmax_cols(s, valid):
    sm = jnp.where(valid, s, NEG)
    m = jnp.max(sm, axis=0, keepdims=True)
    p = jnp.where(valid, jnp.exp2(sm - m), 0.0)
    l = jnp.sum(p, axis=0, keepdims=True)
    return p * jnp.where(l > 0.0, 1.0 / l, 0.0)


def _nsa_kernel(q_ref, kc_ref, vc_ref, ks_ref, vs_ref, kw_ref, vw_ref, gt_ref, o_ref,
                sel_ref, acc_ref, m_ref, l_ref, kn_ref, *, q_base, nsp, tw, win_base):
    i = pl.program_id(1)
    q0 = q_base + i * Q_LANES
    ncol = NSA_GROUP * Q_LANES
    groups = range(NSA_KV)
    qas = [q_ref[g].reshape(ncol, LANES) for g in groups]
    lane_c = lax.broadcasted_iota(jnp.int32, (1, ncol), 1)
    qpos_c = q0 + (lane_c & (Q_LANES - 1))
    rows_of = lambda g: slice(g * HEAD_DIM, (g + 1) * HEAD_DIM)

    n_row = lax.broadcasted_iota(jnp.int32, (nsp, ncol), 0)
    valid_e = (n_row * L_SEL + (L_CMP - 1)) <= qpos_c
    valid_o = (n_row * L_SEL + (2 * L_CMP - 1)) <= qpos_c
    blk = lax.broadcasted_iota(jnp.int32, (nsp, Q_LANES), 0)
    qpos_q = q0 + lax.broadcasted_iota(jnp.int32, (nsp, Q_LANES), 1)
    cur = qpos_q >> 6
    bonus = jnp.where((blk == 0) | (blk == cur) | (blk == cur - 1), FORCE_BONUS, 0.0)
    in_range = blk <= cur
    blk_f = blk.astype(F32)
    o_c = []
    for g in groups:
        s_c = _nt(kc_ref[g, 0], qas[g])
        sm_e = jnp.where(valid_e, s_c[0:nsp], NEG)
        sm_o = jnp.where(valid_o, s_c[nsp:2 * nsp], NEG)
        m = jnp.maximum(jnp.max(sm_e, axis=0, keepdims=True), jnp.max(sm_o, axis=0, keepdims=True))
        p_e = jnp.where(valid_e, jnp.exp2(sm_e - m), 0.0)
        p_o = jnp.where(valid_o, jnp.exp2(sm_o - m), 0.0)
        l = jnp.sum(p_e, axis=0, keepdims=True) + jnp.sum(p_o, axis=0, keepdims=True)
        inv = jnp.where(l > 0.0, 1.0 / l, 0.0)
        p_e = p_e * inv
        p_o = p_o * inv
        pcat = jnp.concatenate([p_e, p_o], axis=0).astype(BF16)
        o_c.append(_tn(vc_ref[0], pcat)[rows_of(g)])

        imp = p_e[:, 0:Q_LANES] + p_o[:, 0:Q_LANES]
        for r in range(1, NSA_GROUP):
            imp = imp + p_e[:, r * Q_LANES:(r + 1) * Q_LANES] + p_o[:, r * Q_LANES:(r + 1) * Q_LANES]
        score = jnp.where(in_range, imp + bonus, -jnp.inf)
        sel = jnp.zeros((nsp, Q_LANES), F32)
        for _ in range(N_SELECT):
            top = jnp.max(score, axis=0, keepdims=True)
            first = jnp.min(jnp.where(score == top, blk_f, 1e9), axis=0, keepdims=True)
            pick = blk_f == first
            sel = jnp.where(pick, jnp.where(top > -jnp.inf, 1.0, sel), sel)
            score = jnp.where(pick, -jnp.inf, score)
        sel_ref[g] = sel

    @pl.when(i == 0)
    def _():
        for g in groups:
            kn_ref[g] = jnp.broadcast_to(_max_key_norm(ks_ref.at[g, 0], ks_ref.shape[2]), kn_ref.shape[1:])

    for g in groups:
        _flash_init(g, m_ref, l_ref, acc_ref)
    def sel_tile(k0, size, which):
        n_blk = size // L_SEL
        causal = (k0 + lax.broadcasted_iota(jnp.int32, (size, Q_LANES), 0)
                  <= q0 + lax.broadcasted_iota(jnp.int32, (size, Q_LANES), 1))
        for g in which:
            vt = vs_ref[0, 0, rows_of(g), pl.ds(k0, size)].astype(BF16)
            s = _nt(ks_ref[g, 0, pl.ds(k0, size), :], qas[g])
            sel_rows = sel_ref[g, pl.ds(pl.multiple_of(k0 // L_SEL, 8), n_blk), :]
            selt = jnp.concatenate(
                [jnp.broadcast_to(sel_rows[j:j + 1, :], (L_SEL, Q_LANES)) for j in range(n_blk)], axis=0)
            valid = jnp.where(causal, selt, 0.0) > 0.0
            s = jnp.concatenate(
                [jnp.where(valid, s[:, r * Q_LANES:(r + 1) * Q_LANES], NEG) for r in range(NSA_GROUP)], axis=1)
            _flash_update(g, s, vt, m_ref, l_ref, acc_ref)

    n_before = (q0 + Q_LANES + KEY_TILE - 1) // KEY_TILE - 1
    k_own = pl.multiple_of(n_before * KEY_TILE, KEY_TILE)
    used = q0 - n_before * KEY_TILE + Q_LANES
    for size in range(Q_LANES, KEY_TILE + 1, Q_LANES):
        @pl.when(used == size)
        def _(size=size):
            sel_tile(k_own, size, groups)

    first = []
    for g in groups:
        slope = jnp.concatenate([jnp.full((1, Q_LANES), LOG2E * 2.0 ** -(g * NSA_GROUP + r + 1), F32)
                                 for r in range(NSA_GROUP)], axis=1)
        first.append(_first_needed_tile(qas[g], m_ref[g], kn_ref[g][0:1, 0:1], slope, n_before))
    _sweep_earlier_tiles(first, n_before, sel_tile)

    w0 = jnp.clip(q0 - win_base - WINDOW, 0, tw - WIN_ROWS)
    w0 = pl.multiple_of(w0, Q_LANES)
    kpos = win_base + w0 + lax.broadcasted_iota(jnp.int32, (WIN_ROWS, ncol), 0)
    dist = qpos_c - kpos
    in_window = (dist >= 0) & (dist < WINDOW)

    outs = []
    for g in groups:
        l_s = l_ref[g]
        o_s = acc_ref[g] * jnp.where(l_s > 0.0, 1.0 / l_s, 0.0)
        p_w = _masked_softmax_cols(_nt(kw_ref[g, 0, pl.ds(w0, WIN_ROWS), :], qas[g]), in_window)
        o_w = _mm(vw_ref[0, rows_of(g), pl.ds(w0, WIN_ROWS)].astype(BF16), p_w.astype(BF16))
        gt = gt_ref[g]
        for r in range(NSA_GROUP):
            cs = slice(r * Q_LANES, (r + 1) * Q_LANES)
            outs.append(gt[r:r + 1, :] * o_c[g][:, cs] + gt[4 + r:5 + r, :] * o_s[:, cs]
                        + gt[8 + r:9 + r, :] * o_w[:, cs])
    o_ref[...] = jnp.concatenate(outs, axis=0).T


def _nsa(qa, kc, vc, ksa, nsa_t, kwa, win_t, gt, *, layer, bx, nqb, q_base, win_base):
    nq = qa.shape[2]
    n2 = kc.shape[2]
    tk = ksa.shape[2]
    tw = kwa.shape[2]
    assert nq == bx * nqb * Q_LANES and tk % KEY_TILE == 0 and tw >= WIN_ROWS
    assert nsa_t.shape[2:] == (512, tk) and win_t.shape[1:] == (256, tw)
    kern = functools.partial(_nsa_kernel, q_base=q_base, nsp=n2 // 2, tw=tw, win_base=win_base)
    ncol = NSA_GROUP * Q_LANES
    return pl.pallas_call(
        kern, grid=(bx, nqb),
        in_specs=[
            pl.BlockSpec((NSA_KV, NSA_GROUP, Q_LANES, LANES), lambda b, i: (0, 0, b * nqb + i, 0)),
            pl.BlockSpec((NSA_KV, 1, n2, LANES), lambda b, i: (0, b, 0, 0)),
            pl.BlockSpec((1, n2, LANES), lambda b, i: (b, 0, 0)),
            pl.BlockSpec((NSA_KV, 1, tk, LANES), lambda b, i: (0, b, 0, 0)),
            pl.BlockSpec((1, 1, LANES, tk), lambda b, i: (layer, b, 3, 0)),
            pl.BlockSpec((NSA_KV, 1, tw, LANES), lambda b, i: (0, b, 0, 0)),
            pl.BlockSpec((1, LANES, tw), lambda b, i: (b, 1, 0)),
            pl.BlockSpec((NSA_KV, 16, Q_LANES), lambda b, i: (0, 0, b * nqb + i)),
        ],
        out_specs=pl.BlockSpec((Q_LANES, NSA_HEADS * HEAD_DIM), lambda b, i: (b * nqb + i, 0)),
        out_shape=jax.ShapeDtypeStruct((nq, NSA_HEADS * HEAD_DIM), F32),
        scratch_shapes=[pltpu.VMEM((NSA_KV, n2 // 2, Q_LANES), F32), pltpu.VMEM((NSA_KV, HEAD_DIM, ncol), F32),
                        pltpu.VMEM((NSA_KV, 1, ncol), F32), pltpu.VMEM((NSA_KV, 1, ncol), F32),
                        pltpu.VMEM((NSA_KV, 8, LANES), F32)],
        compiler_params=pltpu.CompilerParams(dimension_semantics=("arbitrary",) * 2, vmem_limit_bytes=VMEM_LIMIT),
        name="nsa",
    )(qa, kc, vc, ksa, nsa_t, kwa, win_t, gt)


def _diff_kernel(q_ref, k_ref, v_ref, sl_ref, lp_ref, sg_ref, o_ref, acc_ref, m_ref, l_ref, kn_ref, *,
                 q_base, tq, lam_init):
    pair = pl.program_id(1)
    i = pl.program_id(2)
    q0 = q_base + i * tq
    ncol = 2 * tq
    heads = range(2)
    lp = lp_ref[...]
    lam = (jnp.exp(jnp.sum(lp[0:1] * lp[1:2], keepdims=True)) - jnp.exp(jnp.sum(lp[2:3] * lp[3:4], keepdims=True))
           + lam_init)
    n_full = q0 // KEY_TILE
    qas = [q_ref[hh].reshape(ncol, LANES) for hh in heads]
    for hh in heads:
        _flash_init(hh, m_ref, l_ref, acc_ref)

    @pl.when(i == 0)
    def _():
        for hh in heads:
            kn_ref[hh] = jnp.broadcast_to(_max_key_norm(k_ref.at[hh, 0], k_ref.shape[2]), kn_ref.shape[1:])

    def step(k0, size, which, causal=None):
        for hh in which:
            vt = v_ref[0, 0, hh * HEAD_DIM:(hh + 1) * HEAD_DIM, pl.ds(k0, size)].astype(BF16)
            s = _nt(k_ref[hh, 0, pl.ds(k0, size), :], qas[hh])
            if causal is not None:
                s = jnp.where(causal, s, NEG)
            _flash_update(hh, s, vt, m_ref, l_ref, acc_ref)

    k_diag = pl.multiple_of(n_full * KEY_TILE, KEY_TILE)

    tok = k_diag + lax.broadcasted_iota(jnp.int32, (KEY_TILE, ncol), 0)
    step(k_diag, KEY_TILE, heads, tok <= q0 + lax.broadcasted_iota(jnp.int32, (KEY_TILE, ncol), 1) % tq)

    first = []
    for hh in heads:
        slope = jnp.concatenate([sl_ref[pl.ds(2 * pair + hh, 1), :]] * (ncol // LANES), axis=1)
        first.append(_first_needed_tile(qas[hh], m_ref[hh], kn_ref[hh][0:1, 0:1], slope, n_full))
    _sweep_earlier_tiles(first, n_full, step)

    outs = []
    for hh in heads:
        o = acc_ref[hh] / l_ref[hh]
        d = o[:, 0:tq] - lam * o[:, tq:ncol]
        d = d * lax.rsqrt(jnp.mean(d * d, axis=0, keepdims=True) + EPS) * sg_ref[:, 0:tq] * (1.0 - lam_init)
        outs.append(d)
    o_ref[...] = jnp.concatenate(outs, axis=0).T


def _diff(dqa, dka, dif_t, slopes, lp, sg, *, layer, bx, nqb, tq, q_base, lam_init):
    nq = dqa.shape[2]
    tk = dka.shape[2]
    assert nq == bx * nqb * tq and tk % KEY_TILE == 0 and KEY_TILE % tq == 0 and dif_t.shape[2:] == (1024, tk)
    kern = functools.partial(_diff_kernel, q_base=q_base, tq=tq, lam_init=lam_init)
    return pl.pallas_call(
        kern, grid=(bx, DIFF_HEADS // 2, nqb),
        in_specs=[
            pl.BlockSpec((2, 2, tq, LANES), lambda b, h, i: (h, 0, b * nqb + i, 0)),
            pl.BlockSpec((2, 1, tk, LANES), lambda b, h, i: (h, b, 0, 0)),
            pl.BlockSpec((1, 1, LANES, tk), lambda b, h, i: (layer, b, 4 + h, 0)),
            pl.BlockSpec(slopes.shape, lambda b, h, i: (0, 0)),
            pl.BlockSpec(lp.shape, lambda b, h, i: (0, 0)),
            pl.BlockSpec(sg.shape, lambda b, h, i: (0, 0)),
        ],
        out_specs=pl.BlockSpec((tq, LANES), lambda b, h, i: (b * nqb + i, h)),
        out_shape=jax.ShapeDtypeStruct((nq, DIFF_HEADS * HEAD_DIM), F32),
        scratch_shapes=[pltpu.VMEM((2, HEAD_DIM, 2 * tq), F32), pltpu.VMEM((2, 1, 2 * tq), F32),
                        pltpu.VMEM((2, 1, 2 * tq), F32), pltpu.VMEM((2, 8, LANES), F32)],
        compiler_params=pltpu.CompilerParams(dimension_semantics=("arbitrary",) * 3, vmem_limit_bytes=VMEM_LIMIT),
        name="diff",
    )(dqa, dka, dif_t, slopes, lp, sg)


F_CHUNKS = 2


def _ffn_core(x, on, od, wo_ref, g2_ref, wg_ref, wu_ref, cw_ref, cb_ref, wd_ref, prev_rows, g_store):
    o = jnp.concatenate([on, od], axis=1).astype(BF16)
    xm = x + _mm(o, wo_ref[...])
    h2 = (xm * lax.rsqrt(jnp.mean(xm * xm, axis=-1, keepdims=True) + EPS) * g2_ref[...]).astype(BF16)
    d_ff = wg_ref.shape[1]
    fc = d_ff // F_CHUNKS
    y = jnp.zeros(x.shape, F32)
    for c in range(F_CHUNKS):
        c0, c1 = c * fc, (c + 1) * fc
        g = _mm(h2, wg_ref[:, c0:c1])
        u = _mm(h2, wu_ref[:, c0:c1])
        gm1, gm2 = prev_rows(g, c0, c1)
        g_store(g, c0, c1)
        gc = cb_ref[:, c0:c1] + cw_ref[0:1, c0:c1] * gm2 + cw_ref[1:2, c0:c1] * gm1 + cw_ref[2:3, c0:c1] * g
        act = gc * (1.0 / (1.0 + jnp.exp(-gc))) * u
        y = y + _mm(act.astype(BF16), wd_ref[c0:c1, :])
    return xm + y


def _ffn_prompt_kernel(x_ref, on_ref, od_ref, wo_ref, g2_ref, wg_ref, wu_ref, cw_ref, cb_ref, wd_ref,
                       y_ref, cv_ref, carry_ref, *, tiles_per_seq):
    i = pl.program_id(0)
    tr = x_ref.shape[0]

    @pl.when(i % tiles_per_seq == 0)
    def _():
        carry_ref[...] = jnp.zeros(carry_ref.shape, F32)

    def prev_rows(g, c0, c1):
        row = lax.broadcasted_iota(jnp.int32, g.shape, 0)
        p1 = carry_ref[7:8, c0:c1]
        p2 = carry_ref[6:7, c0:c1]
        gm1 = jnp.where(row == 0, p1, pltpu.roll(g, 1, 0))
        gm2 = jnp.where(row == 0, p2, jnp.where(row == 1, p1, pltpu.roll(g, 2, 0)))
        return gm1, gm2

    def g_store(g, c0, c1):
        carry_ref[:, c0:c1] = g[tr - 8:tr]
        cv_ref[0, :, c0:c1] = g[tr - 8:tr]

    y_ref[...] = _ffn_core(x_ref[...], on_ref[...], od_ref[...], wo_ref, g2_ref, wg_ref, wu_ref, cw_ref, cb_ref,
                           wd_ref, prev_rows, g_store)


def _ffn_sample_kernel(x_ref, on_ref, od_ref, st1_ref, st2_ref, wo_ref, g2_ref, wg_ref, wu_ref, cw_ref, cb_ref,
                       wd_ref, y_ref, g_ref, *, seq):
    def prev_rows(g, c0, c1):
        rs = lax.broadcasted_iota(jnp.int32, g.shape, 0) % seq
        s1 = st1_ref[:, c0:c1]
        gm1 = jnp.where(rs == 0, s1, pltpu.roll(g, 1, 0))
        gm2 = jnp.where(rs == 0, st2_ref[:, c0:c1], jnp.where(rs == 1, s1, pltpu.roll(g, 2, 0)))
        return gm1, gm2

    def g_store(g, c0, c1):
        g_ref[:, c0:c1] = g

    y_ref[...] = _ffn_core(x_ref[...], on_ref[...], od_ref[...], wo_ref, g2_ref, wg_ref, wu_ref, cw_ref, cb_ref,
                           wd_ref, prev_rows, g_store)


def _const_spec(a):
    return pl.BlockSpec(a.shape, lambda i: (0,) * a.ndim, pipeline_mode=pl.Buffered(1))


def _ffn_prompt(x2d, on, od, wo, g2, wg, wu, cw, cb, wd, *, seq_len, tr=256):
    n, d = x2d.shape
    f = wg.shape[1]
    assert n % tr == 0 and seq_len % tr == 0
    tps = seq_len // tr
    row = lambda w: pl.BlockSpec((tr, w), lambda i: (i, 0))
    return pl.pallas_call(
        functools.partial(_ffn_prompt_kernel, tiles_per_seq=tps), grid=(n // tr,),
        in_specs=[row(d), row(on.shape[1]), row(od.shape[1])] + [_const_spec(a) for a in (wo, g2, wg, wu, cw, cb, wd)],
        out_specs=(row(d), pl.BlockSpec((1, 8, f), lambda i: (i // tps, 0, 0))),
        out_shape=(jax.ShapeDtypeStruct((n, d), F32), jax.ShapeDtypeStruct((n // seq_len, 8, f), F32)),
        scratch_shapes=[pltpu.VMEM((8, f), F32)],
        compiler_params=pltpu.CompilerParams(dimension_semantics=("arbitrary",), vmem_limit_bytes=VMEM_LIMIT),
        name="ffn_prompt",
    )(x2d, on, od, wo, g2, wg, wu, cw, cb, wd)


def _ffn_sample(x2d, on, od, st1, st2, wo, g2, wg, wu, cw, cb, wd, *, seq):
    n, d = x2d.shape
    f = wg.shape[1]
    args = (x2d, on, od, st1, st2, wo, g2, wg, wu, cw, cb, wd)
    return pl.pallas_call(
        functools.partial(_ffn_sample_kernel, seq=seq), grid=(1,),
        in_specs=[_const_spec(a) for a in args],
        out_specs=(pl.BlockSpec((n, d), lambda i: (0, 0)), pl.BlockSpec((n, f), lambda i: (0, 0))),
        out_shape=(jax.ShapeDtypeStruct((n, d), F32), jax.ShapeDtypeStruct((n, f), F32)),
        compiler_params=pltpu.CompilerParams(dimension_semantics=("arbitrary",), vmem_limit_bytes=VMEM_LIMIT),
        name="ffn_sample",
    )(*args)


PAGES_PER_STEP = 8
STEPS_PER_CMP_TILE = LANES // (PAGES_PER_STEP * LANES // L_CMP)
T_NSA, T_WIN, T_DIF, T_ROWS = 0, 512, 768, 1792


def _softmax_rows(s):
    m = jnp.max(s, axis=-1, keepdims=True)
    p = jnp.exp2(s - m)
    return p * (1.0 / jnp.sum(p, axis=-1, keepdims=True))


def _masked_softmax_rows(s, valid):
    sm = jnp.where(valid, s, NEG)
    m = jnp.max(sm, axis=-1, keepdims=True)
    p = jnp.where(valid, jnp.exp2(sm - m), 0.0)
    l = jnp.sum(p, axis=-1, keepdims=True)
    return p * jnp.where(l > 0.0, 1.0 / l, 0.0)


def _sattn_kernel(pt_ref, *refs, n_steps, past_len, dec, lam_init):
    pg = PAGES_PER_STEP
    nsa_pg, dif_pg = refs[0:pg], refs[pg:2 * pg]
    (win_ref, nn_ref, nw_ref, nd_ref, qn_ref, qd_ref, gate_ref, sn_ref, sd_ref, wck_ref, wcv_ref, gkc_ref,
     lp_ref, sg_ref, o_ref, ns_ref,
     ssel, sdif, psel, pdif, vsel, vdif, kc_scr, tail_scr, oc_scr, ow_scr) = refs[2 * pg:]
    s = pl.program_id(1)
    span = pg * LANES
    groups, heads = range(NSA_KV), range(DIFF_HEADS)
    cat = lambda parts: jnp.concatenate(parts, axis=1)
    hi_lo = lambda x: (x.astype(BF16), (x - x.astype(BF16).astype(F32)).astype(BF16))

    @pl.when(s < n_steps)
    def _():
        @pl.when(s == 0)
        def _():
            kc_scr[...] = jnp.zeros(kc_scr.shape, F32)

        shift = (s % STEPS_PER_CMP_TILE) * (span // L_CMP)
        col = pl.multiple_of((s // STEPS_PER_CMP_TILE) * LANES, LANES)
        for part, w_ref in ((0, wck_ref), (1, wcv_ref)):
            x_hi, x_lo = hi_lo(cat([r[0, 0, part * 128:(part + 1) * 128, :] for r in nsa_pg]))
            blk = _mm(x_hi, w_ref[0]) + _mm(x_lo, w_ref[0]) + _mm(x_hi, w_ref[1])
            kc_scr[part * 128:(part + 1) * 128, pl.ds(col, LANES)] += pltpu.roll(blk, shift, 1)

        k0 = pl.multiple_of(s * span, span)
        kpos = (k0 + lax.broadcasted_iota(jnp.int32, (1, span), 1)).astype(F32)
        for g in groups:
            k8 = cat([r[0, 0, 256 + g * HEAD_DIM:256 + (g + 1) * HEAD_DIM, :] for r in nsa_pg]).astype(BF16)
            ssel[g, :, pl.ds(k0, span)] = _mm(qn_ref[0, g], k8) + cat([sn_ref[g]] * pg) * kpos
        for h in heads:
            k8 = cat([r[0, 0, h * HEAD_DIM:(h + 1) * HEAD_DIM, :] for r in dif_pg]).astype(BF16)
            sdif[h, :, pl.ds(k0, span)] = _mm(qd_ref[0, h], k8) + cat([sd_ref[h]] * pg) * kpos
        vsel[:, pl.ds(k0, span)] = cat([r[0, 0, 384:512, :] for r in nsa_pg]).astype(BF16)
        vdif[:, pl.ds(k0, span)] = cat([r[0, 0, 512:1024, :] for r in dif_pg]).astype(BF16)

    @pl.when(s == n_steps)
    def _():
        new_rows = cat([nn_ref[...], nw_ref[...], nd_ref[...]])
        tail_scr[...] = jnp.concatenate([new_rows, jnp.zeros((LANES - dec, T_ROWS), F32)], axis=0).T
        nrow, drow = NSA_GROUP * dec, 2 * dec
        tail_pos = (past_len + lax.broadcasted_iota(jnp.int32, (1, LANES), 1)).astype(F32)
        causal32 = (lax.broadcasted_iota(jnp.int32, (nrow, LANES), 1)
                    <= lax.broadcasted_iota(jnp.int32, (nrow, LANES), 0) % dec)
        causal16 = (lax.broadcasted_iota(jnp.int32, (drow, LANES), 1)
                    <= lax.broadcasted_iota(jnp.int32, (drow, LANES), 0) % dec)

        for g in groups:
            kt = tail_scr[T_NSA + 256 + g * HEAD_DIM:T_NSA + 256 + (g + 1) * HEAD_DIM, :].astype(BF16)
            sc = _mm(qn_ref[0, g], kt) + sn_ref[g] * tail_pos
            ssel[g, :, past_len:past_len + LANES] = jnp.where(causal32, sc, NEG)
        for h in heads:
            kt = tail_scr[T_DIF + h * HEAD_DIM:T_DIF + (h + 1) * HEAD_DIM, :].astype(BF16)
            sc = _mm(qd_ref[0, h], kt) + sd_ref[h] * tail_pos
            sdif[h, :, past_len:past_len + LANES] = jnp.where(causal16, sc, NEG)

        n_cmp = kc_scr.shape[1]
        cend = lax.broadcasted_iota(jnp.int32, (nrow, n_cmp), 1) * L_CMP + (L_CMP - 1)
        qpos_c = past_len + lax.broadcasted_iota(jnp.int32, (nrow, n_cmp), 0) % dec
        imps = []
        for g in groups:
            kc = kc_scr[g * HEAD_DIM:(g + 1) * HEAD_DIM, :]
            kc = kc * lax.rsqrt(jnp.mean(kc * kc, axis=0, keepdims=True) + EPS) * cat([gkc_ref[...]] * (n_cmp // LANES))
            s_c = _mm(qn_ref[0, g], kc.astype(BF16)) + cat([sn_ref[g]] * (n_cmp // LANES)) * cend.astype(F32)
            p_c = _masked_softmax_rows(s_c, cend <= qpos_c)
            vc = kc_scr[128 + g * HEAD_DIM:128 + (g + 1) * HEAD_DIM, :].astype(BF16)
            oc_scr[g] = _nt(p_c.astype(BF16), vc)
            imps.append(sum(p_c[r * dec:(r + 1) * dec] for r in range(NSA_GROUP)))
        imp2 = jnp.concatenate(imps, axis=0)
        pair = (lax.broadcasted_iota(jnp.int32, (n_cmp, LANES), 0) // 2
                == lax.broadcasted_iota(jnp.int32, (n_cmp, LANES), 1)).astype(BF16)
        i_hi, i_lo = hi_lo(imp2)
        imp = _mm(i_hi, pair) + _mm(i_lo, pair)

        n_blk = past_len // L_SEL
        blk = lax.broadcasted_iota(jnp.int32, imp.shape, 1)
        blk_f = blk.astype(F32)
        bonus = jnp.where((blk == 0) | (blk == n_blk - 1), FORCE_BONUS, 0.0)
        score = jnp.where(blk < n_blk, imp + bonus, -jnp.inf)
        sel = jnp.zeros(imp.shape, F32)
        for _ in range(N_SELECT - 1):
            top = jnp.max(score, axis=1, keepdims=True)
            first = jnp.min(jnp.where(score == top, blk_f, 1e9), axis=1, keepdims=True)
            pick = blk_f == first
            sel = jnp.where(pick, jnp.where(top > -jnp.inf, 1.0, sel), sel)
            score = jnp.where(pick, -jnp.inf, score)
        sel = sel.astype(BF16)

        for c in range(n_steps):
            tok = c * span + lax.broadcasted_iota(jnp.int32, (LANES, span), 1)
            expand = ((tok >> 6) == lax.broadcasted_iota(jnp.int32, (LANES, span), 0)).astype(BF16)
            chosen = _mm(sel, expand)
            for g in groups:
                keep = jnp.concatenate([chosen[g * dec:(g + 1) * dec]] * NSA_GROUP, axis=0) > 0.5
                ssel[g, :, c * span:(c + 1) * span] = jnp.where(keep, ssel[g, :, c * span:(c + 1) * span], NEG)
        for g in groups:
            psel[g] = _softmax_rows(ssel[g]).astype(BF16)
        for h in heads:
            pdif[h] = _softmax_rows(sdif[h]).astype(BF16)

        w_buf = win_ref.shape[3]
        kwpos = past_len - w_buf + lax.broadcasted_iota(jnp.int32, (nrow, w_buf + LANES), 1)
        dist = past_len + lax.broadcasted_iota(jnp.int32, (nrow, w_buf + LANES), 0) % dec - kwpos
        in_window = (dist >= 0) & (dist < WINDOW)
        for g in groups:
            kw = cat([win_ref[0, 0, g * HEAD_DIM:(g + 1) * HEAD_DIM, :],
                      tail_scr[T_WIN + g * HEAD_DIM:T_WIN + (g + 1) * HEAD_DIM, :]]).astype(BF16)
            vw = cat([win_ref[0, 0, 128 + g * HEAD_DIM:128 + (g + 1) * HEAD_DIM, :],
                      tail_scr[T_WIN + 128 + g * HEAD_DIM:T_WIN + 128 + (g + 1) * HEAD_DIM, :]]).astype(BF16)
            s_w = _mm(qn_ref[0, g], kw) + cat([sn_ref[g]] * (w_buf // LANES + 1)) * kwpos.astype(F32)
            ow_scr[g] = _nt(_masked_softmax_rows(s_w, in_window).astype(BF16), vw)

        shifted = pltpu.roll(win_ref[0, 0], w_buf - dec, 1)
        fresh = cat([jnp.zeros((256, w_buf - LANES), F32), pltpu.roll(tail_scr[T_WIN:T_WIN + 256, :], LANES - dec, 1)])
        ns_ref[0] = jnp.where(lax.broadcasted_iota(jnp.int32, (256, w_buf), 1) >= w_buf - dec, fresh, shifted)

        lp = lp_ref[...]
        lam = (jnp.exp(jnp.sum(lp[0:1] * lp[1:2], keepdims=True))
               - jnp.exp(jnp.sum(lp[2:3] * lp[3:4], keepdims=True)) + lam_init)
        pieces = []
        for g in groups:
            vt = tail_scr[T_NSA + 384 + g * HEAD_DIM:T_NSA + 384 + (g + 1) * HEAD_DIM, :].astype(BF16)
            o_s = (_nt(psel[g, :, 0:past_len], vsel[g * HEAD_DIM:(g + 1) * HEAD_DIM, :])
                   + _nt(psel[g, :, past_len:past_len + LANES], vt))
            o = gate_ref[0, g, 0] * oc_scr[g] + gate_ref[0, g, 1] * o_s + gate_ref[0, g, 2] * ow_scr[g]
            pieces += [o[r * dec:(r + 1) * dec] for r in range(NSA_GROUP)]
        for h in heads:
            vt = tail_scr[T_DIF + 512 + h * HEAD_DIM:T_DIF + 512 + (h + 1) * HEAD_DIM, :].astype(BF16)
            o = (_nt(pdif[h, :, 0:past_len], vdif[h * HEAD_DIM:(h + 1) * HEAD_DIM, :])
                 + _nt(pdif[h, :, past_len:past_len + LANES], vt))
            d = o[0:dec] - lam * o[dec:2 * dec]
            pieces.append(d * lax.rsqrt(jnp.mean(d * d, axis=-1, keepdims=True) + EPS) * sg_ref[...] * (1.0 - lam_init))
        o_ref[0] = cat(pieces)


def _sattn(page_table, cache_nsa, cache_diff, state_win, new_nsa, new_win, new_dif, qn, qd, gates, sn, sd,
           wck, wcv, gkc, lp, sg, *, layer, lam_init):
    bs, n_pages = page_table.shape
    page = cache_nsa.shape[3]
    dec = new_nsa.shape[0] // bs
    past_len = n_pages * page
    pg = PAGES_PER_STEP
    n_steps = n_pages // pg
    w_buf = state_win.shape[3]
    assert page == LANES and n_pages % pg == 0 and dec == 8 and w_buf % LANES == 0
    n_cmp = -(-n_steps // STEPS_PER_CMP_TILE) * LANES
    tks = past_len + LANES
    pt = page_table.reshape(-1)

    def kpage(j):
        return lambda b, s, pt_ref: (layer, pt_ref[b * n_pages + jnp.minimum(s, n_steps - 1) * pg + j], 0, 0)

    const = lambda a: pl.BlockSpec(a.shape, lambda b, s, pt_ref: (0,) * a.ndim)
    per_b = lambda a: pl.BlockSpec((1,) + a.shape[1:], lambda b, s, pt_ref: (b,) + (0,) * (a.ndim - 1))
    in_specs = (
        [pl.BlockSpec((1, 1, 512, page), kpage(j)) for j in range(pg)]
        + [pl.BlockSpec((1, 1, 1024, page), kpage(j)) for j in range(pg)]
        + [pl.BlockSpec((1, 1, 256, w_buf), lambda b, s, pt_ref: (layer, b, 0, 0)),
           pl.BlockSpec((dec, 512), lambda b, s, pt_ref: (b, 0)),
           pl.BlockSpec((dec, 256), lambda b, s, pt_ref: (b, 0)),
           pl.BlockSpec((dec, 1024), lambda b, s, pt_ref: (b, 0)),
           per_b(qn), per_b(qd), per_b(gates), const(sn), const(sd), const(wck), const(wcv), const(gkc),
           const(lp), const(sg)])
    out_specs = (pl.BlockSpec((1, dec, 1024), lambda b, s, pt_ref: (b, 0, 0)),
                 pl.BlockSpec((1, 256, w_buf), lambda b, s, pt_ref: (b, 0, 0)))
    scratch = [
        pltpu.VMEM((NSA_KV, NSA_GROUP * dec, tks), F32), pltpu.VMEM((DIFF_HEADS, 2 * dec, tks), F32),
        pltpu.VMEM((NSA_KV, NSA_GROUP * dec, tks), BF16), pltpu.VMEM((DIFF_HEADS, 2 * dec, tks), BF16),
        pltpu.VMEM((NSA_KV * HEAD_DIM, past_len), BF16), pltpu.VMEM((DIFF_HEADS * HEAD_DIM, past_len), BF16),
        pltpu.VMEM((256, n_cmp), F32), pltpu.VMEM((T_ROWS, LANES), F32),
        pltpu.VMEM((NSA_KV, NSA_GROUP * dec, HEAD_DIM), F32), pltpu.VMEM((NSA_KV, NSA_GROUP * dec, HEAD_DIM), F32),
    ]
    grid_spec = pltpu.PrefetchScalarGridSpec(num_scalar_prefetch=1, grid=(bs, n_steps + 1), in_specs=in_specs,
                                             out_specs=out_specs, scratch_shapes=scratch)
    caches = [cache_nsa] * pg + [cache_diff] * pg
    return pl.pallas_call(
        functools.partial(_sattn_kernel, n_steps=n_steps, past_len=past_len, dec=dec, lam_init=lam_init),
        grid_spec=grid_spec,
        out_shape=(jax.ShapeDtypeStruct((bs, dec, 1024), F32), jax.ShapeDtypeStruct((bs, 256, w_buf), F32)),
        compiler_params=pltpu.CompilerParams(dimension_semantics=("arbitrary",) * 2, vmem_limit_bytes=VMEM_LIMIT),
        name="sattn",
    )(pt, *caches, state_win, new_nsa, new_win, new_dif, qn, qd, gates, sn, sd, wck, wcv, gkc, lp, sg)


def _alibi_slopes(n):
    return 2.0 ** (-8.0 * jnp.arange(1, n + 1, dtype=F32) / n)


def _slope_rows(slopes):
    slopes = slopes * LOG2E
    s0 = slopes.astype(BF16).astype(F32)
    s1 = (slopes - s0).astype(BF16).astype(F32)
    s2 = (slopes - s0 - s1).astype(BF16).astype(F32)
    cols = jnp.stack([64.0 * s0, s0, 64.0 * s1, s1, 64.0 * s2, s2], axis=1)
    out = jnp.zeros((slopes.shape[0], LANES), F32)
    return out.at[:, AUG0:AUG0 + N_AUG].set(cols)


def _block_ones(group):
    idx = np.arange(LANES) // group
    return jnp.asarray((idx[:, None] == idx[None, :]).astype(np.float32), dtype=BF16)


_GATE_SRC = np.full((LANES,), -1, np.int64)
for _g in range(NSA_KV):
    for _r in range(NSA_GROUP):
        for _j in range(3):
            _GATE_SRC[_g * 16 + _j * 4 + _r] = 1280 + _g * 12 + _r * 3 + _j


def _permute_w_in(w):
    main = jnp.concatenate([w[:, 0:1280], w[:, 1304:2840]], axis=1)
    gate = jnp.where(jnp.asarray(_GATE_SRC >= 0)[None, :], w[:, np.maximum(_GATE_SRC, 0)], 0.0)
    return jnp.concatenate([main, gate], axis=1).astype(BF16)


def _compress_weights(w):
    t = np.arange(PAGES_PER_STEP * LANES)
    place = jnp.asarray(t[:, None] // L_CMP == np.arange(LANES)[None, :])
    full = jnp.where(place, jnp.tile(w, PAGES_PER_STEP * LANES // L_CMP)[:, None], 0.0)
    hi = full.astype(BF16)
    return jnp.stack([hi, (full - hi.astype(F32)).astype(BF16)])


def _gain_row(nsa_qg, nsa_kg, diff_qg, diff_kg):
    one = lambda n: jnp.ones((n,), F32)
    parts = [jnp.tile(nsa_qg, NSA_HEADS), one(256), jnp.tile(nsa_kg[1], NSA_KV), one(128),
             jnp.tile(nsa_kg[2], NSA_KV), one(128), jnp.tile(diff_qg.reshape(-1), DIFF_HEADS),
             jnp.tile(diff_kg.reshape(-1), DIFF_HEADS), one(512 + LANES)]
    return jnp.concatenate(parts)[None, :]


def kernel(x_prompt, x_sample, cache_nsa, cache_diff, state_win, state_conv, page_table, norm1_g, norm2_g, w_in, w_out, nsa_qnorm_g, nsa_knorm_g, nsa_cmp_w, diff_qnorm_g, diff_knorm_g, diff_lambda, diff_subnorm_g, w_gate, w_up, conv_w, conv_b, w_down):
    batch, seq, d_model = x_prompt.shape
    dec_batch, dec_seq, _ = x_sample.shape
    depth, n_pool, page = cache_nsa.shape[:3]
    n_pages = page_table.shape[1]
    past_len = n_pages * page
    w_buf = state_win.shape[2]
    d_ff = w_gate.shape[2]
    assert dec_seq < L_CMP and seq % KEY_TILE == 0 and w_buf == WINDOW

    s64, s32 = _block_ones(HEAD_DIM), _block_ones(DIFF_HALF)
    qcn = _slope_rows(_alibi_slopes(NSA_HEADS))
    qcd = _slope_rows(_alibi_slopes(DIFF_HEADS))
    fmaj = lambda a: jnp.moveaxis(a, 2, -1).reshape(a.shape[0], a.shape[1], -1, a.shape[2])
    cache_nsa4 = fmaj(cache_nsa)
    cache_diff4 = fmaj(cache_diff)
    state_win4 = fmaj(state_win)

    sn = jnp.broadcast_to((_alibi_slopes(NSA_HEADS) * LOG2E).reshape(NSA_KV, NSA_GROUP, 1, 1),
                          (NSA_KV, NSA_GROUP, dec_seq, LANES)).reshape(NSA_KV, NSA_GROUP * dec_seq, LANES)
    sd = jnp.broadcast_to((_alibi_slopes(DIFF_HEADS) * LOG2E).reshape(DIFF_HEADS, 1, 1), (DIFF_HEADS, 2 * dec_seq, LANES))
    xp = x_prompt.reshape(batch * seq, d_model)
    xs = x_sample.reshape(dec_batch * dec_seq, d_model)

    nsa_t = jnp.zeros((depth, batch, 512, seq), F32)
    dif_t = jnp.zeros((depth, batch, 1024, seq), F32)
    outs = [[] for _ in range(8)]
    for l in range(depth):
        lam_init = 0.8 - 0.6 * math.exp(-0.3 * l)
        w_in_p = _permute_w_in(w_in[l])
        gain = _gain_row(nsa_qnorm_g[l], nsa_knorm_g[l], diff_qnorm_g[l], diff_knorm_g[l])
        gain_kc = jnp.tile(nsa_knorm_g[l, 0], NSA_KV)[None, :]
        cw = jnp.concatenate([jnp.broadcast_to(nsa_cmp_w[l, 0][:, None], (L_CMP, LANES)),
                              jnp.broadcast_to(nsa_cmp_w[l, 1][:, None], (L_CMP, LANES))], axis=1)
        lp = jnp.zeros((8, LANES), F32).at[0:4, 0:DIFF_HALF].set(diff_lambda[l])
        sg = jnp.broadcast_to(diff_subnorm_g[l][:, None], (HEAD_DIM, DIFF_TQ))
        sg_row = jnp.broadcast_to(diff_subnorm_g[l][None, :], (dec_seq, HEAD_DIM))
        gkc = jnp.broadcast_to(nsa_knorm_g[l, 0][:, None], (HEAD_DIM, LANES))
        wck, wcv = _compress_weights(nsa_cmp_w[l, 0]), _compress_weights(nsa_cmp_w[l, 1])
        g1 = norm1_g[l][None, :]
        g2 = norm2_g[l][None, :]
        wo, wg, wu, wd = (a.astype(BF16) for a in (w_out[l], w_gate[l], w_up[l], w_down[l]))
        cwf, cbf = conv_w[l], conv_b[l][None, :]
        proj = functools.partial(_proj, g1=g1, w=w_in_p, gain=gain, s64=s64, s32=s32, qcn=qcn, qcd=qcd, cw=cw)

        (nsa_t, win_t, dif_t, qa, ksa, kwa, gt, dqa, dka, craw) = proj(
            xp, seq_len=seq, pos_base=0, stacked=(l, depth, (nsa_t, dif_t)))
        kc, vc = _cmpfin(craw.reshape(batch, seq // L_CMP, 256), s64, gain_kc)
        o_nsa = _nsa(qa, kc, vc, ksa.reshape(NSA_KV, batch, seq, LANES), nsa_t,
                     kwa.reshape(NSA_KV, batch, seq, LANES), win_t, gt,
                     layer=l, bx=batch, nqb=seq // Q_LANES, q_base=0, win_base=0)
        o_dif = _diff(dqa, dka.reshape(DIFF_HEADS, batch, seq, LANES), dif_t, sd[:, 0], lp, sg,
                      layer=l, bx=batch, nqb=seq // DIFF_TQ, tq=DIFF_TQ, q_base=0, lam_init=lam_init)
        xp, cv = _ffn_prompt(xp, o_nsa, o_dif, wo, g2, wg, wu, cwf, cbf, wd, seq_len=seq)
        w_keep = min(WINDOW, seq)
        outs[4].append(jnp.moveaxis(win_t[:, :, seq - w_keep:].reshape(batch, 2, NSA_KV, HEAD_DIM, w_keep), -1, 1))
        outs[6].append(cv[:, 8 - 2:8])

        (nsa_rows, win_rows, dif_rows, qa, _, _, gt, dqa, _, _) = proj(xs, seq_len=dec_seq, pos_base=past_len)
        per_seq = lambda a, lead: a[..., :HEAD_DIM].reshape(lead + (dec_batch, dec_seq, HEAD_DIM))
        qn = jnp.transpose(per_seq(qa, (NSA_KV, NSA_GROUP)), (2, 0, 1, 3, 4)).reshape(
            dec_batch, NSA_KV, NSA_GROUP * dec_seq, HEAD_DIM)
        qd = jnp.transpose(per_seq(dqa, (DIFF_HEADS, 2)), (2, 0, 1, 3, 4)).reshape(
            dec_batch, DIFF_HEADS, 2 * dec_seq, HEAD_DIM)
        gates = jnp.transpose(gt.reshape(NSA_KV, 4, NSA_GROUP, dec_batch, dec_seq), (3, 0, 1, 2, 4)).reshape(
            dec_batch, NSA_KV, 4, NSA_GROUP * dec_seq, 1)
        gates = jnp.broadcast_to(gates, gates.shape[:-1] + (HEAD_DIM,))
        o_s, new_state = _sattn(page_table, cache_nsa4, cache_diff4, state_win4, nsa_rows, win_rows, dif_rows,
                                qn, qd, gates, sn, sd, wck, wcv, gkc, lp, sg_row, layer=l, lam_init=lam_init)
        o_s = o_s.reshape(dec_batch * dec_seq, 2 * NSA_HEADS * HEAD_DIM)
        st1 = jnp.repeat(state_conv[l][:, 1], dec_seq, axis=0)
        st2 = jnp.repeat(state_conv[l][:, 0], dec_seq, axis=0)
        xs, g_s = _ffn_sample(xs, o_s[:, :NSA_HEADS * HEAD_DIM], o_s[:, NSA_HEADS * HEAD_DIM:], st1, st2,
                              wo, g2, wg, wu, cwf, cbf, wd, seq=dec_seq)
        outs[1].append(nsa_rows.reshape(dec_batch, dec_seq, 4, NSA_KV, HEAD_DIM))
        outs[3].append(dif_rows.reshape(dec_batch, dec_seq, 2, DIFF_HEADS, HEAD_DIM))
        outs[5].append(jnp.moveaxis(new_state.reshape(dec_batch, 2, NSA_KV, HEAD_DIM, w_buf), -1, 1))
        outs[7].append(g_s.reshape(dec_batch, dec_seq, d_ff)[:, dec_seq - 2:])

    outs[0] = jnp.moveaxis(nsa_t.reshape(depth, batch, 4, NSA_KV, HEAD_DIM, seq), -1, 2)
    outs[2] = jnp.moveaxis(dif_t.reshape(depth, batch, 2, DIFF_HEADS, HEAD_DIM, seq), -1, 2)
    stacked = [o if not isinstance(o, list) else jnp.stack(o) for o in outs]
    return (xp.reshape(batch, seq, d_model), xs.reshape(dec_batch, dec_seq, d_model), *stacked)
```

```python
import functools
import math

import jax
import jax.numpy as jnp
import numpy as np
from jax import lax
from jax.experimental import pallas as pl
from jax.experimental.pallas import tpu as pltpu

F32 = jnp.float32
BF16 = jnp.bfloat16

HEAD_DIM = 64
NSA_KV = 2
NSA_GROUP = 4
NSA_HEADS = NSA_KV * NSA_GROUP
DIFF_HEADS = 8
DIFF_HALF = HEAD_DIM // 2
L_CMP = 32
L_SEL = 64
N_SELECT = 16
WINDOW = 512
FORCE_BONUS = 1.0e4
EPS = 1e-6
NEG = -1e30
UNDERFLOW_LOG2 = -160.0
NORM_SLACK = 1.02
LOG2E = math.log2(math.e)

LANES = 128
KEY_TILE = 512
Q_LANES = 256
DIFF_TQ = 512
WIN_ROWS = WINDOW + Q_LANES
AUG0 = HEAD_DIM
N_AUG = 6
VMEM_LIMIT = 56 * 1024 * 1024

C_Q, C_NSA, C_WIN, C_DQ, C_DK, C_DV, C_GATE = 0, 512, 1024, 1280, 1792, 2304, 2816
N_COL = 2944


def _nt(a, b):
    return lax.dot_general(a, b, (((1,), (1,)), ((), ())), preferred_element_type=F32)


def _tn(a, b):
    return lax.dot_general(a, b, (((0,), (0,)), ((), ())), preferred_element_type=F32)


def _mm(a, b):
    return jnp.dot(a, b, preferred_element_type=F32)


def _group_meansq(z, smat, inv_n):
    zz = z * z
    hi = zz.astype(BF16)
    lo = (zz - hi.astype(F32)).astype(BF16)
    return (_mm(hi, smat) + _mm(lo, smat)) * inv_n


def _pos_rows(pos, lane):
    is_aug = (lane >= AUG0) & (lane < AUG0 + N_AUG)
    val = jnp.where((lane & 1) == 0, pos >> 6, pos & 63)
    return jnp.where(is_aug, val, 0).astype(F32)


def _split_heads(slab, fill, lane):
    even = jnp.where(lane < HEAD_DIM, slab, fill)
    odd = jnp.where(lane < HEAD_DIM, pltpu.roll(slab, HEAD_DIM, 1), fill)
    return even, odd


def _proj_kernel(x_ref, g1_ref, w_ref, gain_ref, s64_ref, s32_ref, qcn_ref, qcd_ref, cw_ref, *rest,
                 seq_len, pos_base, fmajor):
    nsa_ref, win_ref, dif_ref, qa_ref, ksa_ref, kwa_ref, gt_ref, dqa_ref, dka_ref, cr_ref = rest[-10:]
    tr = x_ref.shape[0]

    def put(ref, c0, slab):
        if fmajor:
            ref[(0,) * (len(ref.shape) - 2) + (slice(c0, c0 + LANES), slice(None))] = slab.T
        else:
            ref[:, c0:c0 + LANES] = slab
    i = pl.program_id(0)
    x = x_ref[...]
    h = x * lax.rsqrt(jnp.mean(x * x, axis=-1, keepdims=True) + EPS) * g1_ref[...]
    hb = h.astype(BF16)

    lane = lax.broadcasted_iota(jnp.int32, (tr, LANES), 1)
    row = lax.broadcasted_iota(jnp.int32, (tr, LANES), 0)
    pos = pos_base + (i * tr + row) % seq_len
    prow = _pos_rows(pos, lane)
    s64 = s64_ref[...]
    s32 = s32_ref[...]

    def seg(c0, width):
        return _mm(hb, w_ref[:, c0:c0 + width])

    def normed(z, c0, smat, inv_n):
        return z * lax.rsqrt(_group_meansq(z, smat, inv_n) + EPS) * gain_ref[:, c0:c0 + LANES]

    zq = seg(C_Q, 512)
    for s in range(4):
        zn = normed(zq[:, s * LANES:(s + 1) * LANES], C_Q + s * LANES, s64, 1.0 / HEAD_DIM) * (HEAD_DIM ** -0.5 * LOG2E)
        for par in range(2):
            hd = 2 * s + par
            src = zn if par == 0 else pltpu.roll(zn, HEAD_DIM, 1)
            qa = jnp.where(lane < HEAD_DIM, src, qcn_ref[hd:hd + 1, :])
            qa_ref[hd // NSA_GROUP, hd % NSA_GROUP] = qa.astype(BF16)

    zc = seg(C_NSA, 512)
    ks = normed(zc[:, 256:384], C_NSA + 256, s64, 1.0 / HEAD_DIM)
    put(nsa_ref, 0, zc[:, 0:128])
    put(nsa_ref, 128, zc[:, 128:256])
    put(nsa_ref, 256, ks)
    put(nsa_ref, 384, zc[:, 384:512])
    k0, k1 = _split_heads(ks, prow, lane)
    ksa_ref[0] = k0.astype(BF16)
    ksa_ref[1] = k1.astype(BF16)
    craw = zc[:, 0:256].reshape(tr // L_CMP, L_CMP, 256) * cw_ref[...][None]
    cr_ref[...] = jnp.sum(craw, axis=1)

    zw = seg(C_WIN, 256)
    kw = normed(zw[:, 0:128], C_WIN, s64, 1.0 / HEAD_DIM)
    put(win_ref, 0, kw)
    put(win_ref, 128, zw[:, 128:256])
    k0, k1 = _split_heads(kw, prow, lane)
    kwa_ref[0] = k0.astype(BF16)
    kwa_ref[1] = k1.astype(BF16)

    zdq = seg(C_DQ, 512)
    for s in range(4):
        zn = normed(zdq[:, s * LANES:(s + 1) * LANES], C_DQ + s * LANES, s32, 1.0 / DIFF_HALF) * (DIFF_HALF ** -0.5 * LOG2E)
        for par in range(2):
            hd = 2 * s + par
            src = zn if par == 0 else pltpu.roll(zn, HEAD_DIM, 1)
            fill = qcd_ref[hd:hd + 1, :]
            dqa_ref[hd, 0] = jnp.where(lane < DIFF_HALF, src, fill).astype(BF16)
            dqa_ref[hd, 1] = jnp.where((lane >= DIFF_HALF) & (lane < HEAD_DIM), src, fill).astype(BF16)

    zdk = seg(C_DK, 512)
    for s in range(4):
        kn = normed(zdk[:, s * LANES:(s + 1) * LANES], C_DK + s * LANES, s32, 1.0 / DIFF_HALF)
        put(dif_ref, s * LANES, kn)
        k0, k1 = _split_heads(kn, prow, lane)
        dka_ref[2 * s] = k0.astype(BF16)
        dka_ref[2 * s + 1] = k1.astype(BF16)
    zdv = seg(C_DV, 512)
    for s in range(4):
        put(dif_ref, 512 + s * LANES, zdv[:, s * LANES:(s + 1) * LANES])

    zg = seg(C_GATE, LANES)
    sg = 1.0 / (1.0 + jnp.exp(-zg))
    gt_ref[...] = sg.T[0:2 * 16].reshape(NSA_KV, 16, tr)


def _proj(x2d, g1, w, gain, s64, s32, qcn, qcd, cw, *, seq_len, pos_base, stacked=None, tr=512):
    n = x2d.shape[0]
    tr = min(tr, n)
    assert n % tr == 0 and tr % L_CMP == 0
    fmajor = stacked is not None
    kern = functools.partial(_proj_kernel, seq_len=seq_len, pos_base=pos_base, fmajor=fmajor)
    full = lambda shape: pl.BlockSpec(shape, lambda i: (0,) * len(shape))
    if fmajor:
        layer, depth, prev = stacked
        assert seq_len % tr == 0
        tps, nb = seq_len // tr, n // seq_len
        cache_shapes = [jax.ShapeDtypeStruct((depth, nb, 512, seq_len), F32),
                        jax.ShapeDtypeStruct((nb, 256, seq_len), F32),
                        jax.ShapeDtypeStruct((depth, nb, 1024, seq_len), F32)]
        cache_specs = [pl.BlockSpec((1, 1, 512, tr), lambda i: (layer, i // tps, 0, i % tps)),
                       pl.BlockSpec((1, 256, tr), lambda i: (i // tps, 0, i % tps)),
                       pl.BlockSpec((1, 1, 1024, tr), lambda i: (layer, i // tps, 0, i % tps))]
    else:
        prev = None
        cache_shapes = [jax.ShapeDtypeStruct((n, w_), F32) for w_ in (512, 256, 1024)]
        cache_specs = [pl.BlockSpec((tr, w_), lambda i: (i, 0)) for w_ in (512, 256, 1024)]
    out_shape = tuple(cache_shapes) + (
        jax.ShapeDtypeStruct((NSA_KV, NSA_GROUP, n, LANES), BF16),
        jax.ShapeDtypeStruct((NSA_KV, n, LANES), BF16),
        jax.ShapeDtypeStruct((NSA_KV, n, LANES), BF16),
        jax.ShapeDtypeStruct((NSA_KV, 16, n), F32),
        jax.ShapeDtypeStruct((DIFF_HEADS, 2, n, LANES), BF16),
        jax.ShapeDtypeStruct((DIFF_HEADS, n, LANES), BF16),
        jax.ShapeDtypeStruct((n // L_CMP, 256), F32),
    )
    out_specs = tuple(cache_specs) + (
        pl.BlockSpec((NSA_KV, NSA_GROUP, tr, LANES), lambda i: (0, 0, i, 0)),
        pl.BlockSpec((NSA_KV, tr, LANES), lambda i: (0, i, 0)),
        pl.BlockSpec((NSA_KV, tr, LANES), lambda i: (0, i, 0)),
        pl.BlockSpec((NSA_KV, 16, tr), lambda i: (0, 0, i)),
        pl.BlockSpec((DIFF_HEADS, 2, tr, LANES), lambda i: (0, 0, i, 0)),
        pl.BlockSpec((DIFF_HEADS, tr, LANES), lambda i: (0, i, 0)),
        pl.BlockSpec((tr // L_CMP, 256), lambda i: (i, 0)),
    )
    in_specs = [
        pl.BlockSpec((tr, x2d.shape[1]), lambda i: (i, 0)),
        full(g1.shape), full(w.shape), full(gain.shape), full(s64.shape), full(s32.shape),
        full(qcn.shape), full(qcd.shape), full(cw.shape),
    ]
    args = [x2d, g1, w, gain, s64, s32, qcn, qcd, cw]
    aliases = {}
    if prev is not None:
        in_specs += [pl.BlockSpec(memory_space=pl.ANY)] * 2
        aliases = {len(args): 0, len(args) + 1: 2}
        args += list(prev)
    return pl.pallas_call(
        kern, grid=(n // tr,), in_specs=in_specs, out_specs=out_specs, out_shape=out_shape,
        input_output_aliases=aliases,
        compiler_params=pltpu.CompilerParams(dimension_semantics=("arbitrary",), vmem_limit_bytes=VMEM_LIMIT),
        name="proj",
    )(*args)


def _cmpfin_kernel(raw_ref, s64_ref, gain_ref, kc_ref, vc_ref, k_scr, v_scr, *, nsp):
    lane = lax.broadcasted_iota(jnp.int32, (nsp, LANES), 1)
    n = lax.broadcasted_iota(jnp.int32, (nsp, LANES), 0)
    k_scr[...] = raw_ref[0, :, 0:128]
    v_scr[...] = raw_ref[0, :, 128:256]
    for j in range(2):
        k = k_scr[pl.ds(j, nsp, stride=2), :]
        kn = k * lax.rsqrt(_group_meansq(k, s64_ref[...], 1.0 / HEAD_DIM) + EPS) * gain_ref[...]
        cend = n * L_SEL + (L_CMP - 1 + L_CMP * j)
        prow = _pos_rows(cend, lane)
        k0, k1 = _split_heads(kn, prow, lane)
        kc_ref[0, 0, j * nsp:(j + 1) * nsp, :] = k0.astype(BF16)
        kc_ref[1, 0, j * nsp:(j + 1) * nsp, :] = k1.astype(BF16)
        vc_ref[0, j * nsp:(j + 1) * nsp, :] = v_scr[pl.ds(j, nsp, stride=2), :].astype(BF16)


def _cmpfin(raw, s64, gain_kc):
    bx, n2, _ = raw.shape
    nsp = n2 // 2
    assert nsp % 16 == 0
    return pl.pallas_call(
        functools.partial(_cmpfin_kernel, nsp=nsp),
        grid=(bx,),
        in_specs=[pl.BlockSpec((1, n2, 256), lambda b: (b, 0, 0)),
                  pl.BlockSpec(s64.shape, lambda b: (0, 0)),
                  pl.BlockSpec(gain_kc.shape, lambda b: (0, 0))],
        out_specs=(pl.BlockSpec((NSA_KV, 1, n2, LANES), lambda b: (0, b, 0, 0)),
                   pl.BlockSpec((1, n2, LANES), lambda b: (b, 0, 0))),
        out_shape=(jax.ShapeDtypeStruct((NSA_KV, bx, n2, LANES), BF16),
                   jax.ShapeDtypeStruct((bx, n2, LANES), BF16)),
        scratch_shapes=[pltpu.VMEM((n2, LANES), F32), pltpu.VMEM((n2, LANES), F32)],
        name="cmpfin",
    )(raw, s64, gain_kc)


def _flash_init(c, m_ref, l_ref, acc_ref):
    m_ref[c] = jnp.full(m_ref.shape[1:], NEG, F32)
    l_ref[c] = jnp.zeros(l_ref.shape[1:], F32)
    acc_ref[c] = jnp.zeros(acc_ref.shape[1:], F32)


def _flash_probs(c, s, m_ref, l_ref):
    m_old = m_ref[c]
    m_new = jnp.maximum(m_old, jnp.max(s, axis=0, keepdims=True))
    p = jnp.exp2(s - m_new)
    a = jnp.exp2(m_old - m_new)
    l_ref[c] = a * l_ref[c] + jnp.sum(p, axis=0, keepdims=True)
    m_ref[c] = m_new
    return p.astype(BF16), a


def _flash_update(c, s, vt_tile, m_ref, l_ref, acc_ref):
    p, a = _flash_probs(c, s, m_ref, l_ref)
    acc_ref[c] = a * acc_ref[c] + _mm(vt_tile, p)


def _max_key_norm(k_ref_2d, n_keys):
    feat = lax.broadcasted_iota(jnp.int32, (1, LANES), 1) < HEAD_DIM

    def chunk(c, best):
        kf = k_ref_2d[pl.ds(pl.multiple_of(c * KEY_TILE, KEY_TILE), KEY_TILE), :].astype(F32)
        sq = jnp.sum(jnp.where(feat, kf * kf, 0.0), axis=1, keepdims=True)
        return jnp.maximum(best, jnp.max(sq, axis=0, keepdims=True))

    return jnp.sqrt(lax.fori_loop(0, n_keys // KEY_TILE, chunk, jnp.zeros((1, 1), F32)))


def _first_needed_tile(qa, m, k_norm, slope_row, n_max):
    feat = lax.broadcasted_iota(jnp.int32, (1, LANES), 1) < HEAD_DIM
    qf = qa.astype(F32)
    qsq = jnp.where(feat, qf * qf, 0.0)
    hi = qsq.astype(BF16)
    lo = (qsq - hi.astype(F32)).astype(BF16)
    ones = jnp.ones((8, LANES), BF16)
    q_norm = jnp.sqrt((_nt(ones, hi) + _nt(ones, lo))[0:1])
    cutoff = (m + UNDERFLOW_LOG2 - NORM_SLACK * q_norm * k_norm) / slope_row
    tile = jnp.floor(jnp.min(cutoff, axis=1, keepdims=True) * (1.0 / KEY_TILE))
    return jnp.clip(tile, 0.0, n_max.astype(F32)).astype(jnp.int32)[0, 0]


def _sweep_earlier_tiles(first, n_tiles, step):
    both = (0, 1)

    @pl.when(jnp.logical_and(n_tiles % 2 == 1, jnp.minimum(first[0], first[1]) < n_tiles))
    def _():
        step(pl.multiple_of((n_tiles - 1) * KEY_TILE, KEY_TILE), KEY_TILE, both)

    j_end = n_tiles // 2
    j0 = [jnp.minimum(f // 2, j_end) for f in first]
    j_both = jnp.maximum(j0[0], j0[1])

    def run(chains):
        def body(j, carry):
            step(pl.multiple_of(j * 2 * KEY_TILE, 2 * KEY_TILE), 2 * KEY_TILE, chains)
            return carry
        return body

    lax.fori_loop(j0[0], j_both, run((0,)), 0)
    lax.fori_loop(j0[1], j_both, run((1,)), 0)
    lax.fori_loop(j_both, j_end, run(both), 0)


def _masked_softmax_cols(s, valid):
    sm = jnp.where(valid, s, NEG)
    m = jnp.max(sm, axis=0, keepdims=True)
    p = jnp.where(valid, jnp.exp2(sm - m), 0.0)
    l = jnp.sum(p, axis=0, keepdims=True)
    return p * jnp.where(l > 0.0, 1.0 / l, 0.0)


def _nsa_kernel(q_ref, kc_ref, vc_ref, ks_ref, vs_ref, kw_ref, vw_ref, gt_ref, o_ref,
                sel_ref, acc_ref, m_ref, l_ref, kn_ref, *, q_base, nsp, tw, win_base):
    i = pl.program_id(1)
    q0 = q_base + i * Q_LANES
    ncol = NSA_GROUP * Q_LANES
    groups = range(NSA_KV)
    qas = [q_ref[g].reshape(ncol, LANES) for g in groups]
    lane_c = lax.broadcasted_iota(jnp.int32, (1, ncol), 1)
    qpos_c = q0 + (lane_c & (Q_LANES - 1))
    rows_of = lambda g: slice(g * HEAD_DIM, (g + 1) * HEAD_DIM)

    n_row = lax.broadcasted_iota(jnp.int32, (nsp, ncol), 0)
    valid_e = (n_row * L_SEL + (L_CMP - 1)) <= qpos_c
    valid_o = (n_row * L_SEL + (2 * L_CMP - 1)) <= qpos_c
    blk = lax.broadcasted_iota(jnp.int32, (nsp, Q_LANES), 0)
    qpos_q = q0 + lax.broadcasted_iota(jnp.int32, (nsp, Q_LANES), 1)
    cur = qpos_q >> 6
    bonus = jnp.where((blk == 0) | (blk == cur) | (blk == cur - 1), FORCE_BONUS, 0.0)
    in_range = blk <= cur
    blk_f = blk.astype(F32)
    o_c = []
    for g in groups:
        s_c = _nt(kc_ref[g, 0], qas[g])
        sm_e = jnp.where(valid_e, s_c[0:nsp], NEG)
        sm_o = jnp.where(valid_o, s_c[nsp:2 * nsp], NEG)
        m = jnp.maximum(jnp.max(sm_e, axis=0, keepdims=True), jnp.max(sm_o, axis=0, keepdims=True))
        p_e = jnp.where(valid_e, jnp.exp2(sm_e - m), 0.0)
        p_o = jnp.where(valid_o, jnp.exp2(sm_o - m), 0.0)
        l = jnp.sum(p_e, axis=0, keepdims=True) + jnp.sum(p_o, axis=0, keepdims=True)
        inv = jnp.where(l > 0.0, 1.0 / l, 0.0)
        p_e = p_e * inv
        p_o = p_o * inv
        pcat = jnp.concatenate([p_e, p_o], axis=0).astype(BF16)
        o_c.append(_tn(vc_ref[0], pcat)[rows_of(g)])

        imp = p_e[:, 0:Q_LANES] + p_o[:, 0:Q_LANES]
        for r in range(1, NSA_GROUP):
            imp = imp + p_e[:, r * Q_LANES:(r + 1) * Q_LANES] + p_o[:, r * Q_LANES:(r + 1) * Q_LANES]
        score = jnp.where(in_range, imp + bonus, -jnp.inf)
        sel = jnp.zeros((nsp, Q_LANES), F32)
        for _ in range(N_SELECT):
            top = jnp.max(score, axis=0, keepdims=True)
            first = jnp.min(jnp.where(score == top, blk_f, 1e9), axis=0, keepdims=True)
            pick = blk_f == first
            sel = jnp.where(pick, jnp.where(top > -jnp.inf, 1.0, sel), sel)
            score = jnp.where(pick, -jnp.inf, score)
        sel_ref[g] = sel

    @pl.when(i == 0)
    def _():
        for g in groups:
            kn_ref[g] = jnp.broadcast_to(_max_key_norm(ks_ref.at[g, 0], ks_ref.shape[2]), kn_ref.shape[1:])

    for g in groups:
        _flash_init(g, m_ref, l_ref, acc_ref)
    def sel_tile(k0, size, which):
        n_blk = size // L_SEL
        causal = (k0 + lax.broadcasted_iota(jnp.int32, (size, Q_LANES), 0)
                  <= q0 + lax.broadcasted_iota(jnp.int32, (size, Q_LANES), 1))
        for g in which:
            vt = vs_ref[0, 0, rows_of(g), pl.ds(k0, size)].astype(BF16)
            s = _nt(ks_ref[g, 0, pl.ds(k0, size), :], qas[g])
            sel_rows = sel_ref[g, pl.ds(pl.multiple_of(k0 // L_SEL, 8), n_blk), :]
            selt = jnp.concatenate(
                [jnp.broadcast_to(sel_rows[j:j + 1, :], (L_SEL, Q_LANES)) for j in range(n_blk)], axis=0)
            valid = jnp.where(causal, selt, 0.0) > 0.0
            s = jnp.concatenate(
                [jnp.where(valid, s[:, r * Q_LANES:(r + 1) * Q_LANES], NEG) for r in range(NSA_GROUP)], axis=1)
            _flash_update(g, s, vt, m_ref, l_ref, acc_ref)

    n_before = (q0 + Q_LANES + KEY_TILE - 1) // KEY_TILE - 1
    k_own = pl.multiple_of(n_before * KEY_TILE, KEY_TILE)
    used = q0 - n_before * KEY_TILE + Q_LANES
    for size in range(Q_LANES, KEY_TILE + 1, Q_LANES):
        @pl.when(used == size)
        def _(size=size):
            sel_tile(k_own, size, groups)

    first = []
    for g in groups:
        slope = jnp.concatenate([jnp.full((1, Q_LANES), LOG2E * 2.0 ** -(g * NSA_GROUP + r + 1), F32)
                                 for r in range(NSA_GROUP)], axis=1)
        first.append(_first_needed_tile(qas[g], m_ref[g], kn_ref[g][0:1, 0:1], slope, n_before))
    _sweep_earlier_tiles(first, n_before, sel_tile)

    w0 = jnp.clip(q0 - win_base - WINDOW, 0, tw - WIN_ROWS)
    w0 = pl.multiple_of(w0, Q_LANES)
    kpos = win_base + w0 + lax.broadcasted_iota(jnp.int32, (WIN_ROWS, ncol), 0)
    dist = qpos_c - kpos
    in_window = (dist >= 0) & (dist < WINDOW)

    outs = []
    for g in groups:
        l_s = l_ref[g]
        o_s = acc_ref[g] * jnp.where(l_s > 0.0, 1.0 / l_s, 0.0)
        p_w = _masked_softmax_cols(_nt(kw_ref[g, 0, pl.ds(w0, WIN_ROWS), :], qas[g]), in_window)
        o_w = _mm(vw_ref[0, rows_of(g), pl.ds(w0, WIN_ROWS)].astype(BF16), p_w.astype(BF16))
        gt = gt_ref[g]
        for r in range(NSA_GROUP):
            cs = slice(r * Q_LANES, (r + 1) * Q_LANES)
            outs.append(gt[r:r + 1, :] * o_c[g][:, cs] + gt[4 + r:5 + r, :] * o_s[:, cs]
                        + gt[8 + r:9 + r, :] * o_w[:, cs])
    o_ref[...] = jnp.concatenate(outs, axis=0).T


def _nsa(qa, kc, vc, ksa, nsa_t, kwa, win_t, gt, *, layer, bx, nqb, q_base, win_base):
    nq = qa.shape[2]
    n2 = kc.shape[2]
    tk = ksa.shape[2]
    tw = kwa.shape[2]
    assert nq == bx * nqb * Q_LANES and tk % KEY_TILE == 0 and tw >= WIN_ROWS
    assert nsa_t.shape[2:] == (512, tk) and win_t.shape[1:] == (256, tw)
    kern = functools.partial(_nsa_kernel, q_base=q_base, nsp=n2 // 2, tw=tw, win_base=win_base)
    ncol = NSA_GROUP * Q_LANES
    return pl.pallas_call(
        kern, grid=(bx, nqb),
        in_specs=[
            pl.BlockSpec((NSA_KV, NSA_GROUP, Q_LANES, LANES), lambda b, i: (0, 0, b * nqb + i, 0)),
            pl.BlockSpec((NSA_KV, 1, n2, LANES), lambda b, i: (0, b, 0, 0)),
            pl.BlockSpec((1, n2, LANES), lambda b, i: (b, 0, 0)),
            pl.BlockSpec((NSA_KV, 1, tk, LANES), lambda b, i: (0, b, 0, 0)),
            pl.BlockSpec((1, 1, LANES, tk), lambda b, i: (layer, b, 3, 0)),
            pl.BlockSpec((NSA_KV, 1, tw, LANES), lambda b, i: (0, b, 0, 0)),
            pl.BlockSpec((1, LANES, tw), lambda b, i: (b, 1, 0)),
            pl.BlockSpec((NSA_KV, 16, Q_LANES), lambda b, i: (0, 0, b * nqb + i)),
        ],
        out_specs=pl.BlockSpec((Q_LANES, NSA_HEADS * HEAD_DIM), lambda b, i: (b * nqb + i, 0)),
        out_shape=jax.ShapeDtypeStruct((nq, NSA_HEADS * HEAD_DIM), F32),
        scratch_shapes=[pltpu.VMEM((NSA_KV, n2 // 2, Q_LANES), F32), pltpu.VMEM((NSA_KV, HEAD_DIM, ncol), F32),
                        pltpu.VMEM((NSA_KV, 1, ncol), F32), pltpu.VMEM((NSA_KV, 1, ncol), F32),
                        pltpu.VMEM((NSA_KV, 8, LANES), F32)],
        compiler_params=pltpu.CompilerParams(dimension_semantics=("arbitrary",) * 2, vmem_limit_bytes=VMEM_LIMIT),
        name="nsa",
    )(qa, kc, vc, ksa, nsa_t, kwa, win_t, gt)


def _diff_kernel(q_ref, k_ref, v_ref, sl_ref, lp_ref, sg_ref, o_ref, acc_ref, m_ref, l_ref, kn_ref, *,
                 q_base, tq, lam_init):
    pair = pl.program_id(1)
    i = pl.program_id(2)
    q0 = q_base + i * tq
    ncol = 2 * tq
    heads = range(2)
    lp = lp_ref[...]
    lam = (jnp.exp(jnp.sum(lp[0:1] * lp[1:2], keepdims=True)) - jnp.exp(jnp.sum(lp[2:3] * lp[3:4], keepdims=True))
           + lam_init)
    n_full = q0 // KEY_TILE
    qas = [q_ref[hh].reshape(ncol, LANES) for hh in heads]
    for hh in heads:
        _flash_init(hh, m_ref, l_ref, acc_ref)

    @pl.when(i == 0)
    def _():
        for hh in heads:
            kn_ref[hh] = jnp.broadcast_to(_max_key_norm(k_ref.at[hh, 0], k_ref.shape[2]), kn_ref.shape[1:])

    def step(k0, size, which, causal=None):
        for hh in which:
            vt = v_ref[0, 0, hh * HEAD_DIM:(hh + 1) * HEAD_DIM, pl.ds(k0, size)].astype(BF16)
            s = _nt(k_ref[hh, 0, pl.ds(k0, size), :], qas[hh])
            if causal is not None:
                s = jnp.where(causal, s, NEG)
            _flash_update(hh, s, vt, m_ref, l_ref, acc_ref)

    k_diag = pl.multiple_of(n_full * KEY_TILE, KEY_TILE)

    tok = k_diag + lax.broadcasted_iota(jnp.int32, (KEY_TILE, ncol), 0)
    step(k_diag, KEY_TILE, heads, tok <= q0 + lax.broadcasted_iota(jnp.int32, (KEY_TILE, ncol), 1) % tq)

    first = []
    for hh in heads:
        slope = jnp.concatenate([sl_ref[pl.ds(2 * pair + hh, 1), :]] * (ncol // LANES), axis=1)
        first.append(_first_needed_tile(qas[hh], m_ref[hh], kn_ref[hh][0:1, 0:1], slope, n_full))
    _sweep_earlier_tiles(first, n_full, step)

    outs = []
    for hh in heads:
        o = acc_ref[hh] / l_ref[hh]
        d = o[:, 0:tq] - lam * o[:, tq:ncol]
        d = d * lax.rsqrt(jnp.mean(d * d, axis=0, keepdims=True) + EPS) * sg_ref[:, 0:tq] * (1.0 - lam_init)
        outs.append(d)
    o_ref[...] = jnp.concatenate(outs, axis=0).T


def _diff(dqa, dka, dif_t, slopes, lp, sg, *, layer, bx, nqb, tq, q_base, lam_init):
    nq = dqa.shape[2]
    tk = dka.shape[2]
    assert nq == bx * nqb * tq and tk % KEY_TILE == 0 and KEY_TILE % tq == 0 and dif_t.shape[2:] == (1024, tk)
    kern = functools.partial(_diff_kernel, q_base=q_base, tq=tq, lam_init=lam_init)
    return pl.pallas_call(
        kern, grid=(bx, DIFF_HEADS // 2, nqb),
        in_specs=[
            pl.BlockSpec((2, 2, tq, LANES), lambda b, h, i: (h, 0, b * nqb + i, 0)),
            pl.BlockSpec((2, 1, tk, LANES), lambda b, h, i: (h, b, 0, 0)),
            pl.BlockSpec((1, 1, LANES, tk), lambda b, h, i: (layer, b, 4 + h, 0)),
            pl.BlockSpec(slopes.shape, lambda b, h, i: (0, 0)),
            pl.BlockSpec(lp.shape, lambda b, h, i: (0, 0)),
            pl.BlockSpec(sg.shape, lambda b, h, i: (0, 0)),
        ],
        out_specs=pl.BlockSpec((tq, LANES), lambda b, h, i: (b * nqb + i, h)),
        out_shape=jax.ShapeDtypeStruct((nq, DIFF_HEADS * HEAD_DIM), F32),
        scratch_shapes=[pltpu.VMEM((2, HEAD_DIM, 2 * tq), F32), pltpu.VMEM((2, 1, 2 * tq), F32),
                        pltpu.VMEM((2, 1, 2 * tq), F32), pltpu.VMEM((2, 8, LANES), F32)],
        compiler_params=pltpu.CompilerParams(dimension_semantics=("arbitrary",) * 3, vmem_limit_bytes=VMEM_LIMIT),
        name="diff",
    )(dqa, dka, dif_t, slopes, lp, sg)


F_CHUNKS = 2


def _ffn_core(x, on, od, wo_ref, g2_ref, wg_ref, wu_ref, cw_ref, cb_ref, wd_ref, prev_rows, g_store):
    o = jnp.concatenate([on, od], axis=1).astype(BF16)
    xm = x + _mm(o, wo_ref[...])
    h2 = (xm * lax.rsqrt(jnp.mean(xm * xm, axis=-1, keepdims=True) + EPS) * g2_ref[...]).astype(BF16)
    d_ff = wg_ref.shape[1]
    fc = d_ff // F_CHUNKS
    y = jnp.zeros(x.shape, F32)
    for c in range(F_CHUNKS):
        c0, c1 = c * fc, (c + 1) * fc
        g = _mm(h2, wg_ref[:, c0:c1])
        u = _mm(h2, wu_ref[:, c0:c1])
        gm1, gm2 = prev_rows(g, c0, c1)
        g_store(g, c0, c1)
        gc = cb_ref[:, c0:c1] + cw_ref[0:1, c0:c1] * gm2 + cw_ref[1:2, c0:c1] * gm1 + cw_ref[2:3, c0:c1] * g
        act = gc * (1.0 / (1.0 + jnp.exp(-gc))) * u
        y = y + _mm(act.astype(BF16), wd_ref[c0:c1, :])
    return xm + y


def _ffn_prompt_kernel(x_ref, on_ref, od_ref, wo_ref, g2_ref, wg_ref, wu_ref, cw_ref, cb_ref, wd_ref,
                       y_ref, cv_ref, carry_ref, *, tiles_per_seq):
    i = pl.program_id(0)
    tr = x_ref.shape[0]

    @pl.when(i % tiles_per_seq == 0)
    def _():
        carry_ref[...] = jnp.zeros(carry_ref.shape, F32)

    def prev_rows(g, c0, c1):
        row = lax.broadcasted_iota(jnp.int32, g.shape, 0)
        p1 = carry_ref[7:8, c0:c1]
        p2 = carry_ref[6:7, c0:c1]
        gm1 = jnp.where(row == 0, p1, pltpu.roll(g, 1, 0))
        gm2 = jnp.where(row == 0, p2, jnp.where(row == 1, p1, pltpu.roll(g, 2, 0)))
        return gm1, gm2

    def g_store(g, c0, c1):
        carry_ref[:, c0:c1] = g[tr - 8:tr]
        cv_ref[0, :, c0:c1] = g[tr - 8:tr]

    y_ref[...] = _ffn_core(x_ref[...], on_ref[...], od_ref[...], wo_ref, g2_ref, wg_ref, wu_ref, cw_ref, cb_ref,
                           wd_ref, prev_rows, g_store)


def _ffn_sample_kernel(x_ref, on_ref, od_ref, st1_ref, st2_ref, wo_ref, g2_ref, wg_ref, wu_ref, cw_ref, cb_ref,
                       wd_ref, y_ref, g_ref, *, seq):
    def prev_rows(g, c0, c1):
        rs = lax.broadcasted_iota(jnp.int32, g.shape, 0) % seq
        s1 = st1_ref[:, c0:c1]
        gm1 = jnp.where(rs == 0, s1, pltpu.roll(g, 1, 0))
        gm2 = jnp.where(rs == 0, st2_ref[:, c0:c1], jnp.where(rs == 1, s1, pltpu.roll(g, 2, 0)))
        return gm1, gm2

    def g_store(g, c0, c1):
        g_ref[:, c0:c1] = g

    y_ref[...] = _ffn_core(x_ref[...], on_ref[...], od_ref[...], wo_ref, g2_ref, wg_ref, wu_ref, cw_ref, cb_ref,
                           wd_ref, prev_rows, g_store)


def _const_spec(a):
    return pl.BlockSpec(a.shape, lambda i: (0,) * a.ndim, pipeline_mode=pl.Buffered(1))


def _ffn_prompt(x2d, on, od, wo, g2, wg, wu, cw, cb, wd, *, seq_len, tr=256):
    n, d = x2d.shape
    f = wg.shape[1]
    assert n % tr == 0 and seq_len % tr == 0
    tps = seq_len // tr
    row = lambda w: pl.BlockSpec((tr, w), lambda i: (i, 0))
    return pl.pallas_call(
        functools.partial(_ffn_prompt_kernel, tiles_per_seq=tps), grid=(n // tr,),
        in_specs=[row(d), row(on.shape[1]), row(od.shape[1])] + [_const_spec(a) for a in (wo, g2, wg, wu, cw, cb, wd)],
        out_specs=(row(d), pl.BlockSpec((1, 8, f), lambda i: (i // tps, 0, 0))),
        out_shape=(jax.ShapeDtypeStruct((n, d), F32), jax.ShapeDtypeStruct((n // seq_len, 8, f), F32)),
        scratch_shapes=[pltpu.VMEM((8, f), F32)],
        compiler_params=pltpu.CompilerParams(dimension_semantics=("arbitrary",), vmem_limit_bytes=VMEM_LIMIT),
        name="ffn_prompt",
    )(x2d, on, od, wo, g2, wg, wu, cw, cb, wd)


def _ffn_sample(x2d, on, od, st1, st2, wo, g2, wg, wu, cw, cb, wd, *, seq):
    n, d = x2d.shape
    f = wg.shape[1]
    args = (x2d, on, od, st1, st2, wo, g2, wg, wu, cw, cb, wd)
    return pl.pallas_call(
        functools.partial(_ffn_sample_kernel, seq=seq), grid=(1,),
        in_specs=[_const_spec(a) for a in args],
        out_specs=(pl.BlockSpec((n, d), lambda i: (0, 0)), pl.BlockSpec((n, f), lambda i: (0, 0))),
        out_shape=(jax.ShapeDtypeStruct((n, d), F32), jax.ShapeDtypeStruct((n, f), F32)),
        compiler_params=pltpu.CompilerParams(dimension_semantics=("arbitrary",), vmem_limit_bytes=VMEM_LIMIT),
        name="ffn_sample",
    )(*args)


PAGES_PER_STEP = 8
STEPS_PER_CMP_TILE = LANES // (PAGES_PER_STEP * LANES // L_CMP)
T_NSA, T_WIN, T_DIF, T_ROWS = 0, 512, 768, 1792


def _softmax_rows(s):
    m = jnp.max(s, axis=-1, keepdims=True)
    p = jnp.exp2(s - m)
    return p * (1.0 / jnp.sum(p, axis=-1, keepdims=True))


def _masked_softmax_rows(s, valid):
    sm = jnp.where(valid, s, NEG)
    m = jnp.max(sm, axis=-1, keepdims=True)
    p = jnp.where(valid, jnp.exp2(sm - m), 0.0)
    l = jnp.sum(p, axis=-1, keepdims=True)
    return p * jnp.where(l > 0.0, 1.0 / l, 0.0)


def _sattn_kernel(pt_ref, *refs, n_steps, past_len, dec, lam_init):
    pg = PAGES_PER_STEP
    nsa_pg, dif_pg = refs[0:pg], refs[pg:2 * pg]
    (win_ref, nn_ref, nw_ref, nd_ref, qn_ref, qd_ref, gate_ref, sn_ref, sd_ref, wck_ref, wcv_ref, gkc_ref,
     lp_ref, sg_ref, o_ref, ns_ref,
     ssel, sdif, psel, pdif, vsel, vdif, kc_scr, tail_scr, oc_scr, ow_scr) = refs[2 * pg:]
    s = pl.program_id(1)
    span = pg * LANES
    groups, heads = range(NSA_KV), range(DIFF_HEADS)
    cat = lambda parts: jnp.concatenate(parts, axis=1)
    hi_lo = lambda x: (x.astype(BF16), (x - x.astype(BF16).astype(F32)).astype(BF16))

    @pl.when(s < n_steps)
    def _():
        @pl.when(s == 0)
        def _():
            kc_scr[...] = jnp.zeros(kc_scr.shape, F32)

        shift = (s % STEPS_PER_CMP_TILE) * (span // L_CMP)
        col = pl.multiple_of((s // STEPS_PER_CMP_TILE) * LANES, LANES)
        for part, w_ref in ((0, wck_ref), (1, wcv_ref)):
            x_hi, x_lo = hi_lo(cat([r[0, 0, part * 128:(part + 1) * 128, :] for r in nsa_pg]))
            blk = _mm(x_hi, w_ref[0]) + _mm(x_lo, w_ref[0]) + _mm(x_hi, w_ref[1])
            kc_scr[part * 128:(part + 1) * 128, pl.ds(col, LANES)] += pltpu.roll(blk, shift, 1)

        k0 = pl.multiple_of(s * span, span)
        kpos = (k0 + lax.broadcasted_iota(jnp.int32, (1, span), 1)).astype(F32)
        for g in groups:
            k8 = cat([r[0, 0, 256 + g * HEAD_DIM:256 + (g + 1) * HEAD_DIM, :] for r in nsa_pg]).astype(BF16)
            ssel[g, :, pl.ds(k0, span)] = _mm(qn_ref[0, g], k8) + cat([sn_ref[g]] * pg) * kpos
        for h in heads:
            k8 = cat([r[0, 0, h * HEAD_DIM:(h + 1) * HEAD_DIM, :] for r in dif_pg]).astype(BF16)
            sdif[h, :, pl.ds(k0, span)] = _mm(qd_ref[0, h], k8) + cat([sd_ref[h]] * pg) * kpos
        vsel[:, pl.ds(k0, span)] = cat([r[0, 0, 384:512, :] for r in nsa_pg]).astype(BF16)
        vdif[:, pl.ds(k0, span)] = cat([r[0, 0, 512:1024, :] for r in dif_pg]).astype(BF16)

    @pl.when(s == n_steps)
    def _():
        new_rows = cat([nn_ref[...], nw_ref[...], nd_ref[...]])
        tail_scr[...] = jnp.concatenate([new_rows, jnp.zeros((LANES - dec, T_ROWS), F32)], axis=0).T
        nrow, drow = NSA_GROUP * dec, 2 * dec
        tail_pos = (past_len + lax.broadcasted_iota(jnp.int32, (1, LANES), 1)).astype(F32)
        causal32 = (lax.broadcasted_iota(jnp.int32, (nrow, LANES), 1)
                    <= lax.broadcasted_iota(jnp.int32, (nrow, LANES), 0) % dec)
        causal16 = (lax.broadcasted_iota(jnp.int32, (drow, LANES), 1)
                    <= lax.broadcasted_iota(jnp.int32, (drow, LANES), 0) % dec)

        for g in groups:
            kt = tail_scr[T_NSA + 256 + g * HEAD_DIM:T_NSA + 256 + (g + 1) * HEAD_DIM, :].astype(BF16)
            sc = _mm(qn_ref[0, g], kt) + sn_ref[g] * tail_pos
            ssel[g, :, past_len:past_len + LANES] = jnp.where(causal32, sc, NEG)
        for h in heads:
            kt = tail_scr[T_DIF + h * HEAD_DIM:T_DIF + (h + 1) * HEAD_DIM, :].astype(BF16)
            sc = _mm(qd_ref[0, h], kt) + sd_ref[h] * tail_pos
            sdif[h, :, past_len:past_len + LANES] = jnp.where(causal16, sc, NEG)

        n_cmp = kc_scr.shape[1]
        cend = lax.broadcasted_iota(jnp.int32, (nrow, n_cmp), 1) * L_CMP + (L_CMP - 1)
        qpos_c = past_len + lax.broadcasted_iota(jnp.int32, (nrow, n_cmp), 0) % dec
        imps = []
        for g in groups:
            kc = kc_scr[g * HEAD_DIM:(g + 1) * HEAD_DIM, :]
            kc = kc * lax.rsqrt(jnp.mean(kc * kc, axis=0, keepdims=True) + EPS) * cat([gkc_ref[...]] * (n_cmp // LANES))
            s_c = _mm(qn_ref[0, g], kc.astype(BF16)) + cat([sn_ref[g]] * (n_cmp // LANES)) * cend.astype(F32)
            p_c = _masked_softmax_rows(s_c, cend <= qpos_c)
            vc = kc_scr[128 + g * HEAD_DIM:128 + (g + 1) * HEAD_DIM, :].astype(BF16)
            oc_scr[g] = _nt(p_c.astype(BF16), vc)
            imps.append(sum(p_c[r * dec:(r + 1) * dec] for r in range(NSA_GROUP)))
        imp2 = jnp.concatenate(imps, axis=0)
        pair = (lax.broadcasted_iota(jnp.int32, (n_cmp, LANES), 0) // 2
                == lax.broadcasted_iota(jnp.int32, (n_cmp, LANES), 1)).astype(BF16)
        i_hi, i_lo = hi_lo(imp2)
        imp = _mm(i_hi, pair) + _mm(i_lo, pair)

        n_blk = past_len // L_SEL
        blk = lax.broadcasted_iota(jnp.int32, imp.shape, 1)
        blk_f = blk.astype(F32)
        bonus = jnp.where((blk == 0) | (blk == n_blk - 1), FORCE_BONUS, 0.0)
        score = jnp.where(blk < n_blk, imp + bonus, -jnp.inf)
        sel = jnp.zeros(imp.shape, F32)
        for _ in range(N_SELECT - 1):
            top = jnp.max(score, axis=1, keepdims=True)
            first = jnp.min(jnp.where(score == top, blk_f, 1e9), axis=1, keepdims=True)
            pick = blk_f == first
            sel = jnp.where(pick, jnp.where(top > -jnp.inf, 1.0, sel), sel)
            score = jnp.where(pick, -jnp.inf, score)
        sel = sel.astype(BF16)

        for c in range(n_steps):
            tok = c * span + lax.broadcasted_iota(jnp.int32, (LANES, span), 1)
            expand = ((tok >> 6) == lax.broadcasted_iota(jnp.int32, (LANES, span), 0)).astype(BF16)
            chosen = _mm(sel, expand)
            for g in groups:
                keep = jnp.concatenate([chosen[g * dec:(g + 1) * dec]] * NSA_GROUP, axis=0) > 0.5
                ssel[g, :, c * span:(c + 1) * span] = jnp.where(keep, ssel[g, :, c * span:(c + 1) * span], NEG)
        for g in groups:
            psel[g] = _softmax_rows(ssel[g]).astype(BF16)
        for h in heads:
            pdif[h] = _softmax_rows(sdif[h]).astype(BF16)

        w_buf = win_ref.shape[3]
        kwpos = past_len - w_buf + lax.broadcasted_iota(jnp.int32, (nrow, w_buf + LANES), 1)
        dist = past_len + lax.broadcasted_iota(jnp.int32, (nrow, w_buf + LANES), 0) % dec - kwpos
        in_window = (dist >= 0) & (dist < WINDOW)
        for g in groups:
            kw = cat([win_ref[0, 0, g * HEAD_DIM:(g + 1) * HEAD_DIM, :],
                      tail_scr[T_WIN + g * HEAD_DIM:T_WIN + (g + 1) * HEAD_DIM, :]]).astype(BF16)
            vw = cat([win_ref[0, 0, 128 + g * HEAD_DIM:128 + (g + 1) * HEAD_DIM, :],
                      tail_scr[T_WIN + 128 + g * HEAD_DIM:T_WIN + 128 + (g + 1) * HEAD_DIM, :]]).astype(BF16)
            s_w = _mm(qn_ref[0, g], kw) + cat([sn_ref[g]] * (w_buf // LANES + 1)) * kwpos.astype(F32)
            ow_scr[g] = _nt(_masked_softmax_rows(s_w, in_window).astype(BF16), vw)

        shifted = pltpu.roll(win_ref[0, 0], w_buf - dec, 1)
        fresh = cat([jnp.zeros((256, w_buf - LANES), F32), pltpu.roll(tail_scr[T_WIN:T_WIN + 256, :], LANES - dec, 1)])
        ns_ref[0] = jnp.where(lax.broadcasted_iota(jnp.int32, (256, w_buf), 1) >= w_buf - dec, fresh, shifted)

        lp = lp_ref[...]
        lam = (jnp.exp(jnp.sum(lp[0:1] * lp[1:2], keepdims=True))
               - jnp.exp(jnp.sum(lp[2:3] * lp[3:4], keepdims=True)) + lam_init)
        pieces = []
        for g in groups:
            vt = tail_scr[T_NSA + 384 + g * HEAD_DIM:T_NSA + 384 + (g + 1) * HEAD_DIM, :].astype(BF16)
            o_s = (_nt(psel[g, :, 0:past_len], vsel[g * HEAD_DIM:(g + 1) * HEAD_DIM, :])
                   + _nt(psel[g, :, past_len:past_len + LANES], vt))
            o = gate_ref[0, g, 0] * oc_scr[g] + gate_ref[0, g, 1] * o_s + gate_ref[0, g, 2] * ow_scr[g]
            pieces += [o[r * dec:(r + 1) * dec] for r in range(NSA_GROUP)]
        for h in heads:
            vt = tail_scr[T_DIF + 512 + h * HEAD_DIM:T_DIF + 512 + (h + 1) * HEAD_DIM, :].astype(BF16)
            o = (_nt(pdif[h, :, 0:past_len], vdif[h * HEAD_DIM:(h + 1) * HEAD_DIM, :])
                 + _nt(pdif[h, :, past_len:past_len + LANES], vt))
            d = o[0:dec] - lam * o[dec:2 * dec]
            pieces.append(d * lax.rsqrt(jnp.mean(d * d, axis=-1, keepdims=True) + EPS) * sg_ref[...] * (1.0 - lam_init))
        o_ref[0] = cat(pieces)


def _sattn(page_table, cache_nsa, cache_diff, state_win, new_nsa, new_win, new_dif, qn, qd, gates, sn, sd,
           wck, wcv, gkc, lp, sg, *, layer, lam_init):
    bs, n_pages = page_table.shape
    page = cache_nsa.shape[3]
    dec = new_nsa.shape[0] // bs
    past_len = n_pages * page
    pg = PAGES_PER_STEP
    n_steps = n_pages // pg
    w_buf = state_win.shape[3]
    assert page == LANES and n_pages % pg == 0 and dec == 8 and w_buf % LANES == 0
    n_cmp = -(-n_steps // STEPS_PER_CMP_TILE) * LANES
    tks = past_len + LANES
    pt = page_table.reshape(-1)

    def kpage(j):
        return lambda b, s, pt_ref: (layer, pt_ref[b * n_pages + jnp.minimum(s, n_steps - 1) * pg + j], 0, 0)

    const = lambda a: pl.BlockSpec(a.shape, lambda b, s, pt_ref: (0,) * a.ndim)
    per_b = lambda a: pl.BlockSpec((1,) + a.shape[1:], lambda b, s, pt_ref: (b,) + (0,) * (a.ndim - 1))
    in_specs = (
        [pl.BlockSpec((1, 1, 512, page), kpage(j)) for j in range(pg)]
        + [pl.BlockSpec((1, 1, 1024, page), kpage(j)) for j in range(pg)]
        + [pl.BlockSpec((1, 1, 256, w_buf), lambda b, s, pt_ref: (layer, b, 0, 0)),
           pl.BlockSpec((dec, 512), lambda b, s, pt_ref: (b, 0)),
           pl.BlockSpec((dec, 256), lambda b, s, pt_ref: (b, 0)),
           pl.BlockSpec((dec, 1024), lambda b, s, pt_ref: (b, 0)),
           per_b(qn), per_b(qd), per_b(gates), const(sn), const(sd), const(wck), const(wcv), const(gkc),
           const(lp), const(sg)])
    out_specs = (pl.BlockSpec((1, dec, 1024), lambda b, s, pt_ref: (b, 0, 0)),
                 pl.BlockSpec((1, 256, w_buf), lambda b, s, pt_ref: (b, 0, 0)))
    scratch = [
        pltpu.VMEM((NSA_KV, NSA_GROUP * dec, tks), F32), pltpu.VMEM((DIFF_HEADS, 2 * dec, tks), F32),
        pltpu.VMEM((NSA_KV, NSA_GROUP * dec, tks), BF16), pltpu.VMEM((DIFF_HEADS, 2 * dec, tks), BF16),
        pltpu.VMEM((NSA_KV * HEAD_DIM, past_len), BF16), pltpu.VMEM((DIFF_HEADS * HEAD_DIM, past_len), BF16),
        pltpu.VMEM((256, n_cmp), F32), pltpu.VMEM((T_ROWS, LANES), F32),
        pltpu.VMEM((NSA_KV, NSA_GROUP * dec, HEAD_DIM), F32), pltpu.VMEM((NSA_KV, NSA_GROUP * dec, HEAD_DIM), F32),
    ]
    grid_spec = pltpu.PrefetchScalarGridSpec(num_scalar_prefetch=1, grid=(bs, n_steps + 1), in_specs=in_specs,
                                             out_specs=out_specs, scratch_shapes=scratch)
    caches = [cache_nsa] * pg + [cache_diff] * pg
    return pl.pallas_call(
        functools.partial(_sattn_kernel, n_steps=n_steps, past_len=past_len, dec=dec, lam_init=lam_init),
        grid_spec=grid_spec,
        out_shape=(jax.ShapeDtypeStruct((bs, dec, 1024), F32), jax.ShapeDtypeStruct((bs, 256, w_buf), F32)),
        compiler_params=pltpu.CompilerParams(dimension_semantics=("arbitrary",) * 2, vmem_limit_bytes=VMEM_LIMIT),
        name="sattn",
    )(pt, *caches, state_win, new_nsa, new_win, new_dif, qn, qd, gates, sn, sd, wck, wcv, gkc, lp, sg)


def _alibi_slopes(n):
    return 2.0 ** (-8.0 * jnp.arange(1, n + 1, dtype=F32) / n)


def _slope_rows(slopes):
    slopes = slopes * LOG2E
    s0 = slopes.astype(BF16).astype(F32)
    s1 = (slopes - s0).astype(BF16).astype(F32)
    s2 = (slopes - s0 - s1).astype(BF16).astype(F32)
    cols = jnp.stack([64.0 * s0, s0, 64.0 * s1, s1, 64.0 * s2, s2], axis=1)
    out = jnp.zeros((slopes.shape[0], LANES), F32)
    return out.at[:, AUG0:AUG0 + N_AUG].set(cols)


def _block_ones(group):
    idx = np.arange(LANES) // group
    return jnp.asarray((idx[:, None] == idx[None, :]).astype(np.float32), dtype=BF16)


_GATE_SRC = np.full((LANES,), -1, np.int64)
for _g in range(NSA_KV):
    for _r in range(NSA_GROUP):
        for _j in range(3):
            _GATE_SRC[_g * 16 + _j * 4 + _r] = 1280 + _g * 12 + _r * 3 + _j


def _permute_w_in(w):
    main = jnp.concatenate([w[:, 0:1280], w[:, 1304:2840]], axis=1)
    gate = jnp.where(jnp.asarray(_GATE_SRC >= 0)[None, :], w[:, np.maximum(_GATE_SRC, 0)], 0.0)
    return jnp.concatenate([main, gate], axis=1).astype(BF16)


def _compress_weights(w):
    t = np.arange(PAGES_PER_STEP * LANES)
    place = jnp.asarray(t[:, None] // L_CMP == np.arange(LANES)[None, :])
    full = jnp.where(place, jnp.tile(w, PAGES_PER_STEP * LANES // L_CMP)[:, None], 0.0)
    hi = full.astype(BF16)
    return jnp.stack([hi, (full - hi.astype(F32)).astype(BF16)])


def _gain_row(nsa_qg, nsa_kg, diff_qg, diff_kg):
    one = lambda n: jnp.ones((n,), F32)
    parts = [jnp.tile(nsa_qg, NSA_HEADS), one(256), jnp.tile(nsa_kg[1], NSA_KV), one(128),
             jnp.tile(nsa_kg[2], NSA_KV), one(128), jnp.tile(diff_qg.reshape(-1), DIFF_HEADS),
             jnp.tile(diff_kg.reshape(-1), DIFF_HEADS), one(512 + LANES)]
    return jnp.concatenate(parts)[None, :]


def kernel(x_prompt, x_sample, cache_nsa, cache_diff, state_win, state_conv, page_table, norm1_g, norm2_g, w_in, w_out, nsa_qnorm_g, nsa_knorm_g, nsa_cmp_w, diff_qnorm_g, diff_knorm_g, diff_lambda, diff_subnorm_g, w_gate, w_up, conv_w, conv_b, w_down):
    batch, seq, d_model = x_prompt.shape
    dec_batch, dec_seq, _ = x_sample.shape
    depth, n_pool, page = cache_nsa.shape[:3]
    n_pages = page_table.shape[1]
    past_len = n_pages * page
    w_buf = state_win.shape[2]
    d_ff = w_gate.shape[2]
    assert dec_seq < L_CMP and seq % KEY_TILE == 0 and w_buf == WINDOW

    s64, s32 = _block_ones(HEAD_DIM), _block_ones(DIFF_HALF)
    qcn = _slope_rows(_alibi_slopes(NSA_HEADS))
    qcd = _slope_rows(_alibi_slopes(DIFF_HEADS))
    fmaj = lambda a: jnp.moveaxis(a, 2, -1).reshape(a.shape[0], a.shape[1], -1, a.shape[2])
    cache_nsa4 = fmaj(cache_nsa)
    cache_diff4 = fmaj(cache_diff)
    state_win4 = fmaj(state_win)

    sn = jnp.broadcast_to((_alibi_slopes(NSA_HEADS) * LOG2E).reshape(NSA_KV, NSA_GROUP, 1, 1),
                          (NSA_KV, NSA_GROUP, dec_seq, LANES)).reshape(NSA_KV, NSA_GROUP * dec_seq, LANES)
    sd = jnp.broadcast_to((_alibi_slopes(DIFF_HEADS) * LOG2E).reshape(DIFF_HEADS, 1, 1), (DIFF_HEADS, 2 * dec_seq, LANES))
    xp = x_prompt.reshape(batch * seq, d_model)
    xs = x_sample.reshape(dec_batch * dec_seq, d_model)

    nsa_t = jnp.zeros((depth, batch, 512, seq), F32)
    dif_t = jnp.zeros((depth, batch, 1024, seq), F32)
    outs = [[] for _ in range(8)]
    for l in range(depth):
        lam_init = 0.8 - 0.6 * math.exp(-0.3 * l)
        w_in_p = _permute_w_in(w_in[l])
        gain = _gain_row(nsa_qnorm_g[l], nsa_knorm_g[l], diff_qnorm_g[l], diff_knorm_g[l])
        gain_kc = jnp.tile(nsa_knorm_g[l, 0], NSA_KV)[None, :]
        cw = jnp.concatenate([jnp.broadcast_to(nsa_cmp_w[l, 0][:, None], (L_CMP, LANES)),
                              jnp.broadcast_to(nsa_cmp_w[l, 1][:, None], (L_CMP, LANES))], axis=1)
        lp = jnp.zeros((8, LANES), F32).at[0:4, 0:DIFF_HALF].set(diff_lambda[l])
        sg = jnp.broadcast_to(diff_subnorm_g[l][:, None], (HEAD_DIM, DIFF_TQ))
        sg_row = jnp.broadcast_to(diff_subnorm_g[l][None, :], (dec_seq, HEAD_DIM))
        gkc = jnp.broadcast_to(nsa_knorm_g[l, 0][:, None], (HEAD_DIM, LANES))
        wck, wcv = _compress_weights(nsa_cmp_w[l, 0]), _compress_weights(nsa_cmp_w[l, 1])
        g1 = norm1_g[l][None, :]
        g2 = norm2_g[l][None, :]
        wo, wg, wu, wd = (a.astype(BF16) for a in (w_out[l], w_gate[l], w_up[l], w_down[l]))
        cwf, cbf = conv_w[l], conv_b[l][None, :]
        proj = functools.partial(_proj, g1=g1, w=w_in_p, gain=gain, s64=s64, s32=s32, qcn=qcn, qcd=qcd, cw=cw)

        (nsa_t, win_t, dif_t, qa, ksa, kwa, gt, dqa, dka, craw) = proj(
            xp, seq_len=seq, pos_base=0, stacked=(l, depth, (nsa_t, dif_t)))
        kc, vc = _cmpfin(craw.reshape(batch, seq // L_CMP, 256), s64, gain_kc)
        o_nsa = _nsa(qa, kc, vc, ksa.reshape(NSA_KV, batch, seq, LANES), nsa_t,
                     kwa.reshape(NSA_KV, batch, seq, LANES), win_t, gt,
                     layer=l, bx=batch, nqb=seq // Q_LANES, q_base=0, win_base=0)
        o_dif = _diff(dqa, dka.reshape(DIFF_HEADS, batch, seq, LANES), dif_t, sd[:, 0], lp, sg,
                      layer=l, bx=batch, nqb=seq // DIFF_TQ, tq=DIFF_TQ, q_base=0, lam_init=lam_init)
        xp, cv = _ffn_prompt(xp, o_nsa, o_dif, wo, g2, wg, wu, cwf, cbf, wd, seq_len=seq)
        w_keep = min(WINDOW, seq)
        outs[4].append(jnp.moveaxis(win_t[:, :, seq - w_keep:].reshape(batch, 2, NSA_KV, HEAD_DIM, w_keep), -1, 1))
        outs[6].append(cv[:, 8 - 2:8])

        (nsa_rows, win_rows, dif_rows, qa, _, _, gt, dqa, _, _) = proj(xs, seq_len=dec_seq, pos_base=past_len)
        per_seq = lambda a, lead: a[..., :HEAD_DIM].reshape(lead + (dec_batch, dec_seq, HEAD_DIM))
        qn = jnp.transpose(per_seq(qa, (NSA_KV, NSA_GROUP)), (2, 0, 1, 3, 4)).reshape(
            dec_batch, NSA_KV, NSA_GROUP * dec_seq, HEAD_DIM)
        qd = jnp.transpose(per_seq(dqa, (DIFF_HEADS, 2)), (2, 0, 1, 3, 4)).reshape(
            dec_batch, DIFF_HEADS, 2 * dec_seq, HEAD_DIM)
        gates = jnp.transpose(gt.reshape(NSA_KV, 4, NSA_GROUP, dec_batch, dec_seq), (3, 0, 1, 2, 4)).reshape(
            dec_batch, NSA_KV, 4, NSA_GROUP * dec_seq, 1)
        gates = jnp.broadcast_to(gates, gates.shape[:-1] + (HEAD_DIM,))
        o_s, new_state = _sattn(page_table, cache_nsa4, cache_diff4, state_win4, nsa_rows, win_rows, dif_rows,
                                qn, qd, gates, sn, sd, wck, wcv, gkc, lp, sg_row, layer=l, lam_init=lam_init)
        o_s = o_s.reshape(dec_batch * dec_seq, 2 * NSA_HEADS * HEAD_DIM)
        st1 = jnp.repeat(state_conv[l][:, 1], dec_seq, axis=0)
        st2 = jnp.repeat(state_conv[l][:, 0], dec_seq, axis=0)
        xs, g_s = _ffn_sample(xs, o_s[:, :NSA_HEADS * HEAD_DIM], o_s[:, NSA_HEADS * HEAD_DIM:], st1, st2,
                              wo, g2, wg, wu, cwf, cbf, wd, seq=dec_seq)
        outs[1].append(nsa_rows.reshape(dec_batch, dec_seq, 4, NSA_KV, HEAD_DIM))
        outs[3].append(dif_rows.reshape(dec_batch, dec_seq, 2, DIFF_HEADS, HEAD_DIM))
        outs[5].append(jnp.moveaxis(new_state.reshape(dec_batch, 2, NSA_KV, HEAD_DIM, w_buf), -1, 1))
        outs[7].append(g_s.reshape(dec_batch, dec_seq, d_ff)[:, dec_seq - 2:])

    outs[0] = jnp.moveaxis(nsa_t.reshape(depth, batch, 4, NSA_KV, HEAD_DIM, seq), -1, 2)
    outs[2] = jnp.moveaxis(dif_t.reshape(depth, batch, 2, DIFF_HEADS, HEAD_DIM, seq), -1, 2)
    stacked = [o if not isinstance(o, list) else jnp.stack(o) for o in outs]
    return (xp.reshape(batch, seq, d_model), xs.reshape(dec_batch, dec_seq, d_model), *stacked)
```

```python
import functools
import math

import jax
import jax.numpy as jnp
import numpy as np
from jax import lax
from jax.experimental import pallas as pl
from jax.experimental.pallas import tpu as pltpu

F32 = jnp.float32
BF16 = jnp.bfloat16

HEAD_DIM = 64
NSA_KV = 2
NSA_GROUP = 4
NSA_HEADS = NSA_KV * NSA_GROUP
DIFF_HEADS = 8
DIFF_HALF = HEAD_DIM // 2
L_CMP = 32
L_SEL = 64
N_SELECT = 16
WINDOW = 512
FORCE_BONUS = 1.0e4
EPS = 1e-6
NEG = -1e30
UNDERFLOW_LOG2 = -160.0
NORM_SLACK = 1.02
LOG2E = math.log2(math.e)

LANES = 128
KEY_TILE = 512
Q_LANES = 128
DIFF_TQ = 512
WIN_ROWS = WINDOW + Q_LANES
AUG0 = HEAD_DIM
N_AUG = 6
VMEM_LIMIT = 56 * 1024 * 1024

C_Q, C_NSA, C_WIN, C_DQ, C_DK, C_DV, C_GATE = 0, 512, 1024, 1280, 1792, 2304, 2816
N_COL = 2944


def _nt(a, b):
    return lax.dot_general(a, b, (((1,), (1,)), ((), ())), preferred_element_type=F32)


def _tn(a, b):
    return lax.dot_general(a, b, (((0,), (0,)), ((), ())), preferred_element_type=F32)


def _mm(a, b):
    return jnp.dot(a, b, preferred_element_type=F32)


def _group_meansq(z, smat, inv_n):
    zz = z * z
    hi = zz.astype(BF16)
    lo = (zz - hi.astype(F32)).astype(BF16)
    return (_mm(hi, smat) + _mm(lo, smat)) * inv_n


def _pos_rows(pos, lane):
    is_aug = (lane >= AUG0) & (lane < AUG0 + N_AUG)
    val = jnp.where((lane & 1) == 0, pos >> 6, pos & 63)
    return jnp.where(is_aug, val, 0).astype(F32)


def _split_heads(slab, fill, lane):
    even = jnp.where(lane < HEAD_DIM, slab, fill)
    odd = jnp.where(lane < HEAD_DIM, pltpu.roll(slab, HEAD_DIM, 1), fill)
    return even, odd


def _proj_kernel(x_ref, g1_ref, w_ref, gain_ref, s64_ref, s32_ref, qcn_ref, qcd_ref, cw_ref, *rest,
                 seq_len, pos_base, fmajor):
    nsa_ref, win_ref, dif_ref, qa_ref, ksa_ref, kwa_ref, gt_ref, dqa_ref, dka_ref, cr_ref = rest[-10:]
    tr = x_ref.shape[0]

    def put(ref, c0, slab):
        if fmajor:
            ref[(0,) * (len(ref.shape) - 2) + (slice(c0, c0 + LANES), slice(None))] = slab.T
        else:
            ref[:, c0:c0 + LANES] = slab
    i = pl.program_id(0)
    x = x_ref[...]
    h = x * lax.rsqrt(jnp.mean(x * x, axis=-1, keepdims=True) + EPS) * g1_ref[...]
    hb = h.astype(BF16)

    lane = lax.broadcasted_iota(jnp.int32, (tr, LANES), 1)
    row = lax.broadcasted_iota(jnp.int32, (tr, LANES), 0)
    pos = pos_base + (i * tr + row) % seq_len
    prow = _pos_rows(pos, lane)
    s64 = s64_ref[...]
    s32 = s32_ref[...]

    def seg(c0, width):
        return _mm(hb, w_ref[:, c0:c0 + width])

    def normed(z, c0, smat, inv_n):
        return z * lax.rsqrt(_group_meansq(z, smat, inv_n) + EPS) * gain_ref[:, c0:c0 + LANES]

    zq = seg(C_Q, 512)
    for s in range(4):
        zn = normed(zq[:, s * LANES:(s + 1) * LANES], C_Q + s * LANES, s64, 1.0 / HEAD_DIM) * (HEAD_DIM ** -0.5 * LOG2E)
        for par in range(2):
            hd = 2 * s + par
            src = zn if par == 0 else pltpu.roll(zn, HEAD_DIM, 1)
            qa = jnp.where(lane < HEAD_DIM, src, qcn_ref[hd:hd + 1, :])
            qa_ref[hd // NSA_GROUP, hd % NSA_GROUP] = qa.astype(BF16)

    zc = seg(C_NSA, 512)
    ks = normed(zc[:, 256:384], C_NSA + 256, s64, 1.0 / HEAD_DIM)
    put(nsa_ref, 0, zc[:, 0:128])
    put(nsa_ref, 128, zc[:, 128:256])
    put(nsa_ref, 256, ks)
    put(nsa_ref, 384, zc[:, 384:512])
    k0, k1 = _split_heads(ks, prow, lane)
    ksa_ref[0] = k0.astype(BF16)
    ksa_ref[1] = k1.astype(BF16)
    craw = zc[:, 0:256].reshape(tr // L_CMP, L_CMP, 256) * cw_ref[...][None]
    cr_ref[...] = jnp.sum(craw, axis=1)

    zw = seg(C_WIN, 256)
    kw = normed(zw[:, 0:128], C_WIN, s64, 1.0 / HEAD_DIM)
    put(win_ref, 0, kw)
    put(win_ref, 128, zw[:, 128:256])
    k0, k1 = _split_heads(kw, prow, lane)
    kwa_ref[0] = k0.astype(BF16)
    kwa_ref[1] = k1.astype(BF16)

    zdq = seg(C_DQ, 512)
    for s in range(4):
        zn = normed(zdq[:, s * LANES:(s + 1) * LANES], C_DQ + s * LANES, s32, 1.0 / DIFF_HALF) * (DIFF_HALF ** -0.5 * LOG2E)
        for par in range(2):
            hd = 2 * s + par
            src = zn if par == 0 else pltpu.roll(zn, HEAD_DIM, 1)
            fill = qcd_ref[hd:hd + 1, :]
            dqa_ref[hd, 0] = jnp.where(lane < DIFF_HALF, src, fill).astype(BF16)
            dqa_ref[hd, 1] = jnp.where((lane >= DIFF_HALF) & (lane < HEAD_DIM), src, fill).astype(BF16)

    zdk = seg(C_DK, 512)
    for s in range(4):
        kn = normed(zdk[:, s * LANES:(s + 1) * LANES], C_DK + s * LANES, s32, 1.0 / DIFF_HALF)
        put(dif_ref, s * LANES, kn)
        k0, k1 = _split_heads(kn, prow, lane)
        dka_ref[2 * s] = k0.astype(BF16)
        dka_ref[2 * s + 1] = k1.astype(BF16)
    zdv = seg(C_DV, 512)
    for s in range(4):
        put(dif_ref, 512 + s * LANES, zdv[:, s * LANES:(s + 1) * LANES])

    zg = seg(C_GATE, LANES)
    sg = 1.0 / (1.0 + jnp.exp(-zg))
    gt_ref[...] = sg.T[0:2 * 16].reshape(NSA_KV, 16, tr)


def _proj(x2d, g1, w, gain, s64, s32, qcn, qcd, cw, *, seq_len, pos_base, stacked=None, tr=512):
    n = x2d.shape[0]
    tr = min(tr, n)
    assert n % tr == 0 and tr % L_CMP == 0
    fmajor = stacked is not None
    kern = functools.partial(_proj_kernel, seq_len=seq_len, pos_base=pos_base, fmajor=fmajor)
    full = lambda shape: pl.BlockSpec(shape, lambda i: (0,) * len(shape))
    if fmajor:
        layer, depth, prev = stacked
        assert seq_len % tr == 0
        tps, nb = seq_len // tr, n // seq_len
        cache_shapes = [jax.ShapeDtypeStruct((depth, nb, 512, seq_len), F32),
                        jax.ShapeDtypeStruct((nb, 256, seq_len), F32),
                        jax.ShapeDtypeStruct((depth, nb, 1024, seq_len), F32)]
        cache_specs = [pl.BlockSpec((1, 1, 512, tr), lambda i: (layer, i // tps, 0, i % tps)),
                       pl.BlockSpec((1, 256, tr), lambda i: (i // tps, 0, i % tps)),
                       pl.BlockSpec((1, 1, 1024, tr), lambda i: (layer, i // tps, 0, i % tps))]
    else:
        prev = None
        cache_shapes = [jax.ShapeDtypeStruct((n, w_), F32) for w_ in (512, 256, 1024)]
        cache_specs = [pl.BlockSpec((tr, w_), lambda i: (i, 0)) for w_ in (512, 256, 1024)]
    out_shape = tuple(cache_shapes) + (
        jax.ShapeDtypeStruct((NSA_KV, NSA_GROUP, n, LANES), BF16),
        jax.ShapeDtypeStruct((NSA_KV, n, LANES), BF16),
        jax.ShapeDtypeStruct((NSA_KV, n, LANES), BF16),
        jax.ShapeDtypeStruct((NSA_KV, 16, n), F32),
        jax.ShapeDtypeStruct((DIFF_HEADS, 2, n, LANES), BF16),
        jax.ShapeDtypeStruct((DIFF_HEADS, n, LANES), BF16),
        jax.ShapeDtypeStruct((n // L_CMP, 256), F32),
    )
    out_specs = tuple(cache_specs) + (
        pl.BlockSpec((NSA_KV, NSA_GROUP, tr, LANES), lambda i: (0, 0, i, 0)),
        pl.BlockSpec((NSA_KV, tr, LANES), lambda i: (0, i, 0)),
        pl.BlockSpec((NSA_KV, tr, LANES), lambda i: (0, i, 0)),
        pl.BlockSpec((NSA_KV, 16, tr), lambda i: (0, 0, i)),
        pl.BlockSpec((DIFF_HEADS, 2, tr, LANES), lambda i: (0, 0, i, 0)),
        pl.BlockSpec((DIFF_HEADS, tr, LANES), lambda i: (0, i, 0)),
        pl.BlockSpec((tr // L_CMP, 256), lambda i: (i, 0)),
    )
    in_specs = [
        pl.BlockSpec((tr, x2d.shape[1]), lambda i: (i, 0)),
        full(g1.shape), full(w.shape), full(gain.shape), full(s64.shape), full(s32.shape),
        full(qcn.shape), full(qcd.shape), full(cw.shape),
    ]
    args = [x2d, g1, w, gain, s64, s32, qcn, qcd, cw]
    aliases = {}
    if prev is not None:
        in_specs += [pl.BlockSpec(memory_space=pl.ANY)] * 2
        aliases = {len(args): 0, len(args) + 1: 2}
        args += list(prev)
    return pl.pallas_call(
        kern, grid=(n // tr,), in_specs=in_specs, out_specs=out_specs, out_shape=out_shape,
        input_output_aliases=aliases,
        compiler_params=pltpu.CompilerParams(dimension_semantics=("arbitrary",), vmem_limit_bytes=VMEM_LIMIT),
        name="proj",
    )(*args)


def _cmpfin_kernel(raw_ref, s64_ref, gain_ref, kc_ref, vc_ref, k_scr, v_scr, *, nsp):
    lane = lax.broadcasted_iota(jnp.int32, (nsp, LANES), 1)
    n = lax.broadcasted_iota(jnp.int32, (nsp, LANES), 0)
    k_scr[...] = raw_ref[0, :, 0:128]
    v_scr[...] = raw_ref[0, :, 128:256]
    for j in range(2):
        k = k_scr[pl.ds(j, nsp, stride=2), :]
        kn = k * lax.rsqrt(_group_meansq(k, s64_ref[...], 1.0 / HEAD_DIM) + EPS) * gain_ref[...]
        cend = n * L_SEL + (L_CMP - 1 + L_CMP * j)
        prow = _pos_rows(cend, lane)
        k0, k1 = _split_heads(kn, prow, lane)
        kc_ref[0, 0, j * nsp:(j + 1) * nsp, :] = k0.astype(BF16)
        kc_ref[1, 0, j * nsp:(j + 1) * nsp, :] = k1.astype(BF16)
        vc_ref[0, j * nsp:(j + 1) * nsp, :] = v_scr[pl.ds(j, nsp, stride=2), :].astype(BF16)


def _cmpfin(raw, s64, gain_kc):
    bx, n2, _ = raw.shape
    nsp = n2 // 2
    assert nsp % 16 == 0
    return pl.pallas_call(
        functools.partial(_cmpfin_kernel, nsp=nsp),
        grid=(bx,),
        in_specs=[pl.BlockSpec((1, n2, 256), lambda b: (b, 0, 0)),
                  pl.BlockSpec(s64.shape, lambda b: (0, 0)),
                  pl.BlockSpec(gain_kc.shape, lambda b: (0, 0))],
        out_specs=(pl.BlockSpec((NSA_KV, 1, n2, LANES), lambda b: (0, b, 0, 0)),
                   pl.BlockSpec((1, n2, LANES), lambda b: (b, 0, 0))),
        out_shape=(jax.ShapeDtypeStruct((NSA_KV, bx, n2, LANES), BF16),
                   jax.ShapeDtypeStruct((bx, n2, LANES), BF16)),
        scratch_shapes=[pltpu.VMEM((n2, LANES), F32), pltpu.VMEM((n2, LANES), F32)],
        name="cmpfin",
    )(raw, s64, gain_kc)


ACC_ROWS = HEAD_DIM + 16


def _flash_init(c, m_ref, acc_ref):
    m_ref[c] = jnp.full(m_ref.shape[1:], NEG, F32)
    acc_ref[c] = jnp.zeros(acc_ref.shape[1:], F32)


def _flash_update(c, s, vt_tile, m_ref, acc_ref):
    m_old = m_ref[c]
    m_new = jnp.maximum(m_old, jnp.max(s, axis=0, keepdims=True))
    p = jnp.exp2(s - m_new).astype(BF16)
    m_ref[c] = m_new
    vt1 = jnp.concatenate([vt_tile, jnp.ones((ACC_ROWS - HEAD_DIM, vt_tile.shape[1]), BF16)], axis=0)
    acc_ref[c] = jnp.exp2(m_old - m_new) * acc_ref[c] + _mm(vt1, p)


def _max_key_norm(k_ref_2d, n_keys):
    feat = lax.broadcasted_iota(jnp.int32, (1, LANES), 1) < HEAD_DIM

    def chunk(c, best):
        kf = k_ref_2d[pl.ds(pl.multiple_of(c * KEY_TILE, KEY_TILE), KEY_TILE), :].astype(F32)
        sq = jnp.sum(jnp.where(feat, kf * kf, 0.0), axis=1, keepdims=True)
        return jnp.maximum(best, jnp.max(sq, axis=0, keepdims=True))

    return jnp.sqrt(lax.fori_loop(0, n_keys // KEY_TILE, chunk, jnp.zeros((1, 1), F32)))


def _first_needed_tile(qa, m, k_norm, slope_row, n_max):
    feat = lax.broadcasted_iota(jnp.int32, (1, LANES), 1) < HEAD_DIM
    qf = qa.astype(F32)
    qsq = jnp.where(feat, qf * qf, 0.0)
    hi = qsq.astype(BF16)
    lo = (qsq - hi.astype(F32)).astype(BF16)
    ones = jnp.ones((8, LANES), BF16)
    q_norm = jnp.sqrt((_nt(ones, hi) + _nt(ones, lo))[0:1])
    cutoff = (m + UNDERFLOW_LOG2 - NORM_SLACK * q_norm * k_norm) / slope_row
    tile = jnp.floor(jnp.min(cutoff, axis=1, keepdims=True) * (1.0 / KEY_TILE))
    return jnp.clip(tile, 0.0, n_max.astype(F32)).astype(jnp.int32)[0, 0]


def _sweep_earlier_tiles(first, n_tiles, step):
    both = (0, 1)

    @pl.when(jnp.logical_and(n_tiles % 2 == 1, jnp.minimum(first[0], first[1]) < n_tiles))
    def _():
        step(pl.multiple_of((n_tiles - 1) * KEY_TILE, KEY_TILE), KEY_TILE, both)

    j_end = n_tiles // 2
    j0 = [jnp.minimum(f // 2, j_end) for f in first]
    j_both = jnp.maximum(j0[0], j0[1])

    def run(chains):
        def body(j, carry):
            step(pl.multiple_of(j * 2 * KEY_TILE, 2 * KEY_TILE), 2 * KEY_TILE, chains)
            return carry
        return body

    lax.fori_loop(j0[0], j_both, run((0,)), 0)
    lax.fori_loop(j0[1], j_both, run((1,)), 0)
    lax.fori_loop(j_both, j_end, run(both), 0)


def _masked_softmax_cols(s, valid):
    sm = jnp.where(valid, s, NEG)
    m = jnp.max(sm, axis=0, keepdims=True)
    p = jnp.where(valid, jnp.exp2(sm - m), 0.0)
    l = jnp.sum(p, axis=0, keepdims=True)
    return p * jnp.where(l > 0.0, 1.0 / l, 0.0)


def _nsa_kernel(q_ref, kc_ref, vc_ref, ks_ref, vs_ref, kw_ref, vw_ref, gt_ref, o_ref,
                sel_ref, acc_ref, m_ref, kn_ref, *, q_base, nsp, tw, win_base):
    i = pl.program_id(1)
    q0 = q_base + i * Q_LANES
    ncol = NSA_GROUP * Q_LANES
    groups = range(NSA_KV)
    qas = [q_ref[g].reshape(ncol, LANES) for g in groups]
    lane_c = lax.broadcasted_iota(jnp.int32, (1, ncol), 1)
    qpos_c = q0 + (lane_c & (Q_LANES - 1))
    rows_of = lambda g: slice(g * HEAD_DIM, (g + 1) * HEAD_DIM)

    n_row = lax.broadcasted_iota(jnp.int32, (nsp, ncol), 0)
    valid_e = (n_row * L_SEL + (L_CMP - 1)) <= qpos_c
    valid_o = (n_row * L_SEL + (2 * L_CMP - 1)) <= qpos_c
    blk = lax.broadcasted_iota(jnp.int32, (nsp, Q_LANES), 0)
    qpos_q = q0 + lax.broadcasted_iota(jnp.int32, (nsp, Q_LANES), 1)
    cur = qpos_q >> 6
    bonus = jnp.where((blk == 0) | (blk == cur) | (blk == cur - 1), FORCE_BONUS, 0.0)
    in_range = blk <= cur
    blk_f = blk.astype(F32)
    o_c = []
    for g in groups:
        s_c = _nt(kc_ref[g, 0], qas[g])
        sm_e = jnp.where(valid_e, s_c[0:nsp], NEG)
        sm_o = jnp.where(valid_o, s_c[nsp:2 * nsp], NEG)
        m = jnp.maximum(jnp.max(sm_e, axis=0, keepdims=True), jnp.max(sm_o, axis=0, keepdims=True))
        p_e = jnp.where(valid_e, jnp.exp2(sm_e - m), 0.0)
        p_o = jnp.where(valid_o, jnp.exp2(sm_o - m), 0.0)
        l = jnp.sum(p_e, axis=0, keepdims=True) + jnp.sum(p_o, axis=0, keepdims=True)
        inv = jnp.where(l > 0.0, 1.0 / l, 0.0)
        p_e = p_e * inv
        p_o = p_o * inv
        pcat = jnp.concatenate([p_e, p_o], axis=0).astype(BF16)
        o_c.append(_tn(vc_ref[0], pcat)[rows_of(g)])

        imp = p_e[:, 0:Q_LANES] + p_o[:, 0:Q_LANES]
        for r in range(1, NSA_GROUP):
            imp = imp + p_e[:, r * Q_LANES:(r + 1) * Q_LANES] + p_o[:, r * Q_LANES:(r + 1) * Q_LANES]
        score = jnp.where(in_range, imp + bonus, -jnp.inf)
        sel = jnp.zeros((nsp, Q_LANES), F32)
        for _ in range(N_SELECT):
            top = jnp.max(score, axis=0, keepdims=True)
            first = jnp.min(jnp.where(score == top, blk_f, 1e9), axis=0, keepdims=True)
            pick = blk_f == first
            sel = jnp.where(pick, jnp.where(top > -jnp.inf, 1.0, sel), sel)
            score = jnp.where(pick, -jnp.inf, score)
        sel_ref[g] = sel

    @pl.when(i == 0)
    def _():
        for g in groups:
            kn_ref[g] = jnp.broadcast_to(_max_key_norm(ks_ref.at[g, 0], ks_ref.shape[2]), kn_ref.shape[1:])

    for g in groups:
        _flash_init(g, m_ref, acc_ref)
    def sel_tile(k0, size, which):
        n_blk = size // L_SEL
        causal = (k0 + lax.broadcasted_iota(jnp.int32, (size, Q_LANES), 0)
                  <= q0 + lax.broadcasted_iota(jnp.int32, (size, Q_LANES), 1))
        for g in which:
            vt = vs_ref[0, 0, rows_of(g), pl.ds(k0, size)].astype(BF16)
            s = _nt(ks_ref[g, 0, pl.ds(k0, size), :], qas[g])
            sel_rows = sel_ref[g, pl.ds(pl.multiple_of(k0 // L_SEL, 8), n_blk), :]
            selt = jnp.concatenate(
                [jnp.broadcast_to(sel_rows[j:j + 1, :], (L_SEL, Q_LANES)) for j in range(n_blk)], axis=0)
            valid = jnp.where(causal, selt, 0.0) > 0.0
            s = jnp.concatenate(
                [jnp.where(valid, s[:, r * Q_LANES:(r + 1) * Q_LANES], NEG) for r in range(NSA_GROUP)], axis=1)
            _flash_update(g, s, vt, m_ref, acc_ref)

    n_before = (q0 + Q_LANES + KEY_TILE - 1) // KEY_TILE - 1
    k_own = pl.multiple_of(n_before * KEY_TILE, KEY_TILE)
    used = q0 - n_before * KEY_TILE + Q_LANES
    for size in range(Q_LANES, KEY_TILE + 1, Q_LANES):
        @pl.when(used == size)
        def _(size=size):
            sel_tile(k_own, size, groups)

    first = []
    for g in groups:
        slope = jnp.concatenate([jnp.full((1, Q_LANES), LOG2E * 2.0 ** -(g * NSA_GROUP + r + 1), F32)
                                 for r in range(NSA_GROUP)], axis=1)
        first.append(_first_needed_tile(qas[g], m_ref[g], kn_ref[g][0:1, 0:1], slope, n_before))
    _sweep_earlier_tiles(first, n_before, sel_tile)

    w0 = jnp.clip(q0 - win_base - WINDOW, 0, tw - WIN_ROWS)
    w0 = pl.multiple_of(w0, Q_LANES)
    kpos = win_base + w0 + lax.broadcasted_iota(jnp.int32, (WIN_ROWS, ncol), 0)
    dist = qpos_c - kpos
    in_window = (dist >= 0) & (dist < WINDOW)

    outs = []
    for g in groups:
        l_s = acc_ref[g, HEAD_DIM:HEAD_DIM + 1, :]
        o_s = acc_ref[g, 0:HEAD_DIM, :] * jnp.where(l_s > 0.0, 1.0 / l_s, 0.0)
        p_w = _masked_softmax_cols(_nt(kw_ref[g, 0, pl.ds(w0, WIN_ROWS), :], qas[g]), in_window)
        o_w = _mm(vw_ref[0, rows_of(g), pl.ds(w0, WIN_ROWS)].astype(BF16), p_w.astype(BF16))
        gt = gt_ref[g]
        for r in range(NSA_GROUP):
            cs = slice(r * Q_LANES, (r + 1) * Q_LANES)
            outs.append(gt[r:r + 1, :] * o_c[g][:, cs] + gt[4 + r:5 + r, :] * o_s[:, cs]
                        + gt[8 + r:9 + r, :] * o_w[:, cs])
    o_ref[...] = jnp.concatenate(outs, axis=0).T


def _nsa(qa, kc, vc, ksa, nsa_t, kwa, win_t, gt, *, layer, bx, nqb, q_base, win_base):
    nq = qa.shape[2]
    n2 = kc.shape[2]
    tk = ksa.shape[2]
    tw = kwa.shape[2]
    assert nq == bx * nqb * Q_LANES and tk % KEY_TILE == 0 and tw >= WIN_ROWS
    assert nsa_t.shape[2:] == (512, tk) and win_t.shape[1:] == (256, tw)
    kern = functools.partial(_nsa_kernel, q_base=q_base, nsp=n2 // 2, tw=tw, win_base=win_base)
    ncol = NSA_GROUP * Q_LANES
    return pl.pallas_call(
        kern, grid=(bx, nqb),
        in_specs=[
            pl.BlockSpec((NSA_KV, NSA_GROUP, Q_LANES, LANES), lambda b, i: (0, 0, b * nqb + i, 0)),
            pl.BlockSpec((NSA_KV, 1, n2, LANES), lambda b, i: (0, b, 0, 0)),
            pl.BlockSpec((1, n2, LANES), lambda b, i: (b, 0, 0)),
            pl.BlockSpec((NSA_KV, 1, tk, LANES), lambda b, i: (0, b, 0, 0)),
            pl.BlockSpec((1, 1, LANES, tk), lambda b, i: (layer, b, 3, 0)),
            pl.BlockSpec((NSA_KV, 1, tw, LANES), lambda b, i: (0, b, 0, 0)),
            pl.BlockSpec((1, LANES, tw), lambda b, i: (b, 1, 0)),
            pl.BlockSpec((NSA_KV, 16, Q_LANES), lambda b, i: (0, 0, b * nqb + i)),
        ],
        out_specs=pl.BlockSpec((Q_LANES, NSA_HEADS * HEAD_DIM), lambda b, i: (b * nqb + i, 0)),
        out_shape=jax.ShapeDtypeStruct((nq, NSA_HEADS * HEAD_DIM), F32),
        scratch_shapes=[pltpu.VMEM((NSA_KV, n2 // 2, Q_LANES), F32), pltpu.VMEM((NSA_KV, ACC_ROWS, ncol), F32),
                        pltpu.VMEM((NSA_KV, 1, ncol), F32),
                        pltpu.VMEM((NSA_KV, 8, LANES), F32)],
        compiler_params=pltpu.CompilerParams(dimension_semantics=("arbitrary",) * 2, vmem_limit_bytes=VMEM_LIMIT),
        name="nsa",
    )(qa, kc, vc, ksa, nsa_t, kwa, win_t, gt)


def _diff_kernel(q_ref, k_ref, v_ref, sl_ref, lp_ref, sg_ref, o_ref, acc_ref, m_ref, kn_ref, *,
                 q_base, tq, lam_init):
    pair = pl.program_id(1)
    i = pl.program_id(2)
    q0 = q_base + i * tq
    ncol = 2 * tq
    heads = range(2)
    lp = lp_ref[...]
    lam = (jnp.exp(jnp.sum(lp[0:1] * lp[1:2], keepdims=True)) - jnp.exp(jnp.sum(lp[2:3] * lp[3:4], keepdims=True))
           + lam_init)
    n_full = q0 // KEY_TILE
    qas = [q_ref[hh].reshape(ncol, LANES) for hh in heads]
    for hh in heads:
        _flash_init(hh, m_ref, acc_ref)

    @pl.when(i == 0)
    def _():
        for hh in heads:
            kn_ref[hh] = jnp.broadcast_to(_max_key_norm(k_ref.at[hh, 0], k_ref.shape[2]), kn_ref.shape[1:])

    def step(k0, size, which, causal=None):
        for hh in which:
            vt = v_ref[0, 0, hh * HEAD_DIM:(hh + 1) * HEAD_DIM, pl.ds(k0, size)].astype(BF16)
            s = _nt(k_ref[hh, 0, pl.ds(k0, size), :], qas[hh])
            if causal is not None:
                s = jnp.where(causal, s, NEG)
            _flash_update(hh, s, vt, m_ref, acc_ref)

    k_diag = pl.multiple_of(n_full * KEY_TILE, KEY_TILE)

    tok = k_diag + lax.broadcasted_iota(jnp.int32, (KEY_TILE, ncol), 0)
    step(k_diag, KEY_TILE, heads, tok <= q0 + lax.broadcasted_iota(jnp.int32, (KEY_TILE, ncol), 1) % tq)

    first = []
    for hh in heads:
        slope = jnp.concatenate([sl_ref[pl.ds(2 * pair + hh, 1), :]] * (ncol // LANES), axis=1)
        first.append(_first_needed_tile(qas[hh], m_ref[hh], kn_ref[hh][0:1, 0:1], slope, n_full))
    _sweep_earlier_tiles(first, n_full, step)

    outs = []
    for hh in heads:
        o = acc_ref[hh, 0:HEAD_DIM, :] / acc_ref[hh, HEAD_DIM:HEAD_DIM + 1, :]
        d = o[:, 0:tq] - lam * o[:, tq:ncol]
        d = d * lax.rsqrt(jnp.mean(d * d, axis=0, keepdims=True) + EPS) * sg_ref[:, 0:tq] * (1.0 - lam_init)
        outs.append(d)
    o_ref[...] = jnp.concatenate(outs, axis=0).T


def _diff(dqa, dka, dif_t, slopes, lp, sg, *, layer, bx, nqb, tq, q_base, lam_init):
    nq = dqa.shape[2]
    tk = dka.shape[2]
    assert nq == bx * nqb * tq and tk % KEY_TILE == 0 and KEY_TILE % tq == 0 and dif_t.shape[2:] == (1024, tk)
    kern = functools.partial(_diff_kernel, q_base=q_base, tq=tq, lam_init=lam_init)
    return pl.pallas_call(
        kern, grid=(bx, DIFF_HEADS // 2, nqb),
        in_specs=[
            pl.BlockSpec((2, 2, tq, LANES), lambda b, h, i: (h, 0, b * nqb + i, 0)),
            pl.BlockSpec((2, 1, tk, LANES), lambda b, h, i: (h, b, 0, 0)),
            pl.BlockSpec((1, 1, LANES, tk), lambda b, h, i: (layer, b, 4 + h, 0)),
            pl.BlockSpec(slopes.shape, lambda b, h, i: (0, 0)),
            pl.BlockSpec(lp.shape, lambda b, h, i: (0, 0)),
            pl.BlockSpec(sg.shape, lambda b, h, i: (0, 0)),
        ],
        out_specs=pl.BlockSpec((tq, LANES), lambda b, h, i: (b * nqb + i, h)),
        out_shape=jax.ShapeDtypeStruct((nq, DIFF_HEADS * HEAD_DIM), F32),
        scratch_shapes=[pltpu.VMEM((2, ACC_ROWS, 2 * tq), F32), pltpu.VMEM((2, 1, 2 * tq), F32),
                        pltpu.VMEM((2, 8, LANES), F32)],
        compiler_params=pltpu.CompilerParams(dimension_semantics=("arbitrary",) * 3, vmem_limit_bytes=VMEM_LIMIT),
        name="diff",
    )(dqa, dka, dif_t, slopes, lp, sg)


F_CHUNKS = 2


def _ffn_core(x, on, od, wo_ref, g2_ref, wg_ref, wu_ref, cw_ref, cb_ref, wd_ref, prev_rows, g_store):
    o = jnp.concatenate([on, od], axis=1).astype(BF16)
    xm = x + _mm(o, wo_ref[...])
    h2 = (xm * lax.rsqrt(jnp.mean(xm * xm, axis=-1, keepdims=True) + EPS) * g2_ref[...]).astype(BF16)
    d_ff = wg_ref.shape[1]
    fc = d_ff // F_CHUNKS
    y = jnp.zeros(x.shape, F32)
    for c in range(F_CHUNKS):
        c0, c1 = c * fc, (c + 1) * fc
        g = _mm(h2, wg_ref[:, c0:c1])
        u = _mm(h2, wu_ref[:, c0:c1])
        gm1, gm2 = prev_rows(g, c0, c1)
        g_store(g, c0, c1)
        gc = cb_ref[:, c0:c1] + cw_ref[0:1, c0:c1] * gm2 + cw_ref[1:2, c0:c1] * gm1 + cw_ref[2:3, c0:c1] * g
        act = gc * (1.0 / (1.0 + jnp.exp(-gc))) * u
        y = y + _mm(act.astype(BF16), wd_ref[c0:c1, :])
    return xm + y


def _ffn_prompt_kernel(x_ref, on_ref, od_ref, wo_ref, g2_ref, wg_ref, wu_ref, cw_ref, cb_ref, wd_ref,
                       y_ref, cv_ref, carry_ref, *, tiles_per_seq):
    i = pl.program_id(0)
    tr = x_ref.shape[0]

    @pl.when(i % tiles_per_seq == 0)
    def _():
        carry_ref[...] = jnp.zeros(carry_ref.shape, F32)

    def prev_rows(g, c0, c1):
        row = lax.broadcasted_iota(jnp.int32, g.shape, 0)
        p1 = carry_ref[7:8, c0:c1]
        p2 = carry_ref[6:7, c0:c1]
        gm1 = jnp.where(row == 0, p1, pltpu.roll(g, 1, 0))
        gm2 = jnp.where(row == 0, p2, jnp.where(row == 1, p1, pltpu.roll(g, 2, 0)))
        return gm1, gm2

    def g_store(g, c0, c1):
        carry_ref[:, c0:c1] = g[tr - 8:tr]
        cv_ref[0, :, c0:c1] = g[tr - 8:tr]

    y_ref[...] = _ffn_core(x_ref[...], on_ref[...], od_ref[...], wo_ref, g2_ref, wg_ref, wu_ref, cw_ref, cb_ref,
                           wd_ref, prev_rows, g_store)


def _ffn_sample_kernel(x_ref, on_ref, od_ref, st1_ref, st2_ref, wo_ref, g2_ref, wg_ref, wu_ref, cw_ref, cb_ref,
                       wd_ref, y_ref, g_ref, *, seq):
    def prev_rows(g, c0, c1):
        rs = lax.broadcasted_iota(jnp.int32, g.shape, 0) % seq
        s1 = st1_ref[:, c0:c1]
        gm1 = jnp.where(rs == 0, s1, pltpu.roll(g, 1, 0))
        gm2 = jnp.where(rs == 0, st2_ref[:, c0:c1], jnp.where(rs == 1, s1, pltpu.roll(g, 2, 0)))
        return gm1, gm2

    def g_store(g, c0, c1):
        g_ref[:, c0:c1] = g

    y_ref[...] = _ffn_core(x_ref[...], on_ref[...], od_ref[...], wo_ref, g2_ref, wg_ref, wu_ref, cw_ref, cb_ref,
                           wd_ref, prev_rows, g_store)


def _const_spec(a):
    return pl.BlockSpec(a.shape, lambda i: (0,) * a.ndim, pipeline_mode=pl.Buffered(1))


def _ffn_prompt(x2d, on, od, wo, g2, wg, wu, cw, cb, wd, *, seq_len, tr=256):
    n, d = x2d.shape
    f = wg.shape[1]
    assert n % tr == 0 and seq_len % tr == 0
    tps = seq_len // tr
    row = lambda w: pl.BlockSpec((tr, w), lambda i: (i, 0))
    return pl.pallas_call(
        functools.partial(_ffn_prompt_kernel, tiles_per_seq=tps), grid=(n // tr,),
        in_specs=[row(d), row(on.shape[1]), row(od.shape[1])] + [_const_spec(a) for a in (wo, g2, wg, wu, cw, cb, wd)],
        out_specs=(row(d), pl.BlockSpec((1, 8, f), lambda i: (i // tps, 0, 0))),
        out_shape=(jax.ShapeDtypeStruct((n, d), F32), jax.ShapeDtypeStruct((n // seq_len, 8, f), F32)),
        scratch_shapes=[pltpu.VMEM((8, f), F32)],
        compiler_params=pltpu.CompilerParams(dimension_semantics=("arbitrary",), vmem_limit_bytes=VMEM_LIMIT),
        name="ffn_prompt",
    )(x2d, on, od, wo, g2, wg, wu, cw, cb, wd)


def _ffn_sample(x2d, on, od, st1, st2, wo, g2, wg, wu, cw, cb, wd, *, seq):
    n, d = x2d.shape
    f = wg.shape[1]
    args = (x2d, on, od, st1, st2, wo, g2, wg, wu, cw, cb, wd)
    return pl.pallas_call(
        functools.partial(_ffn_sample_kernel, seq=seq), grid=(1,),
        in_specs=[_const_spec(a) for a in args],
        out_specs=(pl.BlockSpec((n, d), lambda i: (0, 0)), pl.BlockSpec((n, f), lambda i: (0, 0))),
        out_shape=(jax.ShapeDtypeStruct((n, d), F32), jax.ShapeDtypeStruct((n, f), F32)),
        compiler_params=pltpu.CompilerParams(dimension_semantics=("arbitrary",), vmem_limit_bytes=VMEM_LIMIT),
        name="ffn_sample",
    )(*args)


PAGES_PER_STEP = 8
STEPS_PER_CMP_TILE = LANES // (PAGES_PER_STEP * LANES // L_CMP)
T_NSA, T_WIN, T_DIF, T_ROWS = 0, 512, 768, 1792


def _softmax_rows(s):
    m = jnp.max(s, axis=-1, keepdims=True)
    p = jnp.exp2(s - m)
    return p * (1.0 / jnp.sum(p, axis=-1, keepdims=True))


def _masked_softmax_rows(s, valid):
    sm = jnp.where(valid, s, NEG)
    m = jnp.max(sm, axis=-1, keepdims=True)
    p = jnp.where(valid, jnp.exp2(sm - m), 0.0)
    l = jnp.sum(p, axis=-1, keepdims=True)
    return p * jnp.where(l > 0.0, 1.0 / l, 0.0)


def _sattn_kernel(pt_ref, *refs, n_steps, past_len, dec, lam_init):
    pg = PAGES_PER_STEP
    nsa_pg, dif_pg = refs[0:pg], refs[pg:2 * pg]
    (win_ref, nn_ref, nw_ref, nd_ref, qn_ref, qd_ref, gate_ref, sn_ref, sd_ref, wck_ref, wcv_ref, gkc_ref,
     lp_ref, sg_ref, o_ref, ns_ref,
     ssel, sdif, psel, pdif, vsel, vdif, kc_scr, tail_scr, oc_scr, ow_scr) = refs[2 * pg:]
    s = pl.program_id(1)
    span = pg * LANES
    groups, heads = range(NSA_KV), range(DIFF_HEADS)
    cat = lambda parts: jnp.concatenate(parts, axis=1)
    hi_lo = lambda x: (x.astype(BF16), (x - x.astype(BF16).astype(F32)).astype(BF16))

    @pl.when(s < n_steps)
    def _():
        @pl.when(s == 0)
        def _():
            kc_scr[...] = jnp.zeros(kc_scr.shape, F32)

        shift = (s % STEPS_PER_CMP_TILE) * (span // L_CMP)
        col = pl.multiple_of((s // STEPS_PER_CMP_TILE) * LANES, LANES)
        for part, w_ref in ((0, wck_ref), (1, wcv_ref)):
            x_hi, x_lo = hi_lo(cat([r[0, 0, part * 128:(part + 1) * 128, :] for r in nsa_pg]))
            blk = _mm(x_hi, w_ref[0]) + _mm(x_lo, w_ref[0]) + _mm(x_hi, w_ref[1])
            kc_scr[part * 128:(part + 1) * 128, pl.ds(col, LANES)] += pltpu.roll(blk, shift, 1)

        k0 = pl.multiple_of(s * span, span)
        kpos = (k0 + lax.broadcasted_iota(jnp.int32, (1, span), 1)).astype(F32)
        for g in groups:
            k8 = cat([r[0, 0, 256 + g * HEAD_DIM:256 + (g + 1) * HEAD_DIM, :] for r in nsa_pg]).astype(BF16)
            ssel[g, :, pl.ds(k0, span)] = _mm(qn_ref[0, g], k8) + cat([sn_ref[g]] * pg) * kpos
        for h in heads:
            k8 = cat([r[0, 0, h * HEAD_DIM:(h + 1) * HEAD_DIM, :] for r in dif_pg]).astype(BF16)
            sdif[h, :, pl.ds(k0, span)] = _mm(qd_ref[0, h], k8) + cat([sd_ref[h]] * pg) * kpos
        vsel[:, pl.ds(k0, span)] = cat([r[0, 0, 384:512, :] for r in nsa_pg]).astype(BF16)
        vdif[:, pl.ds(k0, span)] = cat([r[0, 0, 512:1024, :] for r in dif_pg]).astype(BF16)

    @pl.when(s == n_steps)
    def _():
        new_rows = cat([nn_ref[...], nw_ref[...], nd_ref[...]])
        tail_scr[...] = jnp.concatenate([new_rows, jnp.zeros((LANES - dec, T_ROWS), F32)], axis=0).T
        nrow, drow = NSA_GROUP * dec, 2 * dec
        tail_pos = (past_len + lax.broadcasted_iota(jnp.int32, (1, LANES), 1)).astype(F32)
        causal32 = (lax.broadcasted_iota(jnp.int32, (nrow, LANES), 1)
                    <= lax.broadcasted_iota(jnp.int32, (nrow, LANES), 0) % dec)
        causal16 = (lax.broadcasted_iota(jnp.int32, (drow, LANES), 1)
                    <= lax.broadcasted_iota(jnp.int32, (drow, LANES), 0) % dec)

        for g in groups:
            kt = tail_scr[T_NSA + 256 + g * HEAD_DIM:T_NSA + 256 + (g + 1) * HEAD_DIM, :].astype(BF16)
            sc = _mm(qn_ref[0, g], kt) + sn_ref[g] * tail_pos
            ssel[g, :, past_len:past_len + LANES] = jnp.where(causal32, sc, NEG)
        for h in heads:
            kt = tail_scr[T_DIF + h * HEAD_DIM:T_DIF + (h + 1) * HEAD_DIM, :].astype(BF16)
            sc = _mm(qd_ref[0, h], kt) + sd_ref[h] * tail_pos
            sdif[h, :, past_len:past_len + LANES] = jnp.where(causal16, sc, NEG)

        n_cmp = kc_scr.shape[1]
        cend = lax.broadcasted_iota(jnp.int32, (nrow, n_cmp), 1) * L_CMP + (L_CMP - 1)
        qpos_c = past_len + lax.broadcasted_iota(jnp.int32, (nrow, n_cmp), 0) % dec
        imps = []
        for g in groups:
            kc = kc_scr[g * HEAD_DIM:(g + 1) * HEAD_DIM, :]
            kc = kc * lax.rsqrt(jnp.mean(kc * kc, axis=0, keepdims=True) + EPS) * cat([gkc_ref[...]] * (n_cmp // LANES))
            s_c = _mm(qn_ref[0, g], kc.astype(BF16)) + cat([sn_ref[g]] * (n_cmp // LANES)) * cend.astype(F32)
            p_c = _masked_softmax_rows(s_c, cend <= qpos_c)
            vc = kc_scr[128 + g * HEAD_DIM:128 + (g + 1) * HEAD_DIM, :].astype(BF16)
            oc_scr[g] = _nt(p_c.astype(BF16), vc)
            imps.append(sum(p_c[r * dec:(r + 1) * dec] for r in range(NSA_GROUP)))
        imp2 = jnp.concatenate(imps, axis=0)
        pair = (lax.broadcasted_iota(jnp.int32, (n_cmp, LANES), 0) // 2
                == lax.broadcasted_iota(jnp.int32, (n_cmp, LANES), 1)).astype(BF16)
        i_hi, i_lo = hi_lo(imp2)
        imp = _mm(i_hi, pair) + _mm(i_lo, pair)

        n_blk = past_len // L_SEL
        blk = lax.broadcasted_iota(jnp.int32, imp.shape, 1)
        blk_f = blk.astype(F32)
        bonus = jnp.where((blk == 0) | (blk == n_blk - 1), FORCE_BONUS, 0.0)
        score = jnp.where(blk < n_blk, imp + bonus, -jnp.inf)
        sel = jnp.zeros(imp.shape, F32)
        for _ in range(N_SELECT - 1):
            top = jnp.max(score, axis=1, keepdims=True)
            first = jnp.min(jnp.where(score == top, blk_f, 1e9), axis=1, keepdims=True)
            pick = blk_f == first
            sel = jnp.where(pick, jnp.where(top > -jnp.inf, 1.0, sel), sel)
            score = jnp.where(pick, -jnp.inf, score)
        sel = sel.astype(BF16)

        for c in range(n_steps):
            tok = c * span + lax.broadcasted_iota(jnp.int32, (LANES, span), 1)
            expand = ((tok >> 6) == lax.broadcasted_iota(jnp.int32, (LANES, span), 0)).astype(BF16)
            chosen = _mm(sel, expand)
            for g in groups:
                keep = jnp.concatenate([chosen[g * dec:(g + 1) * dec]] * NSA_GROUP, axis=0) > 0.5
                ssel[g, :, c * span:(c + 1) * span] = jnp.where(keep, ssel[g, :, c * span:(c + 1) * span], NEG)
        for g in groups:
            psel[g] = _softmax_rows(ssel[g]).astype(BF16)
        for h in heads:
            pdif[h] = _softmax_rows(sdif[h]).astype(BF16)

        w_buf = win_ref.shape[3]
        kwpos = past_len - w_buf + lax.broadcasted_iota(jnp.int32, (nrow, w_buf + LANES), 1)
        dist = past_len + lax.broadcasted_iota(jnp.int32, (nrow, w_buf + LANES), 0) % dec - kwpos
        in_window = (dist >= 0) & (dist < WINDOW)
        for g in groups:
            kw = cat([win_ref[0, 0, g * HEAD_DIM:(g + 1) * HEAD_DIM, :],
                      tail_scr[T_WIN + g * HEAD_DIM:T_WIN + (g + 1) * HEAD_DIM, :]]).astype(BF16)
            vw = cat([win_ref[0, 0, 128 + g * HEAD_DIM:128 + (g + 1) * HEAD_DIM, :],
                      tail_scr[T_WIN + 128 + g * HEAD_DIM:T_WIN + 128 + (g + 1) * HEAD_DIM, :]]).astype(BF16)
            s_w = _mm(qn_ref[0, g], kw) + cat([sn_ref[g]] * (w_buf // LANES + 1)) * kwpos.astype(F32)
            ow_scr[g] = _nt(_masked_softmax_rows(s_w, in_window).astype(BF16), vw)

        shifted = pltpu.roll(win_ref[0, 0], w_buf - dec, 1)
        fresh = cat([jnp.zeros((256, w_buf - LANES), F32), pltpu.roll(tail_scr[T_WIN:T_WIN + 256, :], LANES - dec, 1)])
        ns_ref[0] = jnp.where(lax.broadcasted_iota(jnp.int32, (256, w_buf), 1) >= w_buf - dec, fresh, shifted)

        lp = lp_ref[...]
        lam = (jnp.exp(jnp.sum(lp[0:1] * lp[1:2], keepdims=True))
               - jnp.exp(jnp.sum(lp[2:3] * lp[3:4], keepdims=True)) + lam_init)
        pieces = []
        for g in groups:
            vt = tail_scr[T_NSA + 384 + g * HEAD_DIM:T_NSA + 384 + (g + 1) * HEAD_DIM, :].astype(BF16)
            o_s = (_nt(psel[g, :, 0:past_len], vsel[g * HEAD_DIM:(g + 1) * HEAD_DIM, :])
                   + _nt(psel[g, :, past_len:past_len + LANES], vt))
            o = gate_ref[0, g, 0] * oc_scr[g] + gate_ref[0, g, 1] * o_s + gate_ref[0, g, 2] * ow_scr[g]
            pieces += [o[r * dec:(r + 1) * dec] for r in range(NSA_GROUP)]
        for h in heads:
            vt = tail_scr[T_DIF + 512 + h * HEAD_DIM:T_DIF + 512 + (h + 1) * HEAD_DIM, :].astype(BF16)
            o = (_nt(pdif[h, :, 0:past_len], vdif[h * HEAD_DIM:(h + 1) * HEAD_DIM, :])
                 + _nt(pdif[h, :, past_len:past_len + LANES], vt))
            d = o[0:dec] - lam * o[dec:2 * dec]
            pieces.append(d * lax.rsqrt(jnp.mean(d * d, axis=-1, keepdims=True) + EPS) * sg_ref[...] * (1.0 - lam_init))
        o_ref[0] = cat(pieces)


def _sattn(page_table, cache_nsa, cache_diff, state_win, new_nsa, new_win, new_dif, qn, qd, gates, sn, sd,
           wck, wcv, gkc, lp, sg, *, layer, lam_init):
    bs, n_pages = page_table.shape
    page = cache_nsa.shape[3]
    dec = new_nsa.shape[0] // bs
    past_len = n_pages * page
    pg = PAGES_PER_STEP
    n_steps = n_pages // pg
    w_buf = state_win.shape[3]
    assert page == LANES and n_pages % pg == 0 and dec == 8 and w_buf % LANES == 0
    n_cmp = -(-n_steps // STEPS_PER_CMP_TILE) * LANES
    tks = past_len + LANES
    pt = page_table.reshape(-1)

    def kpage(j):
        return lambda b, s, pt_ref: (layer, pt_ref[b * n_pages + jnp.minimum(s, n_steps - 1) * pg + j], 0, 0)

    const = lambda a: pl.BlockSpec(a.shape, lambda b, s, pt_ref: (0,) * a.ndim)
    per_b = lambda a: pl.BlockSpec((1,) + a.shape[1:], lambda b, s, pt_ref: (b,) + (0,) * (a.ndim - 1))
    in_specs = (
        [pl.BlockSpec((1, 1, 512, page), kpage(j)) for j in range(pg)]
        + [pl.BlockSpec((1, 1, 1024, page), kpage(j)) for j in range(pg)]
        + [pl.BlockSpec((1, 1, 256, w_buf), lambda b, s, pt_ref: (layer, b, 0, 0)),
           pl.BlockSpec((dec, 512), lambda b, s, pt_ref: (b, 0)),
           pl.BlockSpec((dec, 256), lambda b, s, pt_ref: (b, 0)),
           pl.BlockSpec((dec, 1024), lambda b, s, pt_ref: (b, 0)),
           per_b(qn), per_b(qd), per_b(gates), const(sn), const(sd), const(wck), const(wcv), const(gkc),
           const(lp), const(sg)])
    out_specs = (pl.BlockSpec((1, dec, 1024), lambda b, s, pt_ref: (b, 0, 0)),
                 pl.BlockSpec((1, 256, w_buf), lambda b, s, pt_ref: (b, 0, 0)))
    scratch = [
        pltpu.VMEM((NSA_KV, NSA_GROUP * dec, tks), F32), pltpu.VMEM((DIFF_HEADS, 2 * dec, tks), F32),
        pltpu.VMEM((NSA_KV, NSA_GROUP * dec, tks), BF16), pltpu.VMEM((DIFF_HEADS, 2 * dec, tks), BF16),
        pltpu.VMEM((NSA_KV * HEAD_DIM, past_len), BF16), pltpu.VMEM((DIFF_HEADS * HEAD_DIM, past_len), BF16),
        pltpu.VMEM((256, n_cmp), F32), pltpu.VMEM((T_ROWS, LANES), F32),
        pltpu.VMEM((NSA_KV, NSA_GROUP * dec, HEAD_DIM), F32), pltpu.VMEM((NSA_KV, NSA_GROUP * dec, HEAD_DIM), F32),
    ]
    grid_spec = pltpu.PrefetchScalarGridSpec(num_scalar_prefetch=1, grid=(bs, n_steps + 1), in_specs=in_specs,
                                             out_specs=out_specs, scratch_shapes=scratch)
    caches = [cache_nsa] * pg + [cache_diff] * pg
    return pl.pallas_call(
        functools.partial(_sattn_kernel, n_steps=n_steps, past_len=past_len, dec=dec, lam_init=lam_init),
        grid_spec=grid_spec,
        out_shape=(jax.ShapeDtypeStruct((bs, dec, 1024), F32), jax.ShapeDtypeStruct((bs, 256, w_buf), F32)),
        compiler_params=pltpu.CompilerParams(dimension_semantics=("arbitrary",) * 2, vmem_limit_bytes=VMEM_LIMIT),
        name="sattn",
    )(pt, *caches, state_win, new_nsa, new_win, new_dif, qn, qd, gates, sn, sd, wck, wcv, gkc, lp, sg)


def _alibi_slopes(n):
    return 2.0 ** (-8.0 * jnp.arange(1, n + 1, dtype=F32) / n)


def _slope_rows(slopes):
    slopes = slopes * LOG2E
    s0 = slopes.astype(BF16).astype(F32)
    s1 = (slopes - s0).astype(BF16).astype(F32)
    s2 = (slopes - s0 - s1).astype(BF16).astype(F32)
    cols = jnp.stack([64.0 * s0, s0, 64.0 * s1, s1, 64.0 * s2, s2], axis=1)
    out = jnp.zeros((slopes.shape[0], LANES), F32)
    return out.at[:, AUG0:AUG0 + N_AUG].set(cols)


def _block_ones(group):
    idx = np.arange(LANES) // group
    return jnp.asarray((idx[:, None] == idx[None, :]).astype(np.float32), dtype=BF16)


_GATE_SRC = np.full((LANES,), -1, np.int64)
for _g in range(NSA_KV):
    for _r in range(NSA_GROUP):
        for _j in range(3):
            _GATE_SRC[_g * 16 + _j * 4 + _r] = 1280 + _g * 12 + _r * 3 + _j


def _permute_w_in(w):
    main = jnp.concatenate([w[:, 0:1280], w[:, 1304:2840]], axis=1)
    gate = jnp.where(jnp.asarray(_GATE_SRC >= 0)[None, :], w[:, np.maximum(_GATE_SRC, 0)], 0.0)
    return jnp.concatenate([main, gate], axis=1).astype(BF16)


def _compress_weights(w):
    t = np.arange(PAGES_PER_STEP * LANES)
    place = jnp.asarray(t[:, None] // L_CMP == np.arange(LANES)[None, :])
    full = jnp.where(place, jnp.tile(w, PAGES_PER_STEP * LANES // L_CMP)[:, None], 0.0)
    hi = full.astype(BF16)
    return jnp.stack([hi, (full - hi.astype(F32)).astype(BF16)])


def _gain_row(nsa_qg, nsa_kg, diff_qg, diff_kg):
    one = lambda n: jnp.ones((n,), F32)
    parts = [jnp.tile(nsa_qg, NSA_HEADS), one(256), jnp.tile(nsa_kg[1], NSA_KV), one(128),
             jnp.tile(nsa_kg[2], NSA_KV), one(128), jnp.tile(diff_qg.reshape(-1), DIFF_HEADS),
             jnp.tile(diff_kg.reshape(-1), DIFF_HEADS), one(512 + LANES)]
    return jnp.concatenate(parts)[None, :]


def kernel(x_prompt, x_sample, cache_nsa, cache_diff, state_win, state_conv, page_table, norm1_g, norm2_g, w_in, w_out, nsa_qnorm_g, nsa_knorm_g, nsa_cmp_w, diff_qnorm_g, diff_knorm_g, diff_lambda, diff_subnorm_g, w_gate, w_up, conv_w, conv_b, w_down):
    batch, seq, d_model = x_prompt.shape
    dec_batch, dec_seq, _ = x_sample.shape
    depth, n_pool, page = cache_nsa.shape[:3]
    n_pages = page_table.shape[1]
    past_len = n_pages * page
    w_buf = state_win.shape[2]
    d_ff = w_gate.shape[2]
    assert dec_seq < L_CMP and seq % KEY_TILE == 0 and w_buf == WINDOW

    s64, s32 = _block_ones(HEAD_DIM), _block_ones(DIFF_HALF)
    qcn = _slope_rows(_alibi_slopes(NSA_HEADS))
    qcd = _slope_rows(_alibi_slopes(DIFF_HEADS))
    fmaj = lambda a: jnp.moveaxis(a, 2, -1).reshape(a.shape[0], a.shape[1], -1, a.shape[2])
    cache_nsa4 = fmaj(cache_nsa)
    cache_diff4 = fmaj(cache_diff)
    state_win4 = fmaj(state_win)

    sn = jnp.broadcast_to((_alibi_slopes(NSA_HEADS) * LOG2E).reshape(NSA_KV, NSA_GROUP, 1, 1),
                          (NSA_KV, NSA_GROUP, dec_seq, LANES)).reshape(NSA_KV, NSA_GROUP * dec_seq, LANES)
    sd = jnp.broadcast_to((_alibi_slopes(DIFF_HEADS) * LOG2E).reshape(DIFF_HEADS, 1, 1), (DIFF_HEADS, 2 * dec_seq, LANES))
    xp = x_prompt.reshape(batch * seq, d_model)
    xs = x_sample.reshape(dec_batch * dec_seq, d_model)

    nsa_t = jnp.zeros((depth, batch, 512, seq), F32)
    dif_t = jnp.zeros((depth, batch, 1024, seq), F32)
    outs = [[] for _ in range(8)]
    for l in range(depth):
        lam_init = 0.8 - 0.6 * math.exp(-0.3 * l)
        w_in_p = _permute_w_in(w_in[l])
        gain = _gain_row(nsa_qnorm_g[l], nsa_knorm_g[l], diff_qnorm_g[l], diff_knorm_g[l])
        gain_kc = jnp.tile(nsa_knorm_g[l, 0], NSA_KV)[None, :]
        cw = jnp.concatenate([jnp.broadcast_to(nsa_cmp_w[l, 0][:, None], (L_CMP, LANES)),
                              jnp.broadcast_to(nsa_cmp_w[l, 1][:, None], (L_CMP, LANES))], axis=1)
        lp = jnp.zeros((8, LANES), F32).at[0:4, 0:DIFF_HALF].set(diff_lambda[l])
        sg = jnp.broadcast_to(diff_subnorm_g[l][:, None], (HEAD_DIM, DIFF_TQ))
        sg_row = jnp.broadcast_to(diff_subnorm_g[l][None, :], (dec_seq, HEAD_DIM))
        gkc = jnp.broadcast_to(nsa_knorm_g[l, 0][:, None], (HEAD_DIM, LANES))
        wck, wcv = _compress_weights(nsa_cmp_w[l, 0]), _compress_weights(nsa_cmp_w[l, 1])
        g1 = norm1_g[l][None, :]
        g2 = norm2_g[l][None, :]
        wo, wg, wu, wd = (a.astype(BF16) for a in (w_out[l], w_gate[l], w_up[l], w_down[l]))
        cwf, cbf = conv_w[l], conv_b[l][None, :]
        proj = functools.partial(_proj, g1=g1, w=w_in_p, gain=gain, s64=s64, s32=s32, qcn=qcn, qcd=qcd, cw=cw)

        (nsa_t, win_t, dif_t, qa, ksa, kwa, gt, dqa, dka, craw) = proj(
            xp, seq_len=seq, pos_base=0, stacked=(l, depth, (nsa_t, dif_t)))
        kc, vc = _cmpfin(craw.reshape(batch, seq // L_CMP, 256), s64, gain_kc)
        o_nsa = _nsa(qa, kc, vc, ksa.reshape(NSA_KV, batch, seq, LANES), nsa_t,
                     kwa.reshape(NSA_KV, batch, seq, LANES), win_t, gt,
                     layer=l, bx=batch, nqb=seq // Q_LANES, q_base=0, win_base=0)
        o_dif = _diff(dqa, dka.reshape(DIFF_HEADS, batch, seq, LANES), dif_t, sd[:, 0], lp, sg,
                      layer=l, bx=batch, nqb=seq // DIFF_TQ, tq=DIFF_TQ, q_base=0, lam_init=lam_init)
        xp, cv = _ffn_prompt(xp, o_nsa, o_dif, wo, g2, wg, wu, cwf, cbf, wd, seq_len=seq)
        w_keep = min(WINDOW, seq)
        outs[4].append(jnp.moveaxis(win_t[:, :, seq - w_keep:].reshape(batch, 2, NSA_KV, HEAD_DIM, w_keep), -1, 1))
        outs[6].append(cv[:, 8 - 2:8])

        (nsa_rows, win_rows, dif_rows, qa, _, _, gt, dqa, _, _) = proj(xs, seq_len=dec_seq, pos_base=past_len)
        per_seq = lambda a, lead: a[..., :HEAD_DIM].reshape(lead + (dec_batch, dec_seq, HEAD_DIM))
        qn = jnp.transpose(per_seq(qa, (NSA_KV, NSA_GROUP)), (2, 0, 1, 3, 4)).reshape(
            dec_batch, NSA_KV, NSA_GROUP * dec_seq, HEAD_DIM)
        qd = jnp.transpose(per_seq(dqa, (DIFF_HEADS, 2)), (2, 0, 1, 3, 4)).reshape(
            dec_batch, DIFF_HEADS, 2 * dec_seq, HEAD_DIM)
        gates = jnp.transpose(gt.reshape(NSA_KV, 4, NSA_GROUP, dec_batch, dec_seq), (3, 0, 1, 2, 4)).reshape(
            dec_batch, NSA_KV, 4, NSA_GROUP * dec_seq, 1)
        gates = jnp.broadcast_to(gates, gates.shape[:-1] + (HEAD_DIM,))
        o_s, new_state = _sattn(page_table, cache_nsa4, cache_diff4, state_win4, nsa_rows, win_rows, dif_rows,
                                qn, qd, gates, sn, sd, wck, wcv, gkc, lp, sg_row, layer=l, lam_init=lam_init)
        o_s = o_s.reshape(dec_batch * dec_seq, 2 * NSA_HEADS * HEAD_DIM)
        st1 = jnp.repeat(state_conv[l][:, 1], dec_seq, axis=0)
        st2 = jnp.repeat(state_conv[l][:, 0], dec_seq, axis=0)
        xs, g_s = _ffn_sample(xs, o_s[:, :NSA_HEADS * HEAD_DIM], o_s[:, NSA_HEADS * HEAD_DIM:], st1, st2,
                              wo, g2, wg, wu, cwf, cbf, wd, seq=dec_seq)
        outs[1].append(nsa_rows.reshape(dec_batch, dec_seq, 4, NSA_KV, HEAD_DIM))
        outs[3].append(dif_rows.reshape(dec_batch, dec_seq, 2, DIFF_HEADS, HEAD_DIM))
        outs[5].append(jnp.moveaxis(new_state.reshape(dec_batch, 2, NSA_KV, HEAD_DIM, w_buf), -1, 1))
        outs[7].append(g_s.reshape(dec_batch, dec_seq, d_ff)[:, dec_seq - 2:])

    outs[0] = jnp.moveaxis(nsa_t.reshape(depth, batch, 4, NSA_KV, HEAD_DIM, seq), -1, 2)
    outs[2] = jnp.moveaxis(dif_t.reshape(depth, batch, 2, DIFF_HEADS, HEAD_DIM, seq), -1, 2)
    stacked = [o if not isinstance(o, list) else jnp.stack(o) for o in outs]
    return (xp.reshape(batch, seq, d_model), xs.reshape(dec_batch, dec_seq, d_model), *stacked)
```

```python
import functools
import math

import jax
import jax.numpy as jnp
import numpy as np
from jax import lax
from jax.experimental import pallas as pl
from jax.experimental.pallas import tpu as pltpu

F32 = jnp.float32
BF16 = jnp.bfloat16

HEAD_DIM = 64
NSA_KV = 2
NSA_GROUP = 4
NSA_HEADS = NSA_KV * NSA_GROUP
DIFF_HEADS = 8
DIFF_HALF = HEAD_DIM // 2
L_CMP = 32
L_SEL = 64
N_SELECT = 16
WINDOW = 512
FORCE_BONUS = 1.0e4
EPS = 1e-6
NEG = -1e30
UNDERFLOW_LOG2 = -160.0
NORM_SLACK = 1.02
LOG2E = math.log2(math.e)

LANES = 128
KEY_TILE = 512
Q_LANES = 128
DIFF_TQ = 512
WIN_ROWS = WINDOW + Q_LANES
AUG0 = HEAD_DIM
N_AUG = 6
VMEM_LIMIT = 56 * 1024 * 1024

C_Q, C_NSA, C_WIN, C_DQ, C_DK, C_DV, C_GATE = 0, 512, 1024, 1280, 1792, 2304, 2816
N_COL = 2944


def _nt(a, b):
    return lax.dot_general(a, b, (((1,), (1,)), ((), ())), preferred_element_type=F32)


def _tn(a, b):
    return lax.dot_general(a, b, (((0,), (0,)), ((), ())), preferred_element_type=F32)


def _mm(a, b):
    return jnp.dot(a, b, preferred_element_type=F32)


def _group_meansq(z, smat, inv_n):
    zz = z * z
    hi = zz.astype(BF16)
    lo = (zz - hi.astype(F32)).astype(BF16)
    return (_mm(hi, smat) + _mm(lo, smat)) * inv_n


def _pos_rows(pos, lane):
    is_aug = (lane >= AUG0) & (lane < AUG0 + N_AUG)
    val = jnp.where((lane & 1) == 0, pos >> 6, pos & 63)
    return jnp.where(is_aug, val, 0).astype(F32)


def _split_heads(slab, fill, lane):
    even = jnp.where(lane < HEAD_DIM, slab, fill)
    odd = jnp.where(lane < HEAD_DIM, pltpu.roll(slab, HEAD_DIM, 1), fill)
    return even, odd


def _proj_kernel(x_ref, g1_ref, w_ref, gain_ref, s64_ref, s32_ref, qcn_ref, qcd_ref, cw_ref, *rest,
                 seq_len, pos_base, fmajor):
    nsa_ref, win_ref, dif_ref, qa_ref, ksa_ref, kwa_ref, gt_ref, dqa_ref, dka_ref, cr_ref = rest[-10:]
    tr = x_ref.shape[0]

    def put(ref, c0, slab):
        if fmajor:
            ref[(0,) * (len(ref.shape) - 2) + (slice(c0, c0 + LANES), slice(None))] = slab.T
        else:
            ref[:, c0:c0 + LANES] = slab
    i = pl.program_id(0)
    x = x_ref[...]
    h = x * lax.rsqrt(jnp.mean(x * x, axis=-1, keepdims=True) + EPS) * g1_ref[...]
    hb = h.astype(BF16)

    lane = lax.broadcasted_iota(jnp.int32, (tr, LANES), 1)
    row = lax.broadcasted_iota(jnp.int32, (tr, LANES), 0)
    pos = pos_base + (i * tr + row) % seq_len
    prow = _pos_rows(pos, lane)
    s64 = s64_ref[...]
    s32 = s32_ref[...]

    def seg(c0, width):
        return _mm(hb, w_ref[:, c0:c0 + width])

    def normed(z, c0, smat, inv_n):
        return z * lax.rsqrt(_group_meansq(z, smat, inv_n) + EPS) * gain_ref[:, c0:c0 + LANES]

    zq = seg(C_Q, 512)
    for s in range(4):
        zn = normed(zq[:, s * LANES:(s + 1) * LANES], C_Q + s * LANES, s64, 1.0 / HEAD_DIM) * (HEAD_DIM ** -0.5 * LOG2E)
        for par in range(2):
            hd = 2 * s + par
            src = zn if par == 0 else pltpu.roll(zn, HEAD_DIM, 1)
            qa = jnp.where(lane < HEAD_DIM, src, qcn_ref[hd:hd + 1, :])
            qa_ref[hd // NSA_GROUP, hd % NSA_GROUP] = qa.astype(BF16)

    zc = seg(C_NSA, 512)
    ks = normed(zc[:, 256:384], C_NSA + 256, s64, 1.0 / HEAD_DIM)
    put(nsa_ref, 0, zc[:, 0:128])
    put(nsa_ref, 128, zc[:, 128:256])
    put(nsa_ref, 256, ks)
    put(nsa_ref, 384, zc[:, 384:512])
    k0, k1 = _split_heads(ks, prow, lane)
    ksa_ref[0] = k0.astype(BF16)
    ksa_ref[1] = k1.astype(BF16)
    craw = zc[:, 0:256].reshape(tr // L_CMP, L_CMP, 256) * cw_ref[...][None]
    cr_ref[...] = jnp.sum(craw, axis=1)

    zw = seg(C_WIN, 256)
    kw = normed(zw[:, 0:128], C_WIN, s64, 1.0 / HEAD_DIM)
    put(win_ref, 0, kw)
    put(win_ref, 128, zw[:, 128:256])
    k0, k1 = _split_heads(kw, prow, lane)
    kwa_ref[0] = k0.astype(BF16)
    kwa_ref[1] = k1.astype(BF16)

    zdq = seg(C_DQ, 512)
    for s in range(4):
        zn = normed(zdq[:, s * LANES:(s + 1) * LANES], C_DQ + s * LANES, s32, 1.0 / DIFF_HALF) * (DIFF_HALF ** -0.5 * LOG2E)
        for par in range(2):
            hd = 2 * s + par
            src = zn if par == 0 else pltpu.roll(zn, HEAD_DIM, 1)
            fill = qcd_ref[hd:hd + 1, :]
            dqa_ref[hd, 0] = jnp.where(lane < DIFF_HALF, src, fill).astype(BF16)
            dqa_ref[hd, 1] = jnp.where((lane >= DIFF_HALF) & (lane < HEAD_DIM), src, fill).astype(BF16)

    zdk = seg(C_DK, 512)
    for s in range(4):
        kn = normed(zdk[:, s * LANES:(s + 1) * LANES], C_DK + s * LANES, s32, 1.0 / DIFF_HALF)
        put(dif_ref, s * LANES, kn)
        k0, k1 = _split_heads(kn, prow, lane)
        dka_ref[2 * s] = k0.astype(BF16)
        dka_ref[2 * s + 1] = k1.astype(BF16)
    zdv = seg(C_DV, 512)
    for s in range(4):
        put(dif_ref, 512 + s * LANES, zdv[:, s * LANES:(s + 1) * LANES])

    zg = seg(C_GATE, LANES)
    sg = 1.0 / (1.0 + jnp.exp(-zg))
    gt_ref[...] = sg.T[0:2 * 16].reshape(NSA_KV, 16, tr)


def _proj(x2d, g1, w, gain, s64, s32, qcn, qcd, cw, *, seq_len, pos_base, stacked=None, tr=512):
    n = x2d.shape[0]
    tr = min(tr, n)
    assert n % tr == 0 and tr % L_CMP == 0
    fmajor = stacked is not None
    kern = functools.partial(_proj_kernel, seq_len=seq_len, pos_base=pos_base, fmajor=fmajor)
    full = lambda shape: pl.BlockSpec(shape, lambda i: (0,) * len(shape))
    if fmajor:
        layer, depth, prev = stacked
        assert seq_len % tr == 0
        tps, nb = seq_len // tr, n // seq_len
        cache_shapes = [jax.ShapeDtypeStruct((depth, nb, 512, seq_len), F32),
                        jax.ShapeDtypeStruct((nb, 256, seq_len), F32),
                        jax.ShapeDtypeStruct((depth, nb, 1024, seq_len), F32)]
        cache_specs = [pl.BlockSpec((1, 1, 512, tr), lambda i: (layer, i // tps, 0, i % tps)),
                       pl.BlockSpec((1, 256, tr), lambda i: (i // tps, 0, i % tps)),
                       pl.BlockSpec((1, 1, 1024, tr), lambda i: (layer, i // tps, 0, i % tps))]
    else:
        prev = None
        cache_shapes = [jax.ShapeDtypeStruct((n, w_), F32) for w_ in (512, 256, 1024)]
        cache_specs = [pl.BlockSpec((tr, w_), lambda i: (i, 0)) for w_ in (512, 256, 1024)]
    out_shape = tuple(cache_shapes) + (
        jax.ShapeDtypeStruct((NSA_KV, NSA_GROUP, n, LANES), BF16),
        jax.ShapeDtypeStruct((NSA_KV, n, LANES), BF16),
        jax.ShapeDtypeStruct((NSA_KV, n, LANES), BF16),
        jax.ShapeDtypeStruct((NSA_KV, 16, n), F32),
        jax.ShapeDtypeStruct((DIFF_HEADS, 2, n, LANES), BF16),
        jax.ShapeDtypeStruct((DIFF_HEADS, n, LANES), BF16),
        jax.ShapeDtypeStruct((n // L_CMP, 256), F32),
    )
    out_specs = tuple(cache_specs) + (
        pl.BlockSpec((NSA_KV, NSA_GROUP, tr, LANES), lambda i: (0, 0, i, 0)),
        pl.BlockSpec((NSA_KV, tr, LANES), lambda i: (0, i, 0)),
        pl.BlockSpec((NSA_KV, tr, LANES), lambda i: (0, i, 0)),
        pl.BlockSpec((NSA_KV, 16, tr), lambda i: (0, 0, i)),
        pl.BlockSpec((DIFF_HEADS, 2, tr, LANES), lambda i: (0, 0, i, 0)),
        pl.BlockSpec((DIFF_HEADS, tr, LANES), lambda i: (0, i, 0)),
        pl.BlockSpec((tr // L_CMP, 256), lambda i: (i, 0)),
    )
    in_specs = [
        pl.BlockSpec((tr, x2d.shape[1]), lambda i: (i, 0)),
        full(g1.shape), full(w.shape), full(gain.shape), full(s64.shape), full(s32.shape),
        full(qcn.shape), full(qcd.shape), full(cw.shape),
    ]
    args = [x2d, g1, w, gain, s64, s32, qcn, qcd, cw]
    aliases = {}
    if prev is not None:
        in_specs += [pl.BlockSpec(memory_space=pl.ANY)] * 2
        aliases = {len(args): 0, len(args) + 1: 2}
        args += list(prev)
    return pl.pallas_call(
        kern, grid=(n // tr,), in_specs=in_specs, out_specs=out_specs, out_shape=out_shape,
        input_output_aliases=aliases,
        compiler_params=pltpu.CompilerParams(dimension_semantics=("arbitrary",), vmem_limit_bytes=VMEM_LIMIT),
        name="proj",
    )(*args)


def _cmpfin_kernel(raw_ref, s64_ref, gain_ref, kc_ref, vc_ref, k_scr, v_scr, *, nsp):
    lane = lax.broadcasted_iota(jnp.int32, (nsp, LANES), 1)
    n = lax.broadcasted_iota(jnp.int32, (nsp, LANES), 0)
    k_scr[...] = raw_ref[0, :, 0:128]
    v_scr[...] = raw_ref[0, :, 128:256]
    for j in range(2):
        k = k_scr[pl.ds(j, nsp, stride=2), :]
        kn = k * lax.rsqrt(_group_meansq(k, s64_ref[...], 1.0 / HEAD_DIM) + EPS) * gain_ref[...]
        cend = n * L_SEL + (L_CMP - 1 + L_CMP * j)
        prow = _pos_rows(cend, lane)
        k0, k1 = _split_heads(kn, prow, lane)
        kc_ref[0, 0, j * nsp:(j + 1) * nsp, :] = k0.astype(BF16)
        kc_ref[1, 0, j * nsp:(j + 1) * nsp, :] = k1.astype(BF16)
        vc_ref[0, j * nsp:(j + 1) * nsp, :] = v_scr[pl.ds(j, nsp, stride=2), :].astype(BF16)


def _cmpfin(raw, s64, gain_kc):
    bx, n2, _ = raw.shape
    nsp = n2 // 2
    assert nsp % 16 == 0
    return pl.pallas_call(
        functools.partial(_cmpfin_kernel, nsp=nsp),
        grid=(bx,),
        in_specs=[pl.BlockSpec((1, n2, 256), lambda b: (b, 0, 0)),
                  pl.BlockSpec(s64.shape, lambda b: (0, 0)),
                  pl.BlockSpec(gain_kc.shape, lambda b: (0, 0))],
        out_specs=(pl.BlockSpec((NSA_KV, 1, n2, LANES), lambda b: (0, b, 0, 0)),
                   pl.BlockSpec((1, n2, LANES), lambda b: (b, 0, 0))),
        out_shape=(jax.ShapeDtypeStruct((NSA_KV, bx, n2, LANES), BF16),
                   jax.ShapeDtypeStruct((bx, n2, LANES), BF16)),
        scratch_shapes=[pltpu.VMEM((n2, LANES), F32), pltpu.VMEM((n2, LANES), F32)],
        name="cmpfin",
    )(raw, s64, gain_kc)


ACC_ROWS = HEAD_DIM + 16


def _flash_init(c, m_ref, acc_ref):
    m_ref[c] = jnp.full(m_ref.shape[1:], NEG, F32)
    acc_ref[c] = jnp.zeros(acc_ref.shape[1:], F32)


def _flash_update(c, s, vt_tile, m_ref, acc_ref):
    m_old = m_ref[c]
    m_new = jnp.maximum(m_old, jnp.max(s, axis=0, keepdims=True))
    p = jnp.exp2(s - m_new).astype(BF16)
    m_ref[c] = m_new
    vt1 = jnp.concatenate([vt_tile, jnp.ones((ACC_ROWS - HEAD_DIM, vt_tile.shape[1]), BF16)], axis=0)
    acc_ref[c] = jnp.exp2(m_old - m_new) * acc_ref[c] + _mm(vt1, p)


def _max_key_norm(k_ref_2d, n_keys):
    feat = lax.broadcasted_iota(jnp.int32, (1, LANES), 1) < HEAD_DIM

    def chunk(c, best):
        kf = k_ref_2d[pl.ds(pl.multiple_of(c * KEY_TILE, KEY_TILE), KEY_TILE), :].astype(F32)
        sq = jnp.sum(jnp.where(feat, kf * kf, 0.0), axis=1, keepdims=True)
        return jnp.maximum(best, jnp.max(sq, axis=0, keepdims=True))

    return jnp.sqrt(lax.fori_loop(0, n_keys // KEY_TILE, chunk, jnp.zeros((1, 1), F32)))


def _first_needed_tile(qa, m, k_norm, slope_row, n_max):
    feat = lax.broadcasted_iota(jnp.int32, (1, LANES), 1) < HEAD_DIM
    qf = qa.astype(F32)
    qsq = jnp.where(feat, qf * qf, 0.0)
    hi = qsq.astype(BF16)
    lo = (qsq - hi.astype(F32)).astype(BF16)
    ones = jnp.ones((8, LANES), BF16)
    q_norm = jnp.sqrt((_nt(ones, hi) + _nt(ones, lo))[0:1])
    cutoff = (m + UNDERFLOW_LOG2 - NORM_SLACK * q_norm * k_norm) / slope_row
    tile = jnp.floor(jnp.min(cutoff, axis=1, keepdims=True) * (1.0 / KEY_TILE))
    return jnp.clip(tile, 0.0, n_max.astype(F32)).astype(jnp.int32)[0, 0]


def _sweep_earlier_tiles(first, n_tiles, step):
    both = (0, 1)

    @pl.when(jnp.logical_and(n_tiles % 2 == 1, jnp.minimum(first[0], first[1]) < n_tiles))
    def _():
        step(pl.multiple_of((n_tiles - 1) * KEY_TILE, KEY_TILE), KEY_TILE, both)

    j_end = n_tiles // 2
    j0 = [jnp.minimum(f // 2, j_end) for f in first]
    j_both = jnp.maximum(j0[0], j0[1])

    def run(chains):
        def body(j, carry):
            step(pl.multiple_of(j * 2 * KEY_TILE, 2 * KEY_TILE), 2 * KEY_TILE, chains)
            return carry
        return body

    lax.fori_loop(j0[0], j_both, run((0,)), 0)
    lax.fori_loop(j0[1], j_both, run((1,)), 0)
    lax.fori_loop(j_both, j_end, run(both), 0)


def _masked_softmax_cols(s, valid):
    sm = jnp.where(valid, s, NEG)
    m = jnp.max(sm, axis=0, keepdims=True)
    p = jnp.where(valid, jnp.exp2(sm - m), 0.0)
    l = jnp.sum(p, axis=0, keepdims=True)
    return p * jnp.where(l > 0.0, 1.0 / l, 0.0)


def _nsa_kernel(q_ref, kc_ref, vc_ref, ks_ref, vs_ref, kw_ref, vw_ref, gt_ref, o_ref,
                sel_ref, acc_ref, m_ref, kn_ref, *, q_base, nsp, tw, win_base):
    i = pl.program_id(1)
    q0 = q_base + i * Q_LANES
    ncol = NSA_GROUP * Q_LANES
    groups = range(NSA_KV)
    qas = [q_ref[g].reshape(ncol, LANES) for g in groups]
    lane_c = lax.broadcasted_iota(jnp.int32, (1, ncol), 1)
    qpos_c = q0 + (lane_c & (Q_LANES - 1))
    rows_of = lambda g: slice(g * HEAD_DIM, (g + 1) * HEAD_DIM)

    n_row = lax.broadcasted_iota(jnp.int32, (nsp, ncol), 0)
    valid_e = (n_row * L_SEL + (L_CMP - 1)) <= qpos_c
    valid_o = (n_row * L_SEL + (2 * L_CMP - 1)) <= qpos_c
    blk = lax.broadcasted_iota(jnp.int32, (nsp, Q_LANES), 0)
    qpos_q = q0 + lax.broadcasted_iota(jnp.int32, (nsp, Q_LANES), 1)
    cur = qpos_q >> 6
    bonus = jnp.where((blk == 0) | (blk == cur) | (blk == cur - 1), FORCE_BONUS, 0.0)
    in_range = blk <= cur
    blk_f = blk.astype(F32)
    o_c = []
    for g in groups:
        s_c = _nt(kc_ref[g, 0], qas[g])
        sm_e = jnp.where(valid_e, s_c[0:nsp], NEG)
        sm_o = jnp.where(valid_o, s_c[nsp:2 * nsp], NEG)
        m = jnp.maximum(jnp.max(sm_e, axis=0, keepdims=True), jnp.max(sm_o, axis=0, keepdims=True))
        p_e = jnp.where(valid_e, jnp.exp2(sm_e - m), 0.0)
        p_o = jnp.where(valid_o, jnp.exp2(sm_o - m), 0.0)
        l = jnp.sum(p_e, axis=0, keepdims=True) + jnp.sum(p_o, axis=0, keepdims=True)
        inv = jnp.where(l > 0.0, 1.0 / l, 0.0)
        p_e = p_e * inv
        p_o = p_o * inv
        pcat = jnp.concatenate([p_e, p_o], axis=0).astype(BF16)
        o_c.append(_tn(vc_ref[0], pcat)[rows_of(g)])

        imp = p_e[:, 0:Q_LANES] + p_o[:, 0:Q_LANES]
        for r in range(1, NSA_GROUP):
            imp = imp + p_e[:, r * Q_LANES:(r + 1) * Q_LANES] + p_o[:, r * Q_LANES:(r + 1) * Q_LANES]
        score = jnp.where(in_range, imp + bonus, -jnp.inf)
        sel = jnp.zeros((nsp, Q_LANES), F32)
        for _ in range(N_SELECT):
            top = jnp.max(score, axis=0, keepdims=True)
            first = jnp.min(jnp.where(score == top, blk_f, 1e9), axis=0, keepdims=True)
            pick = blk_f == first
            sel = jnp.where(pick, jnp.where(top > -jnp.inf, 1.0, sel), sel)
            score = jnp.where(pick, -jnp.inf, score)
        sel_ref[g] = sel

    @pl.when(i == 0)
    def _():
        for g in groups:
            kn_ref[g] = jnp.broadcast_to(_max_key_norm(ks_ref.at[g, 0], ks_ref.shape[2]), kn_ref.shape[1:])

    for g in groups:
        _flash_init(g, m_ref, acc_ref)
    def sel_tile(k0, size, which):
        n_blk = size // L_SEL
        causal = (k0 + lax.broadcasted_iota(jnp.int32, (size, Q_LANES), 0)
                  <= q0 + lax.broadcasted_iota(jnp.int32, (size, Q_LANES), 1))
        for g in which:
            vt = vs_ref[0, 0, rows_of(g), pl.ds(k0, size)].astype(BF16)
            s = _nt(ks_ref[g, 0, pl.ds(k0, size), :], qas[g])
            sel_rows = sel_ref[g, pl.ds(pl.multiple_of(k0 // L_SEL, 8), n_blk), :]
            selt = jnp.concatenate(
                [jnp.broadcast_to(sel_rows[j:j + 1, :], (L_SEL, Q_LANES)) for j in range(n_blk)], axis=0)
            valid = jnp.where(causal, selt, 0.0) > 0.0
            s = jnp.concatenate(
                [jnp.where(valid, s[:, r * Q_LANES:(r + 1) * Q_LANES], NEG) for r in range(NSA_GROUP)], axis=1)
            _flash_update(g, s, vt, m_ref, acc_ref)

    n_before = (q0 + Q_LANES + KEY_TILE - 1) // KEY_TILE - 1
    k_own = pl.multiple_of(n_before * KEY_TILE, KEY_TILE)
    used = q0 - n_before * KEY_TILE + Q_LANES
    for size in range(Q_LANES, KEY_TILE + 1, Q_LANES):
        @pl.when(used == size)
        def _(size=size):
            sel_tile(k_own, size, groups)

    first = []
    for g in groups:
        slope = jnp.concatenate([jnp.full((1, Q_LANES), LOG2E * 2.0 ** -(g * NSA_GROUP + r + 1), F32)
                                 for r in range(NSA_GROUP)], axis=1)
        first.append(_first_needed_tile(qas[g], m_ref[g], kn_ref[g][0:1, 0:1], slope, n_before))
    _sweep_earlier_tiles(first, n_before, sel_tile)

    w0 = jnp.clip(q0 - win_base - WINDOW, 0, tw - WIN_ROWS)
    w0 = pl.multiple_of(w0, Q_LANES)
    kpos = win_base + w0 + lax.broadcasted_iota(jnp.int32, (WIN_ROWS, ncol), 0)
    dist = qpos_c - kpos
    in_window = (dist >= 0) & (dist < WINDOW)

    outs = []
    for g in groups:
        l_s = acc_ref[g, HEAD_DIM:HEAD_DIM + 1, :]
        o_s = acc_ref[g, 0:HEAD_DIM, :] * jnp.where(l_s > 0.0, 1.0 / l_s, 0.0)
        p_w = _masked_softmax_cols(_nt(kw_ref[g, 0, pl.ds(w0, WIN_ROWS), :], qas[g]), in_window)
        o_w = _mm(vw_ref[0, rows_of(g), pl.ds(w0, WIN_ROWS)].astype(BF16), p_w.astype(BF16))
        gt = gt_ref[g]
        for r in range(NSA_GROUP):
            cs = slice(r * Q_LANES, (r + 1) * Q_LANES)
            outs.append(gt[r:r + 1, :] * o_c[g][:, cs] + gt[4 + r:5 + r, :] * o_s[:, cs]
                        + gt[8 + r:9 + r, :] * o_w[:, cs])
    o_ref[...] = jnp.concatenate(outs, axis=0).T


def _nsa(qa, kc, vc, ksa, nsa_t, kwa, win_t, gt, *, layer, bx, nqb, q_base, win_base):
    nq = qa.shape[2]
    n2 = kc.shape[2]
    tk = ksa.shape[2]
    tw = kwa.shape[2]
    assert nq == bx * nqb * Q_LANES and tk % KEY_TILE == 0 and tw >= WIN_ROWS
    assert nsa_t.shape[2:] == (512, tk) and win_t.shape[1:] == (256, tw)
    kern = functools.partial(_nsa_kernel, q_base=q_base, nsp=n2 // 2, tw=tw, win_base=win_base)
    ncol = NSA_GROUP * Q_LANES
    return pl.pallas_call(
        kern, grid=(bx, nqb),
        in_specs=[
            pl.BlockSpec((NSA_KV, NSA_GROUP, Q_LANES, LANES), lambda b, i: (0, 0, b * nqb + i, 0)),
            pl.BlockSpec((NSA_KV, 1, n2, LANES), lambda b, i: (0, b, 0, 0)),
            pl.BlockSpec((1, n2, LANES), lambda b, i: (b, 0, 0)),
            pl.BlockSpec((NSA_KV, 1, tk, LANES), lambda b, i: (0, b, 0, 0)),
            pl.BlockSpec((1, 1, LANES, tk), lambda b, i: (layer, b, 3, 0)),
            pl.BlockSpec((NSA_KV, 1, tw, LANES), lambda b, i: (0, b, 0, 0)),
            pl.BlockSpec((1, LANES, tw), lambda b, i: (b, 1, 0)),
            pl.BlockSpec((NSA_KV, 16, Q_LANES), lambda b, i: (0, 0, b * nqb + i)),
        ],
        out_specs=pl.BlockSpec((Q_LANES, NSA_HEADS * HEAD_DIM), lambda b, i: (b * nqb + i, 0)),
        out_shape=jax.ShapeDtypeStruct((nq, NSA_HEADS * HEAD_DIM), F32),
        scratch_shapes=[pltpu.VMEM((NSA_KV, n2 // 2, Q_LANES), F32), pltpu.VMEM((NSA_KV, ACC_ROWS, ncol), F32),
                        pltpu.VMEM((NSA_KV, 1, ncol), F32),
                        pltpu.VMEM((NSA_KV, 8, LANES), F32)],
        compiler_params=pltpu.CompilerParams(dimension_semantics=("arbitrary",) * 2, vmem_limit_bytes=VMEM_LIMIT),
        name="nsa",
    )(qa, kc, vc, ksa, nsa_t, kwa, win_t, gt)


def _diff_kernel(q_ref, k_ref, v_ref, sl_ref, lp_ref, sg_ref, o_ref, acc_ref, m_ref, kn_ref, *,
                 q_base, tq, lam_init):
    pair = pl.program_id(1)
    i = pl.program_id(2)
    q0 = q_base + i * tq
    ncol = 2 * tq
    heads = range(2)
    lp = lp_ref[...]
    lam = (jnp.exp(jnp.sum(lp[0:1] * lp[1:2], keepdims=True)) - jnp.exp(jnp.sum(lp[2:3] * lp[3:4], keepdims=True))
           + lam_init)
    n_full = q0 // KEY_TILE
    qas = [q_ref[hh].reshape(ncol, LANES) for hh in heads]
    for hh in heads:
        _flash_init(hh, m_ref, acc_ref)

    @pl.when(i == 0)
    def _():
        for hh in heads:
            kn_ref[hh] = jnp.broadcast_to(_max_key_norm(k_ref.at[hh, 0], k_ref.shape[2]), kn_ref.shape[1:])

    def step(k0, size, which, causal=None):
        for hh in which:
            vt = v_ref[0, 0, hh * HEAD_DIM:(hh + 1) * HEAD_DIM, pl.ds(k0, size)].astype(BF16)
            s = _nt(k_ref[hh, 0, pl.ds(k0, size), :], qas[hh])
            if causal is not None:
                s = jnp.where(causal, s, NEG)
            _flash_update(hh, s, vt, m_ref, acc_ref)

    k_diag = pl.multiple_of(n_full * KEY_TILE, KEY_TILE)

    tok = k_diag + lax.broadcasted_iota(jnp.int32, (KEY_TILE, ncol), 0)
    step(k_diag, KEY_TILE, heads, tok <= q0 + lax.broadcasted_iota(jnp.int32, (KEY_TILE, ncol), 1) % tq)

    first = []
    for hh in heads:
        slope = jnp.concatenate([sl_ref[pl.ds(2 * pair + hh, 1), :]] * (ncol // LANES), axis=1)
        first.append(_first_needed_tile(qas[hh], m_ref[hh], kn_ref[hh][0:1, 0:1], slope, n_full))
    _sweep_earlier_tiles(first, n_full, step)

    outs = []
    for hh in heads:
        o = acc_ref[hh, 0:HEAD_DIM, :] / acc_ref[hh, HEAD_DIM:HEAD_DIM + 1, :]
        d = o[:, 0:tq] - lam * o[:, tq:ncol]
        d = d * lax.rsqrt(jnp.mean(d * d, axis=0, keepdims=True) + EPS) * sg_ref[:, 0:tq] * (1.0 - lam_init)
        outs.append(d)
    o_ref[...] = jnp.concatenate(outs, axis=0).T


def _diff(dqa, dka, dif_t, slopes, lp, sg, *, layer, bx, nqb, tq, q_base, lam_init):
    nq = dqa.shape[2]
    tk = dka.shape[2]
    assert nq == bx * nqb * tq and tk % KEY_TILE == 0 and KEY_TILE % tq == 0 and dif_t.shape[2:] == (1024, tk)
    kern = functools.partial(_diff_kernel, q_base=q_base, tq=tq, lam_init=lam_init)
    return pl.pallas_call(
        kern, grid=(bx, DIFF_HEADS // 2, nqb),
        in_specs=[
            pl.BlockSpec((2, 2, tq, LANES), lambda b, h, i: (h, 0, b * nqb + i, 0)),
            pl.BlockSpec((2, 1, tk, LANES), lambda b, h, i: (h, b, 0, 0)),
            pl.BlockSpec((1, 1, LANES, tk), lambda b, h, i: (layer, b, 4 + h, 0)),
            pl.BlockSpec(slopes.shape, lambda b, h, i: (0, 0)),
            pl.BlockSpec(lp.shape, lambda b, h, i: (0, 0)),
            pl.BlockSpec(sg.shape, lambda b, h, i: (0, 0)),
        ],
        out_specs=pl.BlockSpec((tq, LANES), lambda b, h, i: (b * nqb + i, h)),
        out_shape=jax.ShapeDtypeStruct((nq, DIFF_HEADS * HEAD_DIM), F32),
        scratch_shapes=[pltpu.VMEM((2, ACC_ROWS, 2 * tq), F32), pltpu.VMEM((2, 1, 2 * tq), F32),
                        pltpu.VMEM((2, 8, LANES), F32)],
        compiler_params=pltpu.CompilerParams(dimension_semantics=("arbitrary",) * 3, vmem_limit_bytes=VMEM_LIMIT),
        name="diff",
    )(dqa, dka, dif_t, slopes, lp, sg)


F_CHUNKS = 2


def _ffn_core(x, on, od, wo_ref, g2_ref, wg_ref, wu_ref, cw_ref, cb_ref, wd_ref, prev_rows, g_store):
    o = jnp.concatenate([on, od], axis=1).astype(BF16)
    xm = x + _mm(o, wo_ref[...])
    h2 = (xm * lax.rsqrt(jnp.mean(xm * xm, axis=-1, keepdims=True) + EPS) * g2_ref[...]).astype(BF16)
    d_ff = wg_ref.shape[1]
    fc = d_ff // F_CHUNKS
    y = jnp.zeros(x.shape, F32)
    for c in range(F_CHUNKS):
        c0, c1 = c * fc, (c + 1) * fc
        g = _mm(h2, wg_ref[:, c0:c1])
        u = _mm(h2, wu_ref[:, c0:c1])
        gm1, gm2 = prev_rows(g, c0, c1)
        g_store(g, c0, c1)
        gc = cb_ref[:, c0:c1] + cw_ref[0:1, c0:c1] * gm2 + cw_ref[1:2, c0:c1] * gm1 + cw_ref[2:3, c0:c1] * g
        act = gc * (1.0 / (1.0 + jnp.exp(-gc))) * u
        y = y + _mm(act.astype(BF16), wd_ref[c0:c1, :])
    return xm + y


def _ffn_prompt_kernel(x_ref, on_ref, od_ref, wo_ref, g2_ref, wg_ref, wu_ref, cw_ref, cb_ref, wd_ref,
                       y_ref, cv_ref, carry_ref, *, tiles_per_seq):
    i = pl.program_id(0)
    tr = x_ref.shape[0]

    @pl.when(i % tiles_per_seq == 0)
    def _():
        carry_ref[...] = jnp.zeros(carry_ref.shape, F32)

    def prev_rows(g, c0, c1):
        row = lax.broadcasted_iota(jnp.int32, g.shape, 0)
        p1 = carry_ref[7:8, c0:c1]
        p2 = carry_ref[6:7, c0:c1]
        gm1 = jnp.where(row == 0, p1, pltpu.roll(g, 1, 0))
        gm2 = jnp.where(row == 0, p2, jnp.where(row == 1, p1, pltpu.roll(g, 2, 0)))
        return gm1, gm2

    def g_store(g, c0, c1):
        carry_ref[:, c0:c1] = g[tr - 8:tr]
        cv_ref[0, :, c0:c1] = g[tr - 8:tr]

    y_ref[...] = _ffn_core(x_ref[...], on_ref[...], od_ref[...], wo_ref, g2_ref, wg_ref, wu_ref, cw_ref, cb_ref,
                           wd_ref, prev_rows, g_store)


def _ffn_sample_kernel(x_ref, on_ref, od_ref, st1_ref, st2_ref, wo_ref, g2_ref, wg_ref, wu_ref, cw_ref, cb_ref,
                       wd_ref, y_ref, g_ref, *, seq):
    def prev_rows(g, c0, c1):
        rs = lax.broadcasted_iota(jnp.int32, g.shape, 0) % seq
        s1 = st1_ref[:, c0:c1]
        gm1 = jnp.where(rs == 0, s1, pltpu.roll(g, 1, 0))
        gm2 = jnp.where(rs == 0, st2_ref[:, c0:c1], jnp.where(rs == 1, s1, pltpu.roll(g, 2, 0)))
        return gm1, gm2

    def g_store(g, c0, c1):
        g_ref[:, c0:c1] = g

    y_ref[...] = _ffn_core(x_ref[...], on_ref[...], od_ref[...], wo_ref, g2_ref, wg_ref, wu_ref, cw_ref, cb_ref,
                           wd_ref, prev_rows, g_store)


def _const_spec(a):
    return pl.BlockSpec(a.shape, lambda i: (0,) * a.ndim, pipeline_mode=pl.Buffered(1))


def _ffn_prompt(x2d, on, od, wo, g2, wg, wu, cw, cb, wd, *, seq_len, tr=256):
    n, d = x2d.shape
    f = wg.shape[1]
    assert n % tr == 0 and seq_len % tr == 0
    tps = seq_len // tr
    row = lambda w: pl.BlockSpec((tr, w), lambda i: (i, 0))
    return pl.pallas_call(
        functools.partial(_ffn_prompt_kernel, tiles_per_seq=tps), grid=(n // tr,),
        in_specs=[row(d), row(on.shape[1]), row(od.shape[1])] + [_const_spec(a) for a in (wo, g2, wg, wu, cw, cb, wd)],
        out_specs=(row(d), pl.BlockSpec((1, 8, f), lambda i: (i // tps, 0, 0))),
        out_shape=(jax.ShapeDtypeStruct((n, d), F32), jax.ShapeDtypeStruct((n // seq_len, 8, f), F32)),
        scratch_shapes=[pltpu.VMEM((8, f), F32)],
        compiler_params=pltpu.CompilerParams(dimension_semantics=("arbitrary",), vmem_limit_bytes=VMEM_LIMIT),
        name="ffn_prompt",
    )(x2d, on, od, wo, g2, wg, wu, cw, cb, wd)


def _ffn_sample(x2d, on, od, st1, st2, wo, g2, wg, wu, cw, cb, wd, *, seq):
    n, d = x2d.shape
    f = wg.shape[1]
    args = (x2d, on, od, st1, st2, wo, g2, wg, wu, cw, cb, wd)
    return pl.pallas_call(
        functools.partial(_ffn_sample_kernel, seq=seq), grid=(1,),
        in_specs=[_const_spec(a) for a in args],
        out_specs=(pl.BlockSpec((n, d), lambda i: (0, 0)), pl.BlockSpec((n, f), lambda i: (0, 0))),
        out_shape=(jax.ShapeDtypeStruct((n, d), F32), jax.ShapeDtypeStruct((n, f), F32)),
        compiler_params=pltpu.CompilerParams(dimension_semantics=("arbitrary",), vmem_limit_bytes=VMEM_LIMIT),
        name="ffn_sample",
    )(*args)


PAGES_PER_STEP = 16
STEPS_PER_CMP_TILE = LANES // (PAGES_PER_STEP * LANES // L_CMP)
T_NSA, T_WIN, T_DIF, T_ROWS = 0, 512, 768, 1792


def _softmax_rows(s):
    m = jnp.max(s, axis=-1, keepdims=True)
    p = jnp.exp2(s - m)
    return p * (1.0 / jnp.sum(p, axis=-1, keepdims=True))


def _masked_softmax_rows(s, valid):
    sm = jnp.where(valid, s, NEG)
    m = jnp.max(sm, axis=-1, keepdims=True)
    p = jnp.where(valid, jnp.exp2(sm - m), 0.0)
    l = jnp.sum(p, axis=-1, keepdims=True)
    return p * jnp.where(l > 0.0, 1.0 / l, 0.0)


def _sattn_kernel(pt_ref, *refs, n_steps, past_len, dec, lam_init):
    pg = PAGES_PER_STEP
    nsa_pg, dif_pg = refs[0:pg], refs[pg:2 * pg]
    (win_ref, nn_ref, nw_ref, nd_ref, qn_ref, qd_ref, gate_ref, sn_ref, sd_ref, wck_ref, wcv_ref, gkc_ref,
     lp_ref, sg_ref, o_ref, ns_ref,
     ssel, sdif, psel, pdif, vsel, vdif, kc_scr, tail_scr, oc_scr, ow_scr) = refs[2 * pg:]
    s = pl.program_id(1)
    span = pg * LANES
    groups, heads = range(NSA_KV), range(DIFF_HEADS)
    cat = lambda parts: jnp.concatenate(parts, axis=1)
    hi_lo = lambda x: (x.astype(BF16), (x - x.astype(BF16).astype(F32)).astype(BF16))

    @pl.when(s < n_steps)
    def _():
        @pl.when(s == 0)
        def _():
            kc_scr[...] = jnp.zeros(kc_scr.shape, F32)

        shift = (s % STEPS_PER_CMP_TILE) * (span // L_CMP)
        col = pl.multiple_of((s // STEPS_PER_CMP_TILE) * LANES, LANES)
        for part, w_ref in ((0, wck_ref), (1, wcv_ref)):
            x_hi, x_lo = hi_lo(cat([r[0, 0, part * 128:(part + 1) * 128, :] for r in nsa_pg]))
            blk = _mm(x_hi, w_ref[0]) + _mm(x_lo, w_ref[0]) + _mm(x_hi, w_ref[1])
            kc_scr[part * 128:(part + 1) * 128, pl.ds(col, LANES)] += pltpu.roll(blk, shift, 1)

        k0 = pl.multiple_of(s * span, span)
        kpos = (k0 + lax.broadcasted_iota(jnp.int32, (1, span), 1)).astype(F32)
        for g in groups:
            k8 = cat([r[0, 0, 256 + g * HEAD_DIM:256 + (g + 1) * HEAD_DIM, :] for r in nsa_pg]).astype(BF16)
            ssel[g, :, pl.ds(k0, span)] = _mm(qn_ref[0, g], k8) + cat([sn_ref[g]] * pg) * kpos
        for h in heads:
            k8 = cat([r[0, 0, h * HEAD_DIM:(h + 1) * HEAD_DIM, :] for r in dif_pg]).astype(BF16)
            sdif[h, :, pl.ds(k0, span)] = _mm(qd_ref[0, h], k8) + cat([sd_ref[h]] * pg) * kpos
        vsel[:, pl.ds(k0, span)] = cat([r[0, 0, 384:512, :] for r in nsa_pg]).astype(BF16)
        vdif[:, pl.ds(k0, span)] = cat([r[0, 0, 512:1024, :] for r in dif_pg]).astype(BF16)

    @pl.when(s == n_steps)
    def _():
        new_rows = cat([nn_ref[...], nw_ref[...], nd_ref[...]])
        tail_scr[...] = jnp.concatenate([new_rows, jnp.zeros((LANES - dec, T_ROWS), F32)], axis=0).T
        nrow, drow = NSA_GROUP * dec, 2 * dec
        tail_pos = (past_len + lax.broadcasted_iota(jnp.int32, (1, LANES), 1)).astype(F32)
        causal32 = (lax.broadcasted_iota(jnp.int32, (nrow, LANES), 1)
                    <= lax.broadcasted_iota(jnp.int32, (nrow, LANES), 0) % dec)
        causal16 = (lax.broadcasted_iota(jnp.int32, (drow, LANES), 1)
                    <= lax.broadcasted_iota(jnp.int32, (drow, LANES), 0) % dec)

        for g in groups:
            kt = tail_scr[T_NSA + 256 + g * HEAD_DIM:T_NSA + 256 + (g + 1) * HEAD_DIM, :].astype(BF16)
            sc = _mm(qn_ref[0, g], kt) + sn_ref[g] * tail_pos
            ssel[g, :, past_len:past_len + LANES] = jnp.where(causal32, sc, NEG)
        for h in heads:
            kt = tail_scr[T_DIF + h * HEAD_DIM:T_DIF + (h + 1) * HEAD_DIM, :].astype(BF16)
            sc = _mm(qd_ref[0, h], kt) + sd_ref[h] * tail_pos
            sdif[h, :, past_len:past_len + LANES] = jnp.where(causal16, sc, NEG)

        n_cmp = kc_scr.shape[1]
        cend = lax.broadcasted_iota(jnp.int32, (nrow, n_cmp), 1) * L_CMP + (L_CMP - 1)
        qpos_c = past_len + lax.broadcasted_iota(jnp.int32, (nrow, n_cmp), 0) % dec
        imps = []
        for g in groups:
            kc = kc_scr[g * HEAD_DIM:(g + 1) * HEAD_DIM, :]
            kc = kc * lax.rsqrt(jnp.mean(kc * kc, axis=0, keepdims=True) + EPS) * cat([gkc_ref[...]] * (n_cmp // LANES))
            s_c = _mm(qn_ref[0, g], kc.astype(BF16)) + cat([sn_ref[g]] * (n_cmp // LANES)) * cend.astype(F32)
            p_c = _masked_softmax_rows(s_c, cend <= qpos_c)
            vc = kc_scr[128 + g * HEAD_DIM:128 + (g + 1) * HEAD_DIM, :].astype(BF16)
            oc_scr[g] = _nt(p_c.astype(BF16), vc)
            imps.append(sum(p_c[r * dec:(r + 1) * dec] for r in range(NSA_GROUP)))
        imp2 = jnp.concatenate(imps, axis=0)
        pair = (lax.broadcasted_iota(jnp.int32, (n_cmp, LANES), 0) // 2
                == lax.broadcasted_iota(jnp.int32, (n_cmp, LANES), 1)).astype(BF16)
        i_hi, i_lo = hi_lo(imp2)
        imp = _mm(i_hi, pair) + _mm(i_lo, pair)

        n_blk = past_len // L_SEL
        blk = lax.broadcasted_iota(jnp.int32, imp.shape, 1)
        blk_f = blk.astype(F32)
        bonus = jnp.where((blk == 0) | (blk == n_blk - 1), FORCE_BONUS, 0.0)
        score = jnp.where(blk < n_blk, imp + bonus, -jnp.inf)
        sel = jnp.zeros(imp.shape, F32)
        for _ in range(N_SELECT - 1):
            top = jnp.max(score, axis=1, keepdims=True)
            first = jnp.min(jnp.where(score == top, blk_f, 1e9), axis=1, keepdims=True)
            pick = blk_f == first
            sel = jnp.where(pick, jnp.where(top > -jnp.inf, 1.0, sel), sel)
            score = jnp.where(pick, -jnp.inf, score)
        sel = sel.astype(BF16)

        for c in range(n_steps):
            tok = c * span + lax.broadcasted_iota(jnp.int32, (LANES, span), 1)
            expand = ((tok >> 6) == lax.broadcasted_iota(jnp.int32, (LANES, span), 0)).astype(BF16)
            chosen = _mm(sel, expand)
            for g in groups:
                keep = jnp.concatenate([chosen[g * dec:(g + 1) * dec]] * NSA_GROUP, axis=0) > 0.5
                ssel[g, :, c * span:(c + 1) * span] = jnp.where(keep, ssel[g, :, c * span:(c + 1) * span], NEG)
        for g in groups:
            psel[g] = _softmax_rows(ssel[g]).astype(BF16)
        for h in heads:
            pdif[h] = _softmax_rows(sdif[h]).astype(BF16)

        w_buf = win_ref.shape[3]
        kwpos = past_len - w_buf + lax.broadcasted_iota(jnp.int32, (nrow, w_buf + LANES), 1)
        dist = past_len + lax.broadcasted_iota(jnp.int32, (nrow, w_buf + LANES), 0) % dec - kwpos
        in_window = (dist >= 0) & (dist < WINDOW)
        for g in groups:
            kw = cat([win_ref[0, 0, g * HEAD_DIM:(g + 1) * HEAD_DIM, :],
                      tail_scr[T_WIN + g * HEAD_DIM:T_WIN + (g + 1) * HEAD_DIM, :]]).astype(BF16)
            vw = cat([win_ref[0, 0, 128 + g * HEAD_DIM:128 + (g + 1) * HEAD_DIM, :],
                      tail_scr[T_WIN + 128 + g * HEAD_DIM:T_WIN + 128 + (g + 1) * HEAD_DIM, :]]).astype(BF16)
            s_w = _mm(qn_ref[0, g], kw) + cat([sn_ref[g]] * (w_buf // LANES + 1)) * kwpos.astype(F32)
            ow_scr[g] = _nt(_masked_softmax_rows(s_w, in_window).astype(BF16), vw)

        shifted = pltpu.roll(win_ref[0, 0], w_buf - dec, 1)
        fresh = cat([jnp.zeros((256, w_buf - LANES), F32), pltpu.roll(tail_scr[T_WIN:T_WIN + 256, :], LANES - dec, 1)])
        ns_ref[0] = jnp.where(lax.broadcasted_iota(jnp.int32, (256, w_buf), 1) >= w_buf - dec, fresh, shifted)

        lp = lp_ref[...]
        lam = (jnp.exp(jnp.sum(lp[0:1] * lp[1:2], keepdims=True))
               - jnp.exp(jnp.sum(lp[2:3] * lp[3:4], keepdims=True)) + lam_init)
        pieces = []
        for g in groups:
            vt = tail_scr[T_NSA + 384 + g * HEAD_DIM:T_NSA + 384 + (g + 1) * HEAD_DIM, :].astype(BF16)
            o_s = (_nt(psel[g, :, 0:past_len], vsel[g * HEAD_DIM:(g + 1) * HEAD_DIM, :])
                   + _nt(psel[g, :, past_len:past_len + LANES], vt))
            o = gate_ref[0, g, 0] * oc_scr[g] + gate_ref[0, g, 1] * o_s + gate_ref[0, g, 2] * ow_scr[g]
            pieces += [o[r * dec:(r + 1) * dec] for r in range(NSA_GROUP)]
        for h in heads:
            vt = tail_scr[T_DIF + 512 + h * HEAD_DIM:T_DIF + 512 + (h + 1) * HEAD_DIM, :].astype(BF16)
            o = (_nt(pdif[h, :, 0:past_len], vdif[h * HEAD_DIM:(h + 1) * HEAD_DIM, :])
                 + _nt(pdif[h, :, past_len:past_len + LANES], vt))
            d = o[0:dec] - lam * o[dec:2 * dec]
            pieces.append(d * lax.rsqrt(jnp.mean(d * d, axis=-1, keepdims=True) + EPS) * sg_ref[...] * (1.0 - lam_init))
        o_ref[0] = cat(pieces)


def _sattn(page_table, cache_nsa, cache_diff, state_win, new_nsa, new_win, new_dif, qn, qd, gates, sn, sd,
           wck, wcv, gkc, lp, sg, *, layer, lam_init):
    bs, n_pages = page_table.shape
    page = cache_nsa.shape[3]
    dec = new_nsa.shape[0] // bs
    past_len = n_pages * page
    pg = PAGES_PER_STEP
    n_steps = n_pages // pg
    w_buf = state_win.shape[3]
    assert page == LANES and n_pages % pg == 0 and dec == 8 and w_buf % LANES == 0
    n_cmp = -(-n_steps // STEPS_PER_CMP_TILE) * LANES
    tks = past_len + LANES
    pt = page_table.reshape(-1)

    def kpage(j):
        return lambda b, s, pt_ref: (layer, pt_ref[b * n_pages + jnp.minimum(s, n_steps - 1) * pg + j], 0, 0)

    const = lambda a: pl.BlockSpec(a.shape, lambda b, s, pt_ref: (0,) * a.ndim)
    per_b = lambda a: pl.BlockSpec((1,) + a.shape[1:], lambda b, s, pt_ref: (b,) + (0,) * (a.ndim - 1))
    in_specs = (
        [pl.BlockSpec((1, 1, 512, page), kpage(j)) for j in range(pg)]
        + [pl.BlockSpec((1, 1, 1024, page), kpage(j)) for j in range(pg)]
        + [pl.BlockSpec((1, 1, 256, w_buf), lambda b, s, pt_ref: (layer, b, 0, 0)),
           pl.BlockSpec((dec, 512), lambda b, s, pt_ref: (b, 0)),
           pl.BlockSpec((dec, 256), lambda b, s, pt_ref: (b, 0)),
           pl.BlockSpec((dec, 1024), lambda b, s, pt_ref: (b, 0)),
           per_b(qn), per_b(qd), per_b(gates), const(sn), const(sd), const(wck), const(wcv), const(gkc),
           const(lp), const(sg)])
    out_specs = (pl.BlockSpec((1, dec, 1024), lambda b, s, pt_ref: (b, 0, 0)),
                 pl.BlockSpec((1, 256, w_buf), lambda b, s, pt_ref: (b, 0, 0)))
    scratch = [
        pltpu.VMEM((NSA_KV, NSA_GROUP * dec, tks), F32), pltpu.VMEM((DIFF_HEADS, 2 * dec, tks), F32),
        pltpu.VMEM((NSA_KV, NSA_GROUP * dec, tks), BF16), pltpu.VMEM((DIFF_HEADS, 2 * dec, tks), BF16),
        pltpu.VMEM((NSA_KV * HEAD_DIM, past_len), BF16), pltpu.VMEM((DIFF_HEADS * HEAD_DIM, past_len), BF16),
        pltpu.VMEM((256, n_cmp), F32), pltpu.VMEM((T_ROWS, LANES), F32),
        pltpu.VMEM((NSA_KV, NSA_GROUP * dec, HEAD_DIM), F32), pltpu.VMEM((NSA_KV, NSA_GROUP * dec, HEAD_DIM), F32),
    ]
    grid_spec = pltpu.PrefetchScalarGridSpec(num_scalar_prefetch=1, grid=(bs, n_steps + 1), in_specs=in_specs,
                                             out_specs=out_specs, scratch_shapes=scratch)
    caches = [cache_nsa] * pg + [cache_diff] * pg
    return pl.pallas_call(
        functools.partial(_sattn_kernel, n_steps=n_steps, past_len=past_len, dec=dec, lam_init=lam_init),
        grid_spec=grid_spec,
        out_shape=(jax.ShapeDtypeStruct((bs, dec, 1024), F32), jax.ShapeDtypeStruct((bs, 256, w_buf), F32)),
        compiler_params=pltpu.CompilerParams(dimension_semantics=("arbitrary",) * 2, vmem_limit_bytes=VMEM_LIMIT),
        name="sattn",
    )(pt, *caches, state_win, new_nsa, new_win, new_dif, qn, qd, gates, sn, sd, wck, wcv, gkc, lp, sg)


def _alibi_slopes(n):
    return 2.0 ** (-8.0 * jnp.arange(1, n + 1, dtype=F32) / n)


def _slope_rows(slopes):
    slopes = slopes * LOG2E
    s0 = slopes.astype(BF16).astype(F32)
    s1 = (slopes - s0).astype(BF16).astype(F32)
    s2 = (slopes - s0 - s1).astype(BF16).astype(F32)
    cols = jnp.stack([64.0 * s0, s0, 64.0 * s1, s1, 64.0 * s2, s2], axis=1)
    out = jnp.zeros((slopes.shape[0], LANES), F32)
    return out.at[:, AUG0:AUG0 + N_AUG].set(cols)


def _block_ones(group):
    idx = np.arange(LANES) // group
    return jnp.asarray((idx[:, None] == idx[None, :]).astype(np.float32), dtype=BF16)


_GATE_SRC = np.full((LANES,), -1, np.int64)
for _g in range(NSA_KV):
    for _r in range(NSA_GROUP):
        for _j in range(3):
            _GATE_SRC[_g * 16 + _j * 4 + _r] = 1280 + _g * 12 + _r * 3 + _j


def _permute_w_in(w):
    main = jnp.concatenate([w[:, 0:1280], w[:, 1304:2840]], axis=1)
    gate = jnp.where(jnp.asarray(_GATE_SRC >= 0)[None, :], w[:, np.maximum(_GATE_SRC, 0)], 0.0)
    return jnp.concatenate([main, gate], axis=1).astype(BF16)


def _compress_weights(w):
    t = np.arange(PAGES_PER_STEP * LANES)
    place = jnp.asarray(t[:, None] // L_CMP == np.arange(LANES)[None, :])
    full = jnp.where(place, jnp.tile(w, PAGES_PER_STEP * LANES // L_CMP)[:, None], 0.0)
    hi = full.astype(BF16)
    return jnp.stack([hi, (full - hi.astype(F32)).astype(BF16)])


def _gain_row(nsa_qg, nsa_kg, diff_qg, diff_kg):
    one = lambda n: jnp.ones((n,), F32)
    parts = [jnp.tile(nsa_qg, NSA_HEADS), one(256), jnp.tile(nsa_kg[1], NSA_KV), one(128),
             jnp.tile(nsa_kg[2], NSA_KV), one(128), jnp.tile(diff_qg.reshape(-1), DIFF_HEADS),
             jnp.tile(diff_kg.reshape(-1), DIFF_HEADS), one(512 + LANES)]
    return jnp.concatenate(parts)[None, :]


def kernel(x_prompt, x_sample, cache_nsa, cache_diff, state_win, state_conv, page_table, norm1_g, norm2_g, w_in, w_out, nsa_qnorm_g, nsa_knorm_g, nsa_cmp_w, diff_qnorm_g, diff_knorm_g, diff_lambda, diff_subnorm_g, w_gate, w_up, conv_w, conv_b, w_down):
    batch, seq, d_model = x_prompt.shape
    dec_batch, dec_seq, _ = x_sample.shape
    depth, n_pool, page = cache_nsa.shape[:3]
    n_pages = page_table.shape[1]
    past_len = n_pages * page
    w_buf = state_win.shape[2]
    d_ff = w_gate.shape[2]
    assert dec_seq < L_CMP and seq % KEY_TILE == 0 and w_buf == WINDOW

    s64, s32 = _block_ones(HEAD_DIM), _block_ones(DIFF_HALF)
    qcn = _slope_rows(_alibi_slopes(NSA_HEADS))
    qcd = _slope_rows(_alibi_slopes(DIFF_HEADS))
    fmaj = lambda a: jnp.moveaxis(a, 2, -1).reshape(a.shape[0], a.shape[1], -1, a.shape[2])
    cache_nsa4 = fmaj(cache_nsa)
    cache_diff4 = fmaj(cache_diff)
    state_win4 = fmaj(state_win)

    sn = jnp.broadcast_to((_alibi_slopes(NSA_HEADS) * LOG2E).reshape(NSA_KV, NSA_GROUP, 1, 1),
                          (NSA_KV, NSA_GROUP, dec_seq, LANES)).reshape(NSA_KV, NSA_GROUP * dec_seq, LANES)
    sd = jnp.broadcast_to((_alibi_slopes(DIFF_HEADS) * LOG2E).reshape(DIFF_HEADS, 1, 1), (DIFF_HEADS, 2 * dec_seq, LANES))
    xp = x_prompt.reshape(batch * seq, d_model)
    xs = x_sample.reshape(dec_batch * dec_seq, d_model)

    nsa_t = jnp.zeros((depth, batch, 512, seq), F32)
    dif_t = jnp.zeros((depth, batch, 1024, seq), F32)
    outs = [[] for _ in range(8)]
    for l in range(depth):
        lam_init = 0.8 - 0.6 * math.exp(-0.3 * l)
        w_in_p = _permute_w_in(w_in[l])
        gain = _gain_row(nsa_qnorm_g[l], nsa_knorm_g[l], diff_qnorm_g[l], diff_knorm_g[l])
        gain_kc = jnp.tile(nsa_knorm_g[l, 0], NSA_KV)[None, :]
        cw = jnp.concatenate([jnp.broadcast_to(nsa_cmp_w[l, 0][:, None], (L_CMP, LANES)),
                              jnp.broadcast_to(nsa_cmp_w[l, 1][:, None], (L_CMP, LANES))], axis=1)
        lp = jnp.zeros((8, LANES), F32).at[0:4, 0:DIFF_HALF].set(diff_lambda[l])
        sg = jnp.broadcast_to(diff_subnorm_g[l][:, None], (HEAD_DIM, DIFF_TQ))
        sg_row = jnp.broadcast_to(diff_subnorm_g[l][None, :], (dec_seq, HEAD_DIM))
        gkc = jnp.broadcast_to(nsa_knorm_g[l, 0][:, None], (HEAD_DIM, LANES))
        wck, wcv = _compress_weights(nsa_cmp_w[l, 0]), _compress_weights(nsa_cmp_w[l, 1])
        g1 = norm1_g[l][None, :]
        g2 = norm2_g[l][None, :]
        wo, wg, wu, wd = (a.astype(BF16) for a in (w_out[l], w_gate[l], w_up[l], w_down[l]))
        cwf, cbf = conv_w[l], conv_b[l][None, :]
        proj = functools.partial(_proj, g1=g1, w=w_in_p, gain=gain, s64=s64, s32=s32, qcn=qcn, qcd=qcd, cw=cw)

        (nsa_t, win_t, dif_t, qa, ksa, kwa, gt, dqa, dka, craw) = proj(
            xp, seq_len=seq, pos_base=0, stacked=(l, depth, (nsa_t, dif_t)))
        kc, vc = _cmpfin(craw.reshape(batch, seq // L_CMP, 256), s64, gain_kc)
        o_nsa = _nsa(qa, kc, vc, ksa.reshape(NSA_KV, batch, seq, LANES), nsa_t,
                     kwa.reshape(NSA_KV, batch, seq, LANES), win_t, gt,
                     layer=l, bx=batch, nqb=seq // Q_LANES, q_base=0, win_base=0)
        o_dif = _diff(dqa, dka.reshape(DIFF_HEADS, batch, seq, LANES), dif_t, sd[:, 0], lp, sg,
                      layer=l, bx=batch, nqb=seq // DIFF_TQ, tq=DIFF_TQ, q_base=0, lam_init=lam_init)
        xp, cv = _ffn_prompt(xp, o_nsa, o_dif, wo, g2, wg, wu, cwf, cbf, wd, seq_len=seq)
        w_keep = min(WINDOW, seq)
        outs[4].append(jnp.moveaxis(win_t[:, :, seq - w_keep:].reshape(batch, 2, NSA_KV, HEAD_DIM, w_keep), -1, 1))
        outs[6].append(cv[:, 8 - 2:8])

        (nsa_rows, win_rows, dif_rows, qa, _, _, gt, dqa, _, _) = proj(xs, seq_len=dec_seq, pos_base=past_len)
        per_seq = lambda a, lead: a[..., :HEAD_DIM].reshape(lead + (dec_batch, dec_seq, HEAD_DIM))
        qn = jnp.transpose(per_seq(qa, (NSA_KV, NSA_GROUP)), (2, 0, 1, 3, 4)).reshape(
            dec_batch, NSA_KV, NSA_GROUP * dec_seq, HEAD_DIM)
        qd = jnp.transpose(per_seq(dqa, (DIFF_HEADS, 2)), (2, 0, 1, 3, 4)).reshape(
            dec_batch, DIFF_HEADS, 2 * dec_seq, HEAD_DIM)
        gates = jnp.transpose(gt.reshape(NSA_KV, 4, NSA_GROUP, dec_batch, dec_seq), (3, 0, 1, 2, 4)).reshape(
            dec_batch, NSA_KV, 4, NSA_GROUP * dec_seq, 1)
        gates = jnp.broadcast_to(gates, gates.shape[:-1] + (HEAD_DIM,))
        o_s, new_state = _sattn(page_table, cache_nsa4, cache_diff4, state_win4, nsa_rows, win_rows, dif_rows,
                                qn, qd, gates, sn, sd, wck, wcv, gkc, lp, sg_row, layer=l, lam_init=lam_init)
        o_s = o_s.reshape(dec_batch * dec_seq, 2 * NSA_HEADS * HEAD_DIM)
        st1 = jnp.repeat(state_conv[l][:, 1], dec_seq, axis=0)
        st2 = jnp.repeat(state_conv[l][:, 0], dec_seq, axis=0)
        xs, g_s = _ffn_sample(xs, o_s[:, :NSA_HEADS * HEAD_DIM], o_s[:, NSA_HEADS * HEAD_DIM:], st1, st2,
                              wo, g2, wg, wu, cwf, cbf, wd, seq=dec_seq)
        outs[1].append(nsa_rows.reshape(dec_batch, dec_seq, 4, NSA_KV, HEAD_DIM))
        outs[3].append(dif_rows.reshape(dec_batch, dec_seq, 2, DIFF_HEADS, HEAD_DIM))
        outs[5].append(jnp.moveaxis(new_state.reshape(dec_batch, 2, NSA_KV, HEAD_DIM, w_buf), -1, 1))
        outs[7].append(g_s.reshape(dec_batch, dec_seq, d_ff)[:, dec_seq - 2:])

    outs[0] = jnp.moveaxis(nsa_t.reshape(depth, batch, 4, NSA_KV, HEAD_DIM, seq), -1, 2)
    outs[2] = jnp.moveaxis(dif_t.reshape(depth, batch, 2, DIFF_HEADS, HEAD_DIM, seq), -1, 2)
    stacked = [o if not isinstance(o, list) else jnp.stack(o) for o in outs]
    return (xp.reshape(batch, seq, d_model), xs.reshape(dec_batch, dec_seq, d_model), *stacked)
```

```python
import functools
import math

import jax
import jax.numpy as jnp
import numpy as np
from jax import lax
from jax.experimental import pallas as pl
from jax.experimental.pallas import tpu as pltpu

F32 = jnp.float32
BF16 = jnp.bfloat16

HEAD_DIM = 64
NSA_KV = 2
NSA_GROUP = 4
NSA_HEADS = NSA_KV * NSA_GROUP
DIFF_HEADS = 8
DIFF_HALF = HEAD_DIM // 2
L_CMP = 32
L_SEL = 64
N_SELECT = 16
WINDOW = 512
FORCE_BONUS = 1.0e4
EPS = 1e-6
NEG = -1e30
UNDERFLOW_LOG2 = -160.0
NORM_SLACK = 1.02
LOG2E = math.log2(math.e)

LANES = 128
KEY_TILE = 512
Q_LANES = 128
DIFF_TQ = 512
WIN_ROWS = WINDOW + Q_LANES
AUG0 = HEAD_DIM
N_AUG = 6
VMEM_LIMIT = 56 * 1024 * 1024

C_Q, C_NSA, C_WIN, C_DQ, C_DK, C_DV, C_GATE = 0, 512, 1024, 1280, 1792, 2304, 2816
N_COL = 2944


def _nt(a, b):
    return lax.dot_general(a, b, (((1,), (1,)), ((), ())), preferred_element_type=F32)


def _tn(a, b):
    return lax.dot_general(a, b, (((0,), (0,)), ((), ())), preferred_element_type=F32)


def _mm(a, b):
    return jnp.dot(a, b, preferred_element_type=F32)


def _group_meansq(z, smat, inv_n):
    zz = z * z
    hi = zz.astype(BF16)
    lo = (zz - hi.astype(F32)).astype(BF16)
    return (_mm(hi, smat) + _mm(lo, smat)) * inv_n


def _pos_rows(pos, lane):
    is_aug = (lane >= AUG0) & (lane < AUG0 + N_AUG)
    val = jnp.where((lane & 1) == 0, pos >> 6, pos & 63)
    return jnp.where(is_aug, val, 0).astype(F32)


def _split_heads(slab, fill, lane):
    even = jnp.where(lane < HEAD_DIM, slab, fill)
    odd = jnp.where(lane < HEAD_DIM, pltpu.roll(slab, HEAD_DIM, 1), fill)
    return even, odd


def _proj_kernel(x_ref, g1_ref, w_ref, gain_ref, s64_ref, s32_ref, qcn_ref, qcd_ref, cw_ref, *rest,
                 seq_len, pos_base, fmajor):
    nsa_ref, win_ref, dif_ref, qa_ref, ksa_ref, kwa_ref, gt_ref, dqa_ref, dka_ref, cr_ref = rest[-10:]
    tr = x_ref.shape[0]

    def put(ref, c0, slab):
        if fmajor:
            ref[(0,) * (len(ref.shape) - 2) + (slice(c0, c0 + LANES), slice(None))] = slab.T
        else:
            ref[:, c0:c0 + LANES] = slab
    i = pl.program_id(0)
    x = x_ref[...]
    h = x * lax.rsqrt(jnp.mean(x * x, axis=-1, keepdims=True) + EPS) * g1_ref[...]
    hb = h.astype(BF16)

    lane = lax.broadcasted_iota(jnp.int32, (tr, LANES), 1)
    row = lax.broadcasted_iota(jnp.int32, (tr, LANES), 0)
    pos = pos_base + (i * tr + row) % seq_len
    prow = _pos_rows(pos, lane)
    s64 = s64_ref[...]
    s32 = s32_ref[...]

    def seg(c0, width):
        return _mm(hb, w_ref[:, c0:c0 + width])

    def normed(z, c0, smat, inv_n):
        return z * lax.rsqrt(_group_meansq(z, smat, inv_n) + EPS) * gain_ref[:, c0:c0 + LANES]

    zq = seg(C_Q, 512)
    for s in range(4):
        zn = normed(zq[:, s * LANES:(s + 1) * LANES], C_Q + s * LANES, s64, 1.0 / HEAD_DIM) * (HEAD_DIM ** -0.5 * LOG2E)
        for par in range(2):
            hd = 2 * s + par
            src = zn if par == 0 else pltpu.roll(zn, HEAD_DIM, 1)
            qa = jnp.where(lane < HEAD_DIM, src, qcn_ref[hd:hd + 1, :])
            qa_ref[hd // NSA_GROUP, hd % NSA_GROUP] = qa.astype(BF16)

    zc = seg(C_NSA, 512)
    ks = normed(zc[:, 256:384], C_NSA + 256, s64, 1.0 / HEAD_DIM)
    put(nsa_ref, 0, zc[:, 0:128])
    put(nsa_ref, 128, zc[:, 128:256])
    put(nsa_ref, 256, ks)
    put(nsa_ref, 384, zc[:, 384:512])
    k0, k1 = _split_heads(ks, prow, lane)
    ksa_ref[0] = k0.astype(BF16)
    ksa_ref[1] = k1.astype(BF16)
    craw = zc[:, 0:256].reshape(tr // L_CMP, L_CMP, 256) * cw_ref[...][None]
    cr_ref[...] = jnp.sum(craw, axis=1)

    zw = seg(C_WIN, 256)
    kw = normed(zw[:, 0:128], C_WIN, s64, 1.0 / HEAD_DIM)
    put(win_ref, 0, kw)
    put(win_ref, 128, zw[:, 128:256])
    k0, k1 = _split_heads(kw, prow, lane)
    kwa_ref[0] = k0.astype(BF16)
    kwa_ref[1] = k1.astype(BF16)

    zdq = seg(C_DQ, 512)
    for s in range(4):
        zn = normed(zdq[:, s * LANES:(s + 1) * LANES], C_DQ + s * LANES, s32, 1.0 / DIFF_HALF) * (DIFF_HALF ** -0.5 * LOG2E)
        for par in range(2):
            hd = 2 * s + par
            src = zn if par == 0 else pltpu.roll(zn, HEAD_DIM, 1)
            fill = qcd_ref[hd:hd + 1, :]
            dqa_ref[hd, 0] = jnp.where(lane < DIFF_HALF, src, fill).astype(BF16)
            dqa_ref[hd, 1] = jnp.where((lane >= DIFF_HALF) & (lane < HEAD_DIM), src, fill).astype(BF16)

    zdk = seg(C_DK, 512)
    for s in range(4):
        kn = normed(zdk[:, s * LANES:(s + 1) * LANES], C_DK + s * LANES, s32, 1.0 / DIFF_HALF)
        put(dif_ref, s * LANES, kn)
        k0, k1 = _split_heads(kn, prow, lane)
        dka_ref[2 * s] = k0.astype(BF16)
        dka_ref[2 * s + 1] = k1.astype(BF16)
    zdv = seg(C_DV, 512)
    for s in range(4):
        put(dif_ref, 512 + s * LANES, zdv[:, s * LANES:(s + 1) * LANES])

    zg = seg(C_GATE, LANES)
    sg = 1.0 / (1.0 + jnp.exp(-zg))
    gt_ref[...] = sg.T[0:2 * 16].reshape(NSA_KV, 16, tr)


def _proj(x2d, g1, w, gain, s64, s32, qcn, qcd, cw, *, seq_len, pos_base, stacked=None, tr=512):
    n = x2d.shape[0]
    tr = min(tr, n)
    assert n % tr == 0 and tr % L_CMP == 0
    fmajor = stacked is not None
    kern = functools.partial(_proj_kernel, seq_len=seq_len, pos_base=pos_base, fmajor=fmajor)
    full = lambda shape: pl.BlockSpec(shape, lambda i: (0,) * len(shape))
    if fmajor:
        layer, depth, prev = stacked
        assert seq_len % tr == 0
        tps, nb = seq_len // tr, n // seq_len
        cache_shapes = [jax.ShapeDtypeStruct((depth, nb, 512, seq_len), F32),
                        jax.ShapeDtypeStruct((nb, 256, seq_len), F32),
                        jax.ShapeDtypeStruct((depth, nb, 1024, seq_len), F32)]
        cache_specs = [pl.BlockSpec((1, 1, 512, tr), lambda i: (layer, i // tps, 0, i % tps)),
                       pl.BlockSpec((1, 256, tr), lambda i: (i // tps, 0, i % tps)),
                       pl.BlockSpec((1, 1, 1024, tr), lambda i: (layer, i // tps, 0, i % tps))]
    else:
        prev = None
        cache_shapes = [jax.ShapeDtypeStruct((n, w_), F32) for w_ in (512, 256, 1024)]
        cache_specs = [pl.BlockSpec((tr, w_), lambda i: (i, 0)) for w_ in (512, 256, 1024)]
    out_shape = tuple(cache_shapes) + (
        jax.ShapeDtypeStruct((NSA_KV, NSA_GROUP, n, LANES), BF16),
        jax.ShapeDtypeStruct((NSA_KV, n, LANES), BF16),
        jax.ShapeDtypeStruct((NSA_KV, n, LANES), BF16),
        jax.ShapeDtypeStruct((NSA_KV, 16, n), F32),
        jax.ShapeDtypeStruct((DIFF_HEADS, 2, n, LANES), BF16),
        jax.ShapeDtypeStruct((DIFF_HEADS, n, LANES), BF16),
        jax.ShapeDtypeStruct((n // L_CMP, 256), F32),
    )
    out_specs = tuple(cache_specs) + (
        pl.BlockSpec((NSA_KV, NSA_GROUP, tr, LANES), lambda i: (0, 0, i, 0)),
        pl.BlockSpec((NSA_KV, tr, LANES), lambda i: (0, i, 0)),
        pl.BlockSpec((NSA_KV, tr, LANES), lambda i: (0, i, 0)),
        pl.BlockSpec((NSA_KV, 16, tr), lambda i: (0, 0, i)),
        pl.BlockSpec((DIFF_HEADS, 2, tr, LANES), lambda i: (0, 0, i, 0)),
        pl.BlockSpec((DIFF_HEADS, tr, LANES), lambda i: (0, i, 0)),
        pl.BlockSpec((tr // L_CMP, 256), lambda i: (i, 0)),
    )
    in_specs = [
        pl.BlockSpec((tr, x2d.shape[1]), lambda i: (i, 0)),
        full(g1.shape), full(w.shape), full(gain.shape), full(s64.shape), full(s32.shape),
        full(qcn.shape), full(qcd.shape), full(cw.shape),
    ]
    args = [x2d, g1, w, gain, s64, s32, qcn, qcd, cw]
    aliases = {}
    if prev is not None:
        in_specs += [pl.BlockSpec(memory_space=pl.ANY)] * 2
        aliases = {len(args): 0, len(args) + 1: 2}
        args += list(prev)
    return pl.pallas_call(
        kern, grid=(n // tr,), in_specs=in_specs, out_specs=out_specs, out_shape=out_shape,
        input_output_aliases=aliases,
        compiler_params=pltpu.CompilerParams(dimension_semantics=("arbitrary",), vmem_limit_bytes=VMEM_LIMIT),
        name="proj",
    )(*args)


def _cmpfin_kernel(raw_ref, s64_ref, gain_ref, kc_ref, vc_ref, k_scr, v_scr, *, nsp):
    lane = lax.broadcasted_iota(jnp.int32, (nsp, LANES), 1)
    n = lax.broadcasted_iota(jnp.int32, (nsp, LANES), 0)
    k_scr[...] = raw_ref[0, :, 0:128]
    v_scr[...] = raw_ref[0, :, 128:256]
    for j in range(2):
        k = k_scr[pl.ds(j, nsp, stride=2), :]
        kn = k * lax.rsqrt(_group_meansq(k, s64_ref[...], 1.0 / HEAD_DIM) + EPS) * gain_ref[...]
        cend = n * L_SEL + (L_CMP - 1 + L_CMP * j)
        prow = _pos_rows(cend, lane)
        k0, k1 = _split_heads(kn, prow, lane)
        kc_ref[0, 0, j * nsp:(j + 1) * nsp, :] = k0.astype(BF16)
        kc_ref[1, 0, j * nsp:(j + 1) * nsp, :] = k1.astype(BF16)
        vc_ref[0, j * nsp:(j + 1) * nsp, :] = v_scr[pl.ds(j, nsp, stride=2), :].astype(BF16)


def _cmpfin(raw, s64, gain_kc):
    bx, n2, _ = raw.shape
    nsp = n2 // 2
    assert nsp % 16 == 0
    return pl.pallas_call(
        functools.partial(_cmpfin_kernel, nsp=nsp),
        grid=(bx,),
        in_specs=[pl.BlockSpec((1, n2, 256), lambda b: (b, 0, 0)),
                  pl.BlockSpec(s64.shape, lambda b: (0, 0)),
                  pl.BlockSpec(gain_kc.shape, lambda b: (0, 0))],
        out_specs=(pl.BlockSpec((NSA_KV, 1, n2, LANES), lambda b: (0, b, 0, 0)),
                   pl.BlockSpec((1, n2, LANES), lambda b: (b, 0, 0))),
        out_shape=(jax.ShapeDtypeStruct((NSA_KV, bx, n2, LANES), BF16),
                   jax.ShapeDtypeStruct((bx, n2, LANES), BF16)),
        scratch_shapes=[pltpu.VMEM((n2, LANES), F32), pltpu.VMEM((n2, LANES), F32)],
        name="cmpfin",
    )(raw, s64, gain_kc)


ACC_ROWS = HEAD_DIM + 16


def _flash_init(c, m_ref, acc_ref):
    m_ref[c] = jnp.full(m_ref.shape[1:], NEG, F32)
    acc_ref[c] = jnp.zeros(acc_ref.shape[1:], F32)


def _flash_update(c, s, vt_tile, m_ref, acc_ref):
    m_old = m_ref[c]
    m_new = jnp.maximum(m_old, jnp.max(s, axis=0, keepdims=True))
    p = jnp.exp2(s - m_new).astype(BF16)
    m_ref[c] = m_new
    vt1 = jnp.concatenate([vt_tile, jnp.ones((ACC_ROWS - HEAD_DIM, vt_tile.shape[1]), BF16)], axis=0)
    acc_ref[c] = jnp.exp2(m_old - m_new) * acc_ref[c] + _mm(vt1, p)


def _max_key_norm(k_ref_2d, n_keys):
    feat = lax.broadcasted_iota(jnp.int32, (1, LANES), 1) < HEAD_DIM

    def chunk(c, best):
        kf = k_ref_2d[pl.ds(pl.multiple_of(c * KEY_TILE, KEY_TILE), KEY_TILE), :].astype(F32)
        sq = jnp.sum(jnp.where(feat, kf * kf, 0.0), axis=1, keepdims=True)
        return jnp.maximum(best, jnp.max(sq, axis=0, keepdims=True))

    return jnp.sqrt(lax.fori_loop(0, n_keys // KEY_TILE, chunk, jnp.zeros((1, 1), F32)))


def _first_needed_tile(qa, m, k_norm, slope_row, n_max):
    feat = lax.broadcasted_iota(jnp.int32, (1, LANES), 1) < HEAD_DIM
    qf = qa.astype(F32)
    qsq = jnp.where(feat, qf * qf, 0.0)
    hi = qsq.astype(BF16)
    lo = (qsq - hi.astype(F32)).astype(BF16)
    ones = jnp.ones((8, LANES), BF16)
    q_norm = jnp.sqrt((_nt(ones, hi) + _nt(ones, lo))[0:1])
    cutoff = (m + UNDERFLOW_LOG2 - NORM_SLACK * q_norm * k_norm) / slope_row
    tile = jnp.floor(jnp.min(cutoff, axis=1, keepdims=True) * (1.0 / KEY_TILE))
    return jnp.clip(tile, 0.0, n_max.astype(F32)).astype(jnp.int32)[0, 0]


def _sweep_earlier_tiles(first, n_tiles, step):
    both = (0, 1)

    @pl.when(jnp.logical_and(n_tiles % 2 == 1, jnp.minimum(first[0], first[1]) < n_tiles))
    def _():
        step(pl.multiple_of((n_tiles - 1) * KEY_TILE, KEY_TILE), KEY_TILE, both)

    j_end = n_tiles // 2
    j0 = [jnp.minimum(f // 2, j_end) for f in first]
    j_both = jnp.maximum(j0[0], j0[1])

    def run(chains):
        def body(j, carry):
            step(pl.multiple_of(j * 2 * KEY_TILE, 2 * KEY_TILE), 2 * KEY_TILE, chains)
            return carry
        return body

    lax.fori_loop(j0[0], j_both, run((0,)), 0)
    lax.fori_loop(j0[1], j_both, run((1,)), 0)
    lax.fori_loop(j_both, j_end, run(both), 0)


def _masked_softmax_cols(s, valid):
    sm = jnp.where(valid, s, NEG)
    m = jnp.max(sm, axis=0, keepdims=True)
    p = jnp.where(valid, jnp.exp2(sm - m), 0.0)
    l = jnp.sum(p, axis=0, keepdims=True)
    return p * jnp.where(l > 0.0, 1.0 / l, 0.0)


def _nsa_kernel(q_ref, kc_ref, vc_ref, ks_ref, vs_ref, kw_ref, vw_ref, gt_ref, o_ref,
                sel_ref, acc_ref, m_ref, kn_ref, *, q_base, nsp, tw, win_base):
    i = pl.program_id(1)
    q0 = q_base + i * Q_LANES
    ncol = NSA_GROUP * Q_LANES
    groups = range(NSA_KV)
    qas = [q_ref[g].reshape(ncol, LANES) for g in groups]
    lane_c = lax.broadcasted_iota(jnp.int32, (1, ncol), 1)
    qpos_c = q0 + (lane_c & (Q_LANES - 1))
    rows_of = lambda g: slice(g * HEAD_DIM, (g + 1) * HEAD_DIM)

    n_row = lax.broadcasted_iota(jnp.int32, (nsp, ncol), 0)
    valid_e = (n_row * L_SEL + (L_CMP - 1)) <= qpos_c
    valid_o = (n_row * L_SEL + (2 * L_CMP - 1)) <= qpos_c
    blk = lax.broadcasted_iota(jnp.int32, (nsp, Q_LANES), 0)
    qpos_q = q0 + lax.broadcasted_iota(jnp.int32, (nsp, Q_LANES), 1)
    cur = qpos_q >> 6
    assert FORCE_BONUS > NSA_GROUP
    forced = jnp.where((blk == 0) | (blk == cur) | (blk == cur - 1), 1.0, 0.0)
    in_range = blk <= cur
    sel_forced = jnp.where(in_range, forced, 0.0)
    rankable = jnp.where(in_range, 1.0 - forced, 0.0) > 0.5
    blk_f = blk.astype(F32)
    o_c = []
    for g in groups:
        s_c = _nt(kc_ref[g, 0], qas[g])
        sm_e = jnp.where(valid_e, s_c[0:nsp], NEG)
        sm_o = jnp.where(valid_o, s_c[nsp:2 * nsp], NEG)
        m = jnp.maximum(jnp.max(sm_e, axis=0, keepdims=True), jnp.max(sm_o, axis=0, keepdims=True))
        p_e = jnp.where(valid_e, jnp.exp2(sm_e - m), 0.0)
        p_o = jnp.where(valid_o, jnp.exp2(sm_o - m), 0.0)
        l = jnp.sum(p_e, axis=0, keepdims=True) + jnp.sum(p_o, axis=0, keepdims=True)
        inv = jnp.where(l > 0.0, 1.0 / l, 0.0)
        p_e = p_e * inv
        p_o = p_o * inv
        pcat = jnp.concatenate([p_e, p_o], axis=0).astype(BF16)
        o_c.append(_tn(vc_ref[0], pcat)[rows_of(g)])

        imp = p_e[:, 0:Q_LANES] + p_o[:, 0:Q_LANES]
        for r in range(1, NSA_GROUP):
            imp = imp + p_e[:, r * Q_LANES:(r + 1) * Q_LANES] + p_o[:, r * Q_LANES:(r + 1) * Q_LANES]
        score = jnp.where(rankable, imp, -jnp.inf)
        sel = sel_forced
        for _ in range(N_SELECT - 3):
            top = jnp.max(score, axis=0, keepdims=True)
            first = jnp.min(jnp.where(score == top, blk_f, 1e9), axis=0, keepdims=True)
            pick = blk_f == first
            sel = jnp.where(pick, jnp.where(top > -jnp.inf, 1.0, sel), sel)
            score = jnp.where(pick, -jnp.inf, score)
        sel_ref[g] = sel

    @pl.when(i == 0)
    def _():
        for g in groups:
            kn_ref[g] = jnp.broadcast_to(_max_key_norm(ks_ref.at[g, 0], ks_ref.shape[2]), kn_ref.shape[1:])

    for g in groups:
        _flash_init(g, m_ref, acc_ref)
    def sel_tile(k0, size, which):
        n_blk = size // L_SEL
        causal = (k0 + lax.broadcasted_iota(jnp.int32, (size, Q_LANES), 0)
                  <= q0 + lax.broadcasted_iota(jnp.int32, (size, Q_LANES), 1))
        for g in which:
            vt = vs_ref[0, 0, rows_of(g), pl.ds(k0, size)].astype(BF16)
            s = _nt(ks_ref[g, 0, pl.ds(k0, size), :], qas[g])
            sel_rows = sel_ref[g, pl.ds(pl.multiple_of(k0 // L_SEL, 8), n_blk), :]
            selt = jnp.concatenate(
                [jnp.broadcast_to(sel_rows[j:j + 1, :], (L_SEL, Q_LANES)) for j in range(n_blk)], axis=0)
            valid = jnp.where(causal, selt, 0.0) > 0.0
            s = jnp.concatenate(
                [jnp.where(valid, s[:, r * Q_LANES:(r + 1) * Q_LANES], NEG) for r in range(NSA_GROUP)], axis=1)
            _flash_update(g, s, vt, m_ref, acc_ref)

    n_before = (q0 + Q_LANES + KEY_TILE - 1) // KEY_TILE - 1
    k_own = pl.multiple_of(n_before * KEY_TILE, KEY_TILE)
    used = q0 - n_before * KEY_TILE + Q_LANES
    for size in range(Q_LANES, KEY_TILE + 1, Q_LANES):
        @pl.when(used == size)
        def _(size=size):
            sel_tile(k_own, size, groups)

    first = []
    for g in groups:
        slope = jnp.concatenate([jnp.full((1, Q_LANES), LOG2E * 2.0 ** -(g * NSA_GROUP + r + 1), F32)
                                 for r in range(NSA_GROUP)], axis=1)
        first.append(_first_needed_tile(qas[g], m_ref[g], kn_ref[g][0:1, 0:1], slope, n_before))
    _sweep_earlier_tiles(first, n_before, sel_tile)

    w0 = jnp.clip(q0 - win_base - WINDOW, 0, tw - WIN_ROWS)
    w0 = pl.multiple_of(w0, Q_LANES)
    kpos = win_base + w0 + lax.broadcasted_iota(jnp.int32, (WIN_ROWS, ncol), 0)
    dist = qpos_c - kpos
    in_window = (dist >= 0) & (dist < WINDOW)

    outs = []
    for g in groups:
        l_s = acc_ref[g, HEAD_DIM:HEAD_DIM + 1, :]
        o_s = acc_ref[g, 0:HEAD_DIM, :] * jnp.where(l_s > 0.0, 1.0 / l_s, 0.0)
        p_w = _masked_softmax_cols(_nt(kw_ref[g, 0, pl.ds(w0, WIN_ROWS), :], qas[g]), in_window)
        o_w = _mm(vw_ref[0, rows_of(g), pl.ds(w0, WIN_ROWS)].astype(BF16), p_w.astype(BF16))
        gt = gt_ref[g]
        for r in range(NSA_GROUP):
            cs = slice(r * Q_LANES, (r + 1) * Q_LANES)
            outs.append(gt[r:r + 1, :] * o_c[g][:, cs] + gt[4 + r:5 + r, :] * o_s[:, cs]
                        + gt[8 + r:9 + r, :] * o_w[:, cs])
    o_ref[...] = jnp.concatenate(outs, axis=0).T


def _nsa(qa, kc, vc, ksa, nsa_t, kwa, win_t, gt, *, layer, bx, nqb, q_base, win_base):
    nq = qa.shape[2]
    n2 = kc.shape[2]
    tk = ksa.shape[2]
    tw = kwa.shape[2]
    assert nq == bx * nqb * Q_LANES and tk % KEY_TILE == 0 and tw >= WIN_ROWS
    assert nsa_t.shape[2:] == (512, tk) and win_t.shape[1:] == (256, tw)
    kern = functools.partial(_nsa_kernel, q_base=q_base, nsp=n2 // 2, tw=tw, win_base=win_base)
    ncol = NSA_GROUP * Q_LANES
    return pl.pallas_call(
        kern, grid=(bx, nqb),
        in_specs=[
            pl.BlockSpec((NSA_KV, NSA_GROUP, Q_LANES, LANES), lambda b, i: (0, 0, b * nqb + i, 0)),
            pl.BlockSpec((NSA_KV, 1, n2, LANES), lambda b, i: (0, b, 0, 0)),
            pl.BlockSpec((1, n2, LANES), lambda b, i: (b, 0, 0)),
            pl.BlockSpec((NSA_KV, 1, tk, LANES), lambda b, i: (0, b, 0, 0)),
            pl.BlockSpec((1, 1, LANES, tk), lambda b, i: (layer, b, 3, 0)),
            pl.BlockSpec((NSA_KV, 1, tw, LANES), lambda b, i: (0, b, 0, 0)),
            pl.BlockSpec((1, LANES, tw), lambda b, i: (b, 1, 0)),
            pl.BlockSpec((NSA_KV, 16, Q_LANES), lambda b, i: (0, 0, b * nqb + i)),
        ],
        out_specs=pl.BlockSpec((Q_LANES, NSA_HEADS * HEAD_DIM), lambda b, i: (b * nqb + i, 0)),
        out_shape=jax.ShapeDtypeStruct((nq, NSA_HEADS * HEAD_DIM), F32),
        scratch_shapes=[pltpu.VMEM((NSA_KV, n2 // 2, Q_LANES), F32), pltpu.VMEM((NSA_KV, ACC_ROWS, ncol), F32),
                        pltpu.VMEM((NSA_KV, 1, ncol), F32),
                        pltpu.VMEM((NSA_KV, 8, LANES), F32)],
        compiler_params=pltpu.CompilerParams(dimension_semantics=("arbitrary",) * 2, vmem_limit_bytes=VMEM_LIMIT),
        name="nsa",
    )(qa, kc, vc, ksa, nsa_t, kwa, win_t, gt)


def _diff_kernel(q_ref, k_ref, v_ref, sl_ref, lp_ref, sg_ref, o_ref, acc_ref, m_ref, kn_ref, *,
                 q_base, tq, lam_init):
    pair = pl.program_id(1)
    i = pl.program_id(2)
    q0 = q_base + i * tq
    ncol = 2 * tq
    heads = range(2)
    lp = lp_ref[...]
    lam = (jnp.exp(jnp.sum(lp[0:1] * lp[1:2], keepdims=True)) - jnp.exp(jnp.sum(lp[2:3] * lp[3:4], keepdims=True))
           + lam_init)
    n_full = q0 // KEY_TILE
    qas = [q_ref[hh].reshape(ncol, LANES) for hh in heads]
    for hh in heads:
        _flash_init(hh, m_ref, acc_ref)

    @pl.when(i == 0)
    def _():
        for hh in heads:
            kn_ref[hh] = jnp.broadcast_to(_max_key_norm(k_ref.at[hh, 0], k_ref.shape[2]), kn_ref.shape[1:])

    def step(k0, size, which, causal=None):
        for hh in which:
            vt = v_ref[0, 0, hh * HEAD_DIM:(hh + 1) * HEAD_DIM, pl.ds(k0, size)].astype(BF16)
            s = _nt(k_ref[hh, 0, pl.ds(k0, size), :], qas[hh])
            if causal is not None:
                s = jnp.where(causal, s, NEG)
            _flash_update(hh, s, vt, m_ref, acc_ref)

    k_diag = pl.multiple_of(n_full * KEY_TILE, KEY_TILE)

    tok = k_diag + lax.broadcasted_iota(jnp.int32, (KEY_TILE, ncol), 0)
    step(k_diag, KEY_TILE, heads, tok <= q0 + lax.broadcasted_iota(jnp.int32, (KEY_TILE, ncol), 1) % tq)

    first = []
    for hh in heads:
        slope = jnp.concatenate([sl_ref[pl.ds(2 * pair + hh, 1), :]] * (ncol // LANES), axis=1)
        first.append(_first_needed_tile(qas[hh], m_ref[hh], kn_ref[hh][0:1, 0:1], slope, n_full))
    _sweep_earlier_tiles(first, n_full, step)

    outs = []
    for hh in heads:
        o = acc_ref[hh, 0:HEAD_DIM, :] / acc_ref[hh, HEAD_DIM:HEAD_DIM + 1, :]
        d = o[:, 0:tq] - lam * o[:, tq:ncol]
        d = d * lax.rsqrt(jnp.mean(d * d, axis=0, keepdims=True) + EPS) * sg_ref[:, 0:tq] * (1.0 - lam_init)
        outs.append(d)
    o_ref[...] = jnp.concatenate(outs, axis=0).T


def _diff(dqa, dka, dif_t, slopes, lp, sg, *, layer, bx, nqb, tq, q_base, lam_init):
    nq = dqa.shape[2]
    tk = dka.shape[2]
    assert nq == bx * nqb * tq and tk % KEY_TILE == 0 and KEY_TILE % tq == 0 and dif_t.shape[2:] == (1024, tk)
    kern = functools.partial(_diff_kernel, q_base=q_base, tq=tq, lam_init=lam_init)
    return pl.pallas_call(
        kern, grid=(bx, DIFF_HEADS // 2, nqb),
        in_specs=[
            pl.BlockSpec((2, 2, tq, LANES), lambda b, h, i: (h, 0, b * nqb + i, 0)),
            pl.BlockSpec((2, 1, tk, LANES), lambda b, h, i: (h, b, 0, 0)),
            pl.BlockSpec((1, 1, LANES, tk), lambda b, h, i: (layer, b, 4 + h, 0)),
            pl.BlockSpec(slopes.shape, lambda b, h, i: (0, 0)),
            pl.BlockSpec(lp.shape, lambda b, h, i: (0, 0)),
            pl.BlockSpec(sg.shape, lambda b, h, i: (0, 0)),
        ],
        out_specs=pl.BlockSpec((tq, LANES), lambda b, h, i: (b * nqb + i, h)),
        out_shape=jax.ShapeDtypeStruct((nq, DIFF_HEADS * HEAD_DIM), F32),
        scratch_shapes=[pltpu.VMEM((2, ACC_ROWS, 2 * tq), F32), pltpu.VMEM((2, 1, 2 * tq), F32),
                        pltpu.VMEM((2, 8, LANES), F32)],
        compiler_params=pltpu.CompilerParams(dimension_semantics=("arbitrary",) * 3, vmem_limit_bytes=VMEM_LIMIT),
        name="diff",
    )(dqa, dka, dif_t, slopes, lp, sg)


F_CHUNKS = 2


def _ffn_core(x, on, od, wo_ref, g2_ref, wg_ref, wu_ref, cw_ref, cb_ref, wd_ref, prev_rows, g_store):
    o = jnp.concatenate([on, od], axis=1).astype(BF16)
    xm = x + _mm(o, wo_ref[...])
    h2 = (xm * lax.rsqrt(jnp.mean(xm * xm, axis=-1, keepdims=True) + EPS) * g2_ref[...]).astype(BF16)
    d_ff = wg_ref.shape[1]
    fc = d_ff // F_CHUNKS
    y = jnp.zeros(x.shape, F32)
    for c in range(F_CHUNKS):
        c0, c1 = c * fc, (c + 1) * fc
        g = _mm(h2, wg_ref[:, c0:c1])
        u = _mm(h2, wu_ref[:, c0:c1])
        gm1, gm2 = prev_rows(g, c0, c1)
        g_store(g, c0, c1)
        gc = cb_ref[:, c0:c1] + cw_ref[0:1, c0:c1] * gm2 + cw_ref[1:2, c0:c1] * gm1 + cw_ref[2:3, c0:c1] * g
        act = gc * (1.0 / (1.0 + jnp.exp(-gc))) * u
        y = y + _mm(act.astype(BF16), wd_ref[c0:c1, :])
    return xm + y


def _ffn_prompt_kernel(x_ref, on_ref, od_ref, wo_ref, g2_ref, wg_ref, wu_ref, cw_ref, cb_ref, wd_ref,
                       y_ref, cv_ref, carry_ref, *, tiles_per_seq):
    i = pl.program_id(0)
    tr = x_ref.shape[0]

    @pl.when(i % tiles_per_seq == 0)
    def _():
        carry_ref[...] = jnp.zeros(carry_ref.shape, F32)

    def prev_rows(g, c0, c1):
        row = lax.broadcasted_iota(jnp.int32, g.shape, 0)
        p1 = carry_ref[7:8, c0:c1]
        p2 = carry_ref[6:7, c0:c1]
        gm1 = jnp.where(row == 0, p1, pltpu.roll(g, 1, 0))
        gm2 = jnp.where(row == 0, p2, jnp.where(row == 1, p1, pltpu.roll(g, 2, 0)))
        return gm1, gm2

    def g_store(g, c0, c1):
        carry_ref[:, c0:c1] = g[tr - 8:tr]
        cv_ref[0, :, c0:c1] = g[tr - 8:tr]

    y_ref[...] = _ffn_core(x_ref[...], on_ref[...], od_ref[...], wo_ref, g2_ref, wg_ref, wu_ref, cw_ref, cb_ref,
                           wd_ref, prev_rows, g_store)


def _ffn_sample_kernel(x_ref, on_ref, od_ref, st1_ref, st2_ref, wo_ref, g2_ref, wg_ref, wu_ref, cw_ref, cb_ref,
                       wd_ref, y_ref, g_ref, *, seq):
    def prev_rows(g, c0, c1):
        rs = lax.broadcasted_iota(jnp.int32, g.shape, 0) % seq
        s1 = st1_ref[:, c0:c1]
        gm1 = jnp.where(rs == 0, s1, pltpu.roll(g, 1, 0))
        gm2 = jnp.where(rs == 0, st2_ref[:, c0:c1], jnp.where(rs == 1, s1, pltpu.roll(g, 2, 0)))
        return gm1, gm2

    def g_store(g, c0, c1):
        g_ref[:, c0:c1] = g

    y_ref[...] = _ffn_core(x_ref[...], on_ref[...], od_ref[...], wo_ref, g2_ref, wg_ref, wu_ref, cw_ref, cb_ref,
                           wd_ref, prev_rows, g_store)


def _const_spec(a):
    return pl.BlockSpec(a.shape, lambda i: (0,) * a.ndim, pipeline_mode=pl.Buffered(1))


def _ffn_prompt(x2d, on, od, wo, g2, wg, wu, cw, cb, wd, *, seq_len, tr=256):
    n, d = x2d.shape
    f = wg.shape[1]
    assert n % tr == 0 and seq_len % tr == 0
    tps = seq_len // tr
    row = lambda w: pl.BlockSpec((tr, w), lambda i: (i, 0))
    return pl.pallas_call(
        functools.partial(_ffn_prompt_kernel, tiles_per_seq=tps), grid=(n // tr,),
        in_specs=[row(d), row(on.shape[1]), row(od.shape[1])] + [_const_spec(a) for a in (wo, g2, wg, wu, cw, cb, wd)],
        out_specs=(row(d), pl.BlockSpec((1, 8, f), lambda i: (i // tps, 0, 0))),
        out_shape=(jax.ShapeDtypeStruct((n, d), F32), jax.ShapeDtypeStruct((n // seq_len, 8, f), F32)),
        scratch_shapes=[pltpu.VMEM((8, f), F32)],
        compiler_params=pltpu.CompilerParams(dimension_semantics=("arbitrary",), vmem_limit_bytes=VMEM_LIMIT),
        name="ffn_prompt",
    )(x2d, on, od, wo, g2, wg, wu, cw, cb, wd)


def _ffn_sample(x2d, on, od, st1, st2, wo, g2, wg, wu, cw, cb, wd, *, seq):
    n, d = x2d.shape
    f = wg.shape[1]
    args = (x2d, on, od, st1, st2, wo, g2, wg, wu, cw, cb, wd)
    return pl.pallas_call(
        functools.partial(_ffn_sample_kernel, seq=seq), grid=(1,),
        in_specs=[_const_spec(a) for a in args],
        out_specs=(pl.BlockSpec((n, d), lambda i: (0, 0)), pl.BlockSpec((n, f), lambda i: (0, 0))),
        out_shape=(jax.ShapeDtypeStruct((n, d), F32), jax.ShapeDtypeStruct((n, f), F32)),
        compiler_params=pltpu.CompilerParams(dimension_semantics=("arbitrary",), vmem_limit_bytes=VMEM_LIMIT),
        name="ffn_sample",
    )(*args)


PAGES_PER_STEP = 16
STEPS_PER_CMP_TILE = LANES // (PAGES_PER_STEP * LANES // L_CMP)
T_NSA, T_WIN, T_DIF, T_ROWS = 0, 512, 768, 1792


def _softmax_rows(s):
    m = jnp.max(s, axis=-1, keepdims=True)
    p = jnp.exp2(s - m)
    return p * (1.0 / jnp.sum(p, axis=-1, keepdims=True))


def _masked_softmax_rows(s, valid):
    sm = jnp.where(valid, s, NEG)
    m = jnp.max(sm, axis=-1, keepdims=True)
    p = jnp.where(valid, jnp.exp2(sm - m), 0.0)
    l = jnp.sum(p, axis=-1, keepdims=True)
    return p * jnp.where(l > 0.0, 1.0 / l, 0.0)


def _sattn_kernel(pt_ref, *refs, n_steps, past_len, dec, lam_init):
    pg = PAGES_PER_STEP
    nsa_pg, dif_pg = refs[0:pg], refs[pg:2 * pg]
    (win_ref, nn_ref, nw_ref, nd_ref, qn_ref, qd_ref, gate_ref, sn_ref, sd_ref, wck_ref, wcv_ref, gkc_ref,
     lp_ref, sg_ref, o_ref, ns_ref,
     ssel, sdif, psel, pdif, vsel, vdif, kc_scr, tail_scr, oc_scr, ow_scr) = refs[2 * pg:]
    s = pl.program_id(1)
    span = pg * LANES
    groups, heads = range(NSA_KV), range(DIFF_HEADS)
    cat = lambda parts: jnp.concatenate(parts, axis=1)
    hi_lo = lambda x: (x.astype(BF16), (x - x.astype(BF16).astype(F32)).astype(BF16))

    @pl.when(s < n_steps)
    def _():
        @pl.when(s == 0)
        def _():
            kc_scr[...] = jnp.zeros(kc_scr.shape, F32)

        shift = (s % STEPS_PER_CMP_TILE) * (span // L_CMP)
        col = pl.multiple_of((s // STEPS_PER_CMP_TILE) * LANES, LANES)
        for part, w_ref in ((0, wck_ref), (1, wcv_ref)):
            x_hi, x_lo = hi_lo(cat([r[0, 0, part * 128:(part + 1) * 128, :] for r in nsa_pg]))
            blk = _mm(x_hi, w_ref[0]) + _mm(x_lo, w_ref[0]) + _mm(x_hi, w_ref[1])
            kc_scr[part * 128:(part + 1) * 128, pl.ds(col, LANES)] += pltpu.roll(blk, shift, 1)

        k0 = pl.multiple_of(s * span, span)
        kpos = (k0 + lax.broadcasted_iota(jnp.int32, (1, span), 1)).astype(F32)
        for g in groups:
            k8 = cat([r[0, 0, 256 + g * HEAD_DIM:256 + (g + 1) * HEAD_DIM, :] for r in nsa_pg]).astype(BF16)
            ssel[g, :, pl.ds(k0, span)] = _mm(qn_ref[0, g], k8) + cat([sn_ref[g]] * pg) * kpos
        for h in heads:
            k8 = cat([r[0, 0, h * HEAD_DIM:(h + 1) * HEAD_DIM, :] for r in dif_pg]).astype(BF16)
            sdif[h, :, pl.ds(k0, span)] = _mm(qd_ref[0, h], k8) + cat([sd_ref[h]] * pg) * kpos
        vsel[:, pl.ds(k0, span)] = cat([r[0, 0, 384:512, :] for r in nsa_pg]).astype(BF16)
        vdif[:, pl.ds(k0, span)] = cat([r[0, 0, 512:1024, :] for r in dif_pg]).astype(BF16)

    @pl.when(s == n_steps)
    def _():
        new_rows = cat([nn_ref[...], nw_ref[...], nd_ref[...]])
        tail_scr[...] = jnp.concatenate([new_rows, jnp.zeros((LANES - dec, T_ROWS), F32)], axis=0).T
        nrow, drow = NSA_GROUP * dec, 2 * dec
        tail_pos = (past_len + lax.broadcasted_iota(jnp.int32, (1, LANES), 1)).astype(F32)
        causal32 = (lax.broadcasted_iota(jnp.int32, (nrow, LANES), 1)
                    <= lax.broadcasted_iota(jnp.int32, (nrow, LANES), 0) % dec)
        causal16 = (lax.broadcasted_iota(jnp.int32, (drow, LANES), 1)
                    <= lax.broadcasted_iota(jnp.int32, (drow, LANES), 0) % dec)

        for g in groups:
            kt = tail_scr[T_NSA + 256 + g * HEAD_DIM:T_NSA + 256 + (g + 1) * HEAD_DIM, :].astype(BF16)
            sc = _mm(qn_ref[0, g], kt) + sn_ref[g] * tail_pos
            ssel[g, :, past_len:past_len + LANES] = jnp.where(causal32, sc, NEG)
        for h in heads:
            kt = tail_scr[T_DIF + h * HEAD_DIM:T_DIF + (h + 1) * HEAD_DIM, :].astype(BF16)
            sc = _mm(qd_ref[0, h], kt) + sd_ref[h] * tail_pos
            sdif[h, :, past_len:past_len + LANES] = jnp.where(causal16, sc, NEG)

        n_cmp = kc_scr.shape[1]
        cend = lax.broadcasted_iota(jnp.int32, (nrow, n_cmp), 1) * L_CMP + (L_CMP - 1)
        qpos_c = past_len + lax.broadcasted_iota(jnp.int32, (nrow, n_cmp), 0) % dec
        imps = []
        for g in groups:
            kc = kc_scr[g * HEAD_DIM:(g + 1) * HEAD_DIM, :]
            kc = kc * lax.rsqrt(jnp.mean(kc * kc, axis=0, keepdims=True) + EPS) * cat([gkc_ref[...]] * (n_cmp // LANES))
            s_c = _mm(qn_ref[0, g], kc.astype(BF16)) + cat([sn_ref[g]] * (n_cmp // LANES)) * cend.astype(F32)
            p_c = _masked_softmax_rows(s_c, cend <= qpos_c)
            vc = kc_scr[128 + g * HEAD_DIM:128 + (g + 1) * HEAD_DIM, :].astype(BF16)
            oc_scr[g] = _nt(p_c.astype(BF16), vc)
            imps.append(sum(p_c[r * dec:(r + 1) * dec] for r in range(NSA_GROUP)))
        imp2 = jnp.concatenate(imps, axis=0)
        pair = (lax.broadcasted_iota(jnp.int32, (n_cmp, LANES), 0) // 2
                == lax.broadcasted_iota(jnp.int32, (n_cmp, LANES), 1)).astype(BF16)
        i_hi, i_lo = hi_lo(imp2)
        imp = _mm(i_hi, pair) + _mm(i_lo, pair)

        n_blk = past_len // L_SEL
        blk = lax.broadcasted_iota(jnp.int32, imp.shape, 1)
        blk_f = blk.astype(F32)
        bonus = jnp.where((blk == 0) | (blk == n_blk - 1), FORCE_BONUS, 0.0)
        score = jnp.where(blk < n_blk, imp + bonus, -jnp.inf)
        sel = jnp.zeros(imp.shape, F32)
        for _ in range(N_SELECT - 1):
            top = jnp.max(score, axis=1, keepdims=True)
            first = jnp.min(jnp.where(score == top, blk_f, 1e9), axis=1, keepdims=True)
            pick = blk_f == first
            sel = jnp.where(pick, jnp.where(top > -jnp.inf, 1.0, sel), sel)
            score = jnp.where(pick, -jnp.inf, score)
        sel = sel.astype(BF16)

        for c in range(n_steps):
            tok = c * span + lax.broadcasted_iota(jnp.int32, (LANES, span), 1)
            expand = ((tok >> 6) == lax.broadcasted_iota(jnp.int32, (LANES, span), 0)).astype(BF16)
            chosen = _mm(sel, expand)
            for g in groups:
                keep = jnp.concatenate([chosen[g * dec:(g + 1) * dec]] * NSA_GROUP, axis=0) > 0.5
                ssel[g, :, c * span:(c + 1) * span] = jnp.where(keep, ssel[g, :, c * span:(c + 1) * span], NEG)
        for g in groups:
            psel[g] = _softmax_rows(ssel[g]).astype(BF16)
        for h in heads:
            pdif[h] = _softmax_rows(sdif[h]).astype(BF16)

        w_buf = win_ref.shape[3]
        kwpos = past_len - w_buf + lax.broadcasted_iota(jnp.int32, (nrow, w_buf + LANES), 1)
        dist = past_len + lax.broadcasted_iota(jnp.int32, (nrow, w_buf + LANES), 0) % dec - kwpos
        in_window = (dist >= 0) & (dist < WINDOW)
        for g in groups:
            kw = cat([win_ref[0, 0, g * HEAD_DIM:(g + 1) * HEAD_DIM, :],
                      tail_scr[T_WIN + g * HEAD_DIM:T_WIN + (g + 1) * HEAD_DIM, :]]).astype(BF16)
            vw = cat([win_ref[0, 0, 128 + g * HEAD_DIM:128 + (g + 1) * HEAD_DIM, :],
                      tail_scr[T_WIN + 128 + g * HEAD_DIM:T_WIN + 128 + (g + 1) * HEAD_DIM, :]]).astype(BF16)
            s_w = _mm(qn_ref[0, g], kw) + cat([sn_ref[g]] * (w_buf // LANES + 1)) * kwpos.astype(F32)
            ow_scr[g] = _nt(_masked_softmax_rows(s_w, in_window).astype(BF16), vw)

        shifted = pltpu.roll(win_ref[0, 0], w_buf - dec, 1)
        fresh = cat([jnp.zeros((256, w_buf - LANES), F32), pltpu.roll(tail_scr[T_WIN:T_WIN + 256, :], LANES - dec, 1)])
        ns_ref[0] = jnp.where(lax.broadcasted_iota(jnp.int32, (256, w_buf), 1) >= w_buf - dec, fresh, shifted)

        lp = lp_ref[...]
        lam = (jnp.exp(jnp.sum(lp[0:1] * lp[1:2], keepdims=True))
               - jnp.exp(jnp.sum(lp[2:3] * lp[3:4], keepdims=True)) + lam_init)
        pieces = []
        for g in groups:
            vt = tail_scr[T_NSA + 384 + g * HEAD_DIM:T_NSA + 384 + (g + 1) * HEAD_DIM, :].astype(BF16)
            o_s = (_nt(psel[g, :, 0:past_len], vsel[g * HEAD_DIM:(g + 1) * HEAD_DIM, :])
                   + _nt(psel[g, :, past_len:past_len + LANES], vt))
            o = gate_ref[0, g, 0] * oc_scr[g] + gate_ref[0, g, 1] * o_s + gate_ref[0, g, 2] * ow_scr[g]
            pieces += [o[r * dec:(r + 1) * dec] for r in range(NSA_GROUP)]
        for h in heads:
            vt = tail_scr[T_DIF + 512 + h * HEAD_DIM:T_DIF + 512 + (h + 1) * HEAD_DIM, :].astype(BF16)
            o = (_nt(pdif[h, :, 0:past_len], vdif[h * HEAD_DIM:(h + 1) * HEAD_DIM, :])
                 + _nt(pdif[h, :, past_len:past_len + LANES], vt))
            d = o[0:dec] - lam * o[dec:2 * dec]
            pieces.append(d * lax.rsqrt(jnp.mean(d * d, axis=-1, keepdims=True) + EPS) * sg_ref[...] * (1.0 - lam_init))
        o_ref[0] = cat(pieces)


def _sattn(page_table, cache_nsa, cache_diff, state_win, new_nsa, new_win, new_dif, qn, qd, gates, sn, sd,
           wck, wcv, gkc, lp, sg, *, layer, lam_init):
    bs, n_pages = page_table.shape
    page = cache_nsa.shape[3]
    dec = new_nsa.shape[0] // bs
    past_len = n_pages * page
    pg = PAGES_PER_STEP
    n_steps = n_pages // pg
    w_buf = state_win.shape[3]
    assert page == LANES and n_pages % pg == 0 and dec == 8 and w_buf % LANES == 0
    n_cmp = -(-n_steps // STEPS_PER_CMP_TILE) * LANES
    tks = past_len + LANES
    pt = page_table.reshape(-1)

    def kpage(j):
        return lambda b, s, pt_ref: (layer, pt_ref[b * n_pages + jnp.minimum(s, n_steps - 1) * pg + j], 0, 0)

    const = lambda a: pl.BlockSpec(a.shape, lambda b, s, pt_ref: (0,) * a.ndim)
    per_b = lambda a: pl.BlockSpec((1,) + a.shape[1:], lambda b, s, pt_ref: (b,) + (0,) * (a.ndim - 1))
    in_specs = (
        [pl.BlockSpec((1, 1, 512, page), kpage(j)) for j in range(pg)]
        + [pl.BlockSpec((1, 1, 1024, page), kpage(j)) for j in range(pg)]
        + [pl.BlockSpec((1, 1, 256, w_buf), lambda b, s, pt_ref: (layer, b, 0, 0)),
           pl.BlockSpec((dec, 512), lambda b, s, pt_ref: (b, 0)),
           pl.BlockSpec((dec, 256), lambda b, s, pt_ref: (b, 0)),
           pl.BlockSpec((dec, 1024), lambda b, s, pt_ref: (b, 0)),
           per_b(qn), per_b(qd), per_b(gates), const(sn), const(sd), const(wck), const(wcv), const(gkc),
           const(lp), const(sg)])
    out_specs = (pl.BlockSpec((1, dec, 1024), lambda b, s, pt_ref: (b, 0, 0)),
                 pl.BlockSpec((1, 256, w_buf), lambda b, s, pt_ref: (b, 0, 0)))
    scratch = [
        pltpu.VMEM((NSA_KV, NSA_GROUP * dec, tks), F32), pltpu.VMEM((DIFF_HEADS, 2 * dec, tks), F32),
        pltpu.VMEM((NSA_KV, NSA_GROUP * dec, tks), BF16), pltpu.VMEM((DIFF_HEADS, 2 * dec, tks), BF16),
        pltpu.VMEM((NSA_KV * HEAD_DIM, past_len), BF16), pltpu.VMEM((DIFF_HEADS * HEAD_DIM, past_len), BF16),
        pltpu.VMEM((256, n_cmp), F32), pltpu.VMEM((T_ROWS, LANES), F32),
        pltpu.VMEM((NSA_KV, NSA_GROUP * dec, HEAD_DIM), F32), pltpu.VMEM((NSA_KV, NSA_GROUP * dec, HEAD_DIM), F32),
    ]
    grid_spec = pltpu.PrefetchScalarGridSpec(num_scalar_prefetch=1, grid=(bs, n_steps + 1), in_specs=in_specs,
                                             out_specs=out_specs, scratch_shapes=scratch)
    caches = [cache_nsa] * pg + [cache_diff] * pg
    return pl.pallas_call(
        functools.partial(_sattn_kernel, n_steps=n_steps, past_len=past_len, dec=dec, lam_init=lam_init),
        grid_spec=grid_spec,
        out_shape=(jax.ShapeDtypeStruct((bs, dec, 1024), F32), jax.ShapeDtypeStruct((bs, 256, w_buf), F32)),
        compiler_params=pltpu.CompilerParams(dimension_semantics=("arbitrary",) * 2, vmem_limit_bytes=VMEM_LIMIT),
        name="sattn",
    )(pt, *caches, state_win, new_nsa, new_win, new_dif, qn, qd, gates, sn, sd, wck, wcv, gkc, lp, sg)


def _alibi_slopes(n):
    return 2.0 ** (-8.0 * jnp.arange(1, n + 1, dtype=F32) / n)


def _slope_rows(slopes):
    slopes = slopes * LOG2E
    s0 = slopes.astype(BF16).astype(F32)
    s1 = (slopes - s0).astype(BF16).astype(F32)
    s2 = (slopes - s0 - s1).astype(BF16).astype(F32)
    cols = jnp.stack([64.0 * s0, s0, 64.0 * s1, s1, 64.0 * s2, s2], axis=1)
    out = jnp.zeros((slopes.shape[0], LANES), F32)
    return out.at[:, AUG0:AUG0 + N_AUG].set(cols)


def _block_ones(group):
    idx = np.arange(LANES) // group
    return jnp.asarray((idx[:, None] == idx[None, :]).astype(np.float32), dtype=BF16)


_GATE_SRC = np.full((LANES,), -1, np.int64)
for _g in range(NSA_KV):
    for _r in range(NSA_GROUP):
        for _j in range(3):
            _GATE_SRC[_g * 16 + _j * 4 + _r] = 1280 + _g * 12 + _r * 3 + _j


def _permute_w_in(w):
    main = jnp.concatenate([w[:, 0:1280], w[:, 1304:2840]], axis=1)
    gate = jnp.where(jnp.asarray(_GATE_SRC >= 0)[None, :], w[:, np.maximum(_GATE_SRC, 0)], 0.0)
    return jnp.concatenate([main, gate], axis=1).astype(BF16)


def _compress_weights(w):
    t = np.arange(PAGES_PER_STEP * LANES)
    place = jnp.asarray(t[:, None] // L_CMP == np.arange(LANES)[None, :])
    full = jnp.where(place, jnp.tile(w, PAGES_PER_STEP * LANES // L_CMP)[:, None], 0.0)
    hi = full.astype(BF16)
    return jnp.stack([hi, (full - hi.astype(F32)).astype(BF16)])


def _gain_row(nsa_qg, nsa_kg, diff_qg, diff_kg):
    one = lambda n: jnp.ones((n,), F32)
    parts = [jnp.tile(nsa_qg, NSA_HEADS), one(256), jnp.tile(nsa_kg[1], NSA_KV), one(128),
             jnp.tile(nsa_kg[2], NSA_KV), one(128), jnp.tile(diff_qg.reshape(-1), DIFF_HEADS),
             jnp.tile(diff_kg.reshape(-1), DIFF_HEADS), one(512 + LANES)]
    return jnp.concatenate(parts)[None, :]


def kernel(x_prompt, x_sample, cache_nsa, cache_diff, state_win, state_conv, page_table, norm1_g, norm2_g, w_in, w_out, nsa_qnorm_g, nsa_knorm_g, nsa_cmp_w, diff_qnorm_g, diff_knorm_g, diff_lambda, diff_subnorm_g, w_gate, w_up, conv_w, conv_b, w_down):
    batch, seq, d_model = x_prompt.shape
    dec_batch, dec_seq, _ = x_sample.shape
    depth, n_pool, page = cache_nsa.shape[:3]
    n_pages = page_table.shape[1]
    past_len = n_pages * page
    w_buf = state_win.shape[2]
    d_ff = w_gate.shape[2]
    assert dec_seq < L_CMP and seq % KEY_TILE == 0 and w_buf == WINDOW

    s64, s32 = _block_ones(HEAD_DIM), _block_ones(DIFF_HALF)
    qcn = _slope_rows(_alibi_slopes(NSA_HEADS))
    qcd = _slope_rows(_alibi_slopes(DIFF_HEADS))
    fmaj = lambda a: jnp.moveaxis(a, 2, -1).reshape(a.shape[0], a.shape[1], -1, a.shape[2])
    cache_nsa4 = fmaj(cache_nsa)
    cache_diff4 = fmaj(cache_diff)
    state_win4 = fmaj(state_win)

    sn = jnp.broadcast_to((_alibi_slopes(NSA_HEADS) * LOG2E).reshape(NSA_KV, NSA_GROUP, 1, 1),
                          (NSA_KV, NSA_GROUP, dec_seq, LANES)).reshape(NSA_KV, NSA_GROUP * dec_seq, LANES)
    sd = jnp.broadcast_to((_alibi_slopes(DIFF_HEADS) * LOG2E).reshape(DIFF_HEADS, 1, 1), (DIFF_HEADS, 2 * dec_seq, LANES))
    xp = x_prompt.reshape(batch * seq, d_model)
    xs = x_sample.reshape(dec_batch * dec_seq, d_model)

    nsa_t = jnp.zeros((depth, batch, 512, seq), F32)
    dif_t = jnp.zeros((depth, batch, 1024, seq), F32)
    outs = [[] for _ in range(8)]
    for l in range(depth):
        lam_init = 0.8 - 0.6 * math.exp(-0.3 * l)
        w_in_p = _permute_w_in(w_in[l])
        gain = _gain_row(nsa_qnorm_g[l], nsa_knorm_g[l], diff_qnorm_g[l], diff_knorm_g[l])
        gain_kc = jnp.tile(nsa_knorm_g[l, 0], NSA_KV)[None, :]
        cw = jnp.concatenate([jnp.broadcast_to(nsa_cmp_w[l, 0][:, None], (L_CMP, LANES)),
                              jnp.broadcast_to(nsa_cmp_w[l, 1][:, None], (L_CMP, LANES))], axis=1)
        lp = jnp.zeros((8, LANES), F32).at[0:4, 0:DIFF_HALF].set(diff_lambda[l])
        sg = jnp.broadcast_to(diff_subnorm_g[l][:, None], (HEAD_DIM, DIFF_TQ))
        sg_row = jnp.broadcast_to(diff_subnorm_g[l][None, :], (dec_seq, HEAD_DIM))
        gkc = jnp.broadcast_to(nsa_knorm_g[l, 0][:, None], (HEAD_DIM, LANES))
        wck, wcv = _compress_weights(nsa_cmp_w[l, 0]), _compress_weights(nsa_cmp_w[l, 1])
        g1 = norm1_g[l][None, :]
        g2 = norm2_g[l][None, :]
        wo, wg, wu, wd = (a.astype(BF16) for a in (w_out[l], w_gate[l], w_up[l], w_down[l]))
        cwf, cbf = conv_w[l], conv_b[l][None, :]
        proj = functools.partial(_proj, g1=g1, w=w_in_p, gain=gain, s64=s64, s32=s32, qcn=qcn, qcd=qcd, cw=cw)

        (nsa_t, win_t, dif_t, qa, ksa, kwa, gt, dqa, dka, craw) = proj(
            xp, seq_len=seq, pos_base=0, stacked=(l, depth, (nsa_t, dif_t)))
        kc, vc = _cmpfin(craw.reshape(batch, seq // L_CMP, 256), s64, gain_kc)
        o_nsa = _nsa(qa, kc, vc, ksa.reshape(NSA_KV, batch, seq, LANES), nsa_t,
                     kwa.reshape(NSA_KV, batch, seq, LANES), win_t, gt,
                     layer=l, bx=batch, nqb=seq // Q_LANES, q_base=0, win_base=0)
        o_dif = _diff(dqa, dka.reshape(DIFF_HEADS, batch, seq, LANES), dif_t, sd[:, 0], lp, sg,
                      layer=l, bx=batch, nqb=seq // DIFF_TQ, tq=DIFF_TQ, q_base=0, lam_init=lam_init)
        xp, cv = _ffn_prompt(xp, o_nsa, o_dif, wo, g2, wg, wu, cwf, cbf, wd, seq_len=seq)
        w_keep = min(WINDOW, seq)
        outs[4].append(jnp.moveaxis(win_t[:, :, seq - w_keep:].reshape(batch, 2, NSA_KV, HEAD_DIM, w_keep), -1, 1))
        outs[6].append(cv[:, 8 - 2:8])

        (nsa_rows, win_rows, dif_rows, qa, _, _, gt, dqa, _, _) = proj(xs, seq_len=dec_seq, pos_base=past_len)
        per_seq = lambda a, lead: a[..., :HEAD_DIM].reshape(lead + (dec_batch, dec_seq, HEAD_DIM))
        qn = jnp.transpose(per_seq(qa, (NSA_KV, NSA_GROUP)), (2, 0, 1, 3, 4)).reshape(
            dec_batch, NSA_KV, NSA_GROUP * dec_seq, HEAD_DIM)
        qd = jnp.transpose(per_seq(dqa, (DIFF_HEADS, 2)), (2, 0, 1, 3, 4)).reshape(
            dec_batch, DIFF_HEADS, 2 * dec_seq, HEAD_DIM)
        gates = jnp.transpose(gt.reshape(NSA_KV, 4, NSA_GROUP, dec_batch, dec_seq), (3, 0, 1, 2, 4)).reshape(
            dec_batch, NSA_KV, 4, NSA_GROUP * dec_seq, 1)
        gates = jnp.broadcast_to(gates, gates.shape[:-1] + (HEAD_DIM,))
        o_s, new_state = _sattn(page_table, cache_nsa4, cache_diff4, state_win4, nsa_rows, win_rows, dif_rows,
                                qn, qd, gates, sn, sd, wck, wcv, gkc, lp, sg_row, layer=l, lam_init=lam_init)
        o_s = o_s.reshape(dec_batch * dec_seq, 2 * NSA_HEADS * HEAD_DIM)
        st1 = jnp.repeat(state_conv[l][:, 1], dec_seq, axis=0)
        st2 = jnp.repeat(state_conv[l][:, 0], dec_seq, axis=0)
        xs, g_s = _ffn_sample(xs, o_s[:, :NSA_HEADS * HEAD_DIM], o_s[:, NSA_HEADS * HEAD_DIM:], st1, st2,
                              wo, g2, wg, wu, cwf, cbf, wd, seq=dec_seq)
        outs[1].append(nsa_rows.reshape(dec_batch, dec_seq, 4, NSA_KV, HEAD_DIM))
        outs[3].append(dif_rows.reshape(dec_batch, dec_seq, 2, DIFF_HEADS, HEAD_DIM))
        outs[5].append(jnp.moveaxis(new_state.reshape(dec_batch, 2, NSA_KV, HEAD_DIM, w_buf), -1, 1))
        outs[7].append(g_s.reshape(dec_batch, dec_seq, d_ff)[:, dec_seq - 2:])

    outs[0] = jnp.moveaxis(nsa_t.reshape(depth, batch, 4, NSA_KV, HEAD_DIM, seq), -1, 2)
    outs[2] = jnp.moveaxis(dif_t.reshape(depth, batch, 2, DIFF_HEADS, HEAD_DIM, seq), -1, 2)
    stacked = [o if not isinstance(o, list) else jnp.stack(o) for o in outs]
    return (xp.reshape(batch, seq, d_model), xs.reshape(dec_batch, dec_seq, d_model), *stacked)
```
